```python
import math
import jax
import jax.numpy as jnp
from jax import lax
import numpy as np

D_MODEL = 2048
BATCH = 4
SEQ = 2048
DEPTH = 2
DEC_BATCH = 32
DEC_SEQ = 8
PAST_LEN = 8192
PAGE_SIZE = 128

RMS_EPS = 1e-6
HEAD_DIM = 64
ATT_GROUPS = ((128, 1), (512, 4), (2048, 16))
HEADS_PER_GROUP = 4
ATT_HEADS = HEADS_PER_GROUP * len(ATT_GROUPS)
ATT_W = ATT_HEADS * HEAD_DIM
ATT_OUT = HEADS_PER_GROUP * HEAD_DIM
ROPE_THETA = 10000.0
POOL_WINDOWS = (2, 4, 8, 16)
POOL_GROUP = 128
POOL_W = POOL_GROUP * len(POOL_WINDOWS)
POOL_HIST = max(POOL_WINDOWS)
POOL_STATE = POOL_HIST - 1
RWKV_HEAD = 64
RWKV_HEADS = 12
RWKV_W = RWKV_HEADS * RWKV_HEAD
W_LORA = 64
A_LORA = 64
G_LORA = 128
RWKV_COLS = 3 * RWKV_W + W_LORA + A_LORA + G_LORA
GN_EPS = 64e-5
N_BRANCH = 3
OFF_POOL = 3 * ATT_W
OFF_RWKV = OFF_POOL + POOL_W
OFF_GATE = OFF_RWKV + RWKV_COLS
IN_COLS = OFF_GATE + N_BRANCH * D_MODEL
PEER_HEADS = 8
PEER_QDIM = 256
N_KEYS = 128
N_EXPERTS = N_KEYS * N_KEYS
PEER_TOPK = 16
PEER_CHUNK = 64
PLE_DIM = 256

kernel_name = 'hybrid_dilated_pool_rwkv7_peer_step'


def rmsnorm(x, g):
    xf = x.astype(jnp.float32)
    y = xf * lax.rsqrt(jnp.mean(xf * xf, axis=-1, keepdims=True) + RMS_EPS)
    return (y * g.astype(jnp.float32)).astype(x.dtype)


def rotary(x, pos):
    half = HEAD_DIM // 2
    inv = ROPE_THETA ** (-jnp.arange(half, dtype=jnp.float32) / half)
    ang = pos.astype(jnp.float32)[:, None] * inv[None, :]
    cos = jnp.cos(ang)[None, :, None, :]
    sin = jnp.sin(ang)[None, :, None, :]
    xf = x.astype(jnp.float32)
    x1, x2 = xf[..., :half], xf[..., half:]
    return jnp.concatenate([x1 * cos - x2 * sin, x2 * cos + x1 * sin], axis=-1).astype(x.dtype)


def _masked_softmax_parts(s, valid):
    s = jnp.where(valid, s, -jnp.inf)
    m = jnp.max(s, axis=-1, keepdims=True)
    e = jnp.exp(s - m)
    den = jnp.sum(e, axis=-1)
    return e, den, m[..., 0] + jnp.log(den)


def dilated_band(q, k, v, dil, span):
    N, S, H, Dh = q.shape
    n = S // dil
    blk = span
    nb = -(-n // blk)
    n_pad = nb * blk

    def split(a):
        a = a.reshape(N, n, dil, H, Dh).transpose(0, 2, 1, 3, 4)
        a = jnp.pad(a, ((0, 0), (0, 0), (0, n_pad - n), (0, 0), (0, 0)))
        return a.reshape(N, dil, nb, blk, H, Dh)

    def band(a):
        prev = jnp.pad(a, ((0, 0), (0, 0), (1, 0), (0, 0), (0, 0), (0, 0)))[:, :, :-1]
        return jnp.concatenate([prev, a], axis=3)

    qb = split(q)
    kc = band(split(k))
    vc = band(split(v))
    s = jnp.einsum('nrbqhd,nrbkhd->nrbhqk', qb, kc, preferred_element_type=jnp.float32) * (Dh ** -0.5)
    qi = jnp.arange(blk)[:, None]
    ki = jnp.arange(2 * blk)[None, :]
    bi = jnp.arange(nb)[:, None, None]
    dist = blk + qi - ki
    kpos = (bi - 1) * blk + ki[None]
    valid = (dist >= 0) & (dist <= span) & (kpos >= 0)
    e, den, lse = _masked_softmax_parts(s, valid[None, None, :, None])
    o = jnp.einsum('nrbhqk,nrbkhd->nrbqhd', e, vc.astype(jnp.float32)) / jnp.swapaxes(den, -1, -2)[..., None]
    lse = jnp.swapaxes(lse, -1, -2)

    def merge(a):
        a = a.reshape((N, dil, n_pad) + a.shape[4:])[:, :, :n]
        a = jnp.swapaxes(a, 1, 2)
        return a.reshape((N, S) + a.shape[3:])

    return merge(o), merge(lse)


def dilated_gather(q, kall, vall, L, dil, span):
    T = q.shape[1]
    Dh = q.shape[-1]
    idx = L + jnp.arange(T)[:, None] - dil * jnp.arange(span + 1)[None, :]
    valid = idx >= 0
    idx = jnp.maximum(idx, 0)
    kg = kall[:, idx]
    vg = vall[:, idx]
    s = jnp.einsum('nthd,ntjhd->nthj', q, kg, preferred_element_type=jnp.float32) * (Dh ** -0.5)
    e, den, lse = _masked_softmax_parts(s, valid[None, :, None, :])
    o = jnp.einsum('nthj,ntjhd->nthd', e, vg.astype(jnp.float32)) / den[..., None]
    return o, lse


def pool_mixer(zp, hist, pos, w_grp, scale):
    N, T, _ = zp.shape
    zf = zp.astype(jnp.float32)
    cs = jnp.cumsum(jnp.concatenate([hist.astype(jnp.float32), zf], axis=1), axis=1)
    parts = []
    for g, w in enumerate(POOL_WINDOWS):
        c = cs[..., g * POOL_GROUP:(g + 1) * POOL_GROUP]
        wsum = c[:, POOL_HIST:] - c[:, POOL_HIST - w:POOL_HIST - w + T]
        cnt = jnp.minimum(w, pos + 1).astype(jnp.float32)
        parts.append(wsum / cnt[None, :, None])
    y = jnp.concatenate(parts, axis=-1) - zf
    y = jnp.einsum('ntgc,gcd->ntgd', y.reshape(N, T, len(POOL_WINDOWS), POOL_GROUP),
                   w_grp.astype(jnp.float32)).reshape(N, T, POOL_W)
    return (y * scale).astype(zp.dtype)


def rwkv7_mixer(zr, prev, S0, mu, w0, w2, a0, a2, g2, k_k, k_a, r_k, ln_w, ln_b):
    N, T, _ = zr.shape
    H, Dh, W = RWKV_HEADS, RWKV_HEAD, RWKV_W
    zf = zr.astype(jnp.float32)
    shifted = jnp.concatenate([prev.astype(jnp.float32)[:, None], zf[:, :-1]], axis=1)
    xm = zf + mu * (shifted - zf)
    r = xm[..., :W]
    k = xm[..., W:2 * W]
    v = xm[..., 2 * W:3 * W]
    o = 3 * W
    wl = xm[..., o:o + W_LORA]
    al = xm[..., o + W_LORA:o + W_LORA + A_LORA]
    gl = xm[..., o + W_LORA + A_LORA:]
    wlog = -jax.nn.softplus(-(w0 + jnp.tanh(wl) @ w2)) - 0.5
    decay = jnp.exp(-jnp.exp(wlog))
    a = jax.nn.sigmoid(a0 + al @ a2)
    g = jax.nn.sigmoid(gl) @ g2

    def heads(t):
        return t.reshape(N, T, H, Dh)

    kk = heads(k * k_k)
    kk = kk / jnp.maximum(jnp.sqrt(jnp.sum(kk * kk, axis=-1, keepdims=True)), 1e-12)
    k = k * (1.0 + (a - 1.0) * k_a)
    rh, wh, kh, vh, ah = heads(r), heads(decay), heads(k), heads(v), heads(a)
    bh = kk * ah

    def step(S, inp):
        r_t, w_t, k_t, v_t, kk_t, b_t = inp
        sa = jnp.einsum('nhij,nhj->nhi', S, -kk_t)
        S = S * w_t[:, :, None, :] + sa[..., None] * b_t[:, :, None, :] + v_t[..., None] * k_t[:, :, None, :]
        return S, jnp.einsum('nhij,nhj->nhi', S, r_t)

    xs = tuple(jnp.swapaxes(t, 0, 1) for t in (rh, wh, kh, vh, kk, bh))
    S_T, ys = lax.scan(step, S0.astype(jnp.float32), xs)
    y = jnp.swapaxes(ys, 0, 1)
    mean = jnp.mean(y, axis=-1, keepdims=True)
    var = jnp.mean(jnp.square(y - mean), axis=-1, keepdims=True)
    yn = ((y - mean) * lax.rsqrt(var + GN_EPS)).reshape(N, T, W) * ln_w + ln_b
    bonus = (jnp.sum(rh * kh * r_k, axis=-1, keepdims=True) * vh).reshape(N, T, W)
    return ((yn + bonus) * g).astype(zr.dtype), S_T


def peer_ffn(h, wq, subkeys, u_tab, v_tab):
    N, T, D = h.shape
    n = N * T
    n_pad = -(-n // PEER_CHUNK) * PEER_CHUNK
    xt = jnp.pad(h.reshape(n, D), ((0, n_pad - n), (0, 0))).reshape(n_pad // PEER_CHUNK, PEER_CHUNK, D)
    half = PEER_QDIM // 2
    sk = subkeys.astype(jnp.float32)

    def chunk(xc):
        q = (xc @ wq).astype(jnp.float32).reshape(PEER_CHUNK, PEER_HEADS, 2, half)
        s1 = jnp.einsum('chd,hkd->chk', q[:, :, 0], sk[:, 0])
        s2 = jnp.einsum('chd,hkd->chk', q[:, :, 1], sk[:, 1])
        t1, i1 = lax.top_k(s1, PEER_TOPK)
        t2, i2 = lax.top_k(s2, PEER_TOPK)
        cand = (t1[..., :, None] + t2[..., None, :]).reshape(PEER_CHUNK, PEER_HEADS, PEER_TOPK * PEER_TOPK)
        cidx = (i1[..., :, None] * N_KEYS + i2[..., None, :]).reshape(PEER_CHUNK, PEER_HEADS, PEER_TOPK * PEER_TOPK)
        top, sel = lax.top_k(cand, PEER_TOPK)
        eidx = jnp.take_along_axis(cidx, sel, axis=-1)
        gate = jax.nn.softmax(top, axis=-1)
        act = jax.nn.gelu(jnp.einsum('cd,chkd->chk', xc, u_tab[eidx], preferred_element_type=jnp.float32),
                          approximate=False)
        return jnp.einsum('chk,chkd->cd', gate * act, v_tab[eidx].astype(jnp.float32)).astype(h.dtype)

    out = lax.map(chunk, xt)
    return out.reshape(n_pad, D)[:n].reshape(N, T, D)


def trunk_layer(x, p_l, pos, state, lw):
    N, T, _ = x.shape
    h = rmsnorm(x, lw['g_mix'])
    z = h @ lw['w_in']
    q = rotary(z[..., :ATT_W].reshape(N, T, ATT_HEADS, HEAD_DIM), pos)
    k = rotary(z[..., ATT_W:2 * ATT_W].reshape(N, T, ATT_HEADS, HEAD_DIM), pos)
    v = z[..., 2 * ATT_W:OFF_POOL].reshape(N, T, ATT_HEADS, HEAD_DIM)
    zp = z[..., OFF_POOL:OFF_RWKV]
    zr = z[..., OFF_RWKV:OFF_GATE]
    gates = jax.nn.sigmoid(z[..., OFF_GATE:].astype(jnp.float32)).reshape(N, T, N_BRANCH, D_MODEL)

    if state is None:
        kv_bufs = (None, None, None)
        pool_hist = jnp.zeros((N, POOL_HIST, POOL_W), zp.dtype)
        shift_prev = jnp.zeros((N, RWKV_COLS), zr.dtype)
        wkv0 = jnp.zeros((N, RWKV_HEADS, RWKV_HEAD, RWKV_HEAD), jnp.float32)
    else:
        kv_bufs = state[:3]
        pool_buf, wkv0, shift_prev = state[3], state[4], state[5]
        pool_hist = jnp.concatenate(
            [jnp.zeros((N, POOL_HIST - POOL_STATE, POOL_W), pool_buf.dtype), pool_buf], axis=1)

    outs, lses, new_kv = [], [], []
    for gi, (win, dil) in enumerate(ATT_GROUPS):
        hs = slice(gi * HEADS_PER_GROUP, (gi + 1) * HEADS_PER_GROUP)
        qg, kg, vg = q[:, :, hs], k[:, :, hs], v[:, :, hs]
        span = win // dil
        buf = kv_bufs[gi]
        if buf is None:
            og, lg = dilated_band(qg, kg, vg, dil, span)
            keep = min(win, T)
            new_kv.append(jnp.stack([kg[:, T - keep:], vg[:, T - keep:]], axis=2))
        else:
            L = buf.shape[1]
            kall = jnp.concatenate([buf[:, :, 0], kg], axis=1)
            vall = jnp.concatenate([buf[:, :, 1], vg], axis=1)
            og, lg = dilated_gather(qg, kall, vall, L, dil, span)
            new_kv.append(jnp.stack([kall[:, T:], vall[:, T:]], axis=2))
        outs.append(og)
        lses.append(lg)
    o_grp = jnp.stack(outs, axis=2)
    alpha = jax.nn.softmax(jnp.stack(lses, axis=2), axis=2)
    att = jnp.sum(alpha[..., None] * o_grp, axis=2).reshape(N, T, ATT_OUT).astype(x.dtype)

    pool = pool_mixer(zp, pool_hist, pos, lw['pool_w'], lw['pool_scale'])
    new_pool = jnp.concatenate([pool_hist[:, 1:], zp], axis=1)[:, -POOL_STATE:]

    rw, wkv_T = rwkv7_mixer(zr, shift_prev, wkv0, lw['rwkv_mu'], lw['rwkv_w0'], lw['rwkv_w2'],
                            lw['rwkv_a0'], lw['rwkv_a2'], lw['rwkv_g2'], lw['rwkv_kk'], lw['rwkv_ka'],
                            lw['rwkv_rk'], lw['rwkv_ln_w'], lw['rwkv_ln_b'])
    new_shift = zr[:, -1]

    mix = (gates[:, :, 0] * (att @ lw['w_attn_o'])
           + gates[:, :, 1] * (pool @ lw['w_pool_o'])
           + gates[:, :, 2] * (rw @ lw['w_rwkv_o']))
    x = x + mix.astype(x.dtype) @ lw['w_out']

    x = x + peer_ffn(rmsnorm(x, lw['g_ffn']), lw['peer_wq'], lw['peer_subkeys'], lw['peer_u'], lw['peer_v'])

    h3 = rmsnorm(x, lw['g_ple'])
    x = x + (jax.nn.sigmoid((h3 @ lw['ple_wg']).astype(jnp.float32)) * (p_l @ lw['ple_wp'])).astype(x.dtype)
    return x, (new_kv[0], new_kv[1], new_kv[2], new_pool, wkv_T.astype(x.dtype), new_shift)


def setup_inputs(seed: int = 0) -> dict:
    key = jax.random.key(seed)
    ks = iter(jax.random.split(key, 64))

    def nrm(shape, scale=1.0):
        return jax.random.normal(next(ks), shape, jnp.float32) * scale

    def gain(shape):
        return 1.0 + 0.05 * jax.random.normal(next(ks), shape, jnp.float32)

    L128, L512, L2048 = (min(w, PAST_LEN) for w, _ in ATT_GROUPS)
    return {
        'x_prompt': nrm((BATCH, SEQ, D_MODEL)),
        'x_sample': nrm((DEC_BATCH, DEC_SEQ, D_MODEL)),
        'p_prompt': nrm((DEPTH, BATCH, SEQ, PLE_DIM)),
        'p_sample': nrm((DEPTH, DEC_BATCH, DEC_SEQ, PLE_DIM)),
        'cache_attn_w128': nrm((DEPTH, DEC_BATCH, L128, 2, HEADS_PER_GROUP, HEAD_DIM)),
        'cache_attn_w512': nrm((DEPTH, DEC_BATCH, L512, 2, HEADS_PER_GROUP, HEAD_DIM)),
        'cache_attn_w2048': nrm((DEPTH, DEC_BATCH, L2048, 2, HEADS_PER_GROUP, HEAD_DIM)),
        'state_pool': nrm((DEPTH, DEC_BATCH, POOL_STATE, POOL_W)),
        'state_rwkv_wkv': nrm((DEPTH, DEC_BATCH, RWKV_HEADS, RWKV_HEAD, RWKV_HEAD), 0.5),
        'state_rwkv_shift': nrm((DEPTH, DEC_BATCH, RWKV_COLS)),
        'g_mix': gain((DEPTH, D_MODEL)),
        'w_in': nrm((DEPTH, D_MODEL, IN_COLS), D_MODEL ** -0.5),
        'w_attn_o': nrm((DEPTH, ATT_OUT, D_MODEL), ATT_OUT ** -0.5),
        'w_pool_o': nrm((DEPTH, POOL_W, D_MODEL), POOL_W ** -0.5),
        'w_rwkv_o': nrm((DEPTH, RWKV_W, D_MODEL), RWKV_W ** -0.5),
        'w_out': nrm((DEPTH, D_MODEL, D_MODEL), D_MODEL ** -0.5),
        'pool_w': nrm((DEPTH, len(POOL_WINDOWS), POOL_GROUP, POOL_GROUP), POOL_GROUP ** -0.5),
        'pool_scale': gain((DEPTH, POOL_W)),
        'rwkv_mu': jax.random.uniform(next(ks), (DEPTH, RWKV_COLS), jnp.float32),
        'rwkv_w0': nrm((DEPTH, RWKV_W), 0.5) - 0.5,
        'rwkv_w2': nrm((DEPTH, W_LORA, RWKV_W), 0.5 * W_LORA ** -0.5),
        'rwkv_a0': nrm((DEPTH, RWKV_W), 0.1),
        'rwkv_a2': nrm((DEPTH, A_LORA, RWKV_W), 0.5 * A_LORA ** -0.5),
        'rwkv_g2': nrm((DEPTH, G_LORA, RWKV_W), G_LORA ** -0.5),
        'rwkv_kk': 0.85 + nrm((DEPTH, RWKV_W), 0.05),
        'rwkv_ka': gain((DEPTH, RWKV_W)),
        'rwkv_rk': nrm((DEPTH, RWKV_HEADS, RWKV_HEAD), 0.1),
        'rwkv_ln_w': gain((DEPTH, RWKV_W)),
        'rwkv_ln_b': nrm((DEPTH, RWKV_W), 0.01),
        'g_ffn': gain((DEPTH, D_MODEL)),
        'peer_wq': nrm((DEPTH, D_MODEL, PEER_HEADS * PEER_QDIM), D_MODEL ** -0.5),
        'peer_subkeys': nrm((DEPTH, PEER_HEADS, 2, N_KEYS, PEER_QDIM // 2), (PEER_QDIM // 2) ** -0.5),
        'peer_u': nrm((DEPTH, N_EXPERTS, D_MODEL), D_MODEL ** -0.5),
        'peer_v': nrm((DEPTH, N_EXPERTS, D_MODEL), PEER_HEADS ** -0.5),
        'g_ple': gain((DEPTH, D_MODEL)),
        'ple_wg': nrm((DEPTH, D_MODEL, D_MODEL), D_MODEL ** -0.5),
        'ple_wp': nrm((DEPTH, PLE_DIM, D_MODEL), 0.5 * PLE_DIM ** -0.5),
        'g_final': gain((D_MODEL,)),
    }


def reference(x_prompt, x_sample, p_prompt, p_sample, cache_attn_w128, cache_attn_w512, cache_attn_w2048,
              state_pool, state_rwkv_wkv, state_rwkv_shift, g_mix, w_in, w_attn_o, w_pool_o, w_rwkv_o, w_out,
              pool_w, pool_scale, rwkv_mu, rwkv_w0, rwkv_w2, rwkv_a0, rwkv_a2, rwkv_g2, rwkv_kk, rwkv_ka,
              rwkv_rk, rwkv_ln_w, rwkv_ln_b, g_ffn, peer_wq, peer_subkeys, peer_u, peer_v, g_ple, ple_wg,
              ple_wp, g_final):
    pos_p = jnp.arange(x_prompt.shape[1], dtype=jnp.int32)
    pos_s = PAST_LEN + jnp.arange(x_sample.shape[1], dtype=jnp.int32)
    xp, xs = x_prompt, x_sample
    new_p, new_s = [], []
    for l in range(DEPTH):
        lw = {
            'g_mix': g_mix[l], 'w_in': w_in[l], 'w_attn_o': w_attn_o[l], 'w_pool_o': w_pool_o[l],
            'w_rwkv_o': w_rwkv_o[l], 'w_out': w_out[l], 'pool_w': pool_w[l], 'pool_scale': pool_scale[l],
            'rwkv_mu': rwkv_mu[l], 'rwkv_w0': rwkv_w0[l], 'rwkv_w2': rwkv_w2[l], 'rwkv_a0': rwkv_a0[l],
            'rwkv_a2': rwkv_a2[l], 'rwkv_g2': rwkv_g2[l], 'rwkv_kk': rwkv_kk[l], 'rwkv_ka': rwkv_ka[l],
            'rwkv_rk': rwkv_rk[l], 'rwkv_ln_w': rwkv_ln_w[l], 'rwkv_ln_b': rwkv_ln_b[l], 'g_ffn': g_ffn[l],
            'peer_wq': peer_wq[l], 'peer_subkeys': peer_subkeys[l], 'peer_u': peer_u[l], 'peer_v': peer_v[l],
            'g_ple': g_ple[l], 'ple_wg': ple_wg[l], 'ple_wp': ple_wp[l],
        }
        xp, st_p = trunk_layer(xp, p_prompt[l], pos_p, None, lw)
        xs, st_s = trunk_layer(xs, p_sample[l], pos_s,
                               (cache_attn_w128[l], cache_attn_w512[l], cache_attn_w2048[l],
                                state_pool[l], state_rwkv_wkv[l], state_rwkv_shift[l]), lw)
        new_p.append(st_p)
        new_s.append(st_s)
    y_prompt = rmsnorm(xp, g_final)
    y_sample = rmsnorm(xs, g_final)
    kv128_prompt = jnp.stack([s[0] for s in new_p])
    kv128_sample = jnp.stack([s[0] for s in new_s])
    kv512_prompt = jnp.stack([s[1] for s in new_p])
    kv512_sample = jnp.stack([s[1] for s in new_s])
    kv2048_prompt = jnp.stack([s[2] for s in new_p])
    kv2048_sample = jnp.stack([s[2] for s in new_s])
    pool_prompt = jnp.stack([s[3] for s in new_p])
    pool_sample = jnp.stack([s[3] for s in new_s])
    wkv_prompt = jnp.stack([s[4] for s in new_p])
    wkv_sample = jnp.stack([s[4] for s in new_s])
    shift_prompt = jnp.stack([s[5] for s in new_p])
    shift_sample = jnp.stack([s[5] for s in new_s])
    return (y_prompt, y_sample, kv128_prompt, kv128_sample, kv512_prompt, kv512_sample,
            kv2048_prompt, kv2048_sample, pool_prompt, pool_sample, wkv_prompt, wkv_sample,
            shift_prompt, shift_sample)
```

```python
import functools
import math

import jax
import jax.numpy as jnp
from jax import lax
from jax.experimental import pallas as pl
from jax.experimental.pallas import tpu as pltpu

F32 = jnp.float32
BF16 = jnp.bfloat16
I32 = jnp.int32

D_MODEL = 2048
RMS_EPS = 1e-6
HEAD_DIM = 64
ATT_GROUPS = ((128, 1), (512, 4), (2048, 16))
HEADS_PER_GROUP = 4
ATT_W = 768
ATT_OUT = 256
ATT_SPAN = 128
ROPE_THETA = 10000.0
POOL_WINDOWS = (2, 4, 8, 16)
POOL_GROUP = 128
POOL_W = 512
POOL_HIST = 16
RWKV_HEAD = 64
RWKV_HEADS = 12
RWKV_W = 768
RWKV_COLS = 2560
RWKV_LORA_OFF = 2304
GN_EPS = 64e-5
RWKV_CHUNK = 64
OFF_POOL = 2304
OFF_RWKV = 2816
OFF_GATE = 5376
PAST_LEN = 8192
PEER_HEADS = 8
N_KEYS = 128
N_EXPERTS = N_KEYS * N_KEYS
PEER_TOPK = 16
PEER_PAIRS = PEER_HEADS * PEER_TOPK
PEER_TB = 384
PEER_EC = 1024
SUBLANES = 8
VMEM_LIMIT = 56 * 1024 * 1024

NEG_INF = float("-inf")


def _params(*sem):
    return pltpu.CompilerParams(dimension_semantics=sem, vmem_limit_bytes=VMEM_LIMIT)


def _dot(a, b):
    return jnp.dot(a.astype(BF16), b.astype(BF16), preferred_element_type=F32)


def _dot_nt(a, b):
    return lax.dot_general(a.astype(BF16), b.astype(BF16), (((1,), (1,)), ((), ())), preferred_element_type=F32)


def _norm_mm_kernel(x_ref, g_ref, w_ref, o_ref, xn_ref):
    @pl.when(pl.program_id(1) == 0)
    def _():
        x = x_ref[...]
        ms = jnp.mean(x * x, axis=-1, keepdims=True)
        xn_ref[...] = ((x * lax.rsqrt(ms + RMS_EPS)) * g_ref[...]).astype(BF16)

    o_ref[...] = jnp.dot(xn_ref[...], w_ref[...], preferred_element_type=F32)


def _norm_mm(x, g, w, *, tm, tn):
    rows, k = x.shape
    n = w.shape[1]
    return pl.pallas_call(
        _norm_mm_kernel, grid=(rows // tm, n // tn),
        in_specs=[
            pl.BlockSpec((tm, k), lambda i, j: (i, 0)),
            pl.BlockSpec((1, k), lambda i, j: (0, 0)),
            pl.BlockSpec((k, tn), lambda i, j: (0, j)),
        ],
        out_specs=pl.BlockSpec((tm, tn), lambda i, j: (i, j)),
        out_shape=jax.ShapeDtypeStruct((rows, n), F32),
        scratch_shapes=[pltpu.VMEM((tm, k), BF16)],
        compiler_params=_params("arbitrary", "arbitrary"), name="norm_mm",
    )(x, g.reshape(1, k), w)


def _split_bf16(a):
    hi = a.astype(BF16)
    lo = (a - hi.astype(F32)).astype(BF16)
    return hi, lo


def _dot3(ah, al, bh, bl, dims=(((1,), (0,)), ((), ()))):
    def d(p, q):
        return lax.dot_general(p, q, dims, preferred_element_type=F32)

    return d(ah, bh) + (d(ah, bl) + d(al, bh))


def _norm_mm3_kernel(x_ref, g_ref, wh_ref, wl_ref, o_ref, xh_ref, xl_ref):
    @pl.when(pl.program_id(1) == 0)
    def _():
        x = x_ref[...]
        ms = jnp.mean(x * x, axis=-1, keepdims=True)
        xh, xl = _split_bf16((x * lax.rsqrt(ms + RMS_EPS)) * g_ref[...])
        xh_ref[...] = xh
        xl_ref[...] = xl

    o_ref[...] = _dot3(xh_ref[...], xl_ref[...], wh_ref[...], wl_ref[...])


def _norm_mm3(x, g, wh, wl, *, tm, tn):
    rows, k = x.shape
    n = wh.shape[1]
    wspec = pl.BlockSpec((k, tn), lambda i, j: (0, j))
    return pl.pallas_call(
        _norm_mm3_kernel, grid=(rows // tm, n // tn),
        in_specs=[pl.BlockSpec((tm, k), lambda i, j: (i, 0)), pl.BlockSpec((1, k), lambda i, j: (0, 0)), wspec, wspec],
        out_specs=[pl.BlockSpec((tm, tn), lambda i, j: (i, j)), pl.BlockSpec((tm, k), lambda i, j: (i, 0))],
        out_shape=[jax.ShapeDtypeStruct((rows, n), F32), jax.ShapeDtypeStruct((rows, k), BF16)],
        scratch_shapes=[pltpu.VMEM((tm, k), BF16)],
        compiler_params=_params("arbitrary", "arbitrary"), name="norm_mm3",
    )(x, g.reshape(1, k), wh, wl)


def _rotary_kernel(q_ref, k_ref, cos_ref, sin_ref, qo_ref, ko_ref):
    cos = cos_ref[...]
    sin = sin_ref[...]
    lane = lax.broadcasted_iota(I32, cos.shape, 1)
    first_half = (lane & (HEAD_DIM - 1)) < HEAD_DIM // 2
    for src, dst in ((q_ref, qo_ref), (k_ref, ko_ref)):
        for c in range(ATT_W // 128):
            x = src[:, c * 128:(c + 1) * 128]
            partner = jnp.where(first_half, pltpu.roll(x, 128 - HEAD_DIM // 2, 1), pltpu.roll(x, HEAD_DIM // 2, 1))
            dst[:, c * 128:(c + 1) * 128] = x * cos + partner * sin


def _rotary(z_qkv, cos_t, sin_t, *, tm, prompt_blocks, pos_blocks):
    rows = z_qkv.shape[0]

    def tab(i):
        return (jnp.where(i < prompt_blocks, i % pos_blocks, pos_blocks), 0)

    return pl.pallas_call(
        _rotary_kernel, grid=(rows // tm,),
        in_specs=[pl.BlockSpec((tm, ATT_W), lambda i: (i, 0)), pl.BlockSpec((tm, ATT_W), lambda i: (i, 1)),
                  pl.BlockSpec((tm, 128), tab), pl.BlockSpec((tm, 128), tab)],
        out_specs=[pl.BlockSpec((tm, ATT_W), lambda i: (i, 0))] * 2,
        out_shape=[jax.ShapeDtypeStruct((rows, ATT_W), F32)] * 2,
        compiler_params=_params("arbitrary"), name="rotary",
    )(z_qkv, z_qkv, cos_t, sin_t)


def _rope_tables(pos):
    half = HEAD_DIM // 2
    inv = ROPE_THETA ** (-jnp.arange(half, dtype=F32) / half)
    ang = pos.astype(F32)[:, None] * inv[None, :]
    cos = jnp.cos(ang)
    sin = jnp.sin(ang)
    cos_t = jnp.concatenate([cos, cos, cos, cos], axis=1)
    sin_t = jnp.concatenate([-sin, sin, -sin, sin], axis=1)
    return cos_t, sin_t


def _band_attn_kernel(q_ref, kc_ref, kp_ref, vc_ref, vp_ref, o_ref, l_ref, *, has_prev):
    b = pl.program_id(2)
    qi = lax.broadcasted_iota(I32, (ATT_SPAN, ATT_SPAN), 0)
    ki = lax.broadcasted_iota(I32, (ATT_SPAN, ATT_SPAN), 1)
    cur_ok = ki <= qi
    prev_ok = jnp.logical_and(ki >= qi, b > 0)
    scale = HEAD_DIM ** -0.5
    outs, lses = [], []
    for h in range(HEADS_PER_GROUP):
        sl = slice(h * HEAD_DIM, (h + 1) * HEAD_DIM)
        q = q_ref[:, sl]
        sc = jnp.where(cur_ok, _dot_nt(q, kc_ref[:, sl]) * scale, NEG_INF)
        m = jnp.max(sc, axis=-1, keepdims=True)
        if has_prev:
            sp = jnp.where(prev_ok, _dot_nt(q, kp_ref[:, sl]) * scale, NEG_INF)
            m = jnp.maximum(m, jnp.max(sp, axis=-1, keepdims=True))
        ec = jnp.exp(sc - m)
        den = jnp.sum(ec, axis=-1, keepdims=True)
        acc = _dot(ec, vc_ref[:, sl])
        if has_prev:
            ep = jnp.exp(sp - m)
            den = den + jnp.sum(ep, axis=-1, keepdims=True)
            acc = acc + _dot(ep, vp_ref[:, sl])
        outs.append(acc / den)
        lses.append(jnp.broadcast_to(m + jnp.log(den), (ATT_SPAN, HEAD_DIM)))
    o_ref[...] = jnp.concatenate(outs, axis=1)
    l_ref[...] = jnp.concatenate(lses, axis=1)


def _band_attn(qr, kr, z_qkv, gi, dil, *, n_seq, seq):
    rows_p = n_seq * seq
    n_cls = seq // dil
    nb = -(-n_cls // ATT_SPAN)
    assert n_cls % ATT_SPAN == 0
    qv = qr.reshape(qr.shape[0] // dil, dil * ATT_W)
    kv = kr.reshape(kr.shape[0] // dil, dil * ATT_W)
    zv = z_qkv.reshape(z_qkv.shape[0] // dil, dil * 3 * ATT_W)
    cq = ATT_W // ATT_OUT
    cz = 3 * ATT_W // ATT_OUT
    voff = 2 * ATT_W // ATT_OUT + gi

    def cur(n, r, b):
        return n * nb + b

    def prev(n, r, b):
        return n * nb + jnp.maximum(b - 1, 0)

    blk = (ATT_SPAN, ATT_OUT)
    o, l = pl.pallas_call(
        functools.partial(_band_attn_kernel, has_prev=nb > 1),
        grid=(n_seq, dil, nb),
        in_specs=[
            pl.BlockSpec(blk, lambda n, r, b: (cur(n, r, b), r * cq + gi)),
            pl.BlockSpec(blk, lambda n, r, b: (cur(n, r, b), r * cq + gi)),
            pl.BlockSpec(blk, lambda n, r, b: (prev(n, r, b), r * cq + gi)),
            pl.BlockSpec(blk, lambda n, r, b: (cur(n, r, b), r * cz + voff)),
            pl.BlockSpec(blk, lambda n, r, b: (prev(n, r, b), r * cz + voff)),
        ],
        out_specs=[pl.BlockSpec(blk, lambda n, r, b: (cur(n, r, b), r))] * 2,
        out_shape=[jax.ShapeDtypeStruct((rows_p // dil, dil * ATT_OUT), F32)] * 2,
        compiler_params=_params("arbitrary", "arbitrary", "arbitrary"), name=f"band_attn_g{gi}",
    )(qv, kv, kv, zv, zv)
    return o.reshape(rows_p, ATT_OUT), l.reshape(rows_p, ATT_OUT)


def _merge_groups(os_, ls_):
    m = jnp.maximum(jnp.maximum(ls_[0], ls_[1]), ls_[2])
    es = [jnp.exp(l - m) for l in ls_]
    tot = es[0] + es[1] + es[2]
    return (es[0] / tot) * os_[0] + (es[1] / tot) * os_[1] + (es[2] / tot) * os_[2]


def _merge_kernel(o0, o1, o2, l0, l1, l2, a_ref):
    a_ref[...] = _merge_groups([o0[...], o1[...], o2[...]], [l0[...], l1[...], l2[...]])


def _merge_attn(os_, ls_, *, tm):
    rows = os_[0].shape[0]
    spec = pl.BlockSpec((tm, ATT_OUT), lambda i: (i, 0))
    return pl.pallas_call(
        _merge_kernel, grid=(rows // tm,), in_specs=[spec] * 6, out_specs=spec,
        out_shape=jax.ShapeDtypeStruct((rows, ATT_OUT), F32),
        compiler_params=_params("arbitrary"), name="merge_attn",
    )(*os_, *ls_)


def _sample_attn_kernel(q_ref, kn_ref, vn_ref, c0_ref, c1_ref, c2_ref, a_ref, *, t_new):
    scale = HEAD_DIM ** -0.5
    outs_g, lses_g = [], []
    for gi, (c_ref, (win, dil)) in enumerate(zip((c0_ref, c1_ref, c2_ref), ATT_GROUPS)):
        cache_len = c_ref.shape[1]
        t_c = lax.broadcasted_iota(I32, (t_new, cache_len), 0)
        c_c = lax.broadcasted_iota(I32, (t_new, cache_len), 1)
        d_c = cache_len + t_c - c_c
        ok_c = jnp.logical_and((d_c & (dil - 1)) == 0, d_c <= ATT_SPAN * dil)
        t_n = lax.broadcasted_iota(I32, (t_new, t_new), 0)
        u_n = lax.broadcasted_iota(I32, (t_new, t_new), 1)
        d_n = t_n - u_n
        ok_n = jnp.logical_and(d_n >= 0, (d_n & (dil - 1)) == 0)
        outs, lses = [], []
        for h in range(HEADS_PER_GROUP):
            col = gi * ATT_OUT + h * HEAD_DIM
            q = q_ref[:, col:col + HEAD_DIM]
            kc = c_ref[0, :, h * HEAD_DIM:(h + 1) * HEAD_DIM]
            vc = c_ref[0, :, ATT_OUT + h * HEAD_DIM:ATT_OUT + (h + 1) * HEAD_DIM]
            s_c = jnp.where(ok_c, _dot_nt(q, kc) * scale, NEG_INF)
            s_n = jnp.where(ok_n, _dot_nt(q, kn_ref[:, col:col + HEAD_DIM]) * scale, NEG_INF)
            m = jnp.maximum(jnp.max(s_c, axis=-1, keepdims=True), jnp.max(s_n, axis=-1, keepdims=True))
            e_c = jnp.exp(s_c - m)
            e_n = jnp.exp(s_n - m)
            den = jnp.sum(e_c, axis=-1, keepdims=True) + jnp.sum(e_n, axis=-1, keepdims=True)
            acc = _dot(e_c, vc) + _dot(e_n, vn_ref[:, col:col + HEAD_DIM])
            outs.append(acc / den)
            lses.append(jnp.broadcast_to(m + jnp.log(den), (t_new, HEAD_DIM)))
        outs_g.append(jnp.concatenate(outs, axis=1))
        lses_g.append(jnp.concatenate(lses, axis=1))
    a_ref[...] = _merge_groups(outs_g, lses_g)


def _sample_attn(qr, kr, z_qkv, caches, *, row0, n_seq, t_new):
    b0 = row0 // t_new
    return pl.pallas_call(
        functools.partial(_sample_attn_kernel, t_new=t_new),
        grid=(n_seq,),
        in_specs=[
            pl.BlockSpec((t_new, ATT_W), lambda n: (b0 + n, 0)),
            pl.BlockSpec((t_new, ATT_W), lambda n: (b0 + n, 0)),
            pl.BlockSpec((t_new, ATT_W), lambda n: (b0 + n, 2)),
        ] + [pl.BlockSpec((1, c.shape[1], 2 * ATT_OUT), lambda n: (n, 0, 0)) for c in caches],
        out_specs=pl.BlockSpec((t_new, ATT_OUT), lambda n: (n, 0)),
        out_shape=jax.ShapeDtypeStruct((n_seq * t_new, ATT_OUT), F32),
        compiler_params=_params("arbitrary"), name="sample_attn",
    )(qr, kr, z_qkv, *caches)


def _pool_kernel(z_ref, h_ref, w_ref, s_ref, o_ref, buf_ref, *, t_len, pos0):
    buf_ref[0:POOL_HIST, :] = h_ref[0]
    buf_ref[POOL_HIST:POOL_HIST + t_len, :] = z_ref[...]
    pos = pos0 + lax.broadcasted_iota(I32, (t_len, POOL_GROUP), 0)
    for g, win in enumerate(POOL_WINDOWS):
        cs = slice(g * POOL_GROUP, (g + 1) * POOL_GROUP)
        z = buf_ref[POOL_HIST:POOL_HIST + t_len, cs]
        wsum = z
        for i in range(1, win):
            wsum = wsum + buf_ref[POOL_HIST - i:POOL_HIST - i + t_len, cs]
        cnt = jnp.minimum(win, pos + 1).astype(F32)
        y = wsum / cnt - z
        o_ref[:, cs] = _dot(y, w_ref[g]) * s_ref[:, cs]


def _pool_mixer(z_pool, hist, pool_w, pool_scale, *, row0, n_seq, t_len, pos0):
    b0 = row0 // t_len
    return pl.pallas_call(
        functools.partial(_pool_kernel, t_len=t_len, pos0=pos0),
        grid=(n_seq,),
        in_specs=[
            pl.BlockSpec((t_len, POOL_W), lambda n: (b0 + n, 0)),
            pl.BlockSpec((1, POOL_HIST, POOL_W), lambda n: (n, 0, 0)),
            pl.BlockSpec((len(POOL_WINDOWS), POOL_GROUP, POOL_GROUP), lambda n: (0, 0, 0)),
            pl.BlockSpec((1, POOL_W), lambda n: (0, 0)),
        ],
        out_specs=pl.BlockSpec((t_len, POOL_W), lambda n: (n, 0)),
        out_shape=jax.ShapeDtypeStruct((n_seq * t_len, POOL_W), F32),
        scratch_shapes=[pltpu.VMEM((POOL_HIST + t_len, POOL_W), F32)],
        compiler_params=_params("arbitrary"), name="pool_mixer",
    )(z_pool, hist, pool_w, pool_scale.reshape(1, POOL_W))


def _rwkv_prep_kernel(z_ref, zp_ref, first_ref, mu_ref, w0_ref, w2_ref, a0_ref, a2_ref, g2_ref, kk_ref, ka_ref, rk_ref,
                      r_o, lw_o, k_o, v_o, kn_o, b_o, g_o, bonus_o, buf_ref, *, tm):
    i = pl.program_id(1)
    z = z_ref[0]
    prev_row = jnp.where(i == 0, first_ref[0], zp_ref[0, SUBLANES - 1:SUBLANES, :])
    buf_ref[SUBLANES:SUBLANES + tm, :] = z
    buf_ref[SUBLANES - 1:SUBLANES, :] = prev_row
    shifted = buf_ref[SUBLANES - 1:SUBLANES - 1 + tm, :]
    xm = z + mu_ref[...] * (shifted - z)
    r = xm[:, 0:RWKV_W]
    k = xm[:, RWKV_W:2 * RWKV_W]
    v = xm[:, 2 * RWKV_W:3 * RWKV_W]
    wa = xm[:, RWKV_LORA_OFF:RWKV_LORA_OFF + 128]
    gl = xm[:, RWKV_LORA_OFF + 128:RWKV_COLS]
    xw = w0_ref[...] + _dot(jnp.tanh(wa), w2_ref[...])
    logw = -math.exp(-0.5) * jax.nn.sigmoid(xw)
    a = jax.nn.sigmoid(a0_ref[...] + _dot(wa, a2_ref[...]))
    g_o[0] = _dot(jax.nn.sigmoid(gl), g2_ref[...])
    kkr = k * kk_ref[...]
    kmod = k * (1.0 + (a - 1.0) * ka_ref[...])
    rkk = r * kmod * rk_ref[...]
    bonus = []
    for h in range(RWKV_HEADS):
        sl = slice(h * RWKV_HEAD, (h + 1) * RWKV_HEAD)
        kh = kkr[:, sl]
        nrm = jnp.sqrt(jnp.sum(kh * kh, axis=-1, keepdims=True))
        kn = kh / jnp.maximum(nrm, 1e-12)
        r_o[0, h] = r[:, sl]
        lw_o[0, h] = logw[:, sl]
        k_o[0, h] = kmod[:, sl]
        v_o[0, h] = v[:, sl]
        kn_o[0, h] = kn
        b_o[0, h] = kn * a[:, sl]
        bonus.append(jnp.sum(rkk[:, sl], axis=-1, keepdims=True) * v[:, sl])
    bonus_o[0] = jnp.concatenate(bonus, axis=1)


def _rwkv_prep(z3, first_prev, lw, *, tm):
    n_seq, t_len, _ = z3.shape
    nblk = t_len // tm
    pb = tm // SUBLANES

    def vec(n):
        return pl.BlockSpec((1, n), lambda s, i: (0, 0))

    hm = jax.ShapeDtypeStruct((n_seq, RWKV_HEADS, t_len, RWKV_HEAD), F32)
    rm = jax.ShapeDtypeStruct((n_seq, t_len, RWKV_W), F32)
    hm_spec = pl.BlockSpec((1, RWKV_HEADS, tm, RWKV_HEAD), lambda s, i: (s, 0, i, 0))
    rm_spec = pl.BlockSpec((1, tm, RWKV_W), lambda s, i: (s, i, 0))
    return pl.pallas_call(
        functools.partial(_rwkv_prep_kernel, tm=tm),
        grid=(n_seq, nblk),
        in_specs=[
            pl.BlockSpec((1, tm, RWKV_COLS), lambda s, i: (s, i, 0)),
            pl.BlockSpec((1, SUBLANES, RWKV_COLS), lambda s, i: (s, jnp.maximum(i * pb - 1, 0), 0)),
            pl.BlockSpec((1, 1, RWKV_COLS), lambda s, i: (s, 0, 0)),
            vec(RWKV_COLS), vec(RWKV_W),
            pl.BlockSpec((128, RWKV_W), lambda s, i: (0, 0)),
            vec(RWKV_W),
            pl.BlockSpec((128, RWKV_W), lambda s, i: (0, 0)),
            pl.BlockSpec((128, RWKV_W), lambda s, i: (0, 0)),
            vec(RWKV_W), vec(RWKV_W), vec(RWKV_W),
        ],
        out_specs=[hm_spec] * 6 + [rm_spec] * 2,
        out_shape=[hm] * 6 + [rm] * 2,
        scratch_shapes=[pltpu.VMEM((tm + SUBLANES, RWKV_COLS), F32)],
        compiler_params=_params("arbitrary", "arbitrary"), name="rwkv_prep",
    )(z3, z3, first_prev, lw["mu"], lw["w0"], lw["w2p"], lw["a0"], lw["a2p"], lw["g2"], lw["kk"], lw["ka"], lw["rk"])


def _bmm(a, b):
    return jnp.einsum("hqk,hkd->hqd", a.astype(BF16), b.astype(BF16), preferred_element_type=F32)


def _bmm_nt(a, b):
    return jnp.einsum("hqd,hkd->hqk", a.astype(BF16), b.astype(BF16), preferred_element_type=F32)


def _bmm_tn(a, b):
    return jnp.einsum("hkq,hkd->hqd", a.astype(BF16), b.astype(BF16), preferred_element_type=F32)


def _rwkv_scan_kernel(r_ref, lw_ref, k_ref, v_ref, kn_ref, b_ref, s0_ref, y_ref, st_ref, s_scr, *, chunk):
    c = pl.program_id(1)
    nh = RWKV_HEADS

    @pl.when(c == 0)
    def _():
        s_scr[...] = s0_ref[0]

    r, logw, k, v, kn, b = r_ref[0], lw_ref[0], k_ref[0], v_ref[0], kn_ref[0], b_ref[0]
    row = lax.broadcasted_iota(I32, (chunk, chunk), 0)
    col = lax.broadcasted_iota(I32, (chunk, chunk), 1)
    incl = row >= col
    strict = row > col
    tri = jnp.broadcast_to(incl.astype(BF16)[None], (nh, chunk, chunk))
    lw_hi = logw.astype(BF16)
    lw_lo = (logw - lw_hi.astype(F32)).astype(BF16)
    cum = (jnp.einsum("hqk,hkd->hqd", tri, lw_hi, preferred_element_type=F32)
           + jnp.einsum("hqk,hkd->hqd", tri, lw_lo, preferred_element_type=F32))
    p_inv = jnp.exp(-cum)
    kt = k * p_inv
    bt = b * p_inv
    kap = kn * jnp.exp(cum - logw)
    rho = r * jnp.exp(cum)
    qq = jnp.concatenate([kap, rho], axis=1)
    gram = _bmm_nt(qq, jnp.concatenate([kt, bt], axis=1))
    a_k = jnp.where(strict[None], gram[:, :chunk, :chunk], 0.0)
    a_b = jnp.where(strict[None], gram[:, :chunk, chunk:], 0.0)
    l_k = jnp.where(incl[None], gram[:, chunk:, :chunk], 0.0)
    l_b = jnp.where(incl[None], gram[:, chunk:, chunk:], 0.0)
    x = jnp.broadcast_to((row == col).astype(F32)[None], (nh, chunk, chunk))
    m = 1
    while m < chunk:
        sh = m.bit_length() - 1
        same = (row >> (sh + 1)) == (col >> (sh + 1))
        lower_left = jnp.logical_and(((row >> sh) & 1) == 1, ((col >> sh) & 1) == 0)
        off = jnp.where(jnp.logical_and(same, lower_left)[None], a_b, 0.0)
        x = x - _bmm(_bmm(x, off), x)
        m *= 2
    s = s_scr[...]
    qs = _bmm_nt(qq, s)
    u = _bmm(x, -(qs[:, :chunk] + _bmm(a_k, v)))
    y = qs[:, chunk:] + _bmm(jnp.concatenate([l_k, l_b], axis=2), jnp.concatenate([v, u], axis=1))
    s_new = (s + _bmm_tn(jnp.concatenate([v, u], axis=1), jnp.concatenate([kt, bt], axis=1))) * jnp.exp(cum[:, chunk - 1:chunk, :])
    s_scr[...] = s_new
    st_ref[0] = s_new
    mean = jnp.mean(y, axis=-1, keepdims=True)
    var = jnp.mean(jnp.square(y - mean), axis=-1, keepdims=True)
    yn = (y - mean) * lax.rsqrt(var + GN_EPS)
    y_ref[0] = jnp.concatenate([yn[h] for h in range(nh)], axis=1)


def _rwkv_scan(prep, s0, *, chunk):
    r, lw, k, v, kn, b = prep
    n_seq, nh, t_len, hd = r.shape
    hm_spec = pl.BlockSpec((1, nh, chunk, hd), lambda s, c: (s, 0, c, 0))
    st_spec = pl.BlockSpec((1, nh, hd, hd), lambda s, c: (s, 0, 0, 0))
    return pl.pallas_call(
        functools.partial(_rwkv_scan_kernel, chunk=chunk),
        grid=(n_seq, t_len // chunk),
        in_specs=[hm_spec] * 6 + [st_spec],
        out_specs=[pl.BlockSpec((1, chunk, RWKV_W), lambda s, c: (s, c, 0)), st_spec],
        out_shape=[jax.ShapeDtypeStruct((n_seq, t_len, RWKV_W), F32), jax.ShapeDtypeStruct((n_seq, nh, hd, hd), F32)],
        scratch_shapes=[pltpu.VMEM((nh, hd, hd), F32)],
        compiler_params=_params("arbitrary", "arbitrary"), name="rwkv_scan",
    )(r, lw, k, v, kn, b, s0)


def _mix_kernel(att_ref, pool_ref, yn_ref, bonus_ref, g_ref, gate_ref, lnw_ref, lnb_ref, wa_ref, wp_ref, wr_ref, o_ref):
    rw = (yn_ref[...] * lnw_ref[...] + lnb_ref[...] + bonus_ref[...]) * g_ref[...]
    mix = jax.nn.sigmoid(gate_ref[:, 0:D_MODEL]) * _dot(att_ref[...], wa_ref[...])
    mix = mix + jax.nn.sigmoid(gate_ref[:, D_MODEL:2 * D_MODEL]) * _dot(pool_ref[...], wp_ref[...])
    mix = mix + jax.nn.sigmoid(gate_ref[:, 2 * D_MODEL:3 * D_MODEL]) * _dot(rw, wr_ref[...])
    o_ref[...] = mix.astype(BF16)


def _mix(att, pool, yn, bonus, g, z_gate, ln_w, ln_b, wa, wp, wr, *, tm):
    rows = att.shape[0]

    def rowspec(n):
        return pl.BlockSpec((tm, n), lambda i: (i, 0))

    def full(a):
        return pl.BlockSpec(a.shape, lambda i: (0, 0))

    ln_w = ln_w.reshape(1, RWKV_W)
    ln_b = ln_b.reshape(1, RWKV_W)
    return pl.pallas_call(
        _mix_kernel, grid=(rows // tm,),
        in_specs=[rowspec(ATT_OUT), rowspec(POOL_W), rowspec(RWKV_W), rowspec(RWKV_W), rowspec(RWKV_W),
                  rowspec(3 * D_MODEL), full(ln_w), full(ln_b), full(wa), full(wp), full(wr)],
        out_specs=rowspec(D_MODEL),
        out_shape=jax.ShapeDtypeStruct((rows, D_MODEL), BF16),
        compiler_params=_params("arbitrary"), name="gated_mix",
    )(att, pool, yn, bonus, g, z_gate, ln_w, ln_b, wa, wp, wr)


def _mm_res_kernel(x_ref, a_ref, w_ref, o_ref):
    o_ref[...] = x_ref[...] + jnp.dot(a_ref[...], w_ref[...], preferred_element_type=F32)


def _mm_residual(x, a, w, *, tm):
    rows, n = x.shape
    k = a.shape[1]
    return pl.pallas_call(
        _mm_res_kernel, grid=(rows // tm,),
        in_specs=[pl.BlockSpec((tm, n), lambda i: (i, 0)), pl.BlockSpec((tm, k), lambda i: (i, 0)),
                  pl.BlockSpec((k, n), lambda i: (0, 0))],
        out_specs=pl.BlockSpec((tm, n), lambda i: (i, 0)),
        out_shape=jax.ShapeDtypeStruct((rows, n), F32),
        compiler_params=_params("arbitrary"), name="out_proj",
    )(x, a, w)


def _topk_cols(s, k):
    n, lanes = s.shape
    iota_n = lax.broadcasted_iota(I32, (n, lanes), 0)
    iota_k = lax.broadcasted_iota(I32, (k, lanes), 0)

    def body(j, carry):
        s, vals, idxs = carry
        m = jnp.max(s, axis=0, keepdims=True)
        idx = jnp.min(jnp.where(s == m, iota_n, n), axis=0, keepdims=True)
        vals = jnp.where(iota_k == j, m, vals)
        idxs = jnp.where(iota_k == j, idx, idxs)
        s = jnp.where(iota_n == idx, NEG_INF, s)
        return s, vals, idxs

    _, vals, idxs = lax.fori_loop(0, k, body, (s, jnp.zeros((k, lanes), F32), jnp.zeros((k, lanes), I32)))
    return vals, idxs


def _gather_rows(table, sel, k):
    out = jnp.zeros(sel.shape, table.dtype)
    for a in range(k):
        out = jnp.where(sel == a, table[a:a + 1, :], out)
    return out


def _peer_select_kernel(q_ref, skh_ref, skl_ref, i1_o, i2_o, gate_o, i1_s, i2_s, g_s, *, tok):
    kk = PEER_TOPK
    nt = (((1,), (1,)), ((), ()))

    def head(h, carry):
        c0 = pl.multiple_of(h * 2 * N_KEYS, 2 * N_KEYS)
        q1h, q1l = _split_bf16(q_ref[:, pl.ds(c0, N_KEYS)])
        q2h, q2l = _split_bf16(q_ref[:, pl.ds(c0 + N_KEYS, N_KEYS)])
        s1 = _dot3(skh_ref[h, 0], skl_ref[h, 0], q1h, q1l, nt)
        s2 = _dot3(skh_ref[h, 1], skl_ref[h, 1], q2h, q2l, nt)
        t1, k1 = _topk_cols(s1, kk)
        t2, k2 = _topk_cols(s2, kk)
        cand = jnp.concatenate([t1[a:a + 1, :] + t2 for a in range(kk)], axis=0)
        top, sel = _topk_cols(cand, kk)
        e1 = _gather_rows(k1, sel >> 4, kk)
        e2 = _gather_rows(k2, sel & (kk - 1), kk)
        ex = jnp.exp(top - jnp.max(top, axis=0, keepdims=True))
        gate = ex / jnp.sum(ex, axis=0, keepdims=True)
        r0 = pl.multiple_of(h * kk, kk)
        i1_s[pl.ds(r0, kk), :] = e1.astype(F32)
        i2_s[pl.ds(r0, kk), :] = e2.astype(F32)
        g_s[pl.ds(r0, kk), :] = gate
        return carry

    lax.fori_loop(0, PEER_HEADS, head, 0)
    i1_o[...] = i1_s[...].T.astype(I32)
    i2_o[...] = i2_s[...].T.astype(I32)
    gate_o[...] = g_s[...].T


def _peer_select(q, sk_hi, sk_lo, *, tok=128):
    rows = q.shape[0]
    spec = pl.BlockSpec((tok, PEER_PAIRS), lambda i: (i, 0))
    return pl.pallas_call(
        functools.partial(_peer_select_kernel, tok=tok), grid=(rows // tok,),
        in_specs=[pl.BlockSpec((tok, q.shape[1]), lambda i: (i, 0)),
                  pl.BlockSpec(sk_hi.shape, lambda i: (0, 0, 0, 0)), pl.BlockSpec(sk_lo.shape, lambda i: (0, 0, 0, 0))],
        out_specs=[spec] * 3,
        out_shape=[jax.ShapeDtypeStruct((rows, PEER_PAIRS), I32), jax.ShapeDtypeStruct((rows, PEER_PAIRS), I32),
                   jax.ShapeDtypeStruct((rows, PEER_PAIRS), F32)],
        scratch_shapes=[pltpu.VMEM((PEER_PAIRS, tok), F32)] * 3,
        compiler_params=_params("arbitrary"), name="peer_select",
    )(q, sk_hi, sk_lo)


def _block_diag_lanes(stack):
    tau = lax.broadcasted_iota(I32, stack.shape, 0) & (SUBLANES - 1)
    zero = jnp.zeros((), stack.dtype)
    return jnp.concatenate([jnp.where(tau == t, stack, zero).astype(BF16) for t in range(SUBLANES)], axis=1)


def _peer_act_kernel(xn_ref, u_ref, i1_ref, i2_ref, act_ref, d_ref):
    c = pl.program_id(1)
    col = pl.multiple_of(c * PEER_EC, PEER_EC)
    d_ref[:, pl.ds(col, PEER_EC)] = lax.dot_general(xn_ref[...], u_ref[...], (((1,), (1,)), ((), ())),
                                                    preferred_element_type=F32)

    @pl.when(c == pl.num_programs(1) - 1)
    def _():
        key_iota = lax.broadcasted_iota(I32, (N_KEYS, PEER_PAIRS), 0)

        def group(g, carry):
            r0 = pl.multiple_of(g * SUBLANES, SUBLANES)
            i1 = i1_ref[pl.ds(r0, SUBLANES), :]
            i2 = i2_ref[pl.ds(r0, SUBLANES), :]
            sel = jnp.concatenate(
                [(key_iota == jnp.broadcast_to(i2[t:t + 1, :], key_iota.shape)).astype(BF16) for t in range(SUBLANES)], axis=0)
            dstack = jnp.concatenate(
                [d_ref[pl.ds(r0, SUBLANES), k * N_KEYS:(k + 1) * N_KEYS] for k in range(N_KEYS)], axis=0)
            picked = jnp.dot(_block_diag_lanes(dstack), sel, preferred_element_type=F32)
            acc = jnp.zeros((SUBLANES, PEER_PAIRS), F32)
            for k in range(N_KEYS):
                acc = acc + jnp.where(i1 == k, picked[k * SUBLANES:(k + 1) * SUBLANES, :], 0.0)
            act_ref[pl.ds(r0, SUBLANES), :] = acc
            return carry

        lax.fori_loop(0, act_ref.shape[0] // SUBLANES, group, 0)


def _peer_act(xn, u_tab, i1, i2):
    rows = xn.shape[0]
    tb = PEER_TB
    pair_spec = pl.BlockSpec((tb, PEER_PAIRS), lambda i, c: (i, 0))
    return pl.pallas_call(
        _peer_act_kernel, grid=(rows // tb, N_EXPERTS // PEER_EC),
        in_specs=[pl.BlockSpec((tb, D_MODEL), lambda i, c: (i, 0)),
                  pl.BlockSpec((PEER_EC, D_MODEL), lambda i, c: (c, 0)), pair_spec, pair_spec],
        out_specs=pair_spec,
        out_shape=jax.ShapeDtypeStruct((rows, PEER_PAIRS), F32),
        scratch_shapes=[pltpu.VMEM((tb, N_EXPERTS), F32)],
        compiler_params=_params("arbitrary", "arbitrary"), name="peer_act",
    )(xn, u_tab, i1, i2)


def _peer_out_kernel(i1_ref, i2_ref, gate_ref, act_ref, v_ref, o_ref, w_ref):
    c = pl.program_id(1)

    @pl.when(c == 0)
    def _():
        key_iota = lax.broadcasted_iota(I32, (N_KEYS, PEER_PAIRS), 0)
        chunk_of_row = lax.broadcasted_iota(I32, (N_KEYS * SUBLANES, PEER_PAIRS), 0) >> 3

        def group(g, carry):
            r0 = pl.multiple_of(g * SUBLANES, SUBLANES)
            i1 = i1_ref[pl.ds(r0, SUBLANES), :]
            i2 = i2_ref[pl.ds(r0, SUBLANES), :]
            a = act_ref[pl.ds(r0, SUBLANES), :]
            wgt = gate_ref[pl.ds(r0, SUBLANES), :] * (0.5 * a * (1.0 + lax.erf(a * (1.0 / math.sqrt(2.0)))))
            hit = (jnp.concatenate([i1] * N_KEYS, axis=0) == chunk_of_row).astype(F32)
            lhs = _block_diag_lanes(hit)
            rhs_t = jnp.concatenate(
                [jnp.where(key_iota == jnp.broadcast_to(i2[t:t + 1, :], key_iota.shape),
                           jnp.broadcast_to(wgt[t:t + 1, :], key_iota.shape), 0.0).astype(BF16) for t in range(SUBLANES)], axis=1)
            dense = lax.dot_general(lhs, rhs_t, (((1,), (1,)), ((), ())), preferred_element_type=F32)
            for k in range(N_KEYS):
                w_ref[pl.ds(r0, SUBLANES), k * N_KEYS:(k + 1) * N_KEYS] = dense[k * SUBLANES:(k + 1) * SUBLANES, :]
            return carry

        lax.fori_loop(0, w_ref.shape[0] // SUBLANES, group, 0)

    col = pl.multiple_of(c * PEER_EC, PEER_EC)
    part = jnp.dot(w_ref[:, pl.ds(col, PEER_EC)].astype(BF16), v_ref[...], preferred_element_type=F32)

    @pl.when(c == 0)
    def _():
        o_ref[...] = part

    @pl.when(c > 0)
    def _():
        o_ref[...] += part


def _peer_out(i1, i2, gate, act, v_tab):
    rows = i1.shape[0]
    tb = PEER_TB
    pair_spec = pl.BlockSpec((tb, PEER_PAIRS), lambda i, c: (i, 0))
    return pl.pallas_call(
        _peer_out_kernel, grid=(rows // tb, N_EXPERTS // PEER_EC),
        in_specs=[pair_spec] * 4 + [pl.BlockSpec((PEER_EC, D_MODEL), lambda i, c: (c, 0))],
        out_specs=pl.BlockSpec((tb, D_MODEL), lambda i, c: (i, 0)),
        out_shape=jax.ShapeDtypeStruct((rows, D_MODEL), F32),
        scratch_shapes=[pltpu.VMEM((tb, N_EXPERTS), F32)],
        compiler_params=_params("arbitrary", "arbitrary"), name="peer_out",
    )(i1, i2, gate, act, v_tab)


def _ple_kernel(x_ref, f_ref, p_ref, g_ref, wg_ref, wp_ref, gf_ref, o_ref, *, final):
    x = x_ref[...] + f_ref[...]
    ms = jnp.mean(x * x, axis=-1, keepdims=True)
    h = (x * lax.rsqrt(ms + RMS_EPS)) * g_ref[...]
    out = x + jax.nn.sigmoid(_dot(h, wg_ref[...])) * _dot(p_ref[...], wp_ref[...])
    if final:
        ms2 = jnp.mean(out * out, axis=-1, keepdims=True)
        out = (out * lax.rsqrt(ms2 + RMS_EPS)) * gf_ref[...]
    o_ref[...] = out


def _ple(x, ffn, p, g, wg, wp, g_final, *, tm, final):
    rows, n = x.shape
    row = pl.BlockSpec((tm, n), lambda i: (i, 0))
    vec = pl.BlockSpec((1, n), lambda i: (0, 0))
    return pl.pallas_call(
        functools.partial(_ple_kernel, final=final), grid=(rows // tm,),
        in_specs=[row, row, pl.BlockSpec((tm, p.shape[1]), lambda i: (i, 0)), vec,
                  pl.BlockSpec(wg.shape, lambda i: (0, 0)), pl.BlockSpec(wp.shape, lambda i: (0, 0)), vec],
        out_specs=row,
        out_shape=jax.ShapeDtypeStruct((rows, n), F32),
        compiler_params=_params("arbitrary"), name="ple",
    )(x, ffn, p, g.reshape(1, n), wg, wp, g_final.reshape(1, n))


def _layer(x, p_rows, lw, state, g_final, dims, final):
    n_p, t_p, n_s, t_s, past = dims
    rows_p = n_p * t_p
    rows_s = n_s * t_s

    z_qkv = _norm_mm(x, lw["g_mix"], lw["w_qkv"], tm=768, tn=768)
    z_pool = _norm_mm(x, lw["g_mix"], lw["w_pool"], tm=768, tn=POOL_W)
    z_rwkv = _norm_mm(x, lw["g_mix"], lw["w_rwkv"], tm=768, tn=RWKV_COLS // 2)
    z_gate = _norm_mm(x, lw["g_mix"], lw["w_gate"], tm=768, tn=768)

    qr, kr = _rotary(z_qkv, lw["cos"], lw["sin"], tm=256, prompt_blocks=rows_p // 256, pos_blocks=t_p // 256)
    os_, ls_ = [], []
    for gi, (_, dil) in enumerate(ATT_GROUPS):
        o, l = _band_attn(qr, kr, z_qkv, gi, dil, n_seq=n_p, seq=t_p)
        os_.append(o)
        ls_.append(l)
    att_p = _merge_attn(os_, ls_, tm=512)
    caches = [c.reshape(n_s, c.shape[1], 2 * ATT_OUT) for c in state["kv"]]
    att_s = _sample_attn(qr, kr, z_qkv, caches, row0=rows_p, n_seq=n_s, t_new=t_s)
    att = jnp.concatenate([att_p, att_s], axis=0)

    pool_p = _pool_mixer(z_pool, jnp.zeros((n_p, POOL_HIST, POOL_W), F32), lw["pool_w"], lw["pool_scale"],
                         row0=0, n_seq=n_p, t_len=t_p, pos0=0)
    hist_s = jnp.concatenate([jnp.zeros((n_s, 1, POOL_W), F32), state["pool"]], axis=1)
    pool_s = _pool_mixer(z_pool, hist_s, lw["pool_w"], lw["pool_scale"], row0=rows_p, n_seq=n_s, t_len=t_s, pos0=past)
    pool = jnp.concatenate([pool_p, pool_s], axis=0)

    zr_p = z_rwkv[:rows_p].reshape(n_p, t_p, RWKV_COLS)
    zr_s = z_rwkv[rows_p:].reshape(n_s, t_s, RWKV_COLS)
    prep_p = _rwkv_prep(zr_p, jnp.zeros((n_p, 1, RWKV_COLS), F32), lw, tm=256)
    prep_s = _rwkv_prep(zr_s, state["shift"][:, None, :], lw, tm=t_s)
    yn_p, wkv_p = _rwkv_scan(prep_p[:6], jnp.zeros((n_p, RWKV_HEADS, RWKV_HEAD, RWKV_HEAD), F32), chunk=RWKV_CHUNK)
    yn_s, wkv_s = _rwkv_scan(prep_s[:6], state["wkv"], chunk=t_s)
    yn = jnp.concatenate([yn_p.reshape(rows_p, RWKV_W), yn_s.reshape(rows_s, RWKV_W)], axis=0)
    gg = jnp.concatenate([prep_p[6].reshape(rows_p, RWKV_W), prep_s[6].reshape(rows_s, RWKV_W)], axis=0)
    bonus = jnp.concatenate([prep_p[7].reshape(rows_p, RWKV_W), prep_s[7].reshape(rows_s, RWKV_W)], axis=0)

    mix = _mix(att, pool, yn, bonus, gg, z_gate, lw["ln_w"], lw["ln_b"], lw["w_attn_o"], lw["w_pool_o"], lw["w_rwkv_o"], tm=384)
    x = _mm_residual(x, mix, lw["w_out"], tm=384)

    q, xn = _norm_mm3(x, lw["g_ffn"], *lw["peer_wq"], tm=768, tn=512)
    i1, i2, gate = _peer_select(q, *lw["peer_subkeys"])
    act = _peer_act(xn, lw["peer_u"], i1, i2)
    ffn = _peer_out(i1, i2, gate, act, lw["peer_v"])

    x = _ple(x, ffn, p_rows, lw["g_ple"], lw["ple_wg"], lw["ple_wp"], g_final, tm=256, final=final)

    def heads(a, n, t):
        return a.reshape(n, t, HEADS_PER_GROUP, HEAD_DIM)

    new_p, new_s = [], []
    for gi, (win, _) in enumerate(ATT_GROUPS):
        ks = slice(gi * ATT_OUT, (gi + 1) * ATT_OUT)
        vs = slice(2 * ATT_W + gi * ATT_OUT, 2 * ATT_W + (gi + 1) * ATT_OUT)
        keep = min(win, t_p)
        k_p = kr[:rows_p, ks].reshape(n_p, t_p, ATT_OUT)[:, t_p - keep:]
        v_p = z_qkv[:rows_p, vs].reshape(n_p, t_p, ATT_OUT)[:, t_p - keep:]
        new_p.append(jnp.stack([heads(k_p, n_p, keep), heads(v_p, n_p, keep)], axis=2))
        k_s = heads(kr[rows_p:, ks], n_s, t_s)
        v_s = heads(z_qkv[rows_p:, vs], n_s, t_s)
        new_s.append(jnp.concatenate([state["kv"][gi][:, t_s:], jnp.stack([k_s, v_s], axis=2)], axis=1))
    zp_p = z_pool[:rows_p].reshape(n_p, t_p, POOL_W)
    zp_s = z_pool[rows_p:].reshape(n_s, t_s, POOL_W)
    new_p += [zp_p[:, t_p - (POOL_HIST - 1):], wkv_p, zr_p[:, -1]]
    new_s += [jnp.concatenate([hist_s[:, 1:], zp_s], axis=1)[:, -(POOL_HIST - 1):], wkv_s, zr_s[:, -1]]
    return x, new_p, new_s


def kernel(x_prompt, x_sample, p_prompt, p_sample, cache_attn_w128, cache_attn_w512, cache_attn_w2048, state_pool, state_rwkv_wkv, state_rwkv_shift, g_mix, w_in, w_attn_o, w_pool_o, w_rwkv_o, w_out, pool_w, pool_scale, rwkv_mu, rwkv_w0, rwkv_w2, rwkv_a0, rwkv_a2, rwkv_g2, rwkv_kk, rwkv_ka, rwkv_rk, rwkv_ln_w, rwkv_ln_b, g_ffn, peer_wq, peer_subkeys, peer_u, peer_v, g_ple, ple_wg, ple_wp, g_final):
    n_p, t_p, _ = x_prompt.shape
    n_s, t_s, _ = x_sample.shape
    depth = w_in.shape[0]
    past = PAST_LEN
    rows_p = n_p * t_p
    rows_s = n_s * t_s
    dims = (n_p, t_p, n_s, t_s, past)

    x = jnp.concatenate([x_prompt.reshape(rows_p, D_MODEL), x_sample.reshape(rows_s, D_MODEL)], axis=0)
    cos_p, sin_p = _rope_tables(jnp.arange(t_p, dtype=I32))
    cos_s, sin_s = _rope_tables(past + jnp.arange(t_s, dtype=I32))
    reps = 256 // t_s
    cos_t = jnp.concatenate([cos_p, jnp.tile(cos_s, (reps, 1))], axis=0)
    sin_t = jnp.concatenate([sin_p, jnp.tile(sin_s, (reps, 1))], axis=0)
    lora_pad = jnp.zeros((128 - 64, RWKV_W), F32)

    def row(a):
        return a.reshape(1, -1)

    new_p, new_s = [], []
    for l in range(depth):
        wl = w_in[l]
        lw = {
            "g_mix": g_mix[l],
            "w_qkv": wl[:, :OFF_POOL].astype(BF16), "w_pool": wl[:, OFF_POOL:OFF_RWKV].astype(BF16),
            "w_rwkv": wl[:, OFF_RWKV:OFF_GATE].astype(BF16), "w_gate": wl[:, OFF_GATE:].astype(BF16),
            "cos": cos_t, "sin": sin_t,
            "pool_w": pool_w[l].astype(BF16), "pool_scale": pool_scale[l],
            "mu": row(rwkv_mu[l]), "w0": row(rwkv_w0[l]), "a0": row(rwkv_a0[l]),
            "w2p": jnp.concatenate([rwkv_w2[l], lora_pad], axis=0).astype(BF16),
            "a2p": jnp.concatenate([lora_pad, rwkv_a2[l]], axis=0).astype(BF16),
            "g2": rwkv_g2[l].astype(BF16),
            "kk": row(rwkv_kk[l]), "ka": row(rwkv_ka[l]), "rk": row(rwkv_rk[l]),
            "ln_w": rwkv_ln_w[l], "ln_b": rwkv_ln_b[l],
            "w_attn_o": w_attn_o[l].astype(BF16), "w_pool_o": w_pool_o[l].astype(BF16),
            "w_rwkv_o": w_rwkv_o[l].astype(BF16), "w_out": w_out[l].astype(BF16),
            "g_ffn": g_ffn[l], "peer_wq": _split_bf16(peer_wq[l]), "peer_subkeys": _split_bf16(peer_subkeys[l]),
            "peer_u": peer_u[l].astype(BF16), "peer_v": peer_v[l].astype(BF16),
            "g_ple": g_ple[l], "ple_wg": ple_wg[l].astype(BF16), "ple_wp": ple_wp[l].astype(BF16),
        }
        state = {"kv": (cache_attn_w128[l], cache_attn_w512[l], cache_attn_w2048[l]), "pool": state_pool[l],
                 "wkv": state_rwkv_wkv[l], "shift": state_rwkv_shift[l]}
        p_rows = jnp.concatenate([p_prompt[l].reshape(rows_p, -1), p_sample[l].reshape(rows_s, -1)], axis=0).astype(BF16)
        x, st_p, st_s = _layer(x, p_rows, lw, state, g_final, dims, l == depth - 1)
        new_p.append(st_p)
        new_s.append(st_s)

    outs = [x[:rows_p].reshape(n_p, t_p, D_MODEL), x[rows_p:].reshape(n_s, t_s, D_MODEL)]
    for j in range(6):
        outs.append(jnp.stack([s[j] for s in new_p]))
        outs.append(jnp.stack([s[j] for s in new_s]))
    return tuple(outs)
```

```python
import functools
import math

import jax
import jax.numpy as jnp
from jax import lax
from jax.experimental import pallas as pl
from jax.experimental.pallas import tpu as pltpu

F32 = jnp.float32
BF16 = jnp.bfloat16
I32 = jnp.int32

D_MODEL = 2048
RMS_EPS = 1e-6
HEAD_DIM = 64
ATT_GROUPS = ((128, 1), (512, 4), (2048, 16))
HEADS_PER_GROUP = 4
ATT_W = 768
ATT_OUT = 256
ATT_SPAN = 128
ROPE_THETA = 10000.0
POOL_WINDOWS = (2, 4, 8, 16)
POOL_GROUP = 128
POOL_W = 512
POOL_HIST = 16
RWKV_HEAD = 64
RWKV_HEADS = 12
RWKV_W = 768
RWKV_COLS = 2560
RWKV_LORA_OFF = 2304
GN_EPS = 64e-5
RWKV_CHUNK = 64
OFF_POOL = 2304
OFF_RWKV = 2816
OFF_GATE = 5376
PAST_LEN = 8192
PEER_HEADS = 8
N_KEYS = 128
N_EXPERTS = N_KEYS * N_KEYS
PEER_TOPK = 16
PEER_PAIRS = PEER_HEADS * PEER_TOPK
PEER_TB = 384
PEER_EC = 1024
SUBLANES = 8
VMEM_LIMIT = 56 * 1024 * 1024

NEG_INF = float("-inf")


def _params(*sem):
    return pltpu.CompilerParams(dimension_semantics=sem, vmem_limit_bytes=VMEM_LIMIT)


def _dot(a, b):
    return jnp.dot(a.astype(BF16), b.astype(BF16), preferred_element_type=F32)


def _dot_nt(a, b):
    return lax.dot_general(a.astype(BF16), b.astype(BF16), (((1,), (1,)), ((), ())), preferred_element_type=F32)


def _norm_mm_kernel(x_ref, g_ref, w_ref, o_ref, xn_ref):
    @pl.when(pl.program_id(1) == 0)
    def _():
        x = x_ref[...]
        ms = jnp.mean(x * x, axis=-1, keepdims=True)
        xn_ref[...] = ((x * lax.rsqrt(ms + RMS_EPS)) * g_ref[...]).astype(BF16)

    o_ref[...] = jnp.dot(xn_ref[...], w_ref[...], preferred_element_type=F32)


def _norm_mm(x, g, w, *, tm, tn):
    rows, k = x.shape
    n = w.shape[1]
    return pl.pallas_call(
        _norm_mm_kernel, grid=(rows // tm, n // tn),
        in_specs=[
            pl.BlockSpec((tm, k), lambda i, j: (i, 0)),
            pl.BlockSpec((1, k), lambda i, j: (0, 0)),
            pl.BlockSpec((k, tn), lambda i, j: (0, j)),
        ],
        out_specs=pl.BlockSpec((tm, tn), lambda i, j: (i, j)),
        out_shape=jax.ShapeDtypeStruct((rows, n), F32),
        scratch_shapes=[pltpu.VMEM((tm, k), BF16)],
        compiler_params=_params("arbitrary", "arbitrary"), name="norm_mm",
    )(x, g.reshape(1, k), w)


def _split_bf16(a):
    hi = a.astype(BF16)
    lo = (a - hi.astype(F32)).astype(BF16)
    return hi, lo


def _dot3(ah, al, bh, bl, dims=(((1,), (0,)), ((), ()))):
    def d(p, q):
        return lax.dot_general(p, q, dims, preferred_element_type=F32)

    return d(ah, bh) + (d(ah, bl) + d(al, bh))


def _norm_mm3_kernel(x_ref, g_ref, wh_ref, wl_ref, o_ref, xh_ref, xl_ref):
    @pl.when(pl.program_id(1) == 0)
    def _():
        x = x_ref[...]
        ms = jnp.mean(x * x, axis=-1, keepdims=True)
        xh, xl = _split_bf16((x * lax.rsqrt(ms + RMS_EPS)) * g_ref[...])
        xh_ref[...] = xh
        xl_ref[...] = xl

    o_ref[...] = _dot3(xh_ref[...], xl_ref[...], wh_ref[...], wl_ref[...])


def _norm_mm3(x, g, wh, wl, *, tm, tn):
    rows, k = x.shape
    n = wh.shape[1]
    wspec = pl.BlockSpec((k, tn), lambda i, j: (0, j))
    return pl.pallas_call(
        _norm_mm3_kernel, grid=(rows // tm, n // tn),
        in_specs=[pl.BlockSpec((tm, k), lambda i, j: (i, 0)), pl.BlockSpec((1, k), lambda i, j: (0, 0)), wspec, wspec],
        out_specs=[pl.BlockSpec((tm, tn), lambda i, j: (i, j)), pl.BlockSpec((tm, k), lambda i, j: (i, 0))],
        out_shape=[jax.ShapeDtypeStruct((rows, n), F32), jax.ShapeDtypeStruct((rows, k), BF16)],
        scratch_shapes=[pltpu.VMEM((tm, k), BF16)],
        compiler_params=_params("arbitrary", "arbitrary"), name="norm_mm3",
    )(x, g.reshape(1, k), wh, wl)


def _rotary_kernel(q_ref, k_ref, cos_ref, sin_ref, qo_ref, ko_ref):
    cos = cos_ref[...]
    sin = sin_ref[...]
    lane = lax.broadcasted_iota(I32, cos.shape, 1)
    first_half = (lane & (HEAD_DIM - 1)) < HEAD_DIM // 2
    for src, dst in ((q_ref, qo_ref), (k_ref, ko_ref)):
        for c in range(ATT_W // 128):
            x = src[:, c * 128:(c + 1) * 128]
            partner = jnp.where(first_half, pltpu.roll(x, 128 - HEAD_DIM // 2, 1), pltpu.roll(x, HEAD_DIM // 2, 1))
            dst[:, c * 128:(c + 1) * 128] = x * cos + partner * sin


def _rotary(z_qkv, cos_t, sin_t, *, tm, prompt_blocks, pos_blocks):
    rows = z_qkv.shape[0]

    def tab(i):
        return (jnp.where(i < prompt_blocks, i % pos_blocks, pos_blocks), 0)

    return pl.pallas_call(
        _rotary_kernel, grid=(rows // tm,),
        in_specs=[pl.BlockSpec((tm, ATT_W), lambda i: (i, 0)), pl.BlockSpec((tm, ATT_W), lambda i: (i, 1)),
                  pl.BlockSpec((tm, 128), tab), pl.BlockSpec((tm, 128), tab)],
        out_specs=[pl.BlockSpec((tm, ATT_W), lambda i: (i, 0))] * 2,
        out_shape=[jax.ShapeDtypeStruct((rows, ATT_W), F32)] * 2,
        compiler_params=_params("arbitrary"), name="rotary",
    )(z_qkv, z_qkv, cos_t, sin_t)


def _rope_tables(pos):
    half = HEAD_DIM // 2
    inv = ROPE_THETA ** (-jnp.arange(half, dtype=F32) / half)
    ang = pos.astype(F32)[:, None] * inv[None, :]
    cos = jnp.cos(ang)
    sin = jnp.sin(ang)
    cos_t = jnp.concatenate([cos, cos, cos, cos], axis=1)
    sin_t = jnp.concatenate([-sin, sin, -sin, sin], axis=1)
    return cos_t, sin_t


def _band_attn_kernel(q_ref, kc_ref, kp_ref, vc_ref, vp_ref, o_ref, l_ref, *, has_prev):
    b = pl.program_id(2)
    qi = lax.broadcasted_iota(I32, (ATT_SPAN, ATT_SPAN), 0)
    ki = lax.broadcasted_iota(I32, (ATT_SPAN, ATT_SPAN), 1)
    cur_ok = ki <= qi
    prev_ok = jnp.logical_and(ki >= qi, b > 0)
    scale = HEAD_DIM ** -0.5
    outs, lses = [], []
    for h in range(HEADS_PER_GROUP):
        sl = slice(h * HEAD_DIM, (h + 1) * HEAD_DIM)
        q = q_ref[:, sl]
        sc = jnp.where(cur_ok, _dot_nt(q, kc_ref[:, sl]) * scale, NEG_INF)
        m = jnp.max(sc, axis=-1, keepdims=True)
        if has_prev:
            sp = jnp.where(prev_ok, _dot_nt(q, kp_ref[:, sl]) * scale, NEG_INF)
            m = jnp.maximum(m, jnp.max(sp, axis=-1, keepdims=True))
        ec = jnp.exp(sc - m)
        den = jnp.sum(ec, axis=-1, keepdims=True)
        acc = _dot(ec, vc_ref[:, sl])
        if has_prev:
            ep = jnp.exp(sp - m)
            den = den + jnp.sum(ep, axis=-1, keepdims=True)
            acc = acc + _dot(ep, vp_ref[:, sl])
        outs.append(acc / den)
        lses.append(jnp.broadcast_to(m + jnp.log(den), (ATT_SPAN, HEAD_DIM)))
    o_ref[...] = jnp.concatenate(outs, axis=1)
    l_ref[...] = jnp.concatenate(lses, axis=1)


def _band_attn(qr, kr, z_qkv, gi, dil, *, n_seq, seq):
    rows_p = n_seq * seq
    n_cls = seq // dil
    nb = -(-n_cls // ATT_SPAN)
    assert n_cls % ATT_SPAN == 0
    qv = qr.reshape(qr.shape[0] // dil, dil * ATT_W)
    kv = kr.reshape(kr.shape[0] // dil, dil * ATT_W)
    zv = z_qkv.reshape(z_qkv.shape[0] // dil, dil * 3 * ATT_W)
    cq = ATT_W // ATT_OUT
    cz = 3 * ATT_W // ATT_OUT
    voff = 2 * ATT_W // ATT_OUT + gi

    def cur(n, r, b):
        return n * nb + b

    def prev(n, r, b):
        return n * nb + jnp.maximum(b - 1, 0)

    blk = (ATT_SPAN, ATT_OUT)
    o, l = pl.pallas_call(
        functools.partial(_band_attn_kernel, has_prev=nb > 1),
        grid=(n_seq, dil, nb),
        in_specs=[
            pl.BlockSpec(blk, lambda n, r, b: (cur(n, r, b), r * cq + gi)),
            pl.BlockSpec(blk, lambda n, r, b: (cur(n, r, b), r * cq + gi)),
            pl.BlockSpec(blk, lambda n, r, b: (prev(n, r, b), r * cq + gi)),
            pl.BlockSpec(blk, lambda n, r, b: (cur(n, r, b), r * cz + voff)),
            pl.BlockSpec(blk, lambda n, r, b: (prev(n, r, b), r * cz + voff)),
        ],
        out_specs=[pl.BlockSpec(blk, lambda n, r, b: (cur(n, r, b), r))] * 2,
        out_shape=[jax.ShapeDtypeStruct((rows_p // dil, dil * ATT_OUT), F32)] * 2,
        compiler_params=_params("arbitrary", "arbitrary", "arbitrary"), name=f"band_attn_g{gi}",
    )(qv, kv, kv, zv, zv)
    return o.reshape(rows_p, ATT_OUT), l.reshape(rows_p, ATT_OUT)


def _merge_groups(os_, ls_):
    m = jnp.maximum(jnp.maximum(ls_[0], ls_[1]), ls_[2])
    es = [jnp.exp(l - m) for l in ls_]
    tot = es[0] + es[1] + es[2]
    return (es[0] / tot) * os_[0] + (es[1] / tot) * os_[1] + (es[2] / tot) * os_[2]


def _merge_kernel(o0, o1, o2, l0, l1, l2, a_ref):
    a_ref[...] = _merge_groups([o0[...], o1[...], o2[...]], [l0[...], l1[...], l2[...]])


def _merge_attn(os_, ls_, *, tm):
    rows = os_[0].shape[0]
    spec = pl.BlockSpec((tm, ATT_OUT), lambda i: (i, 0))
    return pl.pallas_call(
        _merge_kernel, grid=(rows // tm,), in_specs=[spec] * 6, out_specs=spec,
        out_shape=jax.ShapeDtypeStruct((rows, ATT_OUT), F32),
        compiler_params=_params("arbitrary"), name="merge_attn",
    )(*os_, *ls_)


def _sample_attn_kernel(q_ref, kn_ref, vn_ref, c0_ref, c1_ref, c2_ref, a_ref, *, t_new):
    scale = HEAD_DIM ** -0.5
    outs_g, lses_g = [], []
    for gi, (c_ref, (win, dil)) in enumerate(zip((c0_ref, c1_ref, c2_ref), ATT_GROUPS)):
        cache_len = c_ref.shape[1]
        t_c = lax.broadcasted_iota(I32, (t_new, cache_len), 0)
        c_c = lax.broadcasted_iota(I32, (t_new, cache_len), 1)
        d_c = cache_len + t_c - c_c
        ok_c = jnp.logical_and((d_c & (dil - 1)) == 0, d_c <= ATT_SPAN * dil)
        t_n = lax.broadcasted_iota(I32, (t_new, t_new), 0)
        u_n = lax.broadcasted_iota(I32, (t_new, t_new), 1)
        d_n = t_n - u_n
        ok_n = jnp.logical_and(d_n >= 0, (d_n & (dil - 1)) == 0)
        outs, lses = [], []
        for h in range(HEADS_PER_GROUP):
            col = gi * ATT_OUT + h * HEAD_DIM
            q = q_ref[:, col:col + HEAD_DIM]
            kc = c_ref[0, :, h * HEAD_DIM:(h + 1) * HEAD_DIM]
            vc = c_ref[0, :, ATT_OUT + h * HEAD_DIM:ATT_OUT + (h + 1) * HEAD_DIM]
            s_c = jnp.where(ok_c, _dot_nt(q, kc) * scale, NEG_INF)
            s_n = jnp.where(ok_n, _dot_nt(q, kn_ref[:, col:col + HEAD_DIM]) * scale, NEG_INF)
            m = jnp.maximum(jnp.max(s_c, axis=-1, keepdims=True), jnp.max(s_n, axis=-1, keepdims=True))
            e_c = jnp.exp(s_c - m)
            e_n = jnp.exp(s_n - m)
            den = jnp.sum(e_c, axis=-1, keepdims=True) + jnp.sum(e_n, axis=-1, keepdims=True)
            acc = _dot(e_c, vc) + _dot(e_n, vn_ref[:, col:col + HEAD_DIM])
            outs.append(acc / den)
            lses.append(jnp.broadcast_to(m + jnp.log(den), (t_new, HEAD_DIM)))
        outs_g.append(jnp.concatenate(outs, axis=1))
        lses_g.append(jnp.concatenate(lses, axis=1))
    a_ref[...] = _merge_groups(outs_g, lses_g)


def _sample_attn(qr, kr, z_qkv, caches, *, row0, n_seq, t_new):
    b0 = row0 // t_new
    return pl.pallas_call(
        functools.partial(_sample_attn_kernel, t_new=t_new),
        grid=(n_seq,),
        in_specs=[
            pl.BlockSpec((t_new, ATT_W), lambda n: (b0 + n, 0)),
            pl.BlockSpec((t_new, ATT_W), lambda n: (b0 + n, 0)),
            pl.BlockSpec((t_new, ATT_W), lambda n: (b0 + n, 2)),
        ] + [pl.BlockSpec((1, c.shape[1], 2 * ATT_OUT), lambda n: (n, 0, 0)) for c in caches],
        out_specs=pl.BlockSpec((t_new, ATT_OUT), lambda n: (n, 0)),
        out_shape=jax.ShapeDtypeStruct((n_seq * t_new, ATT_OUT), F32),
        compiler_params=_params("arbitrary"), name="sample_attn",
    )(qr, kr, z_qkv, *caches)


def _pool_kernel(z_ref, h_ref, w_ref, s_ref, o_ref, buf_ref, *, t_len, pos0):
    buf_ref[0:POOL_HIST, :] = h_ref[0]
    buf_ref[POOL_HIST:POOL_HIST + t_len, :] = z_ref[...]
    pos = pos0 + lax.broadcasted_iota(I32, (t_len, POOL_GROUP), 0)
    for g, win in enumerate(POOL_WINDOWS):
        cs = slice(g * POOL_GROUP, (g + 1) * POOL_GROUP)
        z = buf_ref[POOL_HIST:POOL_HIST + t_len, cs]
        wsum = z
        for i in range(1, win):
            wsum = wsum + buf_ref[POOL_HIST - i:POOL_HIST - i + t_len, cs]
        cnt = jnp.minimum(win, pos + 1).astype(F32)
        y = wsum / cnt - z
        o_ref[:, cs] = _dot(y, w_ref[g]) * s_ref[:, cs]


def _pool_mixer(z_pool, hist, pool_w, pool_scale, *, row0, n_seq, t_len, pos0):
    b0 = row0 // t_len
    return pl.pallas_call(
        functools.partial(_pool_kernel, t_len=t_len, pos0=pos0),
        grid=(n_seq,),
        in_specs=[
            pl.BlockSpec((t_len, POOL_W), lambda n: (b0 + n, 0)),
            pl.BlockSpec((1, POOL_HIST, POOL_W), lambda n: (n, 0, 0)),
            pl.BlockSpec((len(POOL_WINDOWS), POOL_GROUP, POOL_GROUP), lambda n: (0, 0, 0)),
            pl.BlockSpec((1, POOL_W), lambda n: (0, 0)),
        ],
        out_specs=pl.BlockSpec((t_len, POOL_W), lambda n: (n, 0)),
        out_shape=jax.ShapeDtypeStruct((n_seq * t_len, POOL_W), F32),
        scratch_shapes=[pltpu.VMEM((POOL_HIST + t_len, POOL_W), F32)],
        compiler_params=_params("arbitrary"), name="pool_mixer",
    )(z_pool, hist, pool_w, pool_scale.reshape(1, POOL_W))


def _rwkv_prep_kernel(z_ref, zp_ref, first_ref, mu_ref, w0_ref, w2_ref, a0_ref, a2_ref, g2_ref, kk_ref, ka_ref, rk_ref,
                      r_o, lw_o, k_o, v_o, kn_o, b_o, g_o, bonus_o, buf_ref, *, tm):
    i = pl.program_id(1)
    z = z_ref[0]
    prev_row = jnp.where(i == 0, first_ref[0], zp_ref[0, SUBLANES - 1:SUBLANES, :])
    buf_ref[SUBLANES:SUBLANES + tm, :] = z
    buf_ref[SUBLANES - 1:SUBLANES, :] = prev_row
    shifted = buf_ref[SUBLANES - 1:SUBLANES - 1 + tm, :]
    xm = z + mu_ref[...] * (shifted - z)
    r = xm[:, 0:RWKV_W]
    k = xm[:, RWKV_W:2 * RWKV_W]
    v = xm[:, 2 * RWKV_W:3 * RWKV_W]
    wa = xm[:, RWKV_LORA_OFF:RWKV_LORA_OFF + 128]
    gl = xm[:, RWKV_LORA_OFF + 128:RWKV_COLS]
    xw = w0_ref[...] + _dot(jnp.tanh(wa), w2_ref[...])
    logw = -math.exp(-0.5) * jax.nn.sigmoid(xw)
    a = jax.nn.sigmoid(a0_ref[...] + _dot(wa, a2_ref[...]))
    g_o[0] = _dot(jax.nn.sigmoid(gl), g2_ref[...])
    kkr = k * kk_ref[...]
    kmod = k * (1.0 + (a - 1.0) * ka_ref[...])
    rkk = r * kmod * rk_ref[...]
    bonus = []
    for h in range(RWKV_HEADS):
        sl = slice(h * RWKV_HEAD, (h + 1) * RWKV_HEAD)
        kh = kkr[:, sl]
        nrm = jnp.sqrt(jnp.sum(kh * kh, axis=-1, keepdims=True))
        kn = kh / jnp.maximum(nrm, 1e-12)
        r_o[0, h] = r[:, sl]
        lw_o[0, h] = logw[:, sl]
        k_o[0, h] = kmod[:, sl]
        v_o[0, h] = v[:, sl]
        kn_o[0, h] = kn
        b_o[0, h] = kn * a[:, sl]
        bonus.append(jnp.sum(rkk[:, sl], axis=-1, keepdims=True) * v[:, sl])
    bonus_o[0] = jnp.concatenate(bonus, axis=1)


def _rwkv_prep(z3, first_prev, lw, *, tm):
    n_seq, t_len, _ = z3.shape
    nblk = t_len // tm
    pb = tm // SUBLANES

    def vec(n):
        return pl.BlockSpec((1, n), lambda s, i: (0, 0))

    hm = jax.ShapeDtypeStruct((n_seq, RWKV_HEADS, t_len, RWKV_HEAD), F32)
    rm = jax.ShapeDtypeStruct((n_seq, t_len, RWKV_W), F32)
    hm_spec = pl.BlockSpec((1, RWKV_HEADS, tm, RWKV_HEAD), lambda s, i: (s, 0, i, 0))
    rm_spec = pl.BlockSpec((1, tm, RWKV_W), lambda s, i: (s, i, 0))
    return pl.pallas_call(
        functools.partial(_rwkv_prep_kernel, tm=tm),
        grid=(n_seq, nblk),
        in_specs=[
            pl.BlockSpec((1, tm, RWKV_COLS), lambda s, i: (s, i, 0)),
            pl.BlockSpec((1, SUBLANES, RWKV_COLS), lambda s, i: (s, jnp.maximum(i * pb - 1, 0), 0)),
            pl.BlockSpec((1, 1, RWKV_COLS), lambda s, i: (s, 0, 0)),
            vec(RWKV_COLS), vec(RWKV_W),
            pl.BlockSpec((128, RWKV_W), lambda s, i: (0, 0)),
            vec(RWKV_W),
            pl.BlockSpec((128, RWKV_W), lambda s, i: (0, 0)),
            pl.BlockSpec((128, RWKV_W), lambda s, i: (0, 0)),
            vec(RWKV_W), vec(RWKV_W), vec(RWKV_W),
        ],
        out_specs=[hm_spec] * 6 + [rm_spec] * 2,
        out_shape=[hm] * 6 + [rm] * 2,
        scratch_shapes=[pltpu.VMEM((tm + SUBLANES, RWKV_COLS), F32)],
        compiler_params=_params("arbitrary", "arbitrary"), name="rwkv_prep",
    )(z3, z3, first_prev, lw["mu"], lw["w0"], lw["w2p"], lw["a0"], lw["a2p"], lw["g2"], lw["kk"], lw["ka"], lw["rk"])


def _bmm(a, b):
    return jnp.einsum("hqk,hkd->hqd", a.astype(BF16), b.astype(BF16), preferred_element_type=F32)


def _bmm_nt(a, b):
    return jnp.einsum("hqd,hkd->hqk", a.astype(BF16), b.astype(BF16), preferred_element_type=F32)


def _bmm_tn(a, b):
    return jnp.einsum("hkq,hkd->hqd", a.astype(BF16), b.astype(BF16), preferred_element_type=F32)


def _rwkv_scan_kernel(r_ref, lw_ref, k_ref, v_ref, kn_ref, b_ref, s0_ref, y_ref, st_ref, s_scr, *, chunk):
    c = pl.program_id(1)
    nh = RWKV_HEADS

    @pl.when(c == 0)
    def _():
        s_scr[...] = s0_ref[0]

    r, logw, k, v, kn, b = r_ref[0], lw_ref[0], k_ref[0], v_ref[0], kn_ref[0], b_ref[0]
    row = lax.broadcasted_iota(I32, (chunk, chunk), 0)
    col = lax.broadcasted_iota(I32, (chunk, chunk), 1)
    incl = row >= col
    strict = row > col
    tri = jnp.broadcast_to(incl.astype(BF16)[None], (nh, chunk, chunk))
    lw_hi = logw.astype(BF16)
    lw_lo = (logw - lw_hi.astype(F32)).astype(BF16)
    cum = (jnp.einsum("hqk,hkd->hqd", tri, lw_hi, preferred_element_type=F32)
           + jnp.einsum("hqk,hkd->hqd", tri, lw_lo, preferred_element_type=F32))
    p_inv = jnp.exp(-cum)
    kt = k * p_inv
    bt = b * p_inv
    kap = kn * jnp.exp(cum - logw)
    rho = r * jnp.exp(cum)
    qq = jnp.concatenate([kap, rho], axis=1)
    gram = _bmm_nt(qq, jnp.concatenate([kt, bt], axis=1))
    a_k = jnp.where(strict[None], gram[:, :chunk, :chunk], 0.0)
    a_b = jnp.where(strict[None], gram[:, :chunk, chunk:], 0.0)
    l_k = jnp.where(incl[None], gram[:, chunk:, :chunk], 0.0)
    l_b = jnp.where(incl[None], gram[:, chunk:, chunk:], 0.0)
    x = jnp.broadcast_to((row == col).astype(F32)[None], (nh, chunk, chunk))
    m = 1
    while m < chunk:
        sh = m.bit_length() - 1
        same = (row >> (sh + 1)) == (col >> (sh + 1))
        lower_left = jnp.logical_and(((row >> sh) & 1) == 1, ((col >> sh) & 1) == 0)
        off = jnp.where(jnp.logical_and(same, lower_left)[None], a_b, 0.0)
        x = x - _bmm(_bmm(x, off), x)
        m *= 2
    s = s_scr[...]
    qs = _bmm_nt(qq, s)
    u = _bmm(x, -(qs[:, :chunk] + _bmm(a_k, v)))
    y = qs[:, chunk:] + _bmm(jnp.concatenate([l_k, l_b], axis=2), jnp.concatenate([v, u], axis=1))
    s_new = (s + _bmm_tn(jnp.concatenate([v, u], axis=1), jnp.concatenate([kt, bt], axis=1))) * jnp.exp(cum[:, chunk - 1:chunk, :])
    s_scr[...] = s_new
    st_ref[0] = s_new
    mean = jnp.mean(y, axis=-1, keepdims=True)
    var = jnp.mean(jnp.square(y - mean), axis=-1, keepdims=True)
    yn = (y - mean) * lax.rsqrt(var + GN_EPS)
    y_ref[0] = jnp.concatenate([yn[h] for h in range(nh)], axis=1)


def _rwkv_scan(prep, s0, *, chunk):
    r, lw, k, v, kn, b = prep
    n_seq, nh, t_len, hd = r.shape
    hm_spec = pl.BlockSpec((1, nh, chunk, hd), lambda s, c: (s, 0, c, 0))
    st_spec = pl.BlockSpec((1, nh, hd, hd), lambda s, c: (s, 0, 0, 0))
    return pl.pallas_call(
        functools.partial(_rwkv_scan_kernel, chunk=chunk),
        grid=(n_seq, t_len // chunk),
        in_specs=[hm_spec] * 6 + [st_spec],
        out_specs=[pl.BlockSpec((1, chunk, RWKV_W), lambda s, c: (s, c, 0)), st_spec],
        out_shape=[jax.ShapeDtypeStruct((n_seq, t_len, RWKV_W), F32), jax.ShapeDtypeStruct((n_seq, nh, hd, hd), F32)],
        scratch_shapes=[pltpu.VMEM((nh, hd, hd), F32)],
        compiler_params=_params("arbitrary", "arbitrary"), name="rwkv_scan",
    )(r, lw, k, v, kn, b, s0)


def _mix_kernel(att_ref, pool_ref, yn_ref, bonus_ref, g_ref, gate_ref, lnw_ref, lnb_ref, wa_ref, wp_ref, wr_ref, o_ref):
    rw = (yn_ref[...] * lnw_ref[...] + lnb_ref[...] + bonus_ref[...]) * g_ref[...]
    mix = jax.nn.sigmoid(gate_ref[:, 0:D_MODEL]) * _dot(att_ref[...], wa_ref[...])
    mix = mix + jax.nn.sigmoid(gate_ref[:, D_MODEL:2 * D_MODEL]) * _dot(pool_ref[...], wp_ref[...])
    mix = mix + jax.nn.sigmoid(gate_ref[:, 2 * D_MODEL:3 * D_MODEL]) * _dot(rw, wr_ref[...])
    o_ref[...] = mix.astype(BF16)


def _mix(att, pool, yn, bonus, g, z_gate, ln_w, ln_b, wa, wp, wr, *, tm):
    rows = att.shape[0]

    def rowspec(n):
        return pl.BlockSpec((tm, n), lambda i: (i, 0))

    def full(a):
        return pl.BlockSpec(a.shape, lambda i: (0, 0))

    ln_w = ln_w.reshape(1, RWKV_W)
    ln_b = ln_b.reshape(1, RWKV_W)
    return pl.pallas_call(
        _mix_kernel, grid=(rows // tm,),
        in_specs=[rowspec(ATT_OUT), rowspec(POOL_W), rowspec(RWKV_W), rowspec(RWKV_W), rowspec(RWKV_W),
                  rowspec(3 * D_MODEL), full(ln_w), full(ln_b), full(wa), full(wp), full(wr)],
        out_specs=rowspec(D_MODEL),
        out_shape=jax.ShapeDtypeStruct((rows, D_MODEL), BF16),
        compiler_params=_params("arbitrary"), name="gated_mix",
    )(att, pool, yn, bonus, g, z_gate, ln_w, ln_b, wa, wp, wr)


def _mm_res_kernel(x_ref, a_ref, w_ref, o_ref):
    o_ref[...] = x_ref[...] + jnp.dot(a_ref[...], w_ref[...], preferred_element_type=F32)


def _mm_residual(x, a, w, *, tm):
    rows, n = x.shape
    k = a.shape[1]
    return pl.pallas_call(
        _mm_res_kernel, grid=(rows // tm,),
        in_specs=[pl.BlockSpec((tm, n), lambda i: (i, 0)), pl.BlockSpec((tm, k), lambda i: (i, 0)),
                  pl.BlockSpec((k, n), lambda i: (0, 0))],
        out_specs=pl.BlockSpec((tm, n), lambda i: (i, 0)),
        out_shape=jax.ShapeDtypeStruct((rows, n), F32),
        compiler_params=_params("arbitrary"), name="out_proj",
    )(x, a, w)


def _topk_cols(scores, ids, k):
    lanes = scores[0].shape[1]
    iota_k = lax.broadcasted_iota(I32, (k, lanes), 0)
    big = jnp.iinfo(jnp.int32).max

    def body(j, carry):
        out = []
        for (s, vals, idxs), ident in zip(carry, ids):
            m = jnp.max(s, axis=0, keepdims=True)
            idx = jnp.min(jnp.where(s == m, ident, big), axis=0, keepdims=True)
            vals = jnp.where(iota_k == j, m, vals)
            idxs = jnp.where(iota_k == j, idx, idxs)
            out.append((jnp.where(ident == idx, NEG_INF, s), vals, idxs))
        return tuple(out)

    init = tuple((s, jnp.zeros((k, lanes), F32), jnp.zeros((k, lanes), I32)) for s in scores)
    return [(v, i) for _, v, i in lax.fori_loop(0, k, body, init)]


def _gather_rows(table, sel, k):
    out = jnp.zeros(sel.shape, table.dtype)
    for a in range(k):
        out = jnp.where(sel == a, table[a:a + 1, :], out)
    return out


def _peer_select_kernel(q_ref, skh_ref, skl_ref, i1_o, i2_o, gate_o, i1_s, i2_s, g_s, *, tok):
    kk = PEER_TOPK
    half = kk // 2
    nt = (((1,), (1,)), ((), ()))
    key_id = lax.broadcasted_iota(I32, (N_KEYS, tok), 0)
    row = lax.broadcasted_iota(I32, (half * half + kk, tok), 0)
    cand_id = jnp.where(row < half * half, (row >> (half.bit_length() - 1)) * kk + (row & (half - 1)),
                        jnp.where(row < half * half + half, row - half * half + half, (row - half * half) * kk))

    def head(h, carry):
        c0 = pl.multiple_of(h * 2 * N_KEYS, 2 * N_KEYS)
        q1h, q1l = _split_bf16(q_ref[:, pl.ds(c0, N_KEYS)])
        q2h, q2l = _split_bf16(q_ref[:, pl.ds(c0 + N_KEYS, N_KEYS)])
        s1 = _dot3(skh_ref[h, 0], skl_ref[h, 0], q1h, q1l, nt)
        s2 = _dot3(skh_ref[h, 1], skl_ref[h, 1], q2h, q2l, nt)
        (t1, k1), (t2, k2) = _topk_cols([s1, s2], [key_id, key_id], kk)
        cand = jnp.concatenate([t1[a:a + 1, :] + t2[0:half, :] for a in range(half)]
                               + [t1[0:1, :] + t2[half:kk, :], t1[half:kk, :] + t2[0:1, :]], axis=0)
        ((top, sel),) = _topk_cols([cand], [cand_id], kk)
        e1 = _gather_rows(k1, sel >> 4, kk)
        e2 = _gather_rows(k2, sel & (kk - 1), kk)
        ex = jnp.exp(top - jnp.max(top, axis=0, keepdims=True))
        gate = ex / jnp.sum(ex, axis=0, keepdims=True)
        r0 = pl.multiple_of(h * kk, kk)
        i1_s[pl.ds(r0, kk), :] = e1.astype(F32)
        i2_s[pl.ds(r0, kk), :] = e2.astype(F32)
        g_s[pl.ds(r0, kk), :] = gate
        return carry

    lax.fori_loop(0, PEER_HEADS, head, 0)
    i1_o[...] = i1_s[...].T.astype(I32)
    i2_o[...] = i2_s[...].T.astype(I32)
    gate_o[...] = g_s[...].T


def _peer_select(q, sk_hi, sk_lo, *, tok=128):
    rows = q.shape[0]
    spec = pl.BlockSpec((tok, PEER_PAIRS), lambda i: (i, 0))
    return pl.pallas_call(
        functools.partial(_peer_select_kernel, tok=tok), grid=(rows // tok,),
        in_specs=[pl.BlockSpec((tok, q.shape[1]), lambda i: (i, 0)),
                  pl.BlockSpec(sk_hi.shape, lambda i: (0, 0, 0, 0)), pl.BlockSpec(sk_lo.shape, lambda i: (0, 0, 0, 0))],
        out_specs=[spec] * 3,
        out_shape=[jax.ShapeDtypeStruct((rows, PEER_PAIRS), I32), jax.ShapeDtypeStruct((rows, PEER_PAIRS), I32),
                   jax.ShapeDtypeStruct((rows, PEER_PAIRS), F32)],
        scratch_shapes=[pltpu.VMEM((PEER_PAIRS, tok), F32)] * 3,
        compiler_params=_params("arbitrary"), name="peer_select",
    )(q, sk_hi, sk_lo)


def _block_diag_lanes(stack):
    tau = lax.broadcasted_iota(I32, stack.shape, 0) & (SUBLANES - 1)
    zero = jnp.zeros((), stack.dtype)
    return jnp.concatenate([jnp.where(tau == t, stack, zero).astype(BF16) for t in range(SUBLANES)], axis=1)


def _peer_pick(d_ref, i1_ref, i2_ref, act_ref, first_row):
    n_rows = d_ref.shape[1] // N_KEYS
    for g in range(act_ref.shape[0] // SUBLANES):
        rs = slice(g * SUBLANES, (g + 1) * SUBLANES)
        i1 = i1_ref[rs, :]
        i2 = i2_ref[rs, :]
        acc = act_ref[rs, :]
        for j in range(n_rows):
            got = jnp.take_along_axis(d_ref[rs, j * N_KEYS:(j + 1) * N_KEYS], i2, axis=1, mode="promise_in_bounds")
            acc = acc + jnp.where(i1 == first_row + j, got, 0.0)
        act_ref[rs, :] = acc


def _peer_act_kernel(xn_ref, u_ref, i1_ref, i2_ref, act_ref, da_ref, db_ref):
    first = jnp.logical_and(pl.program_id(0) == 0, pl.program_id(1) == 0)
    c = pl.program_id(1)
    slab = PEER_EC // 2
    slab_rows = slab // N_KEYS
    nt = (((1,), (1,)), ((), ()))

    @pl.when(first)
    def _():
        db_ref[...] = jnp.zeros(db_ref.shape, F32)

    @pl.when(c == 0)
    def _():
        act_ref[...] = jnp.zeros(act_ref.shape, F32)

    base = c * 2 * slab_rows
    _peer_pick(db_ref, i1_ref, i2_ref, act_ref, base - slab_rows)
    da_ref[...] = lax.dot_general(xn_ref[...], u_ref[0:slab, :].astype(BF16), nt, preferred_element_type=F32)
    _peer_pick(da_ref, i1_ref, i2_ref, act_ref, base)
    db_ref[...] = lax.dot_general(xn_ref[...], u_ref[slab:2 * slab, :].astype(BF16), nt, preferred_element_type=F32)

    @pl.when(c == pl.num_programs(1) - 1)
    def _():
        _peer_pick(db_ref, i1_ref, i2_ref, act_ref, base + slab_rows)


def _peer_act(xn, u_tab, i1, i2, *, tb):
    rows = xn.shape[0]
    pair_spec = pl.BlockSpec((tb, PEER_PAIRS), lambda i, c: (i, 0))
    return pl.pallas_call(
        _peer_act_kernel, grid=(rows // tb, N_EXPERTS // PEER_EC),
        in_specs=[pl.BlockSpec((tb, D_MODEL), lambda i, c: (i, 0)),
                  pl.BlockSpec((PEER_EC, D_MODEL), lambda i, c: (c, 0)), pair_spec, pair_spec],
        out_specs=pair_spec,
        out_shape=jax.ShapeDtypeStruct((rows, PEER_PAIRS), F32),
        scratch_shapes=[pltpu.VMEM((tb, PEER_EC // 2), F32)] * 2,
        compiler_params=_params("arbitrary", "arbitrary"), name="peer_act",
    )(xn, u_tab, i1, i2)


def _peer_out_kernel(i1_ref, i2_ref, gate_ref, act_ref, v_ref, o_ref, w_ref):
    c = pl.program_id(1)

    @pl.when(c == 0)
    def _():
        key_iota = lax.broadcasted_iota(I32, (N_KEYS, PEER_PAIRS), 0)
        chunk_of_row = lax.broadcasted_iota(I32, (N_KEYS * SUBLANES, PEER_PAIRS), 0) >> 3

        def group(g, carry):
            r0 = pl.multiple_of(g * SUBLANES, SUBLANES)
            i1 = i1_ref[pl.ds(r0, SUBLANES), :]
            i2 = i2_ref[pl.ds(r0, SUBLANES), :]
            a = act_ref[pl.ds(r0, SUBLANES), :]
            wgt = gate_ref[pl.ds(r0, SUBLANES), :] * (0.5 * a * (1.0 + lax.erf(a * (1.0 / math.sqrt(2.0)))))
            hit = (jnp.concatenate([i1] * N_KEYS, axis=0) == chunk_of_row).astype(F32)
            lhs = _block_diag_lanes(hit)
            rhs_t = jnp.concatenate(
                [jnp.where(key_iota == jnp.broadcast_to(i2[t:t + 1, :], key_iota.shape),
                           jnp.broadcast_to(wgt[t:t + 1, :], key_iota.shape), 0.0).astype(BF16) for t in range(SUBLANES)], axis=1)
            dense = lax.dot_general(lhs, rhs_t, (((1,), (1,)), ((), ())), preferred_element_type=F32)
            for k in range(N_KEYS):
                w_ref[pl.ds(r0, SUBLANES), k * N_KEYS:(k + 1) * N_KEYS] = dense[k * SUBLANES:(k + 1) * SUBLANES, :]
            return carry

        lax.fori_loop(0, w_ref.shape[0] // SUBLANES, group, 0)

    col = pl.multiple_of(c * PEER_EC, PEER_EC)
    part = jnp.dot(w_ref[:, pl.ds(col, PEER_EC)].astype(BF16), v_ref[...], preferred_element_type=F32)

    @pl.when(c == 0)
    def _():
        o_ref[...] = part

    @pl.when(c > 0)
    def _():
        o_ref[...] += part


def _peer_out(i1, i2, gate, act, v_tab):
    rows = i1.shape[0]
    tb = PEER_TB
    pair_spec = pl.BlockSpec((tb, PEER_PAIRS), lambda i, c: (i, 0))
    return pl.pallas_call(
        _peer_out_kernel, grid=(rows // tb, N_EXPERTS // PEER_EC),
        in_specs=[pair_spec] * 4 + [pl.BlockSpec((PEER_EC, D_MODEL), lambda i, c: (c, 0))],
        out_specs=pl.BlockSpec((tb, D_MODEL), lambda i, c: (i, 0)),
        out_shape=jax.ShapeDtypeStruct((rows, D_MODEL), F32),
        scratch_shapes=[pltpu.VMEM((tb, N_EXPERTS), F32)],
        compiler_params=_params("arbitrary", "arbitrary"), name="peer_out",
    )(i1, i2, gate, act, v_tab)


def _ple_kernel(x_ref, f_ref, p_ref, g_ref, wg_ref, wp_ref, gf_ref, o_ref, *, final):
    x = x_ref[...] + f_ref[...]
    ms = jnp.mean(x * x, axis=-1, keepdims=True)
    h = (x * lax.rsqrt(ms + RMS_EPS)) * g_ref[...]
    out = x + jax.nn.sigmoid(_dot(h, wg_ref[...])) * _dot(p_ref[...], wp_ref[...])
    if final:
        ms2 = jnp.mean(out * out, axis=-1, keepdims=True)
        out = (out * lax.rsqrt(ms2 + RMS_EPS)) * gf_ref[...]
    o_ref[...] = out


def _ple(x, ffn, p, g, wg, wp, g_final, *, tm, final):
    rows, n = x.shape
    row = pl.BlockSpec((tm, n), lambda i: (i, 0))
    vec = pl.BlockSpec((1, n), lambda i: (0, 0))
    return pl.pallas_call(
        functools.partial(_ple_kernel, final=final), grid=(rows // tm,),
        in_specs=[row, row, pl.BlockSpec((tm, p.shape[1]), lambda i: (i, 0)), vec,
                  pl.BlockSpec(wg.shape, lambda i: (0, 0)), pl.BlockSpec(wp.shape, lambda i: (0, 0)), vec],
        out_specs=row,
        out_shape=jax.ShapeDtypeStruct((rows, n), F32),
        compiler_params=_params("arbitrary"), name="ple",
    )(x, ffn, p, g.reshape(1, n), wg, wp, g_final.reshape(1, n))


def _layer(x, p_rows, lw, state, g_final, dims, final):
    n_p, t_p, n_s, t_s, past = dims
    rows_p = n_p * t_p
    rows_s = n_s * t_s

    z_qkv = _norm_mm(x, lw["g_mix"], lw["w_qkv"], tm=768, tn=768)
    z_pool = _norm_mm(x, lw["g_mix"], lw["w_pool"], tm=768, tn=POOL_W)
    z_rwkv = _norm_mm(x, lw["g_mix"], lw["w_rwkv"], tm=768, tn=RWKV_COLS // 2)
    z_gate = _norm_mm(x, lw["g_mix"], lw["w_gate"], tm=768, tn=768)

    qr, kr = _rotary(z_qkv, lw["cos"], lw["sin"], tm=256, prompt_blocks=rows_p // 256, pos_blocks=t_p // 256)
    os_, ls_ = [], []
    for gi, (_, dil) in enumerate(ATT_GROUPS):
        o, l = _band_attn(qr, kr, z_qkv, gi, dil, n_seq=n_p, seq=t_p)
        os_.append(o)
        ls_.append(l)
    att_p = _merge_attn(os_, ls_, tm=512)
    caches = [c.reshape(n_s, c.shape[1], 2 * ATT_OUT) for c in state["kv"]]
    att_s = _sample_attn(qr, kr, z_qkv, caches, row0=rows_p, n_seq=n_s, t_new=t_s)
    att = jnp.concatenate([att_p, att_s], axis=0)

    pool_p = _pool_mixer(z_pool, jnp.zeros((n_p, POOL_HIST, POOL_W), F32), lw["pool_w"], lw["pool_scale"],
                         row0=0, n_seq=n_p, t_len=t_p, pos0=0)
    hist_s = jnp.concatenate([jnp.zeros((n_s, 1, POOL_W), F32), state["pool"]], axis=1)
    pool_s = _pool_mixer(z_pool, hist_s, lw["pool_w"], lw["pool_scale"], row0=rows_p, n_seq=n_s, t_len=t_s, pos0=past)
    pool = jnp.concatenate([pool_p, pool_s], axis=0)

    zr_p = z_rwkv[:rows_p].reshape(n_p, t_p, RWKV_COLS)
    zr_s = z_rwkv[rows_p:].reshape(n_s, t_s, RWKV_COLS)
    prep_p = _rwkv_prep(zr_p, jnp.zeros((n_p, 1, RWKV_COLS), F32), lw, tm=256)
    prep_s = _rwkv_prep(zr_s, state["shift"][:, None, :], lw, tm=t_s)
    yn_p, wkv_p = _rwkv_scan(prep_p[:6], jnp.zeros((n_p, RWKV_HEADS, RWKV_HEAD, RWKV_HEAD), F32), chunk=RWKV_CHUNK)
    yn_s, wkv_s = _rwkv_scan(prep_s[:6], state["wkv"], chunk=t_s)
    yn = jnp.concatenate([yn_p.reshape(rows_p, RWKV_W), yn_s.reshape(rows_s, RWKV_W)], axis=0)
    gg = jnp.concatenate([prep_p[6].reshape(rows_p, RWKV_W), prep_s[6].reshape(rows_s, RWKV_W)], axis=0)
    bonus = jnp.concatenate([prep_p[7].reshape(rows_p, RWKV_W), prep_s[7].reshape(rows_s, RWKV_W)], axis=0)

    mix = _mix(att, pool, yn, bonus, gg, z_gate, lw["ln_w"], lw["ln_b"], lw["w_attn_o"], lw["w_pool_o"], lw["w_rwkv_o"], tm=384)
    x = _mm_residual(x, mix, lw["w_out"], tm=384)

    q, xn = _norm_mm3(x, lw["g_ffn"], *lw["peer_wq"], tm=768, tn=512)
    i1, i2, gate = _peer_select(q, *lw["peer_subkeys"])
    act = _peer_act(xn, lw["peer_u"], i1, i2, tb=768)
    ffn = _peer_out(i1, i2, gate, act, lw["peer_v"])

    x = _ple(x, ffn, p_rows, lw["g_ple"], lw["ple_wg"], lw["ple_wp"], g_final, tm=256, final=final)

    def heads(a, n, t):
        return a.reshape(n, t, HEADS_PER_GROUP, HEAD_DIM)

    new_p, new_s = [], []
    for gi, (win, _) in enumerate(ATT_GROUPS):
        ks = slice(gi * ATT_OUT, (gi + 1) * ATT_OUT)
        vs = slice(2 * ATT_W + gi * ATT_OUT, 2 * ATT_W + (gi + 1) * ATT_OUT)
        keep = min(win, t_p)
        k_p = kr[:rows_p, ks].reshape(n_p, t_p, ATT_OUT)[:, t_p - keep:]
        v_p = z_qkv[:rows_p, vs].reshape(n_p, t_p, ATT_OUT)[:, t_p - keep:]
        new_p.append(jnp.stack([heads(k_p, n_p, keep), heads(v_p, n_p, keep)], axis=2))
        k_s = heads(kr[rows_p:, ks], n_s, t_s)
        v_s = heads(z_qkv[rows_p:, vs], n_s, t_s)
        new_s.append(jnp.concatenate([state["kv"][gi][:, t_s:], jnp.stack([k_s, v_s], axis=2)], axis=1))
    zp_p = z_pool[:rows_p].reshape(n_p, t_p, POOL_W)
    zp_s = z_pool[rows_p:].reshape(n_s, t_s, POOL_W)
    new_p += [zp_p[:, t_p - (POOL_HIST - 1):], wkv_p, zr_p[:, -1]]
    new_s += [jnp.concatenate([hist_s[:, 1:], zp_s], axis=1)[:, -(POOL_HIST - 1):], wkv_s, zr_s[:, -1]]
    return x, new_p, new_s


def kernel(x_prompt, x_sample, p_prompt, p_sample, cache_attn_w128, cache_attn_w512, cache_attn_w2048, state_pool, state_rwkv_wkv, state_rwkv_shift, g_mix, w_in, w_attn_o, w_pool_o, w_rwkv_o, w_out, pool_w, pool_scale, rwkv_mu, rwkv_w0, rwkv_w2, rwkv_a0, rwkv_a2, rwkv_g2, rwkv_kk, rwkv_ka, rwkv_rk, rwkv_ln_w, rwkv_ln_b, g_ffn, peer_wq, peer_subkeys, peer_u, peer_v, g_ple, ple_wg, ple_wp, g_final):
    n_p, t_p, _ = x_prompt.shape
    n_s, t_s, _ = x_sample.shape
    depth = w_in.shape[0]
    past = PAST_LEN
    rows_p = n_p * t_p
    rows_s = n_s * t_s
    dims = (n_p, t_p, n_s, t_s, past)

    x = jnp.concatenate([x_prompt.reshape(rows_p, D_MODEL), x_sample.reshape(rows_s, D_MODEL)], axis=0)
    cos_p, sin_p = _rope_tables(jnp.arange(t_p, dtype=I32))
    cos_s, sin_s = _rope_tables(past + jnp.arange(t_s, dtype=I32))
    reps = 256 // t_s
    cos_t = jnp.concatenate([cos_p, jnp.tile(cos_s, (reps, 1))], axis=0)
    sin_t = jnp.concatenate([sin_p, jnp.tile(sin_s, (reps, 1))], axis=0)
    lora_pad = jnp.zeros((128 - 64, RWKV_W), F32)

    def row(a):
        return a.reshape(1, -1)

    new_p, new_s = [], []
    for l in range(depth):
        wl = w_in[l]
        lw = {
            "g_mix": g_mix[l],
            "w_qkv": wl[:, :OFF_POOL].astype(BF16), "w_pool": wl[:, OFF_POOL:OFF_RWKV].astype(BF16),
            "w_rwkv": wl[:, OFF_RWKV:OFF_GATE].astype(BF16), "w_gate": wl[:, OFF_GATE:].astype(BF16),
            "cos": cos_t, "sin": sin_t,
            "pool_w": pool_w[l].astype(BF16), "pool_scale": pool_scale[l],
            "mu": row(rwkv_mu[l]), "w0": row(rwkv_w0[l]), "a0": row(rwkv_a0[l]),
            "w2p": jnp.concatenate([rwkv_w2[l], lora_pad], axis=0).astype(BF16),
            "a2p": jnp.concatenate([lora_pad, rwkv_a2[l]], axis=0).astype(BF16),
            "g2": rwkv_g2[l].astype(BF16),
            "kk": row(rwkv_kk[l]), "ka": row(rwkv_ka[l]), "rk": row(rwkv_rk[l]),
            "ln_w": rwkv_ln_w[l], "ln_b": rwkv_ln_b[l],
            "w_attn_o": w_attn_o[l].astype(BF16), "w_pool_o": w_pool_o[l].astype(BF16),
            "w_rwkv_o": w_rwkv_o[l].astype(BF16), "w_out": w_out[l].astype(BF16),
            "g_ffn": g_ffn[l], "peer_wq": _split_bf16(peer_wq[l]), "peer_subkeys": _split_bf16(peer_subkeys[l]),
            "peer_u": peer_u[l], "peer_v": peer_v[l].astype(BF16),
            "g_ple": g_ple[l], "ple_wg": ple_wg[l].astype(BF16), "ple_wp": ple_wp[l].astype(BF16),
        }
        state = {"kv": (cache_attn_w128[l], cache_attn_w512[l], cache_attn_w2048[l]), "pool": state_pool[l],
                 "wkv": state_rwkv_wkv[l], "shift": state_rwkv_shift[l]}
        p_rows = jnp.concatenate([p_prompt[l].reshape(rows_p, -1), p_sample[l].reshape(rows_s, -1)], axis=0).astype(BF16)
        x, st_p, st_s = _layer(x, p_rows, lw, state, g_final, dims, l == depth - 1)
        new_p.append(st_p)
        new_s.append(st_s)

    outs = [x[:rows_p].reshape(n_p, t_p, D_MODEL), x[rows_p:].reshape(n_s, t_s, D_MODEL)]
    for j in range(6):
        outs.append(jnp.stack([s[j] for s in new_p]))
        outs.append(jnp.stack([s[j] for s in new_s]))
    return tuple(outs)
```

```python
import functools
import math

import jax
import jax.numpy as jnp
from jax import lax
from jax.experimental import pallas as pl
from jax.experimental.pallas import tpu as pltpu

F32 = jnp.float32
BF16 = jnp.bfloat16
I32 = jnp.int32

D_MODEL = 2048
RMS_EPS = 1e-6
HEAD_DIM = 64
ATT_GROUPS = ((128, 1), (512, 4), (2048, 16))
HEADS_PER_GROUP = 4
ATT_W = 768
ATT_OUT = 256
ATT_SPAN = 128
ROPE_THETA = 10000.0
POOL_WINDOWS = (2, 4, 8, 16)
POOL_GROUP = 128
POOL_W = 512
POOL_HIST = 16
RWKV_HEAD = 64
RWKV_HEADS = 12
RWKV_W = 768
RWKV_COLS = 2560
RWKV_LORA_OFF = 2304
GN_EPS = 64e-5
RWKV_CHUNK = 64
OFF_POOL = 2304
OFF_RWKV = 2816
OFF_GATE = 5376
PAST_LEN = 8192
PEER_HEADS = 8
N_KEYS = 128
N_EXPERTS = N_KEYS * N_KEYS
PEER_TOPK = 16
PEER_PAIRS = PEER_HEADS * PEER_TOPK
PEER_TB = 384
PEER_EC = 1024
SUBLANES = 8
VMEM_LIMIT = 56 * 1024 * 1024

NEG_INF = float("-inf")


def _params(*sem):
    return pltpu.CompilerParams(dimension_semantics=sem, vmem_limit_bytes=VMEM_LIMIT)


def _dot(a, b):
    return jnp.dot(a.astype(BF16), b.astype(BF16), preferred_element_type=F32)


def _dot_nt(a, b):
    return lax.dot_general(a.astype(BF16), b.astype(BF16), (((1,), (1,)), ((), ())), preferred_element_type=F32)


def _norm_mm_kernel(x_ref, g_ref, w_ref, o_ref, xn_ref):
    @pl.when(pl.program_id(1) == 0)
    def _():
        x = x_ref[...]
        ms = jnp.mean(x * x, axis=-1, keepdims=True)
        xn_ref[...] = ((x * lax.rsqrt(ms + RMS_EPS)) * g_ref[...]).astype(BF16)

    o_ref[...] = jnp.dot(xn_ref[...], w_ref[...], preferred_element_type=F32)


def _norm_mm(x, g, w, *, tm, tn):
    rows, k = x.shape
    n = w.shape[1]
    return pl.pallas_call(
        _norm_mm_kernel, grid=(rows // tm, n // tn),
        in_specs=[
            pl.BlockSpec((tm, k), lambda i, j: (i, 0)),
            pl.BlockSpec((1, k), lambda i, j: (0, 0)),
            pl.BlockSpec((k, tn), lambda i, j: (0, j)),
        ],
        out_specs=pl.BlockSpec((tm, tn), lambda i, j: (i, j)),
        out_shape=jax.ShapeDtypeStruct((rows, n), F32),
        scratch_shapes=[pltpu.VMEM((tm, k), BF16)],
        compiler_params=_params("arbitrary", "arbitrary"), name="norm_mm",
    )(x, g.reshape(1, k), w)


def _split_bf16(a):
    hi = a.astype(BF16)
    lo = (a - hi.astype(F32)).astype(BF16)
    return hi, lo


def _dot3(ah, al, bh, bl, dims=(((1,), (0,)), ((), ()))):
    def d(p, q):
        return lax.dot_general(p, q, dims, preferred_element_type=F32)

    return d(ah, bh) + (d(ah, bl) + d(al, bh))


def _norm_mm3_kernel(x_ref, g_ref, wh_ref, wl_ref, o_ref, xh_ref, xl_ref):
    @pl.when(pl.program_id(1) == 0)
    def _():
        x = x_ref[...]
        ms = jnp.mean(x * x, axis=-1, keepdims=True)
        xh, xl = _split_bf16((x * lax.rsqrt(ms + RMS_EPS)) * g_ref[...])
        xh_ref[...] = xh
        xl_ref[...] = xl

    o_ref[...] = _dot3(xh_ref[...], xl_ref[...], wh_ref[...], wl_ref[...])


def _norm_mm3(x, g, wh, wl, *, tm, tn):
    rows, k = x.shape
    n = wh.shape[1]
    wspec = pl.BlockSpec((k, tn), lambda i, j: (0, j))
    return pl.pallas_call(
        _norm_mm3_kernel, grid=(rows // tm, n // tn),
        in_specs=[pl.BlockSpec((tm, k), lambda i, j: (i, 0)), pl.BlockSpec((1, k), lambda i, j: (0, 0)), wspec, wspec],
        out_specs=[pl.BlockSpec((tm, tn), lambda i, j: (i, j)), pl.BlockSpec((tm, k), lambda i, j: (i, 0))],
        out_shape=[jax.ShapeDtypeStruct((rows, n), F32), jax.ShapeDtypeStruct((rows, k), BF16)],
        scratch_shapes=[pltpu.VMEM((tm, k), BF16)],
        compiler_params=_params("arbitrary", "arbitrary"), name="norm_mm3",
    )(x, g.reshape(1, k), wh, wl)


def _rotary_kernel(q_ref, k_ref, cos_ref, sin_ref, qo_ref, ko_ref):
    cos = cos_ref[...]
    sin = sin_ref[...]
    lane = lax.broadcasted_iota(I32, cos.shape, 1)
    first_half = (lane & (HEAD_DIM - 1)) < HEAD_DIM // 2
    for src, dst in ((q_ref, qo_ref), (k_ref, ko_ref)):
        for c in range(ATT_W // 128):
            x = src[:, c * 128:(c + 1) * 128]
            partner = jnp.where(first_half, pltpu.roll(x, 128 - HEAD_DIM // 2, 1), pltpu.roll(x, HEAD_DIM // 2, 1))
            dst[:, c * 128:(c + 1) * 128] = x * cos + partner * sin


def _rotary(z_qkv, cos_t, sin_t, *, tm, prompt_blocks, pos_blocks):
    rows = z_qkv.shape[0]

    def tab(i):
        return (jnp.where(i < prompt_blocks, i % pos_blocks, pos_blocks), 0)

    return pl.pallas_call(
        _rotary_kernel, grid=(rows // tm,),
        in_specs=[pl.BlockSpec((tm, ATT_W), lambda i: (i, 0)), pl.BlockSpec((tm, ATT_W), lambda i: (i, 1)),
                  pl.BlockSpec((tm, 128), tab), pl.BlockSpec((tm, 128), tab)],
        out_specs=[pl.BlockSpec((tm, ATT_W), lambda i: (i, 0))] * 2,
        out_shape=[jax.ShapeDtypeStruct((rows, ATT_W), F32)] * 2,
        compiler_params=_params("arbitrary"), name="rotary",
    )(z_qkv, z_qkv, cos_t, sin_t)


def _rope_tables(pos):
    half = HEAD_DIM // 2
    inv = ROPE_THETA ** (-jnp.arange(half, dtype=F32) / half)
    ang = pos.astype(F32)[:, None] * inv[None, :]
    cos = jnp.cos(ang)
    sin = jnp.sin(ang)
    cos_t = jnp.concatenate([cos, cos, cos, cos], axis=1)
    sin_t = jnp.concatenate([-sin, sin, -sin, sin], axis=1)
    return cos_t, sin_t


def _band_attn_kernel(q_ref, kc_ref, kp_ref, vc_ref, vp_ref, o_ref, l_ref, *, has_prev):
    b = pl.program_id(2)
    qi = lax.broadcasted_iota(I32, (ATT_SPAN, ATT_SPAN), 0)
    ki = lax.broadcasted_iota(I32, (ATT_SPAN, ATT_SPAN), 1)
    cur_ok = ki <= qi
    prev_ok = jnp.logical_and(ki >= qi, b > 0)
    scale = HEAD_DIM ** -0.5
    outs, lses = [], []
    for h in range(HEADS_PER_GROUP):
        sl = slice(h * HEAD_DIM, (h + 1) * HEAD_DIM)
        q = q_ref[:, sl]
        sc = jnp.where(cur_ok, _dot_nt(q, kc_ref[:, sl]) * scale, NEG_INF)
        m = jnp.max(sc, axis=-1, keepdims=True)
        if has_prev:
            sp = jnp.where(prev_ok, _dot_nt(q, kp_ref[:, sl]) * scale, NEG_INF)
            m = jnp.maximum(m, jnp.max(sp, axis=-1, keepdims=True))
        ec = jnp.exp(sc - m)
        den = jnp.sum(ec, axis=-1, keepdims=True)
        acc = _dot(ec, vc_ref[:, sl])
        if has_prev:
            ep = jnp.exp(sp - m)
            den = den + jnp.sum(ep, axis=-1, keepdims=True)
            acc = acc + _dot(ep, vp_ref[:, sl])
        outs.append(acc / den)
        lses.append(jnp.broadcast_to(m + jnp.log(den), (ATT_SPAN, HEAD_DIM)))
    o_ref[...] = jnp.concatenate(outs, axis=1)
    l_ref[...] = jnp.concatenate(lses, axis=1)


def _band_attn(qr, kr, z_qkv, gi, dil, *, n_seq, seq):
    rows_p = n_seq * seq
    n_cls = seq // dil
    nb = -(-n_cls // ATT_SPAN)
    assert n_cls % ATT_SPAN == 0
    qv = qr.reshape(qr.shape[0] // dil, dil * ATT_W)
    kv = kr.reshape(kr.shape[0] // dil, dil * ATT_W)
    zv = z_qkv.reshape(z_qkv.shape[0] // dil, dil * 3 * ATT_W)
    cq = ATT_W // ATT_OUT
    cz = 3 * ATT_W // ATT_OUT
    voff = 2 * ATT_W // ATT_OUT + gi

    def cur(n, r, b):
        return n * nb + b

    def prev(n, r, b):
        return n * nb + jnp.maximum(b - 1, 0)

    blk = (ATT_SPAN, ATT_OUT)
    o, l = pl.pallas_call(
        functools.partial(_band_attn_kernel, has_prev=nb > 1),
        grid=(n_seq, dil, nb),
        in_specs=[
            pl.BlockSpec(blk, lambda n, r, b: (cur(n, r, b), r * cq + gi)),
            pl.BlockSpec(blk, lambda n, r, b: (cur(n, r, b), r * cq + gi)),
            pl.BlockSpec(blk, lambda n, r, b: (prev(n, r, b), r * cq + gi)),
            pl.BlockSpec(blk, lambda n, r, b: (cur(n, r, b), r * cz + voff)),
            pl.BlockSpec(blk, lambda n, r, b: (prev(n, r, b), r * cz + voff)),
        ],
        out_specs=[pl.BlockSpec(blk, lambda n, r, b: (cur(n, r, b), r))] * 2,
        out_shape=[jax.ShapeDtypeStruct((rows_p // dil, dil * ATT_OUT), F32)] * 2,
        compiler_params=_params("arbitrary", "arbitrary", "arbitrary"), name=f"band_attn_g{gi}",
    )(qv, kv, kv, zv, zv)
    return o.reshape(rows_p, ATT_OUT), l.reshape(rows_p, ATT_OUT)


def _merge_groups(os_, ls_):
    m = jnp.maximum(jnp.maximum(ls_[0], ls_[1]), ls_[2])
    es = [jnp.exp(l - m) for l in ls_]
    tot = es[0] + es[1] + es[2]
    return (es[0] / tot) * os_[0] + (es[1] / tot) * os_[1] + (es[2] / tot) * os_[2]


def _merge_kernel(o0, o1, o2, l0, l1, l2, a_ref):
    a_ref[...] = _merge_groups([o0[...], o1[...], o2[...]], [l0[...], l1[...], l2[...]])


def _merge_attn(os_, ls_, *, tm):
    rows = os_[0].shape[0]
    spec = pl.BlockSpec((tm, ATT_OUT), lambda i: (i, 0))
    return pl.pallas_call(
        _merge_kernel, grid=(rows // tm,), in_specs=[spec] * 6, out_specs=spec,
        out_shape=jax.ShapeDtypeStruct((rows, ATT_OUT), F32),
        compiler_params=_params("arbitrary"), name="merge_attn",
    )(*os_, *ls_)


def _sample_attn_kernel(q_ref, kn_ref, vn_ref, c0_ref, c1_ref, c2_ref, a_ref, *, t_new):
    scale = HEAD_DIM ** -0.5
    outs_g, lses_g = [], []
    for gi, (c_ref, (win, dil)) in enumerate(zip((c0_ref, c1_ref, c2_ref), ATT_GROUPS)):
        cache_len = c_ref.shape[1]
        t_c = lax.broadcasted_iota(I32, (t_new, cache_len), 0)
        c_c = lax.broadcasted_iota(I32, (t_new, cache_len), 1)
        d_c = cache_len + t_c - c_c
        ok_c = jnp.logical_and((d_c & (dil - 1)) == 0, d_c <= ATT_SPAN * dil)
        t_n = lax.broadcasted_iota(I32, (t_new, t_new), 0)
        u_n = lax.broadcasted_iota(I32, (t_new, t_new), 1)
        d_n = t_n - u_n
        ok_n = jnp.logical_and(d_n >= 0, (d_n & (dil - 1)) == 0)
        outs, lses = [], []
        for h in range(HEADS_PER_GROUP):
            col = gi * ATT_OUT + h * HEAD_DIM
            q = q_ref[:, col:col + HEAD_DIM]
            kc = c_ref[0, :, h * HEAD_DIM:(h + 1) * HEAD_DIM]
            vc = c_ref[0, :, ATT_OUT + h * HEAD_DIM:ATT_OUT + (h + 1) * HEAD_DIM]
            s_c = jnp.where(ok_c, _dot_nt(q, kc) * scale, NEG_INF)
            s_n = jnp.where(ok_n, _dot_nt(q, kn_ref[:, col:col + HEAD_DIM]) * scale, NEG_INF)
            m = jnp.maximum(jnp.max(s_c, axis=-1, keepdims=True), jnp.max(s_n, axis=-1, keepdims=True))
            e_c = jnp.exp(s_c - m)
            e_n = jnp.exp(s_n - m)
            den = jnp.sum(e_c, axis=-1, keepdims=True) + jnp.sum(e_n, axis=-1, keepdims=True)
            acc = _dot(e_c, vc) + _dot(e_n, vn_ref[:, col:col + HEAD_DIM])
            outs.append(acc / den)
            lses.append(jnp.broadcast_to(m + jnp.log(den), (t_new, HEAD_DIM)))
        outs_g.append(jnp.concatenate(outs, axis=1))
        lses_g.append(jnp.concatenate(lses, axis=1))
    a_ref[...] = _merge_groups(outs_g, lses_g)


def _sample_attn(qr, kr, z_qkv, caches, *, row0, n_seq, t_new):
    b0 = row0 // t_new
    return pl.pallas_call(
        functools.partial(_sample_attn_kernel, t_new=t_new),
        grid=(n_seq,),
        in_specs=[
            pl.BlockSpec((t_new, ATT_W), lambda n: (b0 + n, 0)),
            pl.BlockSpec((t_new, ATT_W), lambda n: (b0 + n, 0)),
            pl.BlockSpec((t_new, ATT_W), lambda n: (b0 + n, 2)),
        ] + [pl.BlockSpec((1, c.shape[1], 2 * ATT_OUT), lambda n: (n, 0, 0)) for c in caches],
        out_specs=pl.BlockSpec((t_new, ATT_OUT), lambda n: (n, 0)),
        out_shape=jax.ShapeDtypeStruct((n_seq * t_new, ATT_OUT), F32),
        compiler_params=_params("arbitrary"), name="sample_attn",
    )(qr, kr, z_qkv, *caches)


def _pool_kernel(z_ref, h_ref, w_ref, s_ref, o_ref, buf_ref, *, t_len, pos0):
    buf_ref[0:POOL_HIST, :] = h_ref[0]
    buf_ref[POOL_HIST:POOL_HIST + t_len, :] = z_ref[...]
    pos = pos0 + lax.broadcasted_iota(I32, (t_len, POOL_GROUP), 0)
    for g, win in enumerate(POOL_WINDOWS):
        cs = slice(g * POOL_GROUP, (g + 1) * POOL_GROUP)
        z = buf_ref[POOL_HIST:POOL_HIST + t_len, cs]
        wsum = z
        for i in range(1, win):
            wsum = wsum + buf_ref[POOL_HIST - i:POOL_HIST - i + t_len, cs]
        cnt = jnp.minimum(win, pos + 1).astype(F32)
        y = wsum / cnt - z
        o_ref[:, cs] = _dot(y, w_ref[g]) * s_ref[:, cs]


def _pool_mixer(z_pool, hist, pool_w, pool_scale, *, row0, n_seq, t_len, pos0):
    b0 = row0 // t_len
    return pl.pallas_call(
        functools.partial(_pool_kernel, t_len=t_len, pos0=pos0),
        grid=(n_seq,),
        in_specs=[
            pl.BlockSpec((t_len, POOL_W), lambda n: (b0 + n, 0)),
            pl.BlockSpec((1, POOL_HIST, POOL_W), lambda n: (n, 0, 0)),
            pl.BlockSpec((len(POOL_WINDOWS), POOL_GROUP, POOL_GROUP), lambda n: (0, 0, 0)),
            pl.BlockSpec((1, POOL_W), lambda n: (0, 0)),
        ],
        out_specs=pl.BlockSpec((t_len, POOL_W), lambda n: (n, 0)),
        out_shape=jax.ShapeDtypeStruct((n_seq * t_len, POOL_W), F32),
        scratch_shapes=[pltpu.VMEM((POOL_HIST + t_len, POOL_W), F32)],
        compiler_params=_params("arbitrary"), name="pool_mixer",
    )(z_pool, hist, pool_w, pool_scale.reshape(1, POOL_W))


def _rwkv_prep_kernel(z_ref, zp_ref, first_ref, mu_ref, w0_ref, w2_ref, a0_ref, a2_ref, g2_ref, kk_ref, ka_ref, rk_ref,
                      r_o, lw_o, k_o, v_o, kn_o, b_o, g_o, bonus_o, buf_ref, *, tm):
    i = pl.program_id(1)
    z = z_ref[...]
    prev_row = jnp.where(i == 0, first_ref[0], zp_ref[SUBLANES - 1:SUBLANES, :])
    buf_ref[SUBLANES:SUBLANES + tm, :] = z
    buf_ref[SUBLANES - 1:SUBLANES, :] = prev_row
    shifted = buf_ref[SUBLANES - 1:SUBLANES - 1 + tm, :]
    xm = z + mu_ref[...] * (shifted - z)
    r = xm[:, 0:RWKV_W]
    k = xm[:, RWKV_W:2 * RWKV_W]
    v = xm[:, 2 * RWKV_W:3 * RWKV_W]
    wa = xm[:, RWKV_LORA_OFF:RWKV_LORA_OFF + 128]
    gl = xm[:, RWKV_LORA_OFF + 128:RWKV_COLS]
    xw = w0_ref[...] + _dot(jnp.tanh(wa), w2_ref[...])
    logw = -math.exp(-0.5) * jax.nn.sigmoid(xw)
    a = jax.nn.sigmoid(a0_ref[...] + _dot(wa, a2_ref[...]))
    g_o[...] = _dot(jax.nn.sigmoid(gl), g2_ref[...])
    kkr = k * kk_ref[...]
    kmod = k * (1.0 + (a - 1.0) * ka_ref[...])
    rkk = r * kmod * rk_ref[...]
    bonus = []
    for h in range(RWKV_HEADS):
        sl = slice(h * RWKV_HEAD, (h + 1) * RWKV_HEAD)
        kh = kkr[:, sl]
        nrm = jnp.sqrt(jnp.sum(kh * kh, axis=-1, keepdims=True))
        kn = kh / jnp.maximum(nrm, 1e-12)
        r_o[0, h] = r[:, sl]
        lw_o[0, h] = logw[:, sl]
        k_o[0, h] = kmod[:, sl]
        v_o[0, h] = v[:, sl]
        kn_o[0, h] = kn
        b_o[0, h] = kn * a[:, sl]
        bonus.append(jnp.sum(rkk[:, sl], axis=-1, keepdims=True) * v[:, sl])
    bonus_o[...] = jnp.concatenate(bonus, axis=1)


def _rwkv_prep(z, first_prev, lw, *, row0, n_seq, t_len, tm):
    nblk = t_len // tm
    pb = tm // SUBLANES
    b0 = row0 // tm
    p0 = row0 // SUBLANES

    def vec(n):
        return pl.BlockSpec((1, n), lambda s, i: (0, 0))

    hm = jax.ShapeDtypeStruct((n_seq, RWKV_HEADS, t_len, RWKV_HEAD), F32)
    rm = jax.ShapeDtypeStruct((n_seq * t_len, RWKV_W), F32)
    hm_spec = pl.BlockSpec((1, RWKV_HEADS, tm, RWKV_HEAD), lambda s, i: (s, 0, i, 0))
    rm_spec = pl.BlockSpec((tm, RWKV_W), lambda s, i: (s * nblk + i, 0))
    return pl.pallas_call(
        functools.partial(_rwkv_prep_kernel, tm=tm),
        grid=(n_seq, nblk),
        in_specs=[
            pl.BlockSpec((tm, RWKV_COLS), lambda s, i: (b0 + s * nblk + i, 0)),
            pl.BlockSpec((SUBLANES, RWKV_COLS), lambda s, i: (jnp.maximum(p0 + (s * nblk + i) * pb - 1, 0), 0)),
            pl.BlockSpec((1, 1, RWKV_COLS), lambda s, i: (s, 0, 0)),
            vec(RWKV_COLS), vec(RWKV_W),
            pl.BlockSpec((128, RWKV_W), lambda s, i: (0, 0)),
            vec(RWKV_W),
            pl.BlockSpec((128, RWKV_W), lambda s, i: (0, 0)),
            pl.BlockSpec((128, RWKV_W), lambda s, i: (0, 0)),
            vec(RWKV_W), vec(RWKV_W), vec(RWKV_W),
        ],
        out_specs=[hm_spec] * 6 + [rm_spec] * 2,
        out_shape=[hm] * 6 + [rm] * 2,
        scratch_shapes=[pltpu.VMEM((tm + SUBLANES, RWKV_COLS), F32)],
        compiler_params=_params("arbitrary", "arbitrary"), name="rwkv_prep",
    )(z, z, first_prev, lw["mu"], lw["w0"], lw["w2p"], lw["a0"], lw["a2p"], lw["g2"], lw["kk"], lw["ka"], lw["rk"])


def _bmm(a, b):
    return jnp.einsum("hqk,hkd->hqd", a.astype(BF16), b.astype(BF16), preferred_element_type=F32)


def _bmm_nt(a, b):
    return jnp.einsum("hqd,hkd->hqk", a.astype(BF16), b.astype(BF16), preferred_element_type=F32)


def _bmm_tn(a, b):
    return jnp.einsum("hkq,hkd->hqd", a.astype(BF16), b.astype(BF16), preferred_element_type=F32)


def _rwkv_scan_kernel(r_ref, lw_ref, k_ref, v_ref, kn_ref, b_ref, s0_ref, y_ref, st_ref, s_scr, *, chunk):
    c = pl.program_id(1)
    nh = RWKV_HEADS

    @pl.when(c == 0)
    def _():
        s_scr[...] = s0_ref[0]

    r, logw, k, v, kn, b = r_ref[0], lw_ref[0], k_ref[0], v_ref[0], kn_ref[0], b_ref[0]
    row = lax.broadcasted_iota(I32, (chunk, chunk), 0)
    col = lax.broadcasted_iota(I32, (chunk, chunk), 1)
    incl = row >= col
    strict = row > col
    tri = jnp.broadcast_to(incl.astype(BF16)[None], (nh, chunk, chunk))
    lw_hi = logw.astype(BF16)
    lw_lo = (logw - lw_hi.astype(F32)).astype(BF16)
    cum = (jnp.einsum("hqk,hkd->hqd", tri, lw_hi, preferred_element_type=F32)
           + jnp.einsum("hqk,hkd->hqd", tri, lw_lo, preferred_element_type=F32))
    p_inv = jnp.exp(-cum)
    kt = k * p_inv
    bt = b * p_inv
    kap = kn * jnp.exp(cum - logw)
    rho = r * jnp.exp(cum)
    qq = jnp.concatenate([kap, rho], axis=1)
    gram = _bmm_nt(qq, jnp.concatenate([kt, bt], axis=1))
    a_k = jnp.where(strict[None], gram[:, :chunk, :chunk], 0.0)
    a_b = jnp.where(strict[None], gram[:, :chunk, chunk:], 0.0)
    l_k = jnp.where(incl[None], gram[:, chunk:, :chunk], 0.0)
    l_b = jnp.where(incl[None], gram[:, chunk:, chunk:], 0.0)
    x = jnp.broadcast_to((row == col).astype(F32)[None], (nh, chunk, chunk))
    m = 1
    while m < chunk:
        sh = m.bit_length() - 1
        same = (row >> (sh + 1)) == (col >> (sh + 1))
        lower_left = jnp.logical_and(((row >> sh) & 1) == 1, ((col >> sh) & 1) == 0)
        off = jnp.where(jnp.logical_and(same, lower_left)[None], a_b, 0.0)
        x = x - _bmm(_bmm(x, off), x)
        m *= 2
    s = s_scr[...]
    qs = _bmm_nt(qq, s)
    u = _bmm(x, -(qs[:, :chunk] + _bmm(a_k, v)))
    y = qs[:, chunk:] + _bmm(jnp.concatenate([l_k, l_b], axis=2), jnp.concatenate([v, u], axis=1))
    s_new = (s + _bmm_tn(jnp.concatenate([v, u], axis=1), jnp.concatenate([kt, bt], axis=1))) * jnp.exp(cum[:, chunk - 1:chunk, :])
    s_scr[...] = s_new
    st_ref[0] = s_new
    mean = jnp.mean(y, axis=-1, keepdims=True)
    var = jnp.mean(jnp.square(y - mean), axis=-1, keepdims=True)
    yn = (y - mean) * lax.rsqrt(var + GN_EPS)
    y_ref[0] = jnp.concatenate([yn[h] for h in range(nh)], axis=1)


def _rwkv_scan(prep, s0, *, chunk):
    r, lw, k, v, kn, b = prep
    n_seq, nh, t_len, hd = r.shape
    hm_spec = pl.BlockSpec((1, nh, chunk, hd), lambda s, c: (s, 0, c, 0))
    st_spec = pl.BlockSpec((1, nh, hd, hd), lambda s, c: (s, 0, 0, 0))
    return pl.pallas_call(
        functools.partial(_rwkv_scan_kernel, chunk=chunk),
        grid=(n_seq, t_len // chunk),
        in_specs=[hm_spec] * 6 + [st_spec],
        out_specs=[pl.BlockSpec((1, chunk, RWKV_W), lambda s, c: (s, c, 0)), st_spec],
        out_shape=[jax.ShapeDtypeStruct((n_seq, t_len, RWKV_W), F32), jax.ShapeDtypeStruct((n_seq, nh, hd, hd), F32)],
        scratch_shapes=[pltpu.VMEM((nh, hd, hd), F32)],
        compiler_params=_params("arbitrary", "arbitrary"), name="rwkv_scan",
    )(r, lw, k, v, kn, b, s0)


def _mix_kernel(*refs, n_prompt_blocks):
    pairs, (gate_ref, lnw_ref, lnb_ref, wa_ref, wp_ref, wr_ref, o_ref) = refs[:10], refs[10:]
    is_sample = pl.program_id(0) >= n_prompt_blocks
    att, pool, yn, bonus, g = [jnp.where(is_sample, pairs[2 * j + 1][...], pairs[2 * j][...]) for j in range(5)]
    rw = (yn * lnw_ref[...] + lnb_ref[...] + bonus) * g
    mix = jax.nn.sigmoid(gate_ref[:, 0:D_MODEL]) * _dot(att, wa_ref[...])
    mix = mix + jax.nn.sigmoid(gate_ref[:, D_MODEL:2 * D_MODEL]) * _dot(pool, wp_ref[...])
    mix = mix + jax.nn.sigmoid(gate_ref[:, 2 * D_MODEL:3 * D_MODEL]) * _dot(rw, wr_ref[...])
    o_ref[...] = mix.astype(BF16)


def _mix(branches, z_gate, ln_w, ln_b, wa, wp, wr, *, tm):
    rows = z_gate.shape[0]
    n_prompt_blocks = branches[0][0].shape[0] // tm
    assert rows == (n_prompt_blocks + 1) * tm and all(s.shape[0] == tm for _, s in branches)

    def full(a):
        return pl.BlockSpec(a.shape, lambda i: (0, 0))

    specs, args = [], []
    for p, s in branches:
        specs += [pl.BlockSpec((tm, p.shape[1]), lambda i: (jnp.minimum(i, n_prompt_blocks - 1), 0)), full(s)]
        args += [p, s]
    ln_w = ln_w.reshape(1, RWKV_W)
    ln_b = ln_b.reshape(1, RWKV_W)
    return pl.pallas_call(
        functools.partial(_mix_kernel, n_prompt_blocks=n_prompt_blocks), grid=(rows // tm,),
        in_specs=specs + [pl.BlockSpec((tm, 3 * D_MODEL), lambda i: (i, 0)),
                          full(ln_w), full(ln_b), full(wa), full(wp), full(wr)],
        out_specs=pl.BlockSpec((tm, D_MODEL), lambda i: (i, 0)),
        out_shape=jax.ShapeDtypeStruct((rows, D_MODEL), BF16),
        compiler_params=_params("arbitrary"), name="gated_mix",
    )(*args, z_gate, ln_w, ln_b, wa, wp, wr)


def _mm_res_kernel(x_ref, a_ref, w_ref, o_ref):
    o_ref[...] = x_ref[...] + jnp.dot(a_ref[...], w_ref[...], preferred_element_type=F32)


def _mm_residual(x, a, w, *, tm):
    rows, n = x.shape
    k = a.shape[1]
    return pl.pallas_call(
        _mm_res_kernel, grid=(rows // tm,),
        in_specs=[pl.BlockSpec((tm, n), lambda i: (i, 0)), pl.BlockSpec((tm, k), lambda i: (i, 0)),
                  pl.BlockSpec((k, n), lambda i: (0, 0))],
        out_specs=pl.BlockSpec((tm, n), lambda i: (i, 0)),
        out_shape=jax.ShapeDtypeStruct((rows, n), F32),
        compiler_params=_params("arbitrary"), name="out_proj",
    )(x, a, w)


def _topk_cols(scores, ids, k):
    lanes = scores[0].shape[1]
    iota_k = lax.broadcasted_iota(I32, (k, lanes), 0)
    big = jnp.iinfo(jnp.int32).max

    def body(j, carry):
        out = []
        for (s, vals, idxs), ident in zip(carry, ids):
            m = jnp.max(s, axis=0, keepdims=True)
            idx = jnp.min(jnp.where(s == m, ident, big), axis=0, keepdims=True)
            vals = jnp.where(iota_k == j, m, vals)
            idxs = jnp.where(iota_k == j, idx, idxs)
            out.append((jnp.where(ident == idx, NEG_INF, s), vals, idxs))
        return tuple(out)

    init = tuple((s, jnp.zeros((k, lanes), F32), jnp.zeros((k, lanes), I32)) for s in scores)
    return [(v, i) for _, v, i in lax.fori_loop(0, k, body, init)]


def _gather_rows(table, sel, k):
    out = jnp.zeros(sel.shape, table.dtype)
    for a in range(k):
        out = jnp.where(sel == a, table[a:a + 1, :], out)
    return out


def _peer_select_kernel(q_ref, skh_ref, skl_ref, i1_o, i2_o, gate_o, i1_s, i2_s, g_s, *, tok):
    kk = PEER_TOPK
    half = kk // 2
    nt = (((1,), (1,)), ((), ()))
    key_id = lax.broadcasted_iota(I32, (N_KEYS, tok), 0)
    row = lax.broadcasted_iota(I32, (half * half + kk, tok), 0)
    cand_id = jnp.where(row < half * half, (row >> (half.bit_length() - 1)) * kk + (row & (half - 1)),
                        jnp.where(row < half * half + half, row - half * half + half, (row - half * half) * kk))

    def head(h, carry):
        c0 = pl.multiple_of(h * 2 * N_KEYS, 2 * N_KEYS)
        q1h, q1l = _split_bf16(q_ref[:, pl.ds(c0, N_KEYS)])
        q2h, q2l = _split_bf16(q_ref[:, pl.ds(c0 + N_KEYS, N_KEYS)])
        s1 = _dot3(skh_ref[h, 0], skl_ref[h, 0], q1h, q1l, nt)
        s2 = _dot3(skh_ref[h, 1], skl_ref[h, 1], q2h, q2l, nt)
        (t1, k1), (t2, k2) = _topk_cols([s1, s2], [key_id, key_id], kk)
        cand = jnp.concatenate([t1[a:a + 1, :] + t2[0:half, :] for a in range(half)]
                               + [t1[0:1, :] + t2[half:kk, :], t1[half:kk, :] + t2[0:1, :]], axis=0)
        ((top, sel),) = _topk_cols([cand], [cand_id], kk)
        e1 = _gather_rows(k1, sel >> 4, kk)
        e2 = _gather_rows(k2, sel & (kk - 1), kk)
        ex = jnp.exp(top - jnp.max(top, axis=0, keepdims=True))
        gate = ex / jnp.sum(ex, axis=0, keepdims=True)
        r0 = pl.multiple_of(h * kk, kk)
        i1_s[pl.ds(r0, kk), :] = e1.astype(F32)
        i2_s[pl.ds(r0, kk), :] = e2.astype(F32)
        g_s[pl.ds(r0, kk), :] = gate
        return carry

    lax.fori_loop(0, PEER_HEADS, head, 0)
    i1_o[...] = i1_s[...].T.astype(I32)
    i2_o[...] = i2_s[...].T.astype(I32)
    gate_o[...] = g_s[...].T


def _peer_select(q, sk_hi, sk_lo, *, tok=128):
    rows = q.shape[0]
    spec = pl.BlockSpec((tok, PEER_PAIRS), lambda i: (i, 0))
    return pl.pallas_call(
        functools.partial(_peer_select_kernel, tok=tok), grid=(rows // tok,),
        in_specs=[pl.BlockSpec((tok, q.shape[1]), lambda i: (i, 0)),
                  pl.BlockSpec(sk_hi.shape, lambda i: (0, 0, 0, 0)), pl.BlockSpec(sk_lo.shape, lambda i: (0, 0, 0, 0))],
        out_specs=[spec] * 3,
        out_shape=[jax.ShapeDtypeStruct((rows, PEER_PAIRS), I32), jax.ShapeDtypeStruct((rows, PEER_PAIRS), I32),
                   jax.ShapeDtypeStruct((rows, PEER_PAIRS), F32)],
        scratch_shapes=[pltpu.VMEM((PEER_PAIRS, tok), F32)] * 3,
        compiler_params=_params("arbitrary"), name="peer_select",
    )(q, sk_hi, sk_lo)


def _peer_pick(d_ref, i1_ref, i2_ref, act_ref, first_row):
    n_rows = d_ref.shape[1] // N_KEYS
    for g in range(act_ref.shape[0] // SUBLANES):
        rs = slice(g * SUBLANES, (g + 1) * SUBLANES)
        i1 = i1_ref[rs, :]
        i2 = i2_ref[rs, :]
        acc = act_ref[rs, :]
        for j in range(n_rows):
            got = jnp.take_along_axis(d_ref[rs, j * N_KEYS:(j + 1) * N_KEYS], i2, axis=1, mode="promise_in_bounds")
            acc = acc + jnp.where(i1 == first_row + j, got, 0.0)
        act_ref[rs, :] = acc


def _peer_act_kernel(xn_ref, u_ref, i1_ref, i2_ref, act_ref, da_ref, db_ref):
    first = jnp.logical_and(pl.program_id(0) == 0, pl.program_id(1) == 0)
    c = pl.program_id(1)
    slab = PEER_EC // 2
    slab_rows = slab // N_KEYS
    nt = (((1,), (1,)), ((), ()))

    @pl.when(first)
    def _():
        db_ref[...] = jnp.zeros(db_ref.shape, F32)

    @pl.when(c == 0)
    def _():
        act_ref[...] = jnp.zeros(act_ref.shape, F32)

    base = c * 2 * slab_rows
    _peer_pick(db_ref, i1_ref, i2_ref, act_ref, base - slab_rows)
    da_ref[...] = lax.dot_general(xn_ref[...], u_ref[0, 0:slab, :].astype(BF16), nt, preferred_element_type=F32)
    _peer_pick(da_ref, i1_ref, i2_ref, act_ref, base)
    db_ref[...] = lax.dot_general(xn_ref[...], u_ref[0, slab:2 * slab, :].astype(BF16), nt, preferred_element_type=F32)

    @pl.when(c == pl.num_programs(1) - 1)
    def _():
        _peer_pick(db_ref, i1_ref, i2_ref, act_ref, base + slab_rows)


def _peer_act(xn, u_tabs, layer, i1, i2, *, tb):
    rows = xn.shape[0]
    pair_spec = pl.BlockSpec((tb, PEER_PAIRS), lambda i, c: (i, 0))
    return pl.pallas_call(
        _peer_act_kernel, grid=(rows // tb, N_EXPERTS // PEER_EC),
        in_specs=[pl.BlockSpec((tb, D_MODEL), lambda i, c: (i, 0)),
                  pl.BlockSpec((1, PEER_EC, D_MODEL), lambda i, c: (layer, c, 0)), pair_spec, pair_spec],
        out_specs=pair_spec,
        out_shape=jax.ShapeDtypeStruct((rows, PEER_PAIRS), F32),
        scratch_shapes=[pltpu.VMEM((tb, PEER_EC // 2), F32)] * 2,
        compiler_params=_params("arbitrary", "arbitrary"), name="peer_act",
    )(xn, u_tabs, i1, i2)


def _peer_out_kernel(i1_ref, i2_ref, gate_ref, act_ref, v_ref, o_ref, w_ref):
    c = pl.program_id(1)
    n_groups = w_ref.shape[0]
    rows_per_chunk = PEER_EC // N_KEYS

    @pl.when(c == 0)
    def _():
        key_iota = lax.broadcasted_iota(I32, (N_KEYS, PEER_PAIRS), 0)

        def group(g, carry):
            r0 = pl.multiple_of(g * SUBLANES, SUBLANES)
            i1 = i1_ref[pl.ds(r0, SUBLANES), :]
            i2 = i2_ref[pl.ds(r0, SUBLANES), :]
            a = act_ref[pl.ds(r0, SUBLANES), :]
            wgt = gate_ref[pl.ds(r0, SUBLANES), :] * (0.5 * a * (1.0 + lax.erf(a * (1.0 / math.sqrt(2.0)))))
            for t in range(SUBLANES):
                hit1 = key_iota == jnp.broadcast_to(i1[t:t + 1, :], key_iota.shape)
                hit2 = key_iota == jnp.broadcast_to(i2[t:t + 1, :], key_iota.shape)
                w2 = jnp.where(hit2, jnp.broadcast_to(wgt[t:t + 1, :], key_iota.shape), 0.0)
                w_tok = _dot_nt(jnp.where(hit1, 1.0, 0.0), w2)
                w_ref[g, pl.ds(t, N_KEYS, stride=SUBLANES), :] = w_tok
            return carry

        lax.fori_loop(0, n_groups, group, 0)

    row0 = c * (rows_per_chunk * SUBLANES)
    lhs = jnp.concatenate(
        [w_ref[:, pl.ds(pl.multiple_of(row0 + j * SUBLANES, SUBLANES), SUBLANES), :].reshape(n_groups * SUBLANES, N_KEYS)
         for j in range(rows_per_chunk)], axis=1)
    part = jnp.dot(lhs.astype(BF16), v_ref[0], preferred_element_type=F32)

    @pl.when(c == 0)
    def _():
        o_ref[...] = part

    @pl.when(c > 0)
    def _():
        o_ref[...] += part


def _peer_out(i1, i2, gate, act, v_tabs, layer):
    rows = i1.shape[0]
    tb = PEER_TB
    pair_spec = pl.BlockSpec((tb, PEER_PAIRS), lambda i, c: (i, 0))
    return pl.pallas_call(
        _peer_out_kernel, grid=(rows // tb, N_EXPERTS // PEER_EC),
        in_specs=[pair_spec] * 4 + [pl.BlockSpec((1, PEER_EC, D_MODEL), lambda i, c: (layer, c, 0))],
        out_specs=pl.BlockSpec((tb, D_MODEL), lambda i, c: (i, 0)),
        out_shape=jax.ShapeDtypeStruct((rows, D_MODEL), F32),
        scratch_shapes=[pltpu.VMEM((tb // SUBLANES, N_KEYS * SUBLANES, N_KEYS), F32)],
        compiler_params=_params("arbitrary", "arbitrary"), name="peer_out",
    )(i1, i2, gate, act, v_tabs)


def _ple_kernel(x_ref, f_ref, p_ref, g_ref, wg_ref, wp_ref, gf_ref, o_ref, *, final):
    x = x_ref[...] + f_ref[...]
    ms = jnp.mean(x * x, axis=-1, keepdims=True)
    h = (x * lax.rsqrt(ms + RMS_EPS)) * g_ref[...]
    out = x + jax.nn.sigmoid(_dot(h, wg_ref[...])) * _dot(p_ref[...], wp_ref[...])
    if final:
        ms2 = jnp.mean(out * out, axis=-1, keepdims=True)
        out = (out * lax.rsqrt(ms2 + RMS_EPS)) * gf_ref[...]
    o_ref[...] = out


def _ple(x, ffn, p, g, wg, wp, g_final, *, tm, final):
    rows, n = x.shape
    row = pl.BlockSpec((tm, n), lambda i: (i, 0))
    vec = pl.BlockSpec((1, n), lambda i: (0, 0))
    return pl.pallas_call(
        functools.partial(_ple_kernel, final=final), grid=(rows // tm,),
        in_specs=[row, row, pl.BlockSpec((tm, p.shape[1]), lambda i: (i, 0)), vec,
                  pl.BlockSpec(wg.shape, lambda i: (0, 0)), pl.BlockSpec(wp.shape, lambda i: (0, 0)), vec],
        out_specs=row,
        out_shape=jax.ShapeDtypeStruct((rows, n), F32),
        compiler_params=_params("arbitrary"), name="ple",
    )(x, ffn, p, g.reshape(1, n), wg, wp, g_final.reshape(1, n))


def _layer(x, p_rows, lw, state, g_final, dims, final):
    n_p, t_p, n_s, t_s, past = dims
    rows_p = n_p * t_p
    rows_s = n_s * t_s

    z_qkv = _norm_mm(x, lw["g_mix"], lw["w_qkv"], tm=768, tn=768)
    z_pool = _norm_mm(x, lw["g_mix"], lw["w_pool"], tm=768, tn=POOL_W)
    z_rwkv = _norm_mm(x, lw["g_mix"], lw["w_rwkv"], tm=768, tn=RWKV_COLS // 2)
    z_gate = _norm_mm(x, lw["g_mix"], lw["w_gate"], tm=768, tn=768)

    qr, kr = _rotary(z_qkv, lw["cos"], lw["sin"], tm=256, prompt_blocks=rows_p // 256, pos_blocks=t_p // 256)
    os_, ls_ = [], []
    for gi, (_, dil) in enumerate(ATT_GROUPS):
        o, l = _band_attn(qr, kr, z_qkv, gi, dil, n_seq=n_p, seq=t_p)
        os_.append(o)
        ls_.append(l)
    att_p = _merge_attn(os_, ls_, tm=512)
    caches = [c.reshape(n_s, c.shape[1], 2 * ATT_OUT) for c in state["kv"]]
    att_s = _sample_attn(qr, kr, z_qkv, caches, row0=rows_p, n_seq=n_s, t_new=t_s)

    pool_p = _pool_mixer(z_pool, jnp.zeros((n_p, POOL_HIST, POOL_W), F32), lw["pool_w"], lw["pool_scale"],
                         row0=0, n_seq=n_p, t_len=t_p, pos0=0)
    hist_s = jnp.concatenate([jnp.zeros((n_s, 1, POOL_W), F32), state["pool"]], axis=1)
    pool_s = _pool_mixer(z_pool, hist_s, lw["pool_w"], lw["pool_scale"], row0=rows_p, n_seq=n_s, t_len=t_s, pos0=past)

    prep_p = _rwkv_prep(z_rwkv, jnp.zeros((n_p, 1, RWKV_COLS), F32), lw, row0=0, n_seq=n_p, t_len=t_p, tm=256)
    prep_s = _rwkv_prep(z_rwkv, state["shift"][:, None, :], lw, row0=rows_p, n_seq=n_s, t_len=t_s, tm=t_s)
    yn_p, wkv_p = _rwkv_scan(prep_p[:6], jnp.zeros((n_p, RWKV_HEADS, RWKV_HEAD, RWKV_HEAD), F32), chunk=RWKV_CHUNK)
    yn_s, wkv_s = _rwkv_scan(prep_s[:6], state["wkv"], chunk=t_s)
    branches = [(att_p, att_s), (pool_p, pool_s), (yn_p.reshape(rows_p, RWKV_W), yn_s.reshape(rows_s, RWKV_W)),
                (prep_p[7], prep_s[7]), (prep_p[6], prep_s[6])]

    mix = _mix(branches, z_gate, lw["ln_w"], lw["ln_b"], lw["w_attn_o"], lw["w_pool_o"], lw["w_rwkv_o"], tm=rows_s)
    x = _mm_residual(x, mix, lw["w_out"], tm=384)

    q, xn = _norm_mm3(x, lw["g_ffn"], *lw["peer_wq"], tm=768, tn=512)
    i1, i2, gate = _peer_select(q, *lw["peer_subkeys"])
    act = _peer_act(xn, lw["peer_u"], lw["layer"], i1, i2, tb=768)
    ffn = _peer_out(i1, i2, gate, act, lw["peer_v"], lw["layer"])

    x = _ple(x, ffn, p_rows, lw["g_ple"], lw["ple_wg"], lw["ple_wp"], g_final, tm=256, final=final)

    def heads(a, n, t):
        return a.reshape(n, t, HEADS_PER_GROUP, HEAD_DIM)

    new_p, new_s = [], []
    for gi, (win, _) in enumerate(ATT_GROUPS):
        ks = slice(gi * ATT_OUT, (gi + 1) * ATT_OUT)
        vs = slice(2 * ATT_W + gi * ATT_OUT, 2 * ATT_W + (gi + 1) * ATT_OUT)
        keep = min(win, t_p)
        k_p = kr[:rows_p, ks].reshape(n_p, t_p, ATT_OUT)[:, t_p - keep:]
        v_p = z_qkv[:rows_p, vs].reshape(n_p, t_p, ATT_OUT)[:, t_p - keep:]
        new_p.append(jnp.stack([heads(k_p, n_p, keep), heads(v_p, n_p, keep)], axis=2))
        k_s = heads(kr[rows_p:, ks], n_s, t_s)
        v_s = heads(z_qkv[rows_p:, vs], n_s, t_s)
        new_s.append(jnp.concatenate([state["kv"][gi][:, t_s:], jnp.stack([k_s, v_s], axis=2)], axis=1))
    keep = POOL_HIST - 1
    pool_p_state = jnp.stack([z_pool[(n + 1) * t_p - keep:(n + 1) * t_p] for n in range(n_p)])
    zp_s = z_pool[rows_p:].reshape(n_s, t_s, POOL_W)
    new_p += [pool_p_state, wkv_p, z_rwkv[t_p - 1:rows_p:t_p]]
    new_s += [jnp.concatenate([hist_s[:, 1:], zp_s], axis=1)[:, -keep:], wkv_s, z_rwkv[rows_p + t_s - 1::t_s]]
    return x, new_p, new_s


def kernel(x_prompt, x_sample, p_prompt, p_sample, cache_attn_w128, cache_attn_w512, cache_attn_w2048, state_pool, state_rwkv_wkv, state_rwkv_shift, g_mix, w_in, w_attn_o, w_pool_o, w_rwkv_o, w_out, pool_w, pool_scale, rwkv_mu, rwkv_w0, rwkv_w2, rwkv_a0, rwkv_a2, rwkv_g2, rwkv_kk, rwkv_ka, rwkv_rk, rwkv_ln_w, rwkv_ln_b, g_ffn, peer_wq, peer_subkeys, peer_u, peer_v, g_ple, ple_wg, ple_wp, g_final):
    n_p, t_p, _ = x_prompt.shape
    n_s, t_s, _ = x_sample.shape
    depth = w_in.shape[0]
    past = PAST_LEN
    rows_p = n_p * t_p
    rows_s = n_s * t_s
    dims = (n_p, t_p, n_s, t_s, past)

    x = jnp.concatenate([x_prompt.reshape(rows_p, D_MODEL), x_sample.reshape(rows_s, D_MODEL)], axis=0)
    cos_p, sin_p = _rope_tables(jnp.arange(t_p, dtype=I32))
    cos_s, sin_s = _rope_tables(past + jnp.arange(t_s, dtype=I32))
    reps = 256 // t_s
    cos_t = jnp.concatenate([cos_p, jnp.tile(cos_s, (reps, 1))], axis=0)
    sin_t = jnp.concatenate([sin_p, jnp.tile(sin_s, (reps, 1))], axis=0)
    lora_pad = jnp.zeros((128 - 64, RWKV_W), F32)
    peer_v_bf16 = peer_v.astype(BF16)

    def row(a):
        return a.reshape(1, -1)

    new_p, new_s = [], []
    for l in range(depth):
        wl = w_in[l]
        lw = {
            "g_mix": g_mix[l],
            "w_qkv": wl[:, :OFF_POOL].astype(BF16), "w_pool": wl[:, OFF_POOL:OFF_RWKV].astype(BF16),
            "w_rwkv": wl[:, OFF_RWKV:OFF_GATE].astype(BF16), "w_gate": wl[:, OFF_GATE:].astype(BF16),
            "cos": cos_t, "sin": sin_t,
            "pool_w": pool_w[l].astype(BF16), "pool_scale": pool_scale[l],
            "mu": row(rwkv_mu[l]), "w0": row(rwkv_w0[l]), "a0": row(rwkv_a0[l]),
            "w2p": jnp.concatenate([rwkv_w2[l], lora_pad], axis=0).astype(BF16),
            "a2p": jnp.concatenate([lora_pad, rwkv_a2[l]], axis=0).astype(BF16),
            "g2": rwkv_g2[l].astype(BF16),
            "kk": row(rwkv_kk[l]), "ka": row(rwkv_ka[l]), "rk": row(rwkv_rk[l]),
            "ln_w": rwkv_ln_w[l], "ln_b": rwkv_ln_b[l],
            "w_attn_o": w_attn_o[l].astype(BF16), "w_pool_o": w_pool_o[l].astype(BF16),
            "w_rwkv_o": w_rwkv_o[l].astype(BF16), "w_out": w_out[l].astype(BF16),
            "g_ffn": g_ffn[l], "peer_wq": _split_bf16(peer_wq[l]), "peer_subkeys": _split_bf16(peer_subkeys[l]),
            "peer_u": peer_u, "peer_v": peer_v_bf16, "layer": l,
            "g_ple": g_ple[l], "ple_wg": ple_wg[l].astype(BF16), "ple_wp": ple_wp[l].astype(BF16),
        }
        state = {"kv": (cache_attn_w128[l], cache_attn_w512[l], cache_attn_w2048[l]), "pool": state_pool[l],
                 "wkv": state_rwkv_wkv[l], "shift": state_rwkv_shift[l]}
        p_rows = jnp.concatenate([p_prompt[l].reshape(rows_p, -1), p_sample[l].reshape(rows_s, -1)], axis=0).astype(BF16)
        x, st_p, st_s = _layer(x, p_rows, lw, state, g_final, dims, l == depth - 1)
        new_p.append(st_p)
        new_s.append(st_s)

    outs = [x[:rows_p].reshape(n_p, t_p, D_MODEL), x[rows_p:].reshape(n_s, t_s, D_MODEL)]
    for j in range(6):
        outs.append(jnp.stack([s[j] for s in new_p]))
        outs.append(jnp.stack([s[j] for s in new_s]))
    return tuple(outs)
```

```python
import functools
import math

import jax
import jax.numpy as jnp
from jax import lax
from jax.experimental import pallas as pl
from jax.experimental.pallas import tpu as pltpu

F32 = jnp.float32
BF16 = jnp.bfloat16
I32 = jnp.int32

D_MODEL = 2048
RMS_EPS = 1e-6
HEAD_DIM = 64
ATT_GROUPS = ((128, 1), (512, 4), (2048, 16))
HEADS_PER_GROUP = 4
ATT_W = 768
ATT_OUT = 256
ATT_SPAN = 128
ROPE_THETA = 10000.0
POOL_WINDOWS = (2, 4, 8, 16)
POOL_GROUP = 128
POOL_W = 512
POOL_HIST = 16
RWKV_HEAD = 64
RWKV_HEADS = 12
RWKV_W = 768
RWKV_COLS = 2560
RWKV_LORA_OFF = 2304
GN_EPS = 64e-5
RWKV_CHUNK = 64
OFF_POOL = 2304
OFF_RWKV = 2816
OFF_GATE = 5376
PAST_LEN = 8192
PEER_HEADS = 8
N_KEYS = 128
N_EXPERTS = N_KEYS * N_KEYS
PEER_TOPK = 16
PEER_PAIRS = PEER_HEADS * PEER_TOPK
PEER_TB = 384
PEER_EC = 1024
SUBLANES = 8
VMEM_LIMIT = 56 * 1024 * 1024

NEG_INF = float("-inf")


def _params(*sem):
    return pltpu.CompilerParams(dimension_semantics=sem, vmem_limit_bytes=VMEM_LIMIT)


def _dot(a, b):
    return jnp.dot(a.astype(BF16), b.astype(BF16), preferred_element_type=F32)


def _dot_nt(a, b):
    return lax.dot_general(a.astype(BF16), b.astype(BF16), (((1,), (1,)), ((), ())), preferred_element_type=F32)


def _norm_mm_kernel(x_ref, g_ref, w_ref, o_ref, xn_ref):
    @pl.when(pl.program_id(1) == 0)
    def _():
        x = x_ref[...]
        ms = jnp.mean(x * x, axis=-1, keepdims=True)
        xn_ref[...] = ((x * lax.rsqrt(ms + RMS_EPS)) * g_ref[...]).astype(BF16)

    o_ref[...] = jnp.dot(xn_ref[...], w_ref[...], preferred_element_type=F32)


def _norm_mm(x, g, w, *, tm, tn):
    rows, k = x.shape
    n = w.shape[1]
    return pl.pallas_call(
        _norm_mm_kernel, grid=(rows // tm, n // tn),
        in_specs=[
            pl.BlockSpec((tm, k), lambda i, j: (i, 0)),
            pl.BlockSpec((1, k), lambda i, j: (0, 0)),
            pl.BlockSpec((k, tn), lambda i, j: (0, j)),
        ],
        out_specs=pl.BlockSpec((tm, tn), lambda i, j: (i, j)),
        out_shape=jax.ShapeDtypeStruct((rows, n), F32),
        scratch_shapes=[pltpu.VMEM((tm, k), BF16)],
        compiler_params=_params("arbitrary", "arbitrary"), name="norm_mm",
    )(x, g.reshape(1, k), w)


def _split_bf16(a):
    hi = a.astype(BF16)
    lo = (a - hi.astype(F32)).astype(BF16)
    return hi, lo


def _dot3(ah, al, bh, bl, dims=(((1,), (0,)), ((), ()))):
    def d(p, q):
        return lax.dot_general(p, q, dims, preferred_element_type=F32)

    return d(ah, bh) + (d(ah, bl) + d(al, bh))


def _norm_mm3_kernel(x_ref, g_ref, wh_ref, wl_ref, o_ref, xh_ref, xl_ref):
    @pl.when(pl.program_id(1) == 0)
    def _():
        x = x_ref[...]
        ms = jnp.mean(x * x, axis=-1, keepdims=True)
        xh, xl = _split_bf16((x * lax.rsqrt(ms + RMS_EPS)) * g_ref[...])
        xh_ref[...] = xh
        xl_ref[...] = xl

    o_ref[...] = _dot3(xh_ref[...], xl_ref[...], wh_ref[...], wl_ref[...])


def _norm_mm3(x, g, wh, wl, *, tm, tn):
    rows, k = x.shape
    n = wh.shape[1]
    wspec = pl.BlockSpec((k, tn), lambda i, j: (0, j))
    return pl.pallas_call(
        _norm_mm3_kernel, grid=(rows // tm, n // tn),
        in_specs=[pl.BlockSpec((tm, k), lambda i, j: (i, 0)), pl.BlockSpec((1, k), lambda i, j: (0, 0)), wspec, wspec],
        out_specs=[pl.BlockSpec((tm, tn), lambda i, j: (i, j)), pl.BlockSpec((tm, k), lambda i, j: (i, 0))],
        out_shape=[jax.ShapeDtypeStruct((rows, n), F32), jax.ShapeDtypeStruct((rows, k), BF16)],
        scratch_shapes=[pltpu.VMEM((tm, k), BF16)],
        compiler_params=_params("arbitrary", "arbitrary"), name="norm_mm3",
    )(x, g.reshape(1, k), wh, wl)


def _rotate(x, cos, sin):
    lane = lax.broadcasted_iota(I32, x.shape, 1)
    first_half = (lane & (HEAD_DIM - 1)) < HEAD_DIM // 2
    partner = jnp.where(first_half, pltpu.roll(x, 128 - HEAD_DIM // 2, 1), pltpu.roll(x, HEAD_DIM // 2, 1))
    return x * cos + partner * sin


def _rope_tables(pos):
    half = HEAD_DIM // 2
    inv = ROPE_THETA ** (-jnp.arange(half, dtype=F32) / half)
    ang = pos.astype(F32)[:, None] * inv[None, :]
    cos = jnp.cos(ang)
    sin = jnp.sin(ang)
    cos_t = jnp.concatenate([cos, cos, cos, cos], axis=1)
    sin_t = jnp.concatenate([-sin, sin, -sin, sin], axis=1)
    return cos_t, sin_t


ROT_ROWS = 256


def _prompt_attn_kernel(q_ref, k_ref, v_ref, cos_ref, sin_ref, *rest, dil, seq, keep):
    o_ref, l_ref, kv_ref, qs, ks, vs = rest[-6:]
    chunks = ATT_OUT // 128
    for i in range(seq // ROT_ROWS):
        rs = slice(i * ROT_ROWS, (i + 1) * ROT_ROWS)
        cos = cos_ref[rs, :]
        sin = sin_ref[rs, :]
        for c in range(chunks):
            cs = slice(c * 128, (c + 1) * 128)
            qs[c, rs, :] = _rotate(q_ref[rs, cs], cos, sin)
            ks[c, rs, :] = _rotate(k_ref[rs, cs], cos, sin)
            vs[c, rs, :] = v_ref[rs, cs]
    for h in range(HEADS_PER_GROUP):
        c, half = divmod(h, 2)
        hs = slice(half * HEAD_DIM, (half + 1) * HEAD_DIM)
        for j in range(keep // 128):
            ps = slice(seq - keep + j * 128, seq - keep + (j + 1) * 128)
            kv_ref[0, 0, 0, h, :, j * 128:(j + 1) * 128] = ks[c, ps, hs].T
            kv_ref[0, 0, 1, h, :, j * 128:(j + 1) * 128] = vs[c, ps, hs].T

    def rows_of(start):
        return pl.ds(start, ATT_SPAN, stride=dil) if dil > 1 else pl.ds(start, ATT_SPAN)

    qi = lax.broadcasted_iota(I32, (ATT_SPAN, ATT_SPAN), 0)
    ki = lax.broadcasted_iota(I32, (ATT_SPAN, ATT_SPAN), 1)
    cur_ok = ki <= qi
    prev_ok = ki >= qi
    scale = HEAD_DIM ** -0.5
    n_blocks = seq // dil // ATT_SPAN
    for r in range(dil):
        for b in range(n_blocks):
            cur = rows_of(r + dil * b * ATT_SPAN)
            for c in range(chunks):
                q2, kc2, vc2 = qs[c, cur, :], ks[c, cur, :], vs[c, cur, :]
                if b > 0:
                    prev = rows_of(r + dil * (b - 1) * ATT_SPAN)
                    kp2, vp2 = ks[c, prev, :], vs[c, prev, :]
                outs, lses = [], []
                for half in range(2):
                    hs = slice(half * HEAD_DIM, (half + 1) * HEAD_DIM)
                    q = q2[:, hs]
                    sc = jnp.where(cur_ok, _dot_nt(q, kc2[:, hs]) * scale, NEG_INF)
                    m = jnp.max(sc, axis=-1, keepdims=True)
                    if b > 0:
                        sp = jnp.where(prev_ok, _dot_nt(q, kp2[:, hs]) * scale, NEG_INF)
                        m = jnp.maximum(m, jnp.max(sp, axis=-1, keepdims=True))
                    ec = jnp.exp(sc - m)
                    den = jnp.sum(ec, axis=-1, keepdims=True)
                    acc = _dot(ec, vc2[:, hs])
                    if b > 0:
                        ep = jnp.exp(sp - m)
                        den = den + jnp.sum(ep, axis=-1, keepdims=True)
                        acc = acc + _dot(ep, vp2[:, hs])
                    outs.append(acc / den)
                    lses.append(jnp.broadcast_to(m + jnp.log(den), (ATT_SPAN, HEAD_DIM)))
                o_ref[c, cur, :] = jnp.concatenate(outs, axis=1)
                l_ref[c, cur, :] = jnp.concatenate(lses, axis=1)


def _prompt_attn(z_qkv, cos_t, sin_t, gi, dil, layer, kv_prev, *, depth, n_seq, seq):
    rows_p = n_seq * seq
    win = ATT_GROUPS[gi][0]
    keep = min(win, seq)
    assert seq % (dil * ATT_SPAN) == 0 and keep % 128 == 0 and seq % ROT_ROWS == 0
    tab = pl.BlockSpec((seq, 128), lambda n: (0, 0))
    chunked = pl.BlockSpec((ATT_OUT // 128, seq, 128), lambda n: (0, n, 0))
    kv_shape = (depth, n_seq, 2, HEADS_PER_GROUP, HEAD_DIM, keep)
    kv_spec = pl.BlockSpec((1, 1) + kv_shape[2:], lambda n: (layer, n, 0, 0, 0, 0))
    in_specs = [pl.BlockSpec((seq, ATT_OUT), lambda n: (n, gi)),
                pl.BlockSpec((seq, ATT_OUT), lambda n: (n, ATT_W // ATT_OUT + gi)),
                pl.BlockSpec((seq, ATT_OUT), lambda n: (n, 2 * ATT_W // ATT_OUT + gi)), tab, tab]
    args = [z_qkv, z_qkv, z_qkv, cos_t, sin_t]
    aliases = {}
    if kv_prev is not None:
        in_specs.append(pl.BlockSpec(memory_space=pl.ANY))
        args.append(kv_prev)
        aliases = {len(args) - 1: 2}
    chunk_shape = jax.ShapeDtypeStruct((ATT_OUT // 128, rows_p, 128), F32)
    return pl.pallas_call(
        functools.partial(_prompt_attn_kernel, dil=dil, seq=seq, keep=keep),
        grid=(n_seq,), in_specs=in_specs,
        out_specs=[chunked, chunked, kv_spec],
        out_shape=[chunk_shape, chunk_shape, jax.ShapeDtypeStruct(kv_shape, F32)],
        scratch_shapes=[pltpu.VMEM((ATT_OUT // 128, seq, 128), F32)] * 3,
        input_output_aliases=aliases,
        compiler_params=_params("arbitrary"), name=f"prompt_attn_g{gi}",
    )(*args)


def _merge_groups(os_, ls_):
    m = jnp.maximum(jnp.maximum(ls_[0], ls_[1]), ls_[2])
    es = [jnp.exp(l - m) for l in ls_]
    tot = es[0] + es[1] + es[2]
    return (es[0] / tot) * os_[0] + (es[1] / tot) * os_[1] + (es[2] / tot) * os_[2]


def _merge_kernel(o0, o1, o2, l0, l1, l2, a_ref):
    for c in range(ATT_OUT // 128):
        a_ref[:, c * 128:(c + 1) * 128] = _merge_groups([o0[c], o1[c], o2[c]], [l0[c], l1[c], l2[c]])


def _merge_attn(os_, ls_, *, tm):
    rows = os_[0].shape[1]
    spec = pl.BlockSpec((ATT_OUT // 128, tm, 128), lambda i: (0, i, 0))
    return pl.pallas_call(
        _merge_kernel, grid=(rows // tm,), in_specs=[spec] * 6, out_specs=pl.BlockSpec((tm, ATT_OUT), lambda i: (i, 0)),
        out_shape=jax.ShapeDtypeStruct((rows, ATT_OUT), F32),
        compiler_params=_params("arbitrary"), name="merge_attn",
    )(*os_, *ls_)


def _sample_attn_kernel(z_ref, cos_ref, sin_ref, *rest, t_new):
    caches, (a_ref, n0_ref, n1_ref, n2_ref) = rest[0:3], rest[-4:]
    scale = HEAD_DIM ** -0.5
    cos = cos_ref[...]
    sin = sin_ref[...]
    n_chunks = ATT_W // 128
    q_c = [_rotate(z_ref[:, c * 128:(c + 1) * 128], cos, sin) for c in range(n_chunks)]
    k_c = [_rotate(z_ref[:, ATT_W + c * 128:ATT_W + (c + 1) * 128], cos, sin) for c in range(n_chunks)]
    v_c = [z_ref[:, 2 * ATT_W + c * 128:2 * ATT_W + (c + 1) * 128] for c in range(n_chunks)]
    outs_g, lses_g = [], []
    for gi, (c_ref, n_ref, (win, dil)) in enumerate(zip(caches, (n0_ref, n1_ref, n2_ref), ATT_GROUPS)):
        cache_len = c_ref.shape[-1]
        t_c = lax.broadcasted_iota(I32, (t_new, cache_len), 0)
        c_c = lax.broadcasted_iota(I32, (t_new, cache_len), 1)
        d_c = cache_len + t_c - c_c
        ok_c = jnp.logical_and((d_c & (dil - 1)) == 0, d_c <= ATT_SPAN * dil)
        t_n = lax.broadcasted_iota(I32, (t_new, t_new), 0)
        u_n = lax.broadcasted_iota(I32, (t_new, t_new), 1)
        d_n = t_n - u_n
        ok_n = jnp.logical_and(d_n >= 0, (d_n & (dil - 1)) == 0)
        outs, lses = [], []
        for h in range(HEADS_PER_GROUP):
            chunk, half = divmod(gi * HEADS_PER_GROUP + h, 2)
            hs = slice(half * HEAD_DIM, (half + 1) * HEAD_DIM)
            q, k_new, v_new = q_c[chunk][:, hs], k_c[chunk][:, hs], v_c[chunk][:, hs]
            k_t = c_ref[0, 0, 0, h]
            v_t = c_ref[0, 0, 1, h]
            s_c = jnp.where(ok_c, _dot(q, k_t) * scale, NEG_INF)
            s_n = jnp.where(ok_n, _dot_nt(q, k_new) * scale, NEG_INF)
            m = jnp.maximum(jnp.max(s_c, axis=-1, keepdims=True), jnp.max(s_n, axis=-1, keepdims=True))
            e_c = jnp.exp(s_c - m)
            e_n = jnp.exp(s_n - m)
            den = jnp.sum(e_c, axis=-1, keepdims=True) + jnp.sum(e_n, axis=-1, keepdims=True)
            acc = _dot_nt(e_c, v_t) + _dot(e_n, v_new)
            outs.append(acc / den)
            lses.append(jnp.broadcast_to(m + jnp.log(den), (t_new, HEAD_DIM)))
            n_ref[0, 0, 0, h] = jnp.concatenate([k_t[:, t_new:], k_new.T], axis=1)
            n_ref[0, 0, 1, h] = jnp.concatenate([v_t[:, t_new:], v_new.T], axis=1)
        outs_g.append(jnp.concatenate(outs, axis=1))
        lses_g.append(jnp.concatenate(lses, axis=1))
    a_ref[...] = _merge_groups(outs_g, lses_g)


def _sample_attn(z_qkv, cos_s, sin_s, caches_t, layer, new_prev, *, row0, n_seq, t_new):
    b0 = row0 // t_new

    def cache_spec(c):
        return pl.BlockSpec((1, 1) + c.shape[2:], lambda n: (layer, n, 0, 0, 0, 0))

    tab = pl.BlockSpec((t_new, 128), lambda n: (0, 0))
    in_specs = [pl.BlockSpec((t_new, 3 * ATT_W), lambda n: (b0 + n, 0)), tab, tab] + [cache_spec(c) for c in caches_t]
    args = [z_qkv, cos_s, sin_s, *caches_t]
    aliases = {}
    if new_prev is not None:
        for j, a in enumerate(new_prev):
            in_specs.append(pl.BlockSpec(memory_space=pl.ANY))
            args.append(a)
            aliases[len(args) - 1] = 1 + j
    return pl.pallas_call(
        functools.partial(_sample_attn_kernel, t_new=t_new),
        grid=(n_seq,), in_specs=in_specs,
        out_specs=[pl.BlockSpec((t_new, ATT_OUT), lambda n: (n, 0))] + [cache_spec(c) for c in caches_t],
        out_shape=[jax.ShapeDtypeStruct((n_seq * t_new, ATT_OUT), F32)]
        + [jax.ShapeDtypeStruct(c.shape, F32) for c in caches_t],
        input_output_aliases=aliases,
        compiler_params=_params("arbitrary"), name="sample_attn",
    )(*args)


def _pool_kernel(z_ref, h_ref, w_ref, s_ref, o_ref, buf_ref, *, t_len, pos0):
    buf_ref[0:POOL_HIST, :] = h_ref[0]
    buf_ref[POOL_HIST:POOL_HIST + t_len, :] = z_ref[...]
    pos = pos0 + lax.broadcasted_iota(I32, (t_len, POOL_GROUP), 0)
    for g, win in enumerate(POOL_WINDOWS):
        cs = slice(g * POOL_GROUP, (g + 1) * POOL_GROUP)
        z = buf_ref[POOL_HIST:POOL_HIST + t_len, cs]
        wsum = z
        for i in range(1, win):
            wsum = wsum + buf_ref[POOL_HIST - i:POOL_HIST - i + t_len, cs]
        cnt = jnp.minimum(win, pos + 1).astype(F32)
        y = wsum / cnt - z
        o_ref[:, cs] = _dot(y, w_ref[g]) * s_ref[:, cs]


def _pool_mixer(z_pool, hist, pool_w, pool_scale, *, row0, n_seq, t_len, pos0):
    b0 = row0 // t_len
    return pl.pallas_call(
        functools.partial(_pool_kernel, t_len=t_len, pos0=pos0),
        grid=(n_seq,),
        in_specs=[
            pl.BlockSpec((t_len, POOL_W), lambda n: (b0 + n, 0)),
            pl.BlockSpec((1, POOL_HIST, POOL_W), lambda n: (n, 0, 0)),
            pl.BlockSpec((len(POOL_WINDOWS), POOL_GROUP, POOL_GROUP), lambda n: (0, 0, 0)),
            pl.BlockSpec((1, POOL_W), lambda n: (0, 0)),
        ],
        out_specs=pl.BlockSpec((t_len, POOL_W), lambda n: (n, 0)),
        out_shape=jax.ShapeDtypeStruct((n_seq * t_len, POOL_W), F32),
        scratch_shapes=[pltpu.VMEM((POOL_HIST + t_len, POOL_W), F32)],
        compiler_params=_params("arbitrary"), name="pool_mixer",
    )(z_pool, hist, pool_w, pool_scale.reshape(1, POOL_W))


def _rwkv_prep_kernel(z_ref, zp_ref, first_ref, mu_ref, w0_ref, w2_ref, a0_ref, a2_ref, g2_ref, kk_ref, ka_ref, rk_ref,
                      r_o, lw_o, k_o, v_o, kn_o, b_o, g_o, bonus_o, buf_ref, *, tm):
    i = pl.program_id(1)
    z = z_ref[...]
    prev_row = jnp.where(i == 0, first_ref[0], zp_ref[SUBLANES - 1:SUBLANES, :])
    buf_ref[SUBLANES:SUBLANES + tm, :] = z
    buf_ref[SUBLANES - 1:SUBLANES, :] = prev_row
    shifted = buf_ref[SUBLANES - 1:SUBLANES - 1 + tm, :]
    xm = z + mu_ref[...] * (shifted - z)
    r = xm[:, 0:RWKV_W]
    k = xm[:, RWKV_W:2 * RWKV_W]
    v = xm[:, 2 * RWKV_W:3 * RWKV_W]
    wa = xm[:, RWKV_LORA_OFF:RWKV_LORA_OFF + 128]
    gl = xm[:, RWKV_LORA_OFF + 128:RWKV_COLS]
    xw = w0_ref[...] + _dot(jnp.tanh(wa), w2_ref[...])
    logw = -math.exp(-0.5) * jax.nn.sigmoid(xw)
    a = jax.nn.sigmoid(a0_ref[...] + _dot(wa, a2_ref[...]))
    g_o[...] = _dot(jax.nn.sigmoid(gl), g2_ref[...])
    kkr = k * kk_ref[...]
    kmod = k * (1.0 + (a - 1.0) * ka_ref[...])
    rkk = r * kmod * rk_ref[...]
    bonus = []
    for h in range(RWKV_HEADS):
        sl = slice(h * RWKV_HEAD, (h + 1) * RWKV_HEAD)
        kh = kkr[:, sl]
        nrm = jnp.sqrt(jnp.sum(kh * kh, axis=-1, keepdims=True))
        kn = kh / jnp.maximum(nrm, 1e-12)
        r_o[0, h] = r[:, sl]
        lw_o[0, h] = logw[:, sl]
        k_o[0, h] = kmod[:, sl]
        v_o[0, h] = v[:, sl]
        kn_o[0, h] = kn
        b_o[0, h] = kn * a[:, sl]
        bonus.append(jnp.sum(rkk[:, sl], axis=-1, keepdims=True) * v[:, sl])
    bonus_o[...] = jnp.concatenate(bonus, axis=1)


def _rwkv_prep(z, first_prev, lw, *, row0, n_seq, t_len, tm):
    nblk = t_len // tm
    pb = tm // SUBLANES
    b0 = row0 // tm
    p0 = row0 // SUBLANES

    def vec(n):
        return pl.BlockSpec((1, n), lambda s, i: (0, 0))

    hm = jax.ShapeDtypeStruct((n_seq, RWKV_HEADS, t_len, RWKV_HEAD), F32)
    rm = jax.ShapeDtypeStruct((n_seq * t_len, RWKV_W), F32)
    hm_spec = pl.BlockSpec((1, RWKV_HEADS, tm, RWKV_HEAD), lambda s, i: (s, 0, i, 0))
    rm_spec = pl.BlockSpec((tm, RWKV_W), lambda s, i: (s * nblk + i, 0))
    return pl.pallas_call(
        functools.partial(_rwkv_prep_kernel, tm=tm),
        grid=(n_seq, nblk),
        in_specs=[
            pl.BlockSpec((tm, RWKV_COLS), lambda s, i: (b0 + s * nblk + i, 0)),
            pl.BlockSpec((SUBLANES, RWKV_COLS), lambda s, i: (jnp.maximum(p0 + (s * nblk + i) * pb - 1, 0), 0)),
            pl.BlockSpec((1, 1, RWKV_COLS), lambda s, i: (s, 0, 0)),
            vec(RWKV_COLS), vec(RWKV_W),
            pl.BlockSpec((128, RWKV_W), lambda s, i: (0, 0)),
            vec(RWKV_W),
            pl.BlockSpec((128, RWKV_W), lambda s, i: (0, 0)),
            pl.BlockSpec((128, RWKV_W), lambda s, i: (0, 0)),
            vec(RWKV_W), vec(RWKV_W), vec(RWKV_W),
        ],
        out_specs=[hm_spec] * 6 + [rm_spec] * 2,
        out_shape=[hm] * 6 + [rm] * 2,
        scratch_shapes=[pltpu.VMEM((tm + SUBLANES, RWKV_COLS), F32)],
        compiler_params=_params("arbitrary", "arbitrary"), name="rwkv_prep",
    )(z, z, first_prev, lw["mu"], lw["w0"], lw["w2p"], lw["a0"], lw["a2p"], lw["g2"], lw["kk"], lw["ka"], lw["rk"])


def _bmm(a, b):
    return jnp.einsum("hqk,hkd->hqd", a.astype(BF16), b.astype(BF16), preferred_element_type=F32)


def _bmm_nt(a, b):
    return jnp.einsum("hqd,hkd->hqk", a.astype(BF16), b.astype(BF16), preferred_element_type=F32)


def _bmm_tn(a, b):
    return jnp.einsum("hkq,hkd->hqd", a.astype(BF16), b.astype(BF16), preferred_element_type=F32)


def _rwkv_scan_kernel(r_ref, lw_ref, k_ref, v_ref, kn_ref, b_ref, s0_ref, y_ref, st_ref, s_scr, *, chunk):
    c = pl.program_id(1)
    nh = RWKV_HEADS

    @pl.when(c == 0)
    def _():
        s_scr[...] = s0_ref[0]

    r, logw, k, v, kn, b = r_ref[0], lw_ref[0], k_ref[0], v_ref[0], kn_ref[0], b_ref[0]
    row = lax.broadcasted_iota(I32, (chunk, chunk), 0)
    col = lax.broadcasted_iota(I32, (chunk, chunk), 1)
    incl = row >= col
    strict = row > col
    tri = jnp.broadcast_to(incl.astype(BF16)[None], (nh, chunk, chunk))
    lw_hi = logw.astype(BF16)
    lw_lo = (logw - lw_hi.astype(F32)).astype(BF16)
    cum = (jnp.einsum("hqk,hkd->hqd", tri, lw_hi, preferred_element_type=F32)
           + jnp.einsum("hqk,hkd->hqd", tri, lw_lo, preferred_element_type=F32))
    p_inv = jnp.exp(-cum)
    kt = k * p_inv
    bt = b * p_inv
    kap = kn * jnp.exp(cum - logw)
    rho = r * jnp.exp(cum)
    qq = jnp.concatenate([kap, rho], axis=1)
    gram = _bmm_nt(qq, jnp.concatenate([kt, bt], axis=1))
    a_k = jnp.where(strict[None], gram[:, :chunk, :chunk], 0.0)
    a_b = jnp.where(strict[None], gram[:, :chunk, chunk:], 0.0)
    l_k = jnp.where(incl[None], gram[:, chunk:, :chunk], 0.0)
    l_b = jnp.where(incl[None], gram[:, chunk:, chunk:], 0.0)
    x = jnp.broadcast_to((row == col).astype(F32)[None], (nh, chunk, chunk))
    m = 1
    while m < chunk:
        sh = m.bit_length() - 1
        same = (row >> (sh + 1)) == (col >> (sh + 1))
        lower_left = jnp.logical_and(((row >> sh) & 1) == 1, ((col >> sh) & 1) == 0)
        off = jnp.where(jnp.logical_and(same, lower_left)[None], a_b, 0.0)
        x = x - _bmm(_bmm(x, off), x)
        m *= 2
    s = s_scr[...]
    qs = _bmm_nt(qq, s)
    u = _bmm(x, -(qs[:, :chunk] + _bmm(a_k, v)))
    y = qs[:, chunk:] + _bmm(jnp.concatenate([l_k, l_b], axis=2), jnp.concatenate([v, u], axis=1))
    s_new = (s + _bmm_tn(jnp.concatenate([v, u], axis=1), jnp.concatenate([kt, bt], axis=1))) * jnp.exp(cum[:, chunk - 1:chunk, :])
    s_scr[...] = s_new
    st_ref[0] = s_new
    mean = jnp.mean(y, axis=-1, keepdims=True)
    var = jnp.mean(jnp.square(y - mean), axis=-1, keepdims=True)
    yn = (y - mean) * lax.rsqrt(var + GN_EPS)
    y_ref[0] = jnp.concatenate([yn[h] for h in range(nh)], axis=1)


def _rwkv_scan(prep, s0, *, chunk):
    r, lw, k, v, kn, b = prep
    n_seq, nh, t_len, hd = r.shape
    hm_spec = pl.BlockSpec((1, nh, chunk, hd), lambda s, c: (s, 0, c, 0))
    st_spec = pl.BlockSpec((1, nh, hd, hd), lambda s, c: (s, 0, 0, 0))
    return pl.pallas_call(
        functools.partial(_rwkv_scan_kernel, chunk=chunk),
        grid=(n_seq, t_len // chunk),
        in_specs=[hm_spec] * 6 + [st_spec],
        out_specs=[pl.BlockSpec((1, chunk, RWKV_W), lambda s, c: (s, c, 0)), st_spec],
        out_shape=[jax.ShapeDtypeStruct((n_seq, t_len, RWKV_W), F32), jax.ShapeDtypeStruct((n_seq, nh, hd, hd), F32)],
        scratch_shapes=[pltpu.VMEM((nh, hd, hd), F32)],
        compiler_params=_params("arbitrary", "arbitrary"), name="rwkv_scan",
    )(r, lw, k, v, kn, b, s0)


def _mix_kernel(*refs, n_prompt_blocks):
    pairs, (gate_ref, lnw_ref, lnb_ref, wa_ref, wp_ref, wr_ref, o_ref) = refs[:10], refs[10:]
    is_sample = pl.program_id(0) >= n_prompt_blocks
    att, pool, yn, bonus, g = [jnp.where(is_sample, pairs[2 * j + 1][...], pairs[2 * j][...]) for j in range(5)]
    rw = (yn * lnw_ref[...] + lnb_ref[...] + bonus) * g
    mix = jax.nn.sigmoid(gate_ref[:, 0:D_MODEL]) * _dot(att, wa_ref[...])
    mix = mix + jax.nn.sigmoid(gate_ref[:, D_MODEL:2 * D_MODEL]) * _dot(pool, wp_ref[...])
    mix = mix + jax.nn.sigmoid(gate_ref[:, 2 * D_MODEL:3 * D_MODEL]) * _dot(rw, wr_ref[...])
    o_ref[...] = mix.astype(BF16)


def _mix(branches, z_gate, ln_w, ln_b, wa, wp, wr, *, tm):
    rows = z_gate.shape[0]
    n_prompt_blocks = branches[0][0].shape[0] // tm
    assert rows == (n_prompt_blocks + 1) * tm and all(s.shape[0] == tm for _, s in branches)

    def full(a):
        return pl.BlockSpec(a.shape, lambda i: (0, 0))

    specs, args = [], []
    for p, s in branches:
        specs += [pl.BlockSpec((tm, p.shape[1]), lambda i: (jnp.minimum(i, n_prompt_blocks - 1), 0)), full(s)]
        args += [p, s]
    ln_w = ln_w.reshape(1, RWKV_W)
    ln_b = ln_b.reshape(1, RWKV_W)
    return pl.pallas_call(
        functools.partial(_mix_kernel, n_prompt_blocks=n_prompt_blocks), grid=(rows // tm,),
        in_specs=specs + [pl.BlockSpec((tm, 3 * D_MODEL), lambda i: (i, 0)),
                          full(ln_w), full(ln_b), full(wa), full(wp), full(wr)],
        out_specs=pl.BlockSpec((tm, D_MODEL), lambda i: (i, 0)),
        out_shape=jax.ShapeDtypeStruct((rows, D_MODEL), BF16),
        compiler_params=_params("arbitrary"), name="gated_mix",
    )(*args, z_gate, ln_w, ln_b, wa, wp, wr)


def _mm_res_kernel(x_ref, a_ref, w_ref, o_ref):
    o_ref[...] = x_ref[...] + jnp.dot(a_ref[...], w_ref[...], preferred_element_type=F32)


def _mm_residual(x, a, w, *, tm):
    rows, n = x.shape
    k = a.shape[1]
    return pl.pallas_call(
        _mm_res_kernel, grid=(rows // tm,),
        in_specs=[pl.BlockSpec((tm, n), lambda i: (i, 0)), pl.BlockSpec((tm, k), lambda i: (i, 0)),
                  pl.BlockSpec((k, n), lambda i: (0, 0))],
        out_specs=pl.BlockSpec((tm, n), lambda i: (i, 0)),
        out_shape=jax.ShapeDtypeStruct((rows, n), F32),
        compiler_params=_params("arbitrary"), name="out_proj",
    )(x, a, w)


def _topk_cols(scores, ids, k):
    lanes = scores[0].shape[1]
    iota_k = lax.broadcasted_iota(I32, (k, lanes), 0)
    big = jnp.iinfo(jnp.int32).max

    def body(j, carry):
        out = []
        for (s, vals, idxs), ident in zip(carry, ids):
            m = jnp.max(s, axis=0, keepdims=True)
            idx = jnp.min(jnp.where(s == m, ident, big), axis=0, keepdims=True)
            vals = jnp.where(iota_k == j, m, vals)
            idxs = jnp.where(iota_k == j, idx, idxs)
            out.append((jnp.where(ident == idx, NEG_INF, s), vals, idxs))
        return tuple(out)

    init = tuple((s, jnp.zeros((k, lanes), F32), jnp.zeros((k, lanes), I32)) for s in scores)
    return [(v, i) for _, v, i in lax.fori_loop(0, k, body, init)]


def _gather_rows(table, sel, k):
    out = jnp.zeros(sel.shape, table.dtype)
    for a in range(k):
        out = jnp.where(sel == a, table[a:a + 1, :], out)
    return out


def _peer_select_kernel(q_ref, skh_ref, skl_ref, i1_o, i2_o, gate_o, i1_s, i2_s, g_s, *, tok):
    kk = PEER_TOPK
    half = kk // 2
    nt = (((1,), (1,)), ((), ()))
    key_id = lax.broadcasted_iota(I32, (N_KEYS, tok), 0)
    row = lax.broadcasted_iota(I32, (half * half + kk, tok), 0)
    cand_id = jnp.where(row < half * half, (row >> (half.bit_length() - 1)) * kk + (row & (half - 1)),
                        jnp.where(row < half * half + half, row - half * half + half, (row - half * half) * kk))

    def head(h, carry):
        c0 = pl.multiple_of(h * 2 * N_KEYS, 2 * N_KEYS)
        q1h, q1l = _split_bf16(q_ref[:, pl.ds(c0, N_KEYS)])
        q2h, q2l = _split_bf16(q_ref[:, pl.ds(c0 + N_KEYS, N_KEYS)])
        s1 = _dot3(skh_ref[h, 0], skl_ref[h, 0], q1h, q1l, nt)
        s2 = _dot3(skh_ref[h, 1], skl_ref[h, 1], q2h, q2l, nt)
        (t1, k1), (t2, k2) = _topk_cols([s1, s2], [key_id, key_id], kk)
        cand = jnp.concatenate([t1[a:a + 1, :] + t2[0:half, :] for a in range(half)]
                               + [t1[0:1, :] + t2[half:kk, :], t1[half:kk, :] + t2[0:1, :]], axis=0)
        ((top, sel),) = _topk_cols([cand], [cand_id], kk)
        e1 = _gather_rows(k1, sel >> 4, kk)
        e2 = _gather_rows(k2, sel & (kk - 1), kk)
        ex = jnp.exp(top - jnp.max(top, axis=0, keepdims=True))
        gate = ex / jnp.sum(ex, axis=0, keepdims=True)
        r0 = pl.multiple_of(h * kk, kk)
        i1_s[pl.ds(r0, kk), :] = e1.astype(F32)
        i2_s[pl.ds(r0, kk), :] = e2.astype(F32)
        g_s[pl.ds(r0, kk), :] = gate
        return carry

    lax.fori_loop(0, PEER_HEADS, head, 0)
    i1_o[...] = i1_s[...].T.astype(I32)
    i2_o[...] = i2_s[...].T.astype(I32)
    gate_o[...] = g_s[...].T


def _peer_select(q, sk_hi, sk_lo, *, tok=128):
    rows = q.shape[0]
    spec = pl.BlockSpec((tok, PEER_PAIRS), lambda i: (i, 0))
    return pl.pallas_call(
        functools.partial(_peer_select_kernel, tok=tok), grid=(rows // tok,),
        in_specs=[pl.BlockSpec((tok, q.shape[1]), lambda i: (i, 0)),
                  pl.BlockSpec(sk_hi.shape, lambda i: (0, 0, 0, 0)), pl.BlockSpec(sk_lo.shape, lambda i: (0, 0, 0, 0))],
        out_specs=[spec] * 3,
        out_shape=[jax.ShapeDtypeStruct((rows, PEER_PAIRS), I32), jax.ShapeDtypeStruct((rows, PEER_PAIRS), I32),
                   jax.ShapeDtypeStruct((rows, PEER_PAIRS), F32)],
        scratch_shapes=[pltpu.VMEM((PEER_PAIRS, tok), F32)] * 3,
        compiler_params=_params("arbitrary"), name="peer_select",
    )(q, sk_hi, sk_lo)


def _peer_pick(d_ref, i1_ref, i2_ref, act_ref, first_row):
    n_rows = d_ref.shape[1] // N_KEYS
    for g in range(act_ref.shape[0] // SUBLANES):
        rs = slice(g * SUBLANES, (g + 1) * SUBLANES)
        i1 = i1_ref[rs, :]
        i2 = i2_ref[rs, :]
        acc = act_ref[rs, :]
        for j in range(n_rows):
            got = jnp.take_along_axis(d_ref[rs, j * N_KEYS:(j + 1) * N_KEYS], i2, axis=1, mode="promise_in_bounds")
            acc = acc + jnp.where(i1 == first_row + j, got, 0.0)
        act_ref[rs, :] = acc


def _peer_act_kernel(xn_ref, u_ref, i1_ref, i2_ref, act_ref, da_ref, db_ref):
    first = jnp.logical_and(pl.program_id(0) == 0, pl.program_id(1) == 0)
    c = pl.program_id(1)
    slab = PEER_EC // 2
    slab_rows = slab // N_KEYS
    nt = (((1,), (1,)), ((), ()))

    @pl.when(first)
    def _():
        db_ref[...] = jnp.zeros(db_ref.shape, F32)

    @pl.when(c == 0)
    def _():
        act_ref[...] = jnp.zeros(act_ref.shape, F32)

    base = c * 2 * slab_rows
    _peer_pick(db_ref, i1_ref, i2_ref, act_ref, base - slab_rows)
    da_ref[...] = lax.dot_general(xn_ref[...], u_ref[0, 0:slab, :].astype(BF16), nt, preferred_element_type=F32)
    _peer_pick(da_ref, i1_ref, i2_ref, act_ref, base)
    db_ref[...] = lax.dot_general(xn_ref[...], u_ref[0, slab:2 * slab, :].astype(BF16), nt, preferred_element_type=F32)

    @pl.when(c == pl.num_programs(1) - 1)
    def _():
        _peer_pick(db_ref, i1_ref, i2_ref, act_ref, base + slab_rows)


def _peer_act(xn, u_tabs, layer, i1, i2, *, tb):
    rows = xn.shape[0]
    pair_spec = pl.BlockSpec((tb, PEER_PAIRS), lambda i, c: (i, 0))
    return pl.pallas_call(
        _peer_act_kernel, grid=(rows // tb, N_EXPERTS // PEER_EC),
        in_specs=[pl.BlockSpec((tb, D_MODEL), lambda i, c: (i, 0)),
                  pl.BlockSpec((1, PEER_EC, D_MODEL), lambda i, c: (layer, c, 0)), pair_spec, pair_spec],
        out_specs=pair_spec,
        out_shape=jax.ShapeDtypeStruct((rows, PEER_PAIRS), F32),
        scratch_shapes=[pltpu.VMEM((tb, PEER_EC // 2), F32)] * 2,
        compiler_params=_params("arbitrary", "arbitrary"), name="peer_act",
    )(xn, u_tabs, i1, i2)


def _peer_out_kernel(i1_ref, i2_ref, gate_ref, act_ref, v_ref, o_ref, w_ref):
    c = pl.program_id(1)
    n_groups = w_ref.shape[0]
    rows_per_chunk = PEER_EC // N_KEYS

    @pl.when(c == 0)
    def _():
        key_iota = lax.broadcasted_iota(I32, (N_KEYS, PEER_PAIRS), 0)

        def group(g, carry):
            r0 = pl.multiple_of(g * SUBLANES, SUBLANES)
            i1 = i1_ref[pl.ds(r0, SUBLANES), :]
            i2 = i2_ref[pl.ds(r0, SUBLANES), :]
            a = act_ref[pl.ds(r0, SUBLANES), :]
            wgt = gate_ref[pl.ds(r0, SUBLANES), :] * (0.5 * a * (1.0 + lax.erf(a * (1.0 / math.sqrt(2.0)))))
            for t in range(SUBLANES):
                hit1 = key_iota == jnp.broadcast_to(i1[t:t + 1, :], key_iota.shape)
                hit2 = key_iota == jnp.broadcast_to(i2[t:t + 1, :], key_iota.shape)
                w2 = jnp.where(hit2, jnp.broadcast_to(wgt[t:t + 1, :], key_iota.shape), 0.0)
                w_tok = _dot_nt(jnp.where(hit1, 1.0, 0.0), w2)
                w_ref[g, pl.ds(t, N_KEYS, stride=SUBLANES), :] = w_tok
            return carry

        lax.fori_loop(0, n_groups, group, 0)

    row0 = c * (rows_per_chunk * SUBLANES)
    lhs = jnp.concatenate(
        [w_ref[:, pl.ds(pl.multiple_of(row0 + j * SUBLANES, SUBLANES), SUBLANES), :].reshape(n_groups * SUBLANES, N_KEYS)
         for j in range(rows_per_chunk)], axis=1)
    part = jnp.dot(lhs.astype(BF16), v_ref[0], preferred_element_type=F32)

    @pl.when(c == 0)
    def _():
        o_ref[...] = part

    @pl.when(c > 0)
    def _():
        o_ref[...] += part


def _peer_out(i1, i2, gate, act, v_tabs, layer):
    rows = i1.shape[0]
    tb = PEER_TB
    pair_spec = pl.BlockSpec((tb, PEER_PAIRS), lambda i, c: (i, 0))
    return pl.pallas_call(
        _peer_out_kernel, grid=(rows // tb, N_EXPERTS // PEER_EC),
        in_specs=[pair_spec] * 4 + [pl.BlockSpec((1, PEER_EC, D_MODEL), lambda i, c: (layer, c, 0))],
        out_specs=pl.BlockSpec((tb, D_MODEL), lambda i, c: (i, 0)),
        out_shape=jax.ShapeDtypeStruct((rows, D_MODEL), F32),
        scratch_shapes=[pltpu.VMEM((tb // SUBLANES, N_KEYS * SUBLANES, N_KEYS), F32)],
        compiler_params=_params("arbitrary", "arbitrary"), name="peer_out",
    )(i1, i2, gate, act, v_tabs)


def _ple_kernel(x_ref, f_ref, p_ref, g_ref, wg_ref, wp_ref, gf_ref, o_ref, *, final):
    x = x_ref[...] + f_ref[...]
    ms = jnp.mean(x * x, axis=-1, keepdims=True)
    h = (x * lax.rsqrt(ms + RMS_EPS)) * g_ref[...]
    out = x + jax.nn.sigmoid(_dot(h, wg_ref[...])) * _dot(p_ref[...], wp_ref[...])
    if final:
        ms2 = jnp.mean(out * out, axis=-1, keepdims=True)
        out = (out * lax.rsqrt(ms2 + RMS_EPS)) * gf_ref[...]
    o_ref[...] = out


def _ple(x, ffn, p, g, wg, wp, g_final, *, tm, final):
    rows, n = x.shape
    row = pl.BlockSpec((tm, n), lambda i: (i, 0))
    vec = pl.BlockSpec((1, n), lambda i: (0, 0))
    return pl.pallas_call(
        functools.partial(_ple_kernel, final=final), grid=(rows // tm,),
        in_specs=[row, row, pl.BlockSpec((tm, p.shape[1]), lambda i: (i, 0)), vec,
                  pl.BlockSpec(wg.shape, lambda i: (0, 0)), pl.BlockSpec(wp.shape, lambda i: (0, 0)), vec],
        out_specs=row,
        out_shape=jax.ShapeDtypeStruct((rows, n), F32),
        compiler_params=_params("arbitrary"), name="ple",
    )(x, ffn, p, g.reshape(1, n), wg, wp, g_final.reshape(1, n))


def _layer(x, p_rows, lw, state, g_final, dims, final):
    n_p, t_p, n_s, t_s, past = dims
    rows_p = n_p * t_p
    rows_s = n_s * t_s

    z_qkv = _norm_mm(x, lw["g_mix"], lw["w_qkv"], tm=768, tn=768)
    z_pool = _norm_mm(x, lw["g_mix"], lw["w_pool"], tm=768, tn=POOL_W)
    z_rwkv = _norm_mm(x, lw["g_mix"], lw["w_rwkv"], tm=768, tn=RWKV_COLS // 2)
    z_gate = _norm_mm(x, lw["g_mix"], lw["w_gate"], tm=768, tn=768)

    layer, depth = lw["layer"], lw["depth"]
    os_, ls_, kv_p = [], [], []
    for gi, (_, dil) in enumerate(ATT_GROUPS):
        o, l, kv = _prompt_attn(z_qkv, lw["cos_p"], lw["sin_p"], gi, dil, layer,
                                None if state["kv_p"] is None else state["kv_p"][gi], depth=depth, n_seq=n_p, seq=t_p)
        os_.append(o)
        ls_.append(l)
        kv_p.append(kv)
    att_p = _merge_attn(os_, ls_, tm=512)
    att_s, *kv_s = _sample_attn(z_qkv, lw["cos_s"], lw["sin_s"], state["caches_t"], layer, state["kv_s"],
                                row0=rows_p, n_seq=n_s, t_new=t_s)

    pool_p = _pool_mixer(z_pool, jnp.zeros((n_p, POOL_HIST, POOL_W), F32), lw["pool_w"], lw["pool_scale"],
                         row0=0, n_seq=n_p, t_len=t_p, pos0=0)
    hist_s = jnp.concatenate([jnp.zeros((n_s, 1, POOL_W), F32), state["pool"]], axis=1)
    pool_s = _pool_mixer(z_pool, hist_s, lw["pool_w"], lw["pool_scale"], row0=rows_p, n_seq=n_s, t_len=t_s, pos0=past)

    prep_p = _rwkv_prep(z_rwkv, jnp.zeros((n_p, 1, RWKV_COLS), F32), lw, row0=0, n_seq=n_p, t_len=t_p, tm=256)
    prep_s = _rwkv_prep(z_rwkv, state["shift"][:, None, :], lw, row0=rows_p, n_seq=n_s, t_len=t_s, tm=t_s)
    yn_p, wkv_p = _rwkv_scan(prep_p[:6], jnp.zeros((n_p, RWKV_HEADS, RWKV_HEAD, RWKV_HEAD), F32), chunk=RWKV_CHUNK)
    yn_s, wkv_s = _rwkv_scan(prep_s[:6], state["wkv"], chunk=t_s)
    branches = [(att_p, att_s), (pool_p, pool_s), (yn_p.reshape(rows_p, RWKV_W), yn_s.reshape(rows_s, RWKV_W)),
                (prep_p[7], prep_s[7]), (prep_p[6], prep_s[6])]

    mix = _mix(branches, z_gate, lw["ln_w"], lw["ln_b"], lw["w_attn_o"], lw["w_pool_o"], lw["w_rwkv_o"], tm=rows_s)
    x = _mm_residual(x, mix, lw["w_out"], tm=384)

    q, xn = _norm_mm3(x, lw["g_ffn"], *lw["peer_wq"], tm=768, tn=512)
    i1, i2, gate = _peer_select(q, *lw["peer_subkeys"])
    act = _peer_act(xn, lw["peer_u"], lw["layer"], i1, i2, tb=768)
    ffn = _peer_out(i1, i2, gate, act, lw["peer_v"], lw["layer"])

    x = _ple(x, ffn, p_rows, lw["g_ple"], lw["ple_wg"], lw["ple_wp"], g_final, tm=256, final=final)

    keep = POOL_HIST - 1
    pool_p_state = jnp.stack([z_pool[(n + 1) * t_p - keep:(n + 1) * t_p] for n in range(n_p)])
    zp_s = z_pool[rows_p:].reshape(n_s, t_s, POOL_W)
    new_p = [pool_p_state, wkv_p, z_rwkv[t_p - 1:rows_p:t_p]]
    new_s = [jnp.concatenate([hist_s[:, 1:], zp_s], axis=1)[:, -keep:], wkv_s, z_rwkv[rows_p + t_s - 1::t_s]]
    return x, kv_p, kv_s, new_p, new_s


def kernel(x_prompt, x_sample, p_prompt, p_sample, cache_attn_w128, cache_attn_w512, cache_attn_w2048, state_pool, state_rwkv_wkv, state_rwkv_shift, g_mix, w_in, w_attn_o, w_pool_o, w_rwkv_o, w_out, pool_w, pool_scale, rwkv_mu, rwkv_w0, rwkv_w2, rwkv_a0, rwkv_a2, rwkv_g2, rwkv_kk, rwkv_ka, rwkv_rk, rwkv_ln_w, rwkv_ln_b, g_ffn, peer_wq, peer_subkeys, peer_u, peer_v, g_ple, ple_wg, ple_wp, g_final):
    n_p, t_p, _ = x_prompt.shape
    n_s, t_s, _ = x_sample.shape
    depth = w_in.shape[0]
    past = PAST_LEN
    rows_p = n_p * t_p
    rows_s = n_s * t_s
    dims = (n_p, t_p, n_s, t_s, past)

    x = jnp.concatenate([x_prompt.reshape(rows_p, D_MODEL), x_sample.reshape(rows_s, D_MODEL)], axis=0)
    cos_p, sin_p = _rope_tables(jnp.arange(t_p, dtype=I32))
    cos_s, sin_s = _rope_tables(past + jnp.arange(t_s, dtype=I32))
    lora_pad = jnp.zeros((128 - 64, RWKV_W), F32)
    peer_v_bf16 = peer_v.astype(BF16)
    to_dev = (0, 1, 3, 4, 5, 2)
    from_dev = (0, 1, 5, 2, 3, 4)
    caches_t = [jnp.transpose(c, to_dev) for c in (cache_attn_w128, cache_attn_w512, cache_attn_w2048)]

    def row(a):
        return a.reshape(1, -1)

    new_p, new_s = [], []
    kv_p = kv_s = None
    for l in range(depth):
        wl = w_in[l]
        lw = {
            "g_mix": g_mix[l],
            "w_qkv": wl[:, :OFF_POOL].astype(BF16), "w_pool": wl[:, OFF_POOL:OFF_RWKV].astype(BF16),
            "w_rwkv": wl[:, OFF_RWKV:OFF_GATE].astype(BF16), "w_gate": wl[:, OFF_GATE:].astype(BF16),
            "cos_p": cos_p, "sin_p": sin_p, "cos_s": cos_s, "sin_s": sin_s, "depth": depth,
            "pool_w": pool_w[l].astype(BF16), "pool_scale": pool_scale[l],
            "mu": row(rwkv_mu[l]), "w0": row(rwkv_w0[l]), "a0": row(rwkv_a0[l]),
            "w2p": jnp.concatenate([rwkv_w2[l], lora_pad], axis=0).astype(BF16),
            "a2p": jnp.concatenate([lora_pad, rwkv_a2[l]], axis=0).astype(BF16),
            "g2": rwkv_g2[l].astype(BF16),
            "kk": row(rwkv_kk[l]), "ka": row(rwkv_ka[l]), "rk": row(rwkv_rk[l]),
            "ln_w": rwkv_ln_w[l], "ln_b": rwkv_ln_b[l],
            "w_attn_o": w_attn_o[l].astype(BF16), "w_pool_o": w_pool_o[l].astype(BF16),
            "w_rwkv_o": w_rwkv_o[l].astype(BF16), "w_out": w_out[l].astype(BF16),
            "g_ffn": g_ffn[l], "peer_wq": _split_bf16(peer_wq[l]), "peer_subkeys": _split_bf16(peer_subkeys[l]),
            "peer_u": peer_u, "peer_v": peer_v_bf16, "layer": l,
            "g_ple": g_ple[l], "ple_wg": ple_wg[l].astype(BF16), "ple_wp": ple_wp[l].astype(BF16),
        }
        state = {"caches_t": caches_t, "kv_p": kv_p, "kv_s": kv_s, "pool": state_pool[l],
                 "wkv": state_rwkv_wkv[l], "shift": state_rwkv_shift[l]}
        p_rows = jnp.concatenate([p_prompt[l].reshape(rows_p, -1), p_sample[l].reshape(rows_s, -1)], axis=0).astype(BF16)
        x, kv_p, kv_s, st_p, st_s = _layer(x, p_rows, lw, state, g_final, dims, l == depth - 1)
        new_p.append(st_p)
        new_s.append(st_s)

    outs = [x[:rows_p].reshape(n_p, t_p, D_MODEL), x[rows_p:].reshape(n_s, t_s, D_MODEL)]
    for gi in range(len(ATT_GROUPS)):
        outs.append(jnp.transpose(kv_p[gi], from_dev))
        outs.append(jnp.transpose(kv_s[gi], from_dev))
    for j in range(3):
        outs.append(jnp.stack([s[j] for s in new_p]))
        outs.append(jnp.stack([s[j] for s in new_s]))
    return tuple(outs)
```

```python
import functools
import math

import jax
import jax.numpy as jnp
from jax import lax
from jax.experimental import pallas as pl
from jax.experimental.pallas import tpu as pltpu

F32 = jnp.float32
BF16 = jnp.bfloat16
I32 = jnp.int32

D_MODEL = 2048
RMS_EPS = 1e-6
HEAD_DIM = 64
ATT_GROUPS = ((128, 1), (512, 4), (2048, 16))
HEADS_PER_GROUP = 4
ATT_W = 768
ATT_OUT = 256
ATT_SPAN = 128
ROPE_THETA = 10000.0
POOL_WINDOWS = (2, 4, 8, 16)
POOL_GROUP = 128
POOL_W = 512
POOL_HIST = 16
RWKV_HEAD = 64
RWKV_HEADS = 12
RWKV_W = 768
RWKV_COLS = 2560
RWKV_LORA_OFF = 2304
GN_EPS = 64e-5
RWKV_CHUNK = 64
RWKV_SEQ_GROUP = 4
OFF_POOL = 2304
OFF_RWKV = 2816
OFF_GATE = 5376
PAST_LEN = 8192
PEER_HEADS = 8
N_KEYS = 128
N_EXPERTS = N_KEYS * N_KEYS
PEER_TOPK = 16
PEER_PAIRS = PEER_HEADS * PEER_TOPK
PEER_TB = 384
PEER_ACT_TB = 1408
PEER_EC = 1024
SUBLANES = 8
VMEM_LIMIT = 56 * 1024 * 1024

NEG_INF = float("-inf")


def _params(*sem):
    return pltpu.CompilerParams(dimension_semantics=sem, vmem_limit_bytes=VMEM_LIMIT)


def _dot(a, b):
    return jnp.dot(a.astype(BF16), b.astype(BF16), preferred_element_type=F32)


def _dot_nt(a, b):
    return lax.dot_general(a.astype(BF16), b.astype(BF16), (((1,), (1,)), ((), ())), preferred_element_type=F32)


def _norm_mm_kernel(x_ref, g_ref, w_ref, o_ref, xn_ref):
    @pl.when(pl.program_id(1) == 0)
    def _():
        x = x_ref[...]
        ms = jnp.mean(x * x, axis=-1, keepdims=True)
        xn_ref[...] = ((x * lax.rsqrt(ms + RMS_EPS)) * g_ref[...]).astype(BF16)

    o_ref[...] = jnp.dot(xn_ref[...], w_ref[...], preferred_element_type=F32)


def _norm_mm(x, g, w, *, tm, tn):
    rows, k = x.shape
    n = w.shape[1]
    return pl.pallas_call(
        _norm_mm_kernel, grid=(rows // tm, n // tn),
        in_specs=[
            pl.BlockSpec((tm, k), lambda i, j: (i, 0)),
            pl.BlockSpec((1, k), lambda i, j: (0, 0)),
            pl.BlockSpec((k, tn), lambda i, j: (0, j)),
        ],
        out_specs=pl.BlockSpec((tm, tn), lambda i, j: (i, j)),
        out_shape=jax.ShapeDtypeStruct((rows, n), F32),
        scratch_shapes=[pltpu.VMEM((tm, k), BF16)],
        compiler_params=_params("arbitrary", "arbitrary"), name="norm_mm",
    )(x, g.reshape(1, k), w)


def _split_bf16(a):
    hi = a.astype(BF16)
    lo = (a - hi.astype(F32)).astype(BF16)
    return hi, lo


def _dot3(ah, al, bh, bl, dims=(((1,), (0,)), ((), ()))):
    def d(p, q):
        return lax.dot_general(p, q, dims, preferred_element_type=F32)

    return d(ah, bh) + (d(ah, bl) + d(al, bh))


def _norm_mm3_kernel(x_ref, g_ref, wh_ref, wl_ref, o_ref, xh_ref, xl_ref):
    @pl.when(pl.program_id(1) == 0)
    def _():
        x = x_ref[...]
        ms = jnp.mean(x * x, axis=-1, keepdims=True)
        xh, xl = _split_bf16((x * lax.rsqrt(ms + RMS_EPS)) * g_ref[...])
        xh_ref[...] = xh
        xl_ref[...] = xl

    o_ref[...] = _dot3(xh_ref[...], xl_ref[...], wh_ref[...], wl_ref[...])


def _norm_mm3(x, g, wh, wl, *, tm, tn):
    rows, k = x.shape
    n = wh.shape[1]
    wspec = pl.BlockSpec((k, tn), lambda i, j: (0, j))
    return pl.pallas_call(
        _norm_mm3_kernel, grid=(rows // tm, n // tn),
        in_specs=[pl.BlockSpec((tm, k), lambda i, j: (i, 0)), pl.BlockSpec((1, k), lambda i, j: (0, 0)), wspec, wspec],
        out_specs=[pl.BlockSpec((tm, tn), lambda i, j: (i, j)), pl.BlockSpec((tm, k), lambda i, j: (i, 0))],
        out_shape=[jax.ShapeDtypeStruct((rows, n), F32), jax.ShapeDtypeStruct((rows, k), BF16)],
        scratch_shapes=[pltpu.VMEM((tm, k), BF16)],
        compiler_params=_params("arbitrary", "arbitrary"), name="norm_mm3",
    )(x, g.reshape(1, k), wh, wl)


def _rotate(x, cos, sin):
    lane = lax.broadcasted_iota(I32, x.shape, 1)
    first_half = (lane & (HEAD_DIM - 1)) < HEAD_DIM // 2
    partner = jnp.where(first_half, pltpu.roll(x, 128 - HEAD_DIM // 2, 1), pltpu.roll(x, HEAD_DIM // 2, 1))
    return x * cos + partner * sin


def _rope_tables(pos):
    half = HEAD_DIM // 2
    inv = ROPE_THETA ** (-jnp.arange(half, dtype=F32) / half)
    ang = pos.astype(F32)[:, None] * inv[None, :]
    cos = jnp.cos(ang)
    sin = jnp.sin(ang)
    cos_t = jnp.concatenate([cos, cos, cos, cos], axis=1)
    sin_t = jnp.concatenate([-sin, sin, -sin, sin], axis=1)
    return cos_t, sin_t


ROT_ROWS = 256


def _prompt_attn_kernel(q_ref, k_ref, v_ref, cos_ref, sin_ref, *rest, dil, seq, keep):
    o_ref, l_ref, kv_ref, qs, ks, vs = rest[-6:]
    chunks = ATT_OUT // 128
    for i in range(seq // ROT_ROWS):
        rs = slice(i * ROT_ROWS, (i + 1) * ROT_ROWS)
        cos = cos_ref[rs, :]
        sin = sin_ref[rs, :]
        for c in range(chunks):
            cs = slice(c * 128, (c + 1) * 128)
            qs[c, rs, :] = _rotate(q_ref[rs, cs], cos, sin)
            ks[c, rs, :] = _rotate(k_ref[rs, cs], cos, sin)
            vs[c, rs, :] = v_ref[rs, cs]
    for h in range(HEADS_PER_GROUP):
        c, half = divmod(h, 2)
        hs = slice(half * HEAD_DIM, (half + 1) * HEAD_DIM)
        for j in range(keep // 128):
            ps = slice(seq - keep + j * 128, seq - keep + (j + 1) * 128)
            kv_ref[0, 0, 0, h, :, j * 128:(j + 1) * 128] = ks[c, ps, hs].T
            kv_ref[0, 0, 1, h, :, j * 128:(j + 1) * 128] = vs[c, ps, hs].T

    def rows_of(start):
        return pl.ds(start, ATT_SPAN, stride=dil) if dil > 1 else pl.ds(start, ATT_SPAN)

    qi = lax.broadcasted_iota(I32, (ATT_SPAN, ATT_SPAN), 0)
    ki = lax.broadcasted_iota(I32, (ATT_SPAN, ATT_SPAN), 1)
    cur_ok = ki <= qi
    prev_ok = ki >= qi
    scale = HEAD_DIM ** -0.5
    n_blocks = seq // dil // ATT_SPAN
    for r in range(dil):
        for b in range(n_blocks):
            cur = rows_of(r + dil * b * ATT_SPAN)
            for c in range(chunks):
                q2, kc2, vc2 = qs[c, cur, :], ks[c, cur, :], vs[c, cur, :]
                if b > 0:
                    prev = rows_of(r + dil * (b - 1) * ATT_SPAN)
                    kp2, vp2 = ks[c, prev, :], vs[c, prev, :]
                outs, lses = [], []
                for half in range(2):
                    hs = slice(half * HEAD_DIM, (half + 1) * HEAD_DIM)
                    q = q2[:, hs]
                    sc = jnp.where(cur_ok, _dot_nt(q, kc2[:, hs]) * scale, NEG_INF)
                    m = jnp.max(sc, axis=-1, keepdims=True)
                    if b > 0:
                        sp = jnp.where(prev_ok, _dot_nt(q, kp2[:, hs]) * scale, NEG_INF)
                        m = jnp.maximum(m, jnp.max(sp, axis=-1, keepdims=True))
                    ec = jnp.exp(sc - m)
                    den = jnp.sum(ec, axis=-1, keepdims=True)
                    acc = _dot(ec, vc2[:, hs])
                    if b > 0:
                        ep = jnp.exp(sp - m)
                        den = den + jnp.sum(ep, axis=-1, keepdims=True)
                        acc = acc + _dot(ep, vp2[:, hs])
                    outs.append(acc / den)
                    lses.append(jnp.broadcast_to(m + jnp.log(den), (ATT_SPAN, HEAD_DIM)))
                o_ref[c, cur, :] = jnp.concatenate(outs, axis=1)
                l_ref[c, cur, :] = jnp.concatenate(lses, axis=1)


def _prompt_attn(z_qkv, cos_t, sin_t, gi, dil, layer, kv_prev, *, depth, n_seq, seq):
    rows_p = n_seq * seq
    win = ATT_GROUPS[gi][0]
    keep = min(win, seq)
    assert seq % (dil * ATT_SPAN) == 0 and keep % 128 == 0 and seq % ROT_ROWS == 0
    tab = pl.BlockSpec((seq, 128), lambda n: (0, 0))
    chunked = pl.BlockSpec((ATT_OUT // 128, seq, 128), lambda n: (0, n, 0))
    kv_shape = (depth, n_seq, 2, HEADS_PER_GROUP, HEAD_DIM, keep)
    kv_spec = pl.BlockSpec((1, 1) + kv_shape[2:], lambda n: (layer, n, 0, 0, 0, 0))
    in_specs = [pl.BlockSpec((seq, ATT_OUT), lambda n: (n, gi)),
                pl.BlockSpec((seq, ATT_OUT), lambda n: (n, ATT_W // ATT_OUT + gi)),
                pl.BlockSpec((seq, ATT_OUT), lambda n: (n, 2 * ATT_W // ATT_OUT + gi)), tab, tab]
    args = [z_qkv, z_qkv, z_qkv, cos_t, sin_t]
    aliases = {}
    if kv_prev is not None:
        in_specs.append(pl.BlockSpec(memory_space=pl.ANY))
        args.append(kv_prev)
        aliases = {len(args) - 1: 2}
    chunk_shape = jax.ShapeDtypeStruct((ATT_OUT // 128, rows_p, 128), F32)
    return pl.pallas_call(
        functools.partial(_prompt_attn_kernel, dil=dil, seq=seq, keep=keep),
        grid=(n_seq,), in_specs=in_specs,
        out_specs=[chunked, chunked, kv_spec],
        out_shape=[chunk_shape, chunk_shape, jax.ShapeDtypeStruct(kv_shape, F32)],
        scratch_shapes=[pltpu.VMEM((ATT_OUT // 128, seq, 128), F32)] * 3,
        input_output_aliases=aliases,
        compiler_params=_params("arbitrary"), name=f"prompt_attn_g{gi}",
    )(*args)


def _merge_groups(os_, ls_):
    m = jnp.maximum(jnp.maximum(ls_[0], ls_[1]), ls_[2])
    es = [jnp.exp(l - m) for l in ls_]
    tot = es[0] + es[1] + es[2]
    return (es[0] / tot) * os_[0] + (es[1] / tot) * os_[1] + (es[2] / tot) * os_[2]


def _merge_kernel(o0, o1, o2, l0, l1, l2, a_ref):
    for c in range(ATT_OUT // 128):
        a_ref[:, c * 128:(c + 1) * 128] = _merge_groups([o0[c], o1[c], o2[c]], [l0[c], l1[c], l2[c]])


def _merge_attn(os_, ls_, *, tm):
    rows = os_[0].shape[1]
    spec = pl.BlockSpec((ATT_OUT // 128, tm, 128), lambda i: (0, i, 0))
    return pl.pallas_call(
        _merge_kernel, grid=(rows // tm,), in_specs=[spec] * 6, out_specs=pl.BlockSpec((tm, ATT_OUT), lambda i: (i, 0)),
        out_shape=jax.ShapeDtypeStruct((rows, ATT_OUT), F32),
        compiler_params=_params("arbitrary"), name="merge_attn",
    )(*os_, *ls_)


def _sample_attn_kernel(z_ref, cos_ref, sin_ref, *rest, t_new):
    caches, (a_ref, n0_ref, n1_ref, n2_ref) = rest[0:3], rest[-4:]
    scale = HEAD_DIM ** -0.5
    cos = cos_ref[...]
    sin = sin_ref[...]
    n_chunks = ATT_W // 128
    q_c = [_rotate(z_ref[:, c * 128:(c + 1) * 128], cos, sin) for c in range(n_chunks)]
    k_c = [_rotate(z_ref[:, ATT_W + c * 128:ATT_W + (c + 1) * 128], cos, sin) for c in range(n_chunks)]
    v_c = [z_ref[:, 2 * ATT_W + c * 128:2 * ATT_W + (c + 1) * 128] for c in range(n_chunks)]
    outs_g, lses_g = [], []
    for gi, (c_ref, n_ref, (win, dil)) in enumerate(zip(caches, (n0_ref, n1_ref, n2_ref), ATT_GROUPS)):
        cache_len = c_ref.shape[-1]
        t_c = lax.broadcasted_iota(I32, (t_new, cache_len), 0)
        c_c = lax.broadcasted_iota(I32, (t_new, cache_len), 1)
        d_c = cache_len + t_c - c_c
        ok_c = jnp.logical_and((d_c & (dil - 1)) == 0, d_c <= ATT_SPAN * dil)
        t_n = lax.broadcasted_iota(I32, (t_new, t_new), 0)
        u_n = lax.broadcasted_iota(I32, (t_new, t_new), 1)
        d_n = t_n - u_n
        ok_n = jnp.logical_and(d_n >= 0, (d_n & (dil - 1)) == 0)
        outs, lses = [], []
        for h in range(HEADS_PER_GROUP):
            chunk, half = divmod(gi * HEADS_PER_GROUP + h, 2)
            hs = slice(half * HEAD_DIM, (half + 1) * HEAD_DIM)
            q, k_new, v_new = q_c[chunk][:, hs], k_c[chunk][:, hs], v_c[chunk][:, hs]
            k_t = c_ref[0, 0, 0, h]
            v_t = c_ref[0, 0, 1, h]
            s_c = jnp.where(ok_c, _dot(q, k_t) * scale, NEG_INF)
            s_n = jnp.where(ok_n, _dot_nt(q, k_new) * scale, NEG_INF)
            m = jnp.maximum(jnp.max(s_c, axis=-1, keepdims=True), jnp.max(s_n, axis=-1, keepdims=True))
            e_c = jnp.exp(s_c - m)
            e_n = jnp.exp(s_n - m)
            den = jnp.sum(e_c, axis=-1, keepdims=True) + jnp.sum(e_n, axis=-1, keepdims=True)
            acc = _dot_nt(e_c, v_t) + _dot(e_n, v_new)
            outs.append(acc / den)
            lses.append(jnp.broadcast_to(m + jnp.log(den), (t_new, HEAD_DIM)))
            n_ref[0, 0, 0, h] = jnp.concatenate([k_t[:, t_new:], k_new.T], axis=1)
            n_ref[0, 0, 1, h] = jnp.concatenate([v_t[:, t_new:], v_new.T], axis=1)
        outs_g.append(jnp.concatenate(outs, axis=1))
        lses_g.append(jnp.concatenate(lses, axis=1))
    a_ref[...] = _merge_groups(outs_g, lses_g)


def _sample_attn(z_qkv, cos_s, sin_s, caches_t, layer, new_prev, *, row0, n_seq, t_new):
    b0 = row0 // t_new

    def cache_spec(c):
        return pl.BlockSpec((1, 1) + c.shape[2:], lambda n: (layer, n, 0, 0, 0, 0))

    tab = pl.BlockSpec((t_new, 128), lambda n: (0, 0))
    in_specs = [pl.BlockSpec((t_new, 3 * ATT_W), lambda n: (b0 + n, 0)), tab, tab] + [cache_spec(c) for c in caches_t]
    args = [z_qkv, cos_s, sin_s, *caches_t]
    aliases = {}
    if new_prev is not None:
        for j, a in enumerate(new_prev):
            in_specs.append(pl.BlockSpec(memory_space=pl.ANY))
            args.append(a)
            aliases[len(args) - 1] = 1 + j
    return pl.pallas_call(
        functools.partial(_sample_attn_kernel, t_new=t_new),
        grid=(n_seq,), in_specs=in_specs,
        out_specs=[pl.BlockSpec((t_new, ATT_OUT), lambda n: (n, 0))] + [cache_spec(c) for c in caches_t],
        out_shape=[jax.ShapeDtypeStruct((n_seq * t_new, ATT_OUT), F32)]
        + [jax.ShapeDtypeStruct(c.shape, F32) for c in caches_t],
        input_output_aliases=aliases,
        compiler_params=_params("arbitrary"), name="sample_attn",
    )(*args)


def _pool_kernel(z_ref, h_ref, w_ref, s_ref, o_ref, buf_ref, *, t_len, pos0):
    buf_ref[0:POOL_HIST, :] = h_ref[0]
    buf_ref[POOL_HIST:POOL_HIST + t_len, :] = z_ref[...]
    pos = pos0 + lax.broadcasted_iota(I32, (t_len, POOL_GROUP), 0)
    for g, win in enumerate(POOL_WINDOWS):
        cs = slice(g * POOL_GROUP, (g + 1) * POOL_GROUP)
        z = buf_ref[POOL_HIST:POOL_HIST + t_len, cs]
        wsum = z
        for i in range(1, win):
            wsum = wsum + buf_ref[POOL_HIST - i:POOL_HIST - i + t_len, cs]
        cnt = jnp.minimum(win, pos + 1).astype(F32)
        y = wsum / cnt - z
        o_ref[:, cs] = _dot(y, w_ref[g]) * s_ref[:, cs]


def _pool_mixer(z_pool, hist, pool_w, pool_scale, *, row0, n_seq, t_len, pos0):
    b0 = row0 // t_len
    return pl.pallas_call(
        functools.partial(_pool_kernel, t_len=t_len, pos0=pos0),
        grid=(n_seq,),
        in_specs=[
            pl.BlockSpec((t_len, POOL_W), lambda n: (b0 + n, 0)),
            pl.BlockSpec((1, POOL_HIST, POOL_W), lambda n: (n, 0, 0)),
            pl.BlockSpec((len(POOL_WINDOWS), POOL_GROUP, POOL_GROUP), lambda n: (0, 0, 0)),
            pl.BlockSpec((1, POOL_W), lambda n: (0, 0)),
        ],
        out_specs=pl.BlockSpec((t_len, POOL_W), lambda n: (n, 0)),
        out_shape=jax.ShapeDtypeStruct((n_seq * t_len, POOL_W), F32),
        scratch_shapes=[pltpu.VMEM((POOL_HIST + t_len, POOL_W), F32)],
        compiler_params=_params("arbitrary"), name="pool_mixer",
    )(z_pool, hist, pool_w, pool_scale.reshape(1, POOL_W))


def _rwkv_prep_kernel(z_ref, zp_ref, first_ref, mu_ref, w0_ref, w2_ref, a0_ref, a2_ref, g2_ref, kk_ref, ka_ref, rk_ref,
                      r_o, lw_o, k_o, v_o, kn_o, b_o, g_o, bonus_o, buf_ref, *, tm):
    i = pl.program_id(1)
    z = z_ref[...]
    prev_row = jnp.where(i == 0, first_ref[0], zp_ref[SUBLANES - 1:SUBLANES, :])
    buf_ref[SUBLANES:SUBLANES + tm, :] = z
    buf_ref[SUBLANES - 1:SUBLANES, :] = prev_row
    shifted = buf_ref[SUBLANES - 1:SUBLANES - 1 + tm, :]
    xm = z + mu_ref[...] * (shifted - z)
    r = xm[:, 0:RWKV_W]
    k = xm[:, RWKV_W:2 * RWKV_W]
    v = xm[:, 2 * RWKV_W:3 * RWKV_W]
    wa = xm[:, RWKV_LORA_OFF:RWKV_LORA_OFF + 128]
    gl = xm[:, RWKV_LORA_OFF + 128:RWKV_COLS]
    xw = w0_ref[...] + _dot(jnp.tanh(wa), w2_ref[...])
    logw = -math.exp(-0.5) * jax.nn.sigmoid(xw)
    a = jax.nn.sigmoid(a0_ref[...] + _dot(wa, a2_ref[...]))
    g_o[...] = _dot(jax.nn.sigmoid(gl), g2_ref[...])
    kkr = k * kk_ref[...]
    kmod = k * (1.0 + (a - 1.0) * ka_ref[...])
    rkk = r * kmod * rk_ref[...]
    bonus = []
    for h in range(RWKV_HEADS):
        sl = slice(h * RWKV_HEAD, (h + 1) * RWKV_HEAD)
        kh = kkr[:, sl]
        nrm = jnp.sqrt(jnp.sum(kh * kh, axis=-1, keepdims=True))
        kn = kh / jnp.maximum(nrm, 1e-12)
        r_o[0, h] = r[:, sl]
        lw_o[0, h] = logw[:, sl]
        k_o[0, h] = kmod[:, sl]
        v_o[0, h] = v[:, sl]
        kn_o[0, h] = kn
        b_o[0, h] = kn * a[:, sl]
        bonus.append(jnp.sum(rkk[:, sl], axis=-1, keepdims=True) * v[:, sl])
    bonus_o[...] = jnp.concatenate(bonus, axis=1)


def _rwkv_prep(z, first_prev, lw, *, row0, n_seq, t_len, tm):
    nblk = t_len // tm
    pb = tm // SUBLANES
    b0 = row0 // tm
    p0 = row0 // SUBLANES

    def vec(n):
        return pl.BlockSpec((1, n), lambda s, i: (0, 0))

    hm = jax.ShapeDtypeStruct((n_seq, RWKV_HEADS, t_len, RWKV_HEAD), F32)
    rm = jax.ShapeDtypeStruct((n_seq * t_len, RWKV_W), F32)
    hm_spec = pl.BlockSpec((1, RWKV_HEADS, tm, RWKV_HEAD), lambda s, i: (s, 0, i, 0))
    rm_spec = pl.BlockSpec((tm, RWKV_W), lambda s, i: (s * nblk + i, 0))
    return pl.pallas_call(
        functools.partial(_rwkv_prep_kernel, tm=tm),
        grid=(n_seq, nblk),
        in_specs=[
            pl.BlockSpec((tm, RWKV_COLS), lambda s, i: (b0 + s * nblk + i, 0)),
            pl.BlockSpec((SUBLANES, RWKV_COLS), lambda s, i: (jnp.maximum(p0 + (s * nblk + i) * pb - 1, 0), 0)),
            pl.BlockSpec((1, 1, RWKV_COLS), lambda s, i: (s, 0, 0)),
            vec(RWKV_COLS), vec(RWKV_W),
            pl.BlockSpec((128, RWKV_W), lambda s, i: (0, 0)),
            vec(RWKV_W),
            pl.BlockSpec((128, RWKV_W), lambda s, i: (0, 0)),
            pl.BlockSpec((128, RWKV_W), lambda s, i: (0, 0)),
            vec(RWKV_W), vec(RWKV_W), vec(RWKV_W),
        ],
        out_specs=[hm_spec] * 6 + [rm_spec] * 2,
        out_shape=[hm] * 6 + [rm] * 2,
        scratch_shapes=[pltpu.VMEM((tm + SUBLANES, RWKV_COLS), F32)],
        compiler_params=_params("arbitrary", "arbitrary"), name="rwkv_prep",
    )(z, z, first_prev, lw["mu"], lw["w0"], lw["w2p"], lw["a0"], lw["a2p"], lw["g2"], lw["kk"], lw["ka"], lw["rk"])


def _bmm(a, b):
    return jnp.einsum("hqk,hkd->hqd", a.astype(BF16), b.astype(BF16), preferred_element_type=F32)


def _bmm_nt(a, b):
    return jnp.einsum("hqd,hkd->hqk", a.astype(BF16), b.astype(BF16), preferred_element_type=F32)


def _bmm_tn(a, b):
    return jnp.einsum("hkq,hkd->hqd", a.astype(BF16), b.astype(BF16), preferred_element_type=F32)


def _rwkv_scan_kernel(r_ref, lw_ref, k_ref, v_ref, kn_ref, b_ref, s0_ref, y_ref, st_ref, s_scr, *, chunk, group):
    c = pl.program_id(1)
    nh = group * RWKV_HEADS

    def heads(ref):
        return ref[...].reshape((nh,) + ref.shape[2:])

    @pl.when(c == 0)
    def _():
        s_scr[...] = heads(s0_ref)

    r, logw, k, v, kn, b = [heads(ref) for ref in (r_ref, lw_ref, k_ref, v_ref, kn_ref, b_ref)]
    row = lax.broadcasted_iota(I32, (chunk, chunk), 0)
    col = lax.broadcasted_iota(I32, (chunk, chunk), 1)
    incl = row >= col
    strict = row > col
    tri = jnp.broadcast_to(incl.astype(BF16)[None], (nh, chunk, chunk))
    lw_hi = logw.astype(BF16)
    lw_lo = (logw - lw_hi.astype(F32)).astype(BF16)
    cum = (jnp.einsum("hqk,hkd->hqd", tri, lw_hi, preferred_element_type=F32)
           + jnp.einsum("hqk,hkd->hqd", tri, lw_lo, preferred_element_type=F32))
    p_inv = jnp.exp(-cum)
    kt = k * p_inv
    bt = b * p_inv
    kap = kn * jnp.exp(cum - logw)
    rho = r * jnp.exp(cum)
    qq = jnp.concatenate([kap, rho], axis=1)
    gram = _bmm_nt(qq, jnp.concatenate([kt, bt], axis=1))
    a_k = jnp.where(strict[None], gram[:, :chunk, :chunk], 0.0)
    a_b = jnp.where(strict[None], gram[:, :chunk, chunk:], 0.0)
    l_k = jnp.where(incl[None], gram[:, chunk:, :chunk], 0.0)
    l_b = jnp.where(incl[None], gram[:, chunk:, chunk:], 0.0)
    x = jnp.broadcast_to((row == col).astype(F32)[None], (nh, chunk, chunk))
    m = 1
    while m < chunk:
        sh = m.bit_length() - 1
        same = (row >> (sh + 1)) == (col >> (sh + 1))
        lower_left = jnp.logical_and(((row >> sh) & 1) == 1, ((col >> sh) & 1) == 0)
        off = jnp.where(jnp.logical_and(same, lower_left)[None], a_b, 0.0)
        x = x - _bmm(_bmm(x, off), x)
        m *= 2
    s = s_scr[...]
    qs = _bmm_nt(qq, s)
    u = _bmm(x, -(qs[:, :chunk] + _bmm(a_k, v)))
    y = qs[:, chunk:] + _bmm(jnp.concatenate([l_k, l_b], axis=2), jnp.concatenate([v, u], axis=1))
    s_new = (s + _bmm_tn(jnp.concatenate([v, u], axis=1), jnp.concatenate([kt, bt], axis=1))) * jnp.exp(cum[:, chunk - 1:chunk, :])
    s_scr[...] = s_new
    st_ref[...] = s_new.reshape(st_ref.shape)
    mean = jnp.mean(y, axis=-1, keepdims=True)
    var = jnp.mean(jnp.square(y - mean), axis=-1, keepdims=True)
    yn = (y - mean) * lax.rsqrt(var + GN_EPS)
    for g in range(group):
        y_ref[g] = jnp.concatenate([yn[g * RWKV_HEADS + h] for h in range(RWKV_HEADS)], axis=1)


def _rwkv_scan(prep, s0, *, chunk, group):
    r, lw, k, v, kn, b = prep
    n_seq, nh, t_len, hd = r.shape
    hm_spec = pl.BlockSpec((group, nh, chunk, hd), lambda s, c: (s, 0, c, 0))
    st_spec = pl.BlockSpec((group, nh, hd, hd), lambda s, c: (s, 0, 0, 0))
    return pl.pallas_call(
        functools.partial(_rwkv_scan_kernel, chunk=chunk, group=group),
        grid=(n_seq // group, t_len // chunk),
        in_specs=[hm_spec] * 6 + [st_spec],
        out_specs=[pl.BlockSpec((group, chunk, RWKV_W), lambda s, c: (s, c, 0)), st_spec],
        out_shape=[jax.ShapeDtypeStruct((n_seq, t_len, RWKV_W), F32), jax.ShapeDtypeStruct((n_seq, nh, hd, hd), F32)],
        scratch_shapes=[pltpu.VMEM((group * nh, hd, hd), F32)],
        compiler_params=_params("arbitrary", "arbitrary"), name="rwkv_scan",
    )(r, lw, k, v, kn, b, s0)


def _mix_kernel(*refs, n_prompt_blocks):
    pairs, (gate_ref, lnw_ref, lnb_ref, wa_ref, wp_ref, wr_ref, o_ref) = refs[:10], refs[10:]
    is_sample = pl.program_id(0) >= n_prompt_blocks
    att, pool, yn, bonus, g = [jnp.where(is_sample, pairs[2 * j + 1][...], pairs[2 * j][...]) for j in range(5)]
    rw = (yn * lnw_ref[...] + lnb_ref[...] + bonus) * g
    mix = jax.nn.sigmoid(gate_ref[:, 0:D_MODEL]) * _dot(att, wa_ref[...])
    mix = mix + jax.nn.sigmoid(gate_ref[:, D_MODEL:2 * D_MODEL]) * _dot(pool, wp_ref[...])
    mix = mix + jax.nn.sigmoid(gate_ref[:, 2 * D_MODEL:3 * D_MODEL]) * _dot(rw, wr_ref[...])
    o_ref[...] = mix.astype(BF16)


def _mix(branches, z_gate, ln_w, ln_b, wa, wp, wr, *, tm):
    rows = z_gate.shape[0]
    n_prompt_blocks = branches[0][0].shape[0] // tm
    assert rows == (n_prompt_blocks + 1) * tm and all(s.shape[0] == tm for _, s in branches)

    def full(a):
        return pl.BlockSpec(a.shape, lambda i: (0, 0))

    specs, args = [], []
    for p, s in branches:
        specs += [pl.BlockSpec((tm, p.shape[1]), lambda i: (jnp.minimum(i, n_prompt_blocks - 1), 0)), full(s)]
        args += [p, s]
    ln_w = ln_w.reshape(1, RWKV_W)
    ln_b = ln_b.reshape(1, RWKV_W)
    return pl.pallas_call(
        functools.partial(_mix_kernel, n_prompt_blocks=n_prompt_blocks), grid=(rows // tm,),
        in_specs=specs + [pl.BlockSpec((tm, 3 * D_MODEL), lambda i: (i, 0)),
                          full(ln_w), full(ln_b), full(wa), full(wp), full(wr)],
        out_specs=pl.BlockSpec((tm, D_MODEL), lambda i: (i, 0)),
        out_shape=jax.ShapeDtypeStruct((rows, D_MODEL), BF16),
        compiler_params=_params("arbitrary"), name="gated_mix",
    )(*args, z_gate, ln_w, ln_b, wa, wp, wr)


def _mm_res_kernel(x_ref, a_ref, w_ref, o_ref):
    o_ref[...] = x_ref[...] + jnp.dot(a_ref[...], w_ref[...], preferred_element_type=F32)


def _mm_residual(x, a, w, *, tm):
    rows, n = x.shape
    k = a.shape[1]
    return pl.pallas_call(
        _mm_res_kernel, grid=(rows // tm,),
        in_specs=[pl.BlockSpec((tm, n), lambda i: (i, 0)), pl.BlockSpec((tm, k), lambda i: (i, 0)),
                  pl.BlockSpec((k, n), lambda i: (0, 0))],
        out_specs=pl.BlockSpec((tm, n), lambda i: (i, 0)),
        out_shape=jax.ShapeDtypeStruct((rows, n), F32),
        compiler_params=_params("arbitrary"), name="out_proj",
    )(x, a, w)


def _topk_cols(scores, ids, k):
    lanes = scores[0].shape[1]
    iota_k = lax.broadcasted_iota(I32, (k, lanes), 0)
    big = jnp.iinfo(jnp.int32).max

    def body(j, carry):
        out = []
        for (s, vals, idxs), ident in zip(carry, ids):
            m = jnp.max(s, axis=0, keepdims=True)
            idx = jnp.min(jnp.where(s == m, ident, big), axis=0, keepdims=True)
            vals = jnp.where(iota_k == j, m, vals)
            idxs = jnp.where(iota_k == j, idx, idxs)
            out.append((jnp.where(ident == idx, NEG_INF, s), vals, idxs))
        return tuple(out)

    init = tuple((s, jnp.zeros((k, lanes), F32), jnp.zeros((k, lanes), I32)) for s in scores)
    return [(v, i) for _, v, i in lax.fori_loop(0, k, body, init)]


def _gather_rows(table, sel, k):
    out = jnp.zeros(sel.shape, table.dtype)
    for a in range(k):
        out = jnp.where(sel == a, table[a:a + 1, :], out)
    return out


def _peer_select_kernel(q_ref, skh_ref, skl_ref, i1_o, i2_o, gate_o, i1_s, i2_s, g_s, *, tok):
    kk = PEER_TOPK
    half = kk // 2
    nt = (((1,), (1,)), ((), ()))
    key_id = lax.broadcasted_iota(I32, (N_KEYS, tok), 0)
    row = lax.broadcasted_iota(I32, (half * half + kk, tok), 0)
    cand_id = jnp.where(row < half * half, (row >> (half.bit_length() - 1)) * kk + (row & (half - 1)),
                        jnp.where(row < half * half + half, row - half * half + half, (row - half * half) * kk))

    def head(h, carry):
        c0 = pl.multiple_of(h * 2 * N_KEYS, 2 * N_KEYS)
        q1h, q1l = _split_bf16(q_ref[:, pl.ds(c0, N_KEYS)])
        q2h, q2l = _split_bf16(q_ref[:, pl.ds(c0 + N_KEYS, N_KEYS)])
        s1 = _dot3(skh_ref[h, 0], skl_ref[h, 0], q1h, q1l, nt)
        s2 = _dot3(skh_ref[h, 1], skl_ref[h, 1], q2h, q2l, nt)
        (t1, k1), (t2, k2) = _topk_cols([s1, s2], [key_id, key_id], kk)
        cand = jnp.concatenate([t1[a:a + 1, :] + t2[0:half, :] for a in range(half)]
                               + [t1[0:1, :] + t2[half:kk, :], t1[half:kk, :] + t2[0:1, :]], axis=0)
        ((top, sel),) = _topk_cols([cand], [cand_id], kk)
        e1 = _gather_rows(k1, sel >> 4, kk)
        e2 = _gather_rows(k2, sel & (kk - 1), kk)
        ex = jnp.exp(top - jnp.max(top, axis=0, keepdims=True))
        gate = ex / jnp.sum(ex, axis=0, keepdims=True)
        r0 = pl.multiple_of(h * kk, kk)
        i1_s[pl.ds(r0, kk), :] = e1.astype(F32)
        i2_s[pl.ds(r0, kk), :] = e2.astype(F32)
        g_s[pl.ds(r0, kk), :] = gate
        return carry

    lax.fori_loop(0, PEER_HEADS, head, 0)
    i1_o[...] = i1_s[...].T.astype(I32)
    i2_o[...] = i2_s[...].T.astype(I32)
    gate_o[...] = g_s[...].T


def _peer_select(q, sk_hi, sk_lo, *, tok=128):
    rows = q.shape[0]
    spec = pl.BlockSpec((tok, PEER_PAIRS), lambda i: (i, 0))
    return pl.pallas_call(
        functools.partial(_peer_select_kernel, tok=tok), grid=(rows // tok,),
        in_specs=[pl.BlockSpec((tok, q.shape[1]), lambda i: (i, 0)),
                  pl.BlockSpec(sk_hi.shape, lambda i: (0, 0, 0, 0)), pl.BlockSpec(sk_lo.shape, lambda i: (0, 0, 0, 0))],
        out_specs=[spec] * 3,
        out_shape=[jax.ShapeDtypeStruct((rows, PEER_PAIRS), I32), jax.ShapeDtypeStruct((rows, PEER_PAIRS), I32),
                   jax.ShapeDtypeStruct((rows, PEER_PAIRS), F32)],
        scratch_shapes=[pltpu.VMEM((PEER_PAIRS, tok), F32)] * 3,
        compiler_params=_params("arbitrary"), name="peer_select",
    )(q, sk_hi, sk_lo)


def _peer_pick(d_ref, i1_ref, i2_ref, act_ref, first_row):
    n_rows = d_ref.shape[1] // N_KEYS
    for g in range(act_ref.shape[0] // SUBLANES):
        rs = slice(g * SUBLANES, (g + 1) * SUBLANES)
        i1 = i1_ref[rs, :]
        i2 = i2_ref[rs, :]
        acc = act_ref[rs, :]
        for j in range(n_rows):
            got = jnp.take_along_axis(d_ref[rs, j * N_KEYS:(j + 1) * N_KEYS], i2, axis=1, mode="promise_in_bounds")
            acc = acc + jnp.where(i1 == first_row + j, got, 0.0)
        act_ref[rs, :] = acc


def _peer_act_kernel(xn_ref, u_ref, i1_ref, i2_ref, act_ref, da_ref, db_ref):
    first = jnp.logical_and(pl.program_id(0) == 0, pl.program_id(1) == 0)
    c = pl.program_id(1)
    slab = PEER_EC // 2
    slab_rows = slab // N_KEYS
    nt = (((1,), (1,)), ((), ()))

    @pl.when(first)
    def _():
        db_ref[...] = jnp.zeros(db_ref.shape, F32)

    @pl.when(c == 0)
    def _():
        act_ref[...] = jnp.zeros(act_ref.shape, F32)

    base = c * 2 * slab_rows
    _peer_pick(db_ref, i1_ref, i2_ref, act_ref, base - slab_rows)
    da_ref[...] = lax.dot_general(xn_ref[...], u_ref[0, 0:slab, :].astype(BF16), nt, preferred_element_type=F32)
    _peer_pick(da_ref, i1_ref, i2_ref, act_ref, base)
    db_ref[...] = lax.dot_general(xn_ref[...], u_ref[0, slab:2 * slab, :].astype(BF16), nt, preferred_element_type=F32)

    @pl.when(c == pl.num_programs(1) - 1)
    def _():
        _peer_pick(db_ref, i1_ref, i2_ref, act_ref, base + slab_rows)


def _peer_act(xn, u_tabs, layer, i1, i2, *, tb):
    rows = xn.shape[0]
    pair_spec = pl.BlockSpec((tb, PEER_PAIRS), lambda i, c: (i, 0))
    return pl.pallas_call(
        _peer_act_kernel, grid=(rows // tb, N_EXPERTS // PEER_EC),
        in_specs=[pl.BlockSpec((tb, D_MODEL), lambda i, c: (i, 0)),
                  pl.BlockSpec((1, PEER_EC, D_MODEL), lambda i, c: (layer, c, 0)), pair_spec, pair_spec],
        out_specs=pair_spec,
        out_shape=jax.ShapeDtypeStruct((rows, PEER_PAIRS), F32),
        scratch_shapes=[pltpu.VMEM((tb, PEER_EC // 2), F32)] * 2,
        compiler_params=_params("arbitrary", "arbitrary"), name="peer_act",
    )(xn, u_tabs, i1, i2)


def _peer_out_kernel(i1_ref, i2_ref, gate_ref, act_ref, v_ref, o_ref, w_ref):
    c = pl.program_id(1)
    n_groups = w_ref.shape[0]
    rows_per_chunk = PEER_EC // N_KEYS

    @pl.when(c == 0)
    def _():
        key_iota = lax.broadcasted_iota(I32, (N_KEYS, PEER_PAIRS), 0)

        def group(g, carry):
            r0 = pl.multiple_of(g * SUBLANES, SUBLANES)
            i1 = i1_ref[pl.ds(r0, SUBLANES), :]
            i2 = i2_ref[pl.ds(r0, SUBLANES), :]
            a = act_ref[pl.ds(r0, SUBLANES), :]
            wgt = gate_ref[pl.ds(r0, SUBLANES), :] * (0.5 * a * (1.0 + lax.erf(a * (1.0 / math.sqrt(2.0)))))
            for t in range(SUBLANES):
                hit1 = key_iota == jnp.broadcast_to(i1[t:t + 1, :], key_iota.shape)
                hit2 = key_iota == jnp.broadcast_to(i2[t:t + 1, :], key_iota.shape)
                w2 = jnp.where(hit2, jnp.broadcast_to(wgt[t:t + 1, :], key_iota.shape), 0.0)
                w_tok = _dot_nt(jnp.where(hit1, 1.0, 0.0), w2)
                w_ref[g, pl.ds(t, N_KEYS, stride=SUBLANES), :] = w_tok
            return carry

        lax.fori_loop(0, n_groups, group, 0)

    row0 = c * (rows_per_chunk * SUBLANES)
    lhs = jnp.concatenate(
        [w_ref[:, pl.ds(pl.multiple_of(row0 + j * SUBLANES, SUBLANES), SUBLANES), :].reshape(n_groups * SUBLANES, N_KEYS)
         for j in range(rows_per_chunk)], axis=1)
    part = jnp.dot(lhs.astype(BF16), v_ref[0], preferred_element_type=F32)

    @pl.when(c == 0)
    def _():
        o_ref[...] = part

    @pl.when(c > 0)
    def _():
        o_ref[...] += part


def _peer_out(i1, i2, gate, act, v_tabs, layer):
    rows = i1.shape[0]
    tb = PEER_TB
    pair_spec = pl.BlockSpec((tb, PEER_PAIRS), lambda i, c: (i, 0))
    return pl.pallas_call(
        _peer_out_kernel, grid=(rows // tb, N_EXPERTS // PEER_EC),
        in_specs=[pair_spec] * 4 + [pl.BlockSpec((1, PEER_EC, D_MODEL), lambda i, c: (layer, c, 0))],
        out_specs=pl.BlockSpec((tb, D_MODEL), lambda i, c: (i, 0)),
        out_shape=jax.ShapeDtypeStruct((rows, D_MODEL), F32),
        scratch_shapes=[pltpu.VMEM((tb // SUBLANES, N_KEYS * SUBLANES, N_KEYS), F32)],
        compiler_params=_params("arbitrary", "arbitrary"), name="peer_out",
    )(i1, i2, gate, act, v_tabs)


def _ple_kernel(x_ref, f_ref, p_ref, g_ref, wg_ref, wp_ref, gf_ref, o_ref, *, final):
    x = x_ref[...] + f_ref[...]
    ms = jnp.mean(x * x, axis=-1, keepdims=True)
    h = (x * lax.rsqrt(ms + RMS_EPS)) * g_ref[...]
    out = x + jax.nn.sigmoid(_dot(h, wg_ref[...])) * _dot(p_ref[...], wp_ref[...])
    if final:
        ms2 = jnp.mean(out * out, axis=-1, keepdims=True)
        out = (out * lax.rsqrt(ms2 + RMS_EPS)) * gf_ref[...]
    o_ref[...] = out


def _ple(x, ffn, p, g, wg, wp, g_final, *, tm, final):
    rows, n = x.shape
    row = pl.BlockSpec((tm, n), lambda i: (i, 0))
    vec = pl.BlockSpec((1, n), lambda i: (0, 0))
    return pl.pallas_call(
        functools.partial(_ple_kernel, final=final), grid=(rows // tm,),
        in_specs=[row, row, pl.BlockSpec((tm, p.shape[1]), lambda i: (i, 0)), vec,
                  pl.BlockSpec(wg.shape, lambda i: (0, 0)), pl.BlockSpec(wp.shape, lambda i: (0, 0)), vec],
        out_specs=row,
        out_shape=jax.ShapeDtypeStruct((rows, n), F32),
        compiler_params=_params("arbitrary"), name="ple",
    )(x, ffn, p, g.reshape(1, n), wg, wp, g_final.reshape(1, n))


def _layer(x, p_rows, lw, state, g_final, dims, final):
    n_p, t_p, n_s, t_s, past = dims
    rows_p = n_p * t_p
    rows_s = n_s * t_s

    z_qkv = _norm_mm(x, lw["g_mix"], lw["w_qkv"], tm=768, tn=768)
    z_pool = _norm_mm(x, lw["g_mix"], lw["w_pool"], tm=768, tn=POOL_W)
    z_rwkv = _norm_mm(x, lw["g_mix"], lw["w_rwkv"], tm=768, tn=RWKV_COLS // 2)
    z_gate = _norm_mm(x, lw["g_mix"], lw["w_gate"], tm=768, tn=768)

    layer, depth = lw["layer"], lw["depth"]
    os_, ls_, kv_p = [], [], []
    for gi, (_, dil) in enumerate(ATT_GROUPS):
        o, l, kv = _prompt_attn(z_qkv, lw["cos_p"], lw["sin_p"], gi, dil, layer,
                                None if state["kv_p"] is None else state["kv_p"][gi], depth=depth, n_seq=n_p, seq=t_p)
        os_.append(o)
        ls_.append(l)
        kv_p.append(kv)
    att_p = _merge_attn(os_, ls_, tm=512)
    att_s, *kv_s = _sample_attn(z_qkv, lw["cos_s"], lw["sin_s"], state["caches_t"], layer, state["kv_s"],
                                row0=rows_p, n_seq=n_s, t_new=t_s)

    pool_p = _pool_mixer(z_pool, jnp.zeros((n_p, POOL_HIST, POOL_W), F32), lw["pool_w"], lw["pool_scale"],
                         row0=0, n_seq=n_p, t_len=t_p, pos0=0)
    hist_s = jnp.concatenate([jnp.zeros((n_s, 1, POOL_W), F32), state["pool"]], axis=1)
    pool_s = _pool_mixer(z_pool, hist_s, lw["pool_w"], lw["pool_scale"], row0=rows_p, n_seq=n_s, t_len=t_s, pos0=past)

    prep_p = _rwkv_prep(z_rwkv, jnp.zeros((n_p, 1, RWKV_COLS), F32), lw, row0=0, n_seq=n_p, t_len=t_p, tm=256)
    prep_s = _rwkv_prep(z_rwkv, state["shift"][:, None, :], lw, row0=rows_p, n_seq=n_s, t_len=t_s, tm=t_s)
    yn_p, wkv_p = _rwkv_scan(prep_p[:6], jnp.zeros((n_p, RWKV_HEADS, RWKV_HEAD, RWKV_HEAD), F32),
                             chunk=RWKV_CHUNK, group=RWKV_SEQ_GROUP)
    yn_s, wkv_s = _rwkv_scan(prep_s[:6], state["wkv"], chunk=t_s, group=RWKV_SEQ_GROUP)
    branches = [(att_p, att_s), (pool_p, pool_s), (yn_p.reshape(rows_p, RWKV_W), yn_s.reshape(rows_s, RWKV_W)),
                (prep_p[7], prep_s[7]), (prep_p[6], prep_s[6])]

    mix = _mix(branches, z_gate, lw["ln_w"], lw["ln_b"], lw["w_attn_o"], lw["w_pool_o"], lw["w_rwkv_o"], tm=rows_s)
    x = _mm_residual(x, mix, lw["w_out"], tm=384)

    q, xn = _norm_mm3(x, lw["g_ffn"], *lw["peer_wq"], tm=768, tn=512)
    i1, i2, gate = _peer_select(q, *lw["peer_subkeys"])
    act = _peer_act(xn, lw["peer_u"], lw["layer"], i1, i2, tb=PEER_ACT_TB)
    ffn = _peer_out(i1, i2, gate, act, lw["peer_v"], lw["layer"])

    x = _ple(x, ffn, p_rows, lw["g_ple"], lw["ple_wg"], lw["ple_wp"], g_final, tm=256, final=final)

    keep = POOL_HIST - 1
    pool_p_state = jnp.stack([z_pool[(n + 1) * t_p - keep:(n + 1) * t_p] for n in range(n_p)])
    zp_s = z_pool[rows_p:].reshape(n_s, t_s, POOL_W)
    new_p = [pool_p_state, wkv_p, z_rwkv[t_p - 1:rows_p:t_p]]
    new_s = [jnp.concatenate([hist_s[:, 1:], zp_s], axis=1)[:, -keep:], wkv_s, z_rwkv[rows_p + t_s - 1::t_s]]
    return x, kv_p, kv_s, new_p, new_s


def kernel(x_prompt, x_sample, p_prompt, p_sample, cache_attn_w128, cache_attn_w512, cache_attn_w2048, state_pool, state_rwkv_wkv, state_rwkv_shift, g_mix, w_in, w_attn_o, w_pool_o, w_rwkv_o, w_out, pool_w, pool_scale, rwkv_mu, rwkv_w0, rwkv_w2, rwkv_a0, rwkv_a2, rwkv_g2, rwkv_kk, rwkv_ka, rwkv_rk, rwkv_ln_w, rwkv_ln_b, g_ffn, peer_wq, peer_subkeys, peer_u, peer_v, g_ple, ple_wg, ple_wp, g_final):
    n_p, t_p, _ = x_prompt.shape
    n_s, t_s, _ = x_sample.shape
    depth = w_in.shape[0]
    past = PAST_LEN
    rows_p = n_p * t_p
    rows_s = n_s * t_s
    dims = (n_p, t_p, n_s, t_s, past)

    x = jnp.concatenate([x_prompt.reshape(rows_p, D_MODEL), x_sample.reshape(rows_s, D_MODEL)], axis=0)
    cos_p, sin_p = _rope_tables(jnp.arange(t_p, dtype=I32))
    cos_s, sin_s = _rope_tables(past + jnp.arange(t_s, dtype=I32))
    lora_pad = jnp.zeros((128 - 64, RWKV_W), F32)
    peer_v_bf16 = peer_v.astype(BF16)
    to_dev = (0, 1, 3, 4, 5, 2)
    from_dev = (0, 1, 5, 2, 3, 4)
    caches_t = [jnp.transpose(c, to_dev) for c in (cache_attn_w128, cache_attn_w512, cache_attn_w2048)]

    def row(a):
        return a.reshape(1, -1)

    new_p, new_s = [], []
    kv_p = kv_s = None
    for l in range(depth):
        wl = w_in[l]
        lw = {
            "g_mix": g_mix[l],
            "w_qkv": wl[:, :OFF_POOL].astype(BF16), "w_pool": wl[:, OFF_POOL:OFF_RWKV].astype(BF16),
            "w_rwkv": wl[:, OFF_RWKV:OFF_GATE].astype(BF16), "w_gate": wl[:, OFF_GATE:].astype(BF16),
            "cos_p": cos_p, "sin_p": sin_p, "cos_s": cos_s, "sin_s": sin_s, "depth": depth,
            "pool_w": pool_w[l].astype(BF16), "pool_scale": pool_scale[l],
            "mu": row(rwkv_mu[l]), "w0": row(rwkv_w0[l]), "a0": row(rwkv_a0[l]),
            "w2p": jnp.concatenate([rwkv_w2[l], lora_pad], axis=0).astype(BF16),
            "a2p": jnp.concatenate([lora_pad, rwkv_a2[l]], axis=0).astype(BF16),
            "g2": rwkv_g2[l].astype(BF16),
            "kk": row(rwkv_kk[l]), "ka": row(rwkv_ka[l]), "rk": row(rwkv_rk[l]),
            "ln_w": rwkv_ln_w[l], "ln_b": rwkv_ln_b[l],
            "w_attn_o": w_attn_o[l].astype(BF16), "w_pool_o": w_pool_o[l].astype(BF16),
            "w_rwkv_o": w_rwkv_o[l].astype(BF16), "w_out": w_out[l].astype(BF16),
            "g_ffn": g_ffn[l], "peer_wq": _split_bf16(peer_wq[l]), "peer_subkeys": _split_bf16(peer_subkeys[l]),
            "peer_u": peer_u, "peer_v": peer_v_bf16, "layer": l,
            "g_ple": g_ple[l], "ple_wg": ple_wg[l].astype(BF16), "ple_wp": ple_wp[l].astype(BF16),
        }
        state = {"caches_t": caches_t, "kv_p": kv_p, "kv_s": kv_s, "pool": state_pool[l],
                 "wkv": state_rwkv_wkv[l], "shift": state_rwkv_shift[l]}
        p_rows = jnp.concatenate([p_prompt[l].reshape(rows_p, -1), p_sample[l].reshape(rows_s, -1)], axis=0).astype(BF16)
        x, kv_p, kv_s, st_p, st_s = _layer(x, p_rows, lw, state, g_final, dims, l == depth - 1)
        new_p.append(st_p)
        new_s.append(st_s)

    outs = [x[:rows_p].reshape(n_p, t_p, D_MODEL), x[rows_p:].reshape(n_s, t_s, D_MODEL)]
    for gi in range(len(ATT_GROUPS)):
        outs.append(jnp.transpose(kv_p[gi], from_dev))
        outs.append(jnp.transpose(kv_s[gi], from_dev))
    for j in range(3):
        outs.append(jnp.stack([s[j] for s in new_p]))
        outs.append(jnp.stack([s[j] for s in new_s]))
    return tuple(outs)
```

```python
import functools
import math

import jax
import jax.numpy as jnp
from jax import lax
from jax.experimental import pallas as pl
from jax.experimental.pallas import tpu as pltpu

F32 = jnp.float32
BF16 = jnp.bfloat16
I32 = jnp.int32

D_MODEL = 2048
RMS_EPS = 1e-6
HEAD_DIM = 64
ATT_GROUPS = ((128, 1), (512, 4), (2048, 16))
HEADS_PER_GROUP = 4
ATT_W = 768
ATT_OUT = 256
ATT_SPAN = 128
ROPE_THETA = 10000.0
POOL_WINDOWS = (2, 4, 8, 16)
POOL_GROUP = 128
POOL_W = 512
POOL_HIST = 16
RWKV_HEAD = 64
RWKV_HEADS = 12
RWKV_W = 768
RWKV_COLS = 2560
RWKV_LORA_OFF = 2304
GN_EPS = 64e-5
RWKV_CHUNK = 64
RWKV_SEQ_GROUP = 4
OFF_POOL = 2304
OFF_RWKV = 2816
OFF_GATE = 5376
PAST_LEN = 8192
PEER_HEADS = 8
N_KEYS = 128
N_EXPERTS = N_KEYS * N_KEYS
PEER_TOPK = 16
PEER_PAIRS = PEER_HEADS * PEER_TOPK
PEER_TB = 384
PEER_ACT_TB = 1408
PEER_EC = 1024
SUBLANES = 8
VMEM_LIMIT = 56 * 1024 * 1024

NEG_INF = float("-inf")


def _params(*sem):
    return pltpu.CompilerParams(dimension_semantics=sem, vmem_limit_bytes=VMEM_LIMIT)


def _dot(a, b):
    return jnp.dot(a.astype(BF16), b.astype(BF16), preferred_element_type=F32)


def _dot_nt(a, b):
    return lax.dot_general(a.astype(BF16), b.astype(BF16), (((1,), (1,)), ((), ())), preferred_element_type=F32)


def _norm_mm_kernel(x_ref, g_ref, w_ref, o_ref, xn_ref):
    @pl.when(pl.program_id(1) == 0)
    def _():
        x = x_ref[...]
        ms = jnp.mean(x * x, axis=-1, keepdims=True)
        xn_ref[...] = ((x * lax.rsqrt(ms + RMS_EPS)) * g_ref[...]).astype(BF16)

    o_ref[...] = jnp.dot(xn_ref[...], w_ref[...], preferred_element_type=F32)


def _norm_mm(x, g, w, *, tm, tn):
    rows, k = x.shape
    n = w.shape[1]
    return pl.pallas_call(
        _norm_mm_kernel, grid=(rows // tm, n // tn),
        in_specs=[
            pl.BlockSpec((tm, k), lambda i, j: (i, 0)),
            pl.BlockSpec((1, k), lambda i, j: (0, 0)),
            pl.BlockSpec((k, tn), lambda i, j: (0, j)),
        ],
        out_specs=pl.BlockSpec((tm, tn), lambda i, j: (i, j)),
        out_shape=jax.ShapeDtypeStruct((rows, n), F32),
        scratch_shapes=[pltpu.VMEM((tm, k), BF16)],
        compiler_params=_params("arbitrary", "arbitrary"), name="norm_mm",
    )(x, g.reshape(1, k), w)


def _split_bf16(a):
    hi = a.astype(BF16)
    lo = (a - hi.astype(F32)).astype(BF16)
    return hi, lo


def _dot3(ah, al, bh, bl, dims=(((1,), (0,)), ((), ()))):
    def d(p, q):
        return lax.dot_general(p, q, dims, preferred_element_type=F32)

    return d(ah, bh) + (d(ah, bl) + d(al, bh))


def _norm_mm3_kernel(x_ref, g_ref, wh_ref, wl_ref, o_ref, xh_ref, xl_ref):
    @pl.when(pl.program_id(1) == 0)
    def _():
        x = x_ref[...]
        ms = jnp.mean(x * x, axis=-1, keepdims=True)
        xh, xl = _split_bf16((x * lax.rsqrt(ms + RMS_EPS)) * g_ref[...])
        xh_ref[...] = xh
        xl_ref[...] = xl

    o_ref[...] = _dot3(xh_ref[...], xl_ref[...], wh_ref[...], wl_ref[...])


def _norm_mm3(x, g, wh, wl, *, tm, tn):
    rows, k = x.shape
    n = wh.shape[1]
    wspec = pl.BlockSpec((k, tn), lambda i, j: (0, j))
    return pl.pallas_call(
        _norm_mm3_kernel, grid=(rows // tm, n // tn),
        in_specs=[pl.BlockSpec((tm, k), lambda i, j: (i, 0)), pl.BlockSpec((1, k), lambda i, j: (0, 0)), wspec, wspec],
        out_specs=[pl.BlockSpec((tm, tn), lambda i, j: (i, j)), pl.BlockSpec((tm, k), lambda i, j: (i, 0))],
        out_shape=[jax.ShapeDtypeStruct((rows, n), F32), jax.ShapeDtypeStruct((rows, k), BF16)],
        scratch_shapes=[pltpu.VMEM((tm, k), BF16)],
        compiler_params=_params("arbitrary", "arbitrary"), name="norm_mm3",
    )(x, g.reshape(1, k), wh, wl)


def _rotate(x, cos, sin):
    lane = lax.broadcasted_iota(I32, x.shape, 1)
    first_half = (lane & (HEAD_DIM - 1)) < HEAD_DIM // 2
    partner = jnp.where(first_half, pltpu.roll(x, 128 - HEAD_DIM // 2, 1), pltpu.roll(x, HEAD_DIM // 2, 1))
    return x * cos + partner * sin


def _rope_tables(pos):
    half = HEAD_DIM // 2
    inv = ROPE_THETA ** (-jnp.arange(half, dtype=F32) / half)
    ang = pos.astype(F32)[:, None] * inv[None, :]
    cos = jnp.cos(ang)
    sin = jnp.sin(ang)
    cos_t = jnp.concatenate([cos, cos, cos, cos], axis=1)
    sin_t = jnp.concatenate([-sin, sin, -sin, sin], axis=1)
    return cos_t, sin_t


ROT_ROWS = 256


def _prompt_attn_kernel(q_ref, k_ref, v_ref, cos_ref, sin_ref, *rest, dil, seq, keep):
    o_ref, l_ref, kv_ref, qs, ks, vs = rest[-6:]
    chunks = ATT_OUT // 128
    for i in range(seq // ROT_ROWS):
        rs = slice(i * ROT_ROWS, (i + 1) * ROT_ROWS)
        cos = cos_ref[rs, :]
        sin = sin_ref[rs, :]
        for c in range(chunks):
            cs = slice(c * 128, (c + 1) * 128)
            qs[c, rs, :] = _rotate(q_ref[rs, cs], cos, sin)
            ks[c, rs, :] = _rotate(k_ref[rs, cs], cos, sin)
            vs[c, rs, :] = v_ref[rs, cs]
    for h in range(HEADS_PER_GROUP):
        c, half = divmod(h, 2)
        hs = slice(half * HEAD_DIM, (half + 1) * HEAD_DIM)
        for j in range(keep // 128):
            ps = slice(seq - keep + j * 128, seq - keep + (j + 1) * 128)
            kv_ref[0, 0, 0, h, :, j * 128:(j + 1) * 128] = ks[c, ps, hs].T
            kv_ref[0, 0, 1, h, :, j * 128:(j + 1) * 128] = vs[c, ps, hs].T

    def rows_of(start):
        return pl.ds(start, ATT_SPAN, stride=dil) if dil > 1 else pl.ds(start, ATT_SPAN)

    qi = lax.broadcasted_iota(I32, (ATT_SPAN, ATT_SPAN), 0)
    ki = lax.broadcasted_iota(I32, (ATT_SPAN, ATT_SPAN), 1)
    cur_ok = ki <= qi
    prev_ok = ki >= qi
    scale = HEAD_DIM ** -0.5
    n_blocks = seq // dil // ATT_SPAN
    for r in range(dil):
        for b in range(n_blocks):
            cur = rows_of(r + dil * b * ATT_SPAN)
            for c in range(chunks):
                q2, kc2, vc2 = qs[c, cur, :], ks[c, cur, :], vs[c, cur, :]
                if b > 0:
                    prev = rows_of(r + dil * (b - 1) * ATT_SPAN)
                    kp2, vp2 = ks[c, prev, :], vs[c, prev, :]
                outs, lses = [], []
                for half in range(2):
                    hs = slice(half * HEAD_DIM, (half + 1) * HEAD_DIM)
                    q = q2[:, hs]
                    sc = jnp.where(cur_ok, _dot_nt(q, kc2[:, hs]) * scale, NEG_INF)
                    m = jnp.max(sc, axis=-1, keepdims=True)
                    if b > 0:
                        sp = jnp.where(prev_ok, _dot_nt(q, kp2[:, hs]) * scale, NEG_INF)
                        m = jnp.maximum(m, jnp.max(sp, axis=-1, keepdims=True))
                    ec = jnp.exp(sc - m)
                    den = jnp.sum(ec, axis=-1, keepdims=True)
                    acc = _dot(ec, vc2[:, hs])
                    if b > 0:
                        ep = jnp.exp(sp - m)
                        den = den + jnp.sum(ep, axis=-1, keepdims=True)
                        acc = acc + _dot(ep, vp2[:, hs])
                    outs.append(acc / den)
                    lses.append(jnp.broadcast_to(m + jnp.log(den), (ATT_SPAN, HEAD_DIM)))
                o_ref[c, cur, :] = jnp.concatenate(outs, axis=1)
                l_ref[c, cur, :] = jnp.concatenate(lses, axis=1)


def _prompt_attn(z_qkv, cos_t, sin_t, gi, dil, layer, kv_prev, *, depth, n_seq, seq):
    rows_p = n_seq * seq
    win = ATT_GROUPS[gi][0]
    keep = min(win, seq)
    assert seq % (dil * ATT_SPAN) == 0 and keep % 128 == 0 and seq % ROT_ROWS == 0
    tab = pl.BlockSpec((seq, 128), lambda n: (0, 0))
    chunked = pl.BlockSpec((ATT_OUT // 128, seq, 128), lambda n: (0, n, 0))
    kv_shape = (depth, n_seq, 2, HEADS_PER_GROUP, HEAD_DIM, keep)
    kv_spec = pl.BlockSpec((1, 1) + kv_shape[2:], lambda n: (layer, n, 0, 0, 0, 0))
    in_specs = [pl.BlockSpec((seq, ATT_OUT), lambda n: (n, gi)),
                pl.BlockSpec((seq, ATT_OUT), lambda n: (n, ATT_W // ATT_OUT + gi)),
                pl.BlockSpec((seq, ATT_OUT), lambda n: (n, 2 * ATT_W // ATT_OUT + gi)), tab, tab]
    args = [z_qkv, z_qkv, z_qkv, cos_t, sin_t]
    aliases = {}
    if kv_prev is not None:
        in_specs.append(pl.BlockSpec(memory_space=pl.ANY))
        args.append(kv_prev)
        aliases = {len(args) - 1: 2}
    chunk_shape = jax.ShapeDtypeStruct((ATT_OUT // 128, rows_p, 128), F32)
    return pl.pallas_call(
        functools.partial(_prompt_attn_kernel, dil=dil, seq=seq, keep=keep),
        grid=(n_seq,), in_specs=in_specs,
        out_specs=[chunked, chunked, kv_spec],
        out_shape=[chunk_shape, chunk_shape, jax.ShapeDtypeStruct(kv_shape, F32)],
        scratch_shapes=[pltpu.VMEM((ATT_OUT // 128, seq, 128), F32)] * 3,
        input_output_aliases=aliases,
        compiler_params=_params("arbitrary"), name=f"prompt_attn_g{gi}",
    )(*args)


def _merge_groups(os_, ls_):
    m = jnp.maximum(jnp.maximum(ls_[0], ls_[1]), ls_[2])
    es = [jnp.exp(l - m) for l in ls_]
    tot = es[0] + es[1] + es[2]
    return (es[0] / tot) * os_[0] + (es[1] / tot) * os_[1] + (es[2] / tot) * os_[2]


def _merge_kernel(o0, o1, o2, l0, l1, l2, a_ref):
    for c in range(ATT_OUT // 128):
        a_ref[:, c * 128:(c + 1) * 128] = _merge_groups([o0[c], o1[c], o2[c]], [l0[c], l1[c], l2[c]])


def _merge_attn(os_, ls_, *, tm):
    rows = os_[0].shape[1]
    spec = pl.BlockSpec((ATT_OUT // 128, tm, 128), lambda i: (0, i, 0))
    return pl.pallas_call(
        _merge_kernel, grid=(rows // tm,), in_specs=[spec] * 6, out_specs=pl.BlockSpec((tm, ATT_OUT), lambda i: (i, 0)),
        out_shape=jax.ShapeDtypeStruct((rows, ATT_OUT), F32),
        compiler_params=_params("arbitrary"), name="merge_attn",
    )(*os_, *ls_)


def _sample_attn_kernel(z_ref, cos_ref, sin_ref, *rest, t_new):
    caches, (a_ref, n0_ref, n1_ref, n2_ref) = rest[0:3], rest[-4:]
    scale = HEAD_DIM ** -0.5
    cos = cos_ref[...]
    sin = sin_ref[...]
    n_chunks = ATT_W // 128
    q_c = [_rotate(z_ref[:, c * 128:(c + 1) * 128], cos, sin) for c in range(n_chunks)]
    k_c = [_rotate(z_ref[:, ATT_W + c * 128:ATT_W + (c + 1) * 128], cos, sin) for c in range(n_chunks)]
    v_c = [z_ref[:, 2 * ATT_W + c * 128:2 * ATT_W + (c + 1) * 128] for c in range(n_chunks)]
    outs_g, lses_g = [], []
    for gi, (c_ref, n_ref, (win, dil)) in enumerate(zip(caches, (n0_ref, n1_ref, n2_ref), ATT_GROUPS)):
        cache_len = c_ref.shape[-1]
        t_c = lax.broadcasted_iota(I32, (t_new, cache_len), 0)
        c_c = lax.broadcasted_iota(I32, (t_new, cache_len), 1)
        d_c = cache_len + t_c - c_c
        ok_c = jnp.logical_and((d_c & (dil - 1)) == 0, d_c <= ATT_SPAN * dil)
        t_n = lax.broadcasted_iota(I32, (t_new, t_new), 0)
        u_n = lax.broadcasted_iota(I32, (t_new, t_new), 1)
        d_n = t_n - u_n
        ok_n = jnp.logical_and(d_n >= 0, (d_n & (dil - 1)) == 0)
        outs, lses = [], []
        for h in range(HEADS_PER_GROUP):
            chunk, half = divmod(gi * HEADS_PER_GROUP + h, 2)
            hs = slice(half * HEAD_DIM, (half + 1) * HEAD_DIM)
            q, k_new, v_new = q_c[chunk][:, hs], k_c[chunk][:, hs], v_c[chunk][:, hs]
            k_t = c_ref[0, 0, 0, h]
            v_t = c_ref[0, 0, 1, h]
            s_c = jnp.where(ok_c, _dot(q, k_t) * scale, NEG_INF)
            s_n = jnp.where(ok_n, _dot_nt(q, k_new) * scale, NEG_INF)
            m = jnp.maximum(jnp.max(s_c, axis=-1, keepdims=True), jnp.max(s_n, axis=-1, keepdims=True))
            e_c = jnp.exp(s_c - m)
            e_n = jnp.exp(s_n - m)
            den = jnp.sum(e_c, axis=-1, keepdims=True) + jnp.sum(e_n, axis=-1, keepdims=True)
            acc = _dot_nt(e_c, v_t) + _dot(e_n, v_new)
            outs.append(acc / den)
            lses.append(jnp.broadcast_to(m + jnp.log(den), (t_new, HEAD_DIM)))
            n_ref[0, 0, 0, h] = jnp.concatenate([k_t[:, t_new:], k_new.T], axis=1)
            n_ref[0, 0, 1, h] = jnp.concatenate([v_t[:, t_new:], v_new.T], axis=1)
        outs_g.append(jnp.concatenate(outs, axis=1))
        lses_g.append(jnp.concatenate(lses, axis=1))
    a_ref[...] = _merge_groups(outs_g, lses_g)


def _sample_attn(z_qkv, cos_s, sin_s, caches_t, layer, new_prev, *, row0, n_seq, t_new):
    b0 = row0 // t_new

    def cache_spec(c):
        return pl.BlockSpec((1, 1) + c.shape[2:], lambda n: (layer, n, 0, 0, 0, 0))

    tab = pl.BlockSpec((t_new, 128), lambda n: (0, 0))
    in_specs = [pl.BlockSpec((t_new, 3 * ATT_W), lambda n: (b0 + n, 0)), tab, tab] + [cache_spec(c) for c in caches_t]
    args = [z_qkv, cos_s, sin_s, *caches_t]
    aliases = {}
    if new_prev is not None:
        for j, a in enumerate(new_prev):
            in_specs.append(pl.BlockSpec(memory_space=pl.ANY))
            args.append(a)
            aliases[len(args) - 1] = 1 + j
    return pl.pallas_call(
        functools.partial(_sample_attn_kernel, t_new=t_new),
        grid=(n_seq,), in_specs=in_specs,
        out_specs=[pl.BlockSpec((t_new, ATT_OUT), lambda n: (n, 0))] + [cache_spec(c) for c in caches_t],
        out_shape=[jax.ShapeDtypeStruct((n_seq * t_new, ATT_OUT), F32)]
        + [jax.ShapeDtypeStruct(c.shape, F32) for c in caches_t],
        input_output_aliases=aliases,
        compiler_params=_params("arbitrary"), name="sample_attn",
    )(*args)


def _pool_kernel(z_ref, h_ref, w_ref, s_ref, o_ref, buf_ref, *, t_len, pos0):
    buf_ref[0:POOL_HIST, :] = h_ref[0]
    buf_ref[POOL_HIST:POOL_HIST + t_len, :] = z_ref[...]
    pos = pos0 + lax.broadcasted_iota(I32, (t_len, POOL_GROUP), 0)
    for g, win in enumerate(POOL_WINDOWS):
        cs = slice(g * POOL_GROUP, (g + 1) * POOL_GROUP)
        z = buf_ref[POOL_HIST:POOL_HIST + t_len, cs]
        wsum = z
        for i in range(1, win):
            wsum = wsum + buf_ref[POOL_HIST - i:POOL_HIST - i + t_len, cs]
        cnt = jnp.minimum(win, pos + 1).astype(F32)
        y = wsum / cnt - z
        o_ref[:, cs] = _dot(y, w_ref[g]) * s_ref[:, cs]


def _pool_mixer(z_pool, hist, pool_w, pool_scale, *, row0, n_seq, t_len, pos0):
    b0 = row0 // t_len
    return pl.pallas_call(
        functools.partial(_pool_kernel, t_len=t_len, pos0=pos0),
        grid=(n_seq,),
        in_specs=[
            pl.BlockSpec((t_len, POOL_W), lambda n: (b0 + n, 0)),
            pl.BlockSpec((1, POOL_HIST, POOL_W), lambda n: (n, 0, 0)),
            pl.BlockSpec((len(POOL_WINDOWS), POOL_GROUP, POOL_GROUP), lambda n: (0, 0, 0)),
            pl.BlockSpec((1, POOL_W), lambda n: (0, 0)),
        ],
        out_specs=pl.BlockSpec((t_len, POOL_W), lambda n: (n, 0)),
        out_shape=jax.ShapeDtypeStruct((n_seq * t_len, POOL_W), F32),
        scratch_shapes=[pltpu.VMEM((POOL_HIST + t_len, POOL_W), F32)],
        compiler_params=_params("arbitrary"), name="pool_mixer",
    )(z_pool, hist, pool_w, pool_scale.reshape(1, POOL_W))


def _rwkv_prep_kernel(z_ref, zp_ref, first_ref, mu_ref, w0_ref, w2_ref, a0_ref, a2_ref, g2_ref, kk_ref, ka_ref, rk_ref,
                      r_o, lw_o, k_o, v_o, kn_o, b_o, g_o, bonus_o, buf_ref, *, tm):
    i = pl.program_id(1)
    z = z_ref[...]
    prev_row = jnp.where(i == 0, first_ref[0], zp_ref[SUBLANES - 1:SUBLANES, :])
    buf_ref[SUBLANES:SUBLANES + tm, :] = z
    buf_ref[SUBLANES - 1:SUBLANES, :] = prev_row
    shifted = buf_ref[SUBLANES - 1:SUBLANES - 1 + tm, :]
    xm = z + mu_ref[...] * (shifted - z)
    r = xm[:, 0:RWKV_W]
    k = xm[:, RWKV_W:2 * RWKV_W]
    v = xm[:, 2 * RWKV_W:3 * RWKV_W]
    wa = xm[:, RWKV_LORA_OFF:RWKV_LORA_OFF + 128]
    gl = xm[:, RWKV_LORA_OFF + 128:RWKV_COLS]
    xw = w0_ref[...] + _dot(jnp.tanh(wa), w2_ref[...])
    logw = -math.exp(-0.5) * jax.nn.sigmoid(xw)
    a = jax.nn.sigmoid(a0_ref[...] + _dot(wa, a2_ref[...]))
    g_o[...] = _dot(jax.nn.sigmoid(gl), g2_ref[...])
    kkr = k * kk_ref[...]
    kmod = k * (1.0 + (a - 1.0) * ka_ref[...])
    rkk = r * kmod * rk_ref[...]
    bonus = []
    for h in range(RWKV_HEADS):
        sl = slice(h * RWKV_HEAD, (h + 1) * RWKV_HEAD)
        kh = kkr[:, sl]
        nrm = jnp.sqrt(jnp.sum(kh * kh, axis=-1, keepdims=True))
        kn = kh / jnp.maximum(nrm, 1e-12)
        r_o[0, h] = r[:, sl]
        lw_o[0, h] = logw[:, sl]
        k_o[0, h] = kmod[:, sl]
        v_o[0, h] = v[:, sl]
        kn_o[0, h] = kn
        b_o[0, h] = kn * a[:, sl]
        bonus.append(jnp.sum(rkk[:, sl], axis=-1, keepdims=True) * v[:, sl])
    bonus_o[...] = jnp.concatenate(bonus, axis=1)


def _rwkv_prep(z, first_prev, lw, *, row0, n_seq, t_len, tm):
    nblk = t_len // tm
    pb = tm // SUBLANES
    b0 = row0 // tm
    p0 = row0 // SUBLANES

    def vec(n):
        return pl.BlockSpec((1, n), lambda s, i: (0, 0))

    hm = jax.ShapeDtypeStruct((n_seq, RWKV_HEADS, t_len, RWKV_HEAD), F32)
    rm = jax.ShapeDtypeStruct((n_seq * t_len, RWKV_W), F32)
    hm_spec = pl.BlockSpec((1, RWKV_HEADS, tm, RWKV_HEAD), lambda s, i: (s, 0, i, 0))
    rm_spec = pl.BlockSpec((tm, RWKV_W), lambda s, i: (s * nblk + i, 0))
    return pl.pallas_call(
        functools.partial(_rwkv_prep_kernel, tm=tm),
        grid=(n_seq, nblk),
        in_specs=[
            pl.BlockSpec((tm, RWKV_COLS), lambda s, i: (b0 + s * nblk + i, 0)),
            pl.BlockSpec((SUBLANES, RWKV_COLS), lambda s, i: (jnp.maximum(p0 + (s * nblk + i) * pb - 1, 0), 0)),
            pl.BlockSpec((1, 1, RWKV_COLS), lambda s, i: (s, 0, 0)),
            vec(RWKV_COLS), vec(RWKV_W),
            pl.BlockSpec((128, RWKV_W), lambda s, i: (0, 0)),
            vec(RWKV_W),
            pl.BlockSpec((128, RWKV_W), lambda s, i: (0, 0)),
            pl.BlockSpec((128, RWKV_W), lambda s, i: (0, 0)),
            vec(RWKV_W), vec(RWKV_W), vec(RWKV_W),
        ],
        out_specs=[hm_spec] * 6 + [rm_spec] * 2,
        out_shape=[hm] * 6 + [rm] * 2,
        scratch_shapes=[pltpu.VMEM((tm + SUBLANES, RWKV_COLS), F32)],
        compiler_params=_params("arbitrary", "arbitrary"), name="rwkv_prep",
    )(z, z, first_prev, lw["mu"], lw["w0"], lw["w2p"], lw["a0"], lw["a2p"], lw["g2"], lw["kk"], lw["ka"], lw["rk"])


def _bmm(a, b):
    return jnp.einsum("hqk,hkd->hqd", a.astype(BF16), b.astype(BF16), preferred_element_type=F32)


def _bmm_nt(a, b):
    return jnp.einsum("hqd,hkd->hqk", a.astype(BF16), b.astype(BF16), preferred_element_type=F32)


def _bmm_tn(a, b):
    return jnp.einsum("hkq,hkd->hqd", a.astype(BF16), b.astype(BF16), preferred_element_type=F32)


def _rwkv_scan_kernel(r_ref, lw_ref, k_ref, v_ref, kn_ref, b_ref, s0_ref, y_ref, st_ref, s_scr, *, chunk, group):
    c = pl.program_id(1)
    nh = group * RWKV_HEADS

    def heads(ref):
        return ref[...].reshape((nh,) + ref.shape[2:])

    @pl.when(c == 0)
    def _():
        s_scr[...] = heads(s0_ref)

    r, logw, k, v, kn, b = [heads(ref) for ref in (r_ref, lw_ref, k_ref, v_ref, kn_ref, b_ref)]
    row = lax.broadcasted_iota(I32, (chunk, chunk), 0)
    col = lax.broadcasted_iota(I32, (chunk, chunk), 1)
    incl = row >= col
    strict = row > col
    tri = jnp.broadcast_to(incl.astype(BF16)[None], (nh, chunk, chunk))
    lw_hi = logw.astype(BF16)
    lw_lo = (logw - lw_hi.astype(F32)).astype(BF16)
    cum = (jnp.einsum("hqk,hkd->hqd", tri, lw_hi, preferred_element_type=F32)
           + jnp.einsum("hqk,hkd->hqd", tri, lw_lo, preferred_element_type=F32))
    p_inv = jnp.exp(-cum)
    kt = k * p_inv
    bt = b * p_inv
    kap = kn * jnp.exp(cum - logw)
    rho = r * jnp.exp(cum)
    qq = jnp.concatenate([kap, rho], axis=1)
    gram = _bmm_nt(qq, jnp.concatenate([kt, bt], axis=1))
    a_k = jnp.where(strict[None], gram[:, :chunk, :chunk], 0.0)
    a_b = jnp.where(strict[None], gram[:, :chunk, chunk:], 0.0)
    l_k = jnp.where(incl[None], gram[:, chunk:, :chunk], 0.0)
    l_b = jnp.where(incl[None], gram[:, chunk:, chunk:], 0.0)
    x = jnp.broadcast_to((row == col).astype(F32)[None], (nh, chunk, chunk))
    m = 1
    while m < chunk:
        sh = m.bit_length() - 1
        same = (row >> (sh + 1)) == (col >> (sh + 1))
        lower_left = jnp.logical_and(((row >> sh) & 1) == 1, ((col >> sh) & 1) == 0)
        off = jnp.where(jnp.logical_and(same, lower_left)[None], a_b, 0.0)
        x = x - _bmm(_bmm(x, off), x)
        m *= 2
    s = s_scr[...]
    qs = _bmm_nt(qq, s)
    u = _bmm(x, -(qs[:, :chunk] + _bmm(a_k, v)))
    y = qs[:, chunk:] + _bmm(jnp.concatenate([l_k, l_b], axis=2), jnp.concatenate([v, u], axis=1))
    s_new = (s + _bmm_tn(jnp.concatenate([v, u], axis=1), jnp.concatenate([kt, bt], axis=1))) * jnp.exp(cum[:, chunk - 1:chunk, :])
    s_scr[...] = s_new
    st_ref[...] = s_new.reshape(st_ref.shape)
    mean = jnp.mean(y, axis=-1, keepdims=True)
    var = jnp.mean(jnp.square(y - mean), axis=-1, keepdims=True)
    yn = (y - mean) * lax.rsqrt(var + GN_EPS)
    for g in range(group):
        y_ref[g] = jnp.concatenate([yn[g * RWKV_HEADS + h] for h in range(RWKV_HEADS)], axis=1)


def _rwkv_scan(prep, s0, *, chunk, group):
    r, lw, k, v, kn, b = prep
    n_seq, nh, t_len, hd = r.shape
    hm_spec = pl.BlockSpec((group, nh, chunk, hd), lambda s, c: (s, 0, c, 0))
    st_spec = pl.BlockSpec((group, nh, hd, hd), lambda s, c: (s, 0, 0, 0))
    return pl.pallas_call(
        functools.partial(_rwkv_scan_kernel, chunk=chunk, group=group),
        grid=(n_seq // group, t_len // chunk),
        in_specs=[hm_spec] * 6 + [st_spec],
        out_specs=[pl.BlockSpec((group, chunk, RWKV_W), lambda s, c: (s, c, 0)), st_spec],
        out_shape=[jax.ShapeDtypeStruct((n_seq, t_len, RWKV_W), F32), jax.ShapeDtypeStruct((n_seq, nh, hd, hd), F32)],
        scratch_shapes=[pltpu.VMEM((group * nh, hd, hd), F32)],
        compiler_params=_params("arbitrary", "arbitrary"), name="rwkv_scan",
    )(r, lw, k, v, kn, b, s0)


def _mix_kernel(*refs, n_prompt_blocks):
    pairs, (gate_ref, lnw_ref, lnb_ref, wa_ref, wp_ref, wr_ref, o_ref) = refs[:10], refs[10:]
    is_sample = pl.program_id(0) >= n_prompt_blocks
    att, pool, yn, bonus, g = [jnp.where(is_sample, pairs[2 * j + 1][...], pairs[2 * j][...]) for j in range(5)]
    rw = (yn * lnw_ref[...] + lnb_ref[...] + bonus) * g
    mix = jax.nn.sigmoid(gate_ref[:, 0:D_MODEL]) * _dot(att, wa_ref[...])
    mix = mix + jax.nn.sigmoid(gate_ref[:, D_MODEL:2 * D_MODEL]) * _dot(pool, wp_ref[...])
    mix = mix + jax.nn.sigmoid(gate_ref[:, 2 * D_MODEL:3 * D_MODEL]) * _dot(rw, wr_ref[...])
    o_ref[...] = mix.astype(BF16)


def _mix(branches, z_gate, ln_w, ln_b, wa, wp, wr, *, tm):
    rows = z_gate.shape[0]
    n_prompt_blocks = branches[0][0].shape[0] // tm
    assert rows == (n_prompt_blocks + 1) * tm and all(s.shape[0] == tm for _, s in branches)

    def full(a):
        return pl.BlockSpec(a.shape, lambda i: (0, 0))

    specs, args = [], []
    for p, s in branches:
        specs += [pl.BlockSpec((tm, p.shape[1]), lambda i: (jnp.minimum(i, n_prompt_blocks - 1), 0)), full(s)]
        args += [p, s]
    ln_w = ln_w.reshape(1, RWKV_W)
    ln_b = ln_b.reshape(1, RWKV_W)
    return pl.pallas_call(
        functools.partial(_mix_kernel, n_prompt_blocks=n_prompt_blocks), grid=(rows // tm,),
        in_specs=specs + [pl.BlockSpec((tm, 3 * D_MODEL), lambda i: (i, 0)),
                          full(ln_w), full(ln_b), full(wa), full(wp), full(wr)],
        out_specs=pl.BlockSpec((tm, D_MODEL), lambda i: (i, 0)),
        out_shape=jax.ShapeDtypeStruct((rows, D_MODEL), BF16),
        compiler_params=_params("arbitrary"), name="gated_mix",
    )(*args, z_gate, ln_w, ln_b, wa, wp, wr)


def _mm_res_kernel(x_ref, a_ref, w_ref, o_ref):
    o_ref[...] = x_ref[...] + jnp.dot(a_ref[...], w_ref[...], preferred_element_type=F32)


def _mm_residual(x, a, w, *, tm):
    rows, n = x.shape
    k = a.shape[1]
    return pl.pallas_call(
        _mm_res_kernel, grid=(rows // tm,),
        in_specs=[pl.BlockSpec((tm, n), lambda i: (i, 0)), pl.BlockSpec((tm, k), lambda i: (i, 0)),
                  pl.BlockSpec((k, n), lambda i: (0, 0))],
        out_specs=pl.BlockSpec((tm, n), lambda i: (i, 0)),
        out_shape=jax.ShapeDtypeStruct((rows, n), F32),
        compiler_params=_params("arbitrary"), name="out_proj",
    )(x, a, w)


def _topk_cols(scores, ids, ids_ordered, k):
    lanes = scores[0].shape[1]
    iota_k = lax.broadcasted_iota(I32, (k, lanes), 0)
    big = jnp.iinfo(jnp.int32).max

    def arg_max(s, ident, ordered):
        if ordered:
            tiles = [(s[i:i + SUBLANES], ident[i:i + SUBLANES]) for i in range(0, s.shape[0], SUBLANES)]
            while len(tiles) > 1:
                merged = []
                for (va, ia), (vb, ib) in zip(tiles[0::2], tiles[1::2]):
                    later = vb > va
                    merged.append((jnp.where(later, vb, va), jnp.where(later, ib, ia)))
                tiles = merged + tiles[len(tiles) - len(tiles) % 2:]
            s, ident = tiles[0]
        m = jnp.max(s, axis=0, keepdims=True)
        return m, jnp.min(jnp.where(s == m, ident, big), axis=0, keepdims=True)

    def body(j, carry):
        out = []
        for (s, vals, idxs), ident, ordered in zip(carry, ids, ids_ordered):
            m, idx = arg_max(s, ident, ordered)
            vals = jnp.where(iota_k == j, m, vals)
            idxs = jnp.where(iota_k == j, idx, idxs)
            out.append((jnp.where(ident == idx, NEG_INF, s), vals, idxs))
        return tuple(out)

    init = tuple((s, jnp.zeros((k, lanes), F32), jnp.zeros((k, lanes), I32)) for s in scores)
    return [(v, i) for _, v, i in lax.fori_loop(0, k, body, init)]


def _gather_rows(table, sel, k):
    out = jnp.zeros(sel.shape, table.dtype)
    for a in range(k):
        out = jnp.where(sel == a, table[a:a + 1, :], out)
    return out


def _peer_select_kernel(q_ref, skh_ref, skl_ref, i1_o, i2_o, gate_o, i1_s, i2_s, g_s, *, tok):
    kk = PEER_TOPK
    half = kk // 2
    nt = (((1,), (1,)), ((), ()))
    key_id = lax.broadcasted_iota(I32, (N_KEYS, tok), 0)
    row = lax.broadcasted_iota(I32, (half * half + kk, tok), 0)
    cand_id = jnp.where(row < half * half, (row >> (half.bit_length() - 1)) * kk + (row & (half - 1)),
                        jnp.where(row < half * half + half, row - half * half + half, (row - half * half) * kk))

    def finish(slot, top, sel, k1, k2):
        e1 = _gather_rows(k1, sel >> 4, kk)
        e2 = _gather_rows(k2, sel & (kk - 1), kk)
        ex = jnp.exp(top - jnp.max(top, axis=0, keepdims=True))
        gate = ex / jnp.sum(ex, axis=0, keepdims=True)
        r0 = pl.multiple_of(slot * kk, kk)
        i1_s[pl.ds(r0, kk), :] = e1.astype(F32)
        i2_s[pl.ds(r0, kk), :] = e2.astype(F32)
        g_s[pl.ds(r0, kk), :] = gate

    def head(h, carry):
        cand_prev, k1_prev, k2_prev = carry
        c0 = pl.multiple_of(h * 2 * N_KEYS, 2 * N_KEYS)
        q1h, q1l = _split_bf16(q_ref[:, pl.ds(c0, N_KEYS)])
        q2h, q2l = _split_bf16(q_ref[:, pl.ds(c0 + N_KEYS, N_KEYS)])
        s1 = _dot3(skh_ref[h, 0], skl_ref[h, 0], q1h, q1l, nt)
        s2 = _dot3(skh_ref[h, 1], skl_ref[h, 1], q2h, q2l, nt)
        (t1, k1), (t2, k2), (top, sel) = _topk_cols([s1, s2, cand_prev], [key_id, key_id, cand_id],
                                                    [True, True, False], kk)
        finish(jnp.where(h == 0, PEER_HEADS, h - 1), top, sel, k1_prev, k2_prev)
        cand = jnp.concatenate([t1[a:a + 1, :] + t2[0:half, :] for a in range(half)]
                               + [t1[0:1, :] + t2[half:kk, :], t1[half:kk, :] + t2[0:1, :]], axis=0)
        return cand, k1, k2

    zeros_i = jnp.zeros((kk, tok), I32)
    cand, k1, k2 = lax.fori_loop(0, PEER_HEADS, head, (jnp.zeros(cand_id.shape, F32), zeros_i, zeros_i))
    ((top, sel),) = _topk_cols([cand], [cand_id], [False], kk)
    finish(PEER_HEADS - 1, top, sel, k1, k2)
    i1_o[...] = i1_s[0:PEER_PAIRS, :].T.astype(I32)
    i2_o[...] = i2_s[0:PEER_PAIRS, :].T.astype(I32)
    gate_o[...] = g_s[0:PEER_PAIRS, :].T


def _peer_select(q, sk_hi, sk_lo, *, tok=128):
    rows = q.shape[0]
    spec = pl.BlockSpec((tok, PEER_PAIRS), lambda i: (i, 0))
    return pl.pallas_call(
        functools.partial(_peer_select_kernel, tok=tok), grid=(rows // tok,),
        in_specs=[pl.BlockSpec((tok, q.shape[1]), lambda i: (i, 0)),
                  pl.BlockSpec(sk_hi.shape, lambda i: (0, 0, 0, 0)), pl.BlockSpec(sk_lo.shape, lambda i: (0, 0, 0, 0))],
        out_specs=[spec] * 3,
        out_shape=[jax.ShapeDtypeStruct((rows, PEER_PAIRS), I32), jax.ShapeDtypeStruct((rows, PEER_PAIRS), I32),
                   jax.ShapeDtypeStruct((rows, PEER_PAIRS), F32)],
        scratch_shapes=[pltpu.VMEM((PEER_PAIRS + PEER_TOPK, tok), F32)] * 3,
        compiler_params=_params("arbitrary"), name="peer_select",
    )(q, sk_hi, sk_lo)


def _peer_pick(d_ref, i1_ref, i2_ref, act_ref, first_row):
    n_rows = d_ref.shape[1] // N_KEYS
    for g in range(act_ref.shape[0] // SUBLANES):
        rs = slice(g * SUBLANES, (g + 1) * SUBLANES)
        i1 = i1_ref[rs, :]
        i2 = i2_ref[rs, :]
        acc = act_ref[rs, :]
        for j in range(n_rows):
            got = jnp.take_along_axis(d_ref[rs, j * N_KEYS:(j + 1) * N_KEYS], i2, axis=1, mode="promise_in_bounds")
            acc = acc + jnp.where(i1 == first_row + j, got, 0.0)
        act_ref[rs, :] = acc


def _peer_act_kernel(xn_ref, u_ref, i1_ref, i2_ref, act_ref, da_ref, db_ref):
    first = jnp.logical_and(pl.program_id(0) == 0, pl.program_id(1) == 0)
    c = pl.program_id(1)
    slab = PEER_EC // 2
    slab_rows = slab // N_KEYS
    nt = (((1,), (1,)), ((), ()))

    @pl.when(first)
    def _():
        db_ref[...] = jnp.zeros(db_ref.shape, F32)

    @pl.when(c == 0)
    def _():
        act_ref[...] = jnp.zeros(act_ref.shape, F32)

    base = c * 2 * slab_rows
    _peer_pick(db_ref, i1_ref, i2_ref, act_ref, base - slab_rows)
    da_ref[...] = lax.dot_general(xn_ref[...], u_ref[0, 0:slab, :].astype(BF16), nt, preferred_element_type=F32)
    _peer_pick(da_ref, i1_ref, i2_ref, act_ref, base)
    db_ref[...] = lax.dot_general(xn_ref[...], u_ref[0, slab:2 * slab, :].astype(BF16), nt, preferred_element_type=F32)

    @pl.when(c == pl.num_programs(1) - 1)
    def _():
        _peer_pick(db_ref, i1_ref, i2_ref, act_ref, base + slab_rows)


def _peer_act(xn, u_tabs, layer, i1, i2, *, tb):
    rows = xn.shape[0]
    pair_spec = pl.BlockSpec((tb, PEER_PAIRS), lambda i, c: (i, 0))
    return pl.pallas_call(
        _peer_act_kernel, grid=(rows // tb, N_EXPERTS // PEER_EC),
        in_specs=[pl.BlockSpec((tb, D_MODEL), lambda i, c: (i, 0)),
                  pl.BlockSpec((1, PEER_EC, D_MODEL), lambda i, c: (layer, c, 0)), pair_spec, pair_spec],
        out_specs=pair_spec,
        out_shape=jax.ShapeDtypeStruct((rows, PEER_PAIRS), F32),
        scratch_shapes=[pltpu.VMEM((tb, PEER_EC // 2), F32)] * 2,
        compiler_params=_params("arbitrary", "arbitrary"), name="peer_act",
    )(xn, u_tabs, i1, i2)


def _peer_out_kernel(i1_ref, i2_ref, gate_ref, act_ref, v_ref, o_ref, w_ref):
    c = pl.program_id(1)
    n_groups = w_ref.shape[0]
    rows_per_chunk = PEER_EC // N_KEYS

    @pl.when(c == 0)
    def _():
        key_iota = lax.broadcasted_iota(I32, (N_KEYS, PEER_PAIRS), 0)

        def group(g, carry):
            r0 = pl.multiple_of(g * SUBLANES, SUBLANES)
            i1 = i1_ref[pl.ds(r0, SUBLANES), :]
            i2 = i2_ref[pl.ds(r0, SUBLANES), :]
            a = act_ref[pl.ds(r0, SUBLANES), :]
            wgt = gate_ref[pl.ds(r0, SUBLANES), :] * (0.5 * a * (1.0 + lax.erf(a * (1.0 / math.sqrt(2.0)))))
            for t in range(SUBLANES):
                hit1 = key_iota == jnp.broadcast_to(i1[t:t + 1, :], key_iota.shape)
                hit2 = key_iota == jnp.broadcast_to(i2[t:t + 1, :], key_iota.shape)
                w2 = jnp.where(hit2, jnp.broadcast_to(wgt[t:t + 1, :], key_iota.shape), 0.0)
                w_tok = _dot_nt(jnp.where(hit1, 1.0, 0.0), w2)
                w_ref[g, pl.ds(t, N_KEYS, stride=SUBLANES), :] = w_tok
            return carry

        lax.fori_loop(0, n_groups, group, 0)

    row0 = c * (rows_per_chunk * SUBLANES)
    lhs = jnp.concatenate(
        [w_ref[:, pl.ds(pl.multiple_of(row0 + j * SUBLANES, SUBLANES), SUBLANES), :].reshape(n_groups * SUBLANES, N_KEYS)
         for j in range(rows_per_chunk)], axis=1)
    part = jnp.dot(lhs.astype(BF16), v_ref[0], preferred_element_type=F32)

    @pl.when(c == 0)
    def _():
        o_ref[...] = part

    @pl.when(c > 0)
    def _():
        o_ref[...] += part


def _peer_out(i1, i2, gate, act, v_tabs, layer):
    rows = i1.shape[0]
    tb = PEER_TB
    pair_spec = pl.BlockSpec((tb, PEER_PAIRS), lambda i, c: (i, 0))
    return pl.pallas_call(
        _peer_out_kernel, grid=(rows // tb, N_EXPERTS // PEER_EC),
        in_specs=[pair_spec] * 4 + [pl.BlockSpec((1, PEER_EC, D_MODEL), lambda i, c: (layer, c, 0))],
        out_specs=pl.BlockSpec((tb, D_MODEL), lambda i, c: (i, 0)),
        out_shape=jax.ShapeDtypeStruct((rows, D_MODEL), F32),
        scratch_shapes=[pltpu.VMEM((tb // SUBLANES, N_KEYS * SUBLANES, N_KEYS), F32)],
        compiler_params=_params("arbitrary", "arbitrary"), name="peer_out",
    )(i1, i2, gate, act, v_tabs)


def _ple_kernel(x_ref, f_ref, p_ref, g_ref, wg_ref, wp_ref, gf_ref, o_ref, *, final):
    x = x_ref[...] + f_ref[...]
    ms = jnp.mean(x * x, axis=-1, keepdims=True)
    h = (x * lax.rsqrt(ms + RMS_EPS)) * g_ref[...]
    out = x + jax.nn.sigmoid(_dot(h, wg_ref[...])) * _dot(p_ref[...], wp_ref[...])
    if final:
        ms2 = jnp.mean(out * out, axis=-1, keepdims=True)
        out = (out * lax.rsqrt(ms2 + RMS_EPS)) * gf_ref[...]
    o_ref[...] = out


def _ple(x, ffn, p, g, wg, wp, g_final, *, tm, final):
    rows, n = x.shape
    row = pl.BlockSpec((tm, n), lambda i: (i, 0))
    vec = pl.BlockSpec((1, n), lambda i: (0, 0))
    return pl.pallas_call(
        functools.partial(_ple_kernel, final=final), grid=(rows // tm,),
        in_specs=[row, row, pl.BlockSpec((tm, p.shape[1]), lambda i: (i, 0)), vec,
                  pl.BlockSpec(wg.shape, lambda i: (0, 0)), pl.BlockSpec(wp.shape, lambda i: (0, 0)), vec],
        out_specs=row,
        out_shape=jax.ShapeDtypeStruct((rows, n), F32),
        compiler_params=_params("arbitrary"), name="ple",
    )(x, ffn, p, g.reshape(1, n), wg, wp, g_final.reshape(1, n))


def _layer(x, p_rows, lw, state, g_final, dims, final):
    n_p, t_p, n_s, t_s, past = dims
    rows_p = n_p * t_p
    rows_s = n_s * t_s

    z_qkv = _norm_mm(x, lw["g_mix"], lw["w_qkv"], tm=768, tn=768)
    z_pool = _norm_mm(x, lw["g_mix"], lw["w_pool"], tm=768, tn=POOL_W)
    z_rwkv = _norm_mm(x, lw["g_mix"], lw["w_rwkv"], tm=768, tn=RWKV_COLS // 2)
    z_gate = _norm_mm(x, lw["g_mix"], lw["w_gate"], tm=768, tn=768)

    layer, depth = lw["layer"], lw["depth"]
    os_, ls_, kv_p = [], [], []
    for gi, (_, dil) in enumerate(ATT_GROUPS):
        o, l, kv = _prompt_attn(z_qkv, lw["cos_p"], lw["sin_p"], gi, dil, layer,
                                None if state["kv_p"] is None else state["kv_p"][gi], depth=depth, n_seq=n_p, seq=t_p)
        os_.append(o)
        ls_.append(l)
        kv_p.append(kv)
    att_p = _merge_attn(os_, ls_, tm=512)
    att_s, *kv_s = _sample_attn(z_qkv, lw["cos_s"], lw["sin_s"], state["caches_t"], layer, state["kv_s"],
                                row0=rows_p, n_seq=n_s, t_new=t_s)

    pool_p = _pool_mixer(z_pool, jnp.zeros((n_p, POOL_HIST, POOL_W), F32), lw["pool_w"], lw["pool_scale"],
                         row0=0, n_seq=n_p, t_len=t_p, pos0=0)
    hist_s = jnp.concatenate([jnp.zeros((n_s, 1, POOL_W), F32), state["pool"]], axis=1)
    pool_s = _pool_mixer(z_pool, hist_s, lw["pool_w"], lw["pool_scale"], row0=rows_p, n_seq=n_s, t_len=t_s, pos0=past)

    prep_p = _rwkv_prep(z_rwkv, jnp.zeros((n_p, 1, RWKV_COLS), F32), lw, row0=0, n_seq=n_p, t_len=t_p, tm=256)
    prep_s = _rwkv_prep(z_rwkv, state["shift"][:, None, :], lw, row0=rows_p, n_seq=n_s, t_len=t_s, tm=t_s)
    yn_p, wkv_p = _rwkv_scan(prep_p[:6], jnp.zeros((n_p, RWKV_HEADS, RWKV_HEAD, RWKV_HEAD), F32),
                             chunk=RWKV_CHUNK, group=RWKV_SEQ_GROUP)
    yn_s, wkv_s = _rwkv_scan(prep_s[:6], state["wkv"], chunk=t_s, group=RWKV_SEQ_GROUP)
    branches = [(att_p, att_s), (pool_p, pool_s), (yn_p.reshape(rows_p, RWKV_W), yn_s.reshape(rows_s, RWKV_W)),
                (prep_p[7], prep_s[7]), (prep_p[6], prep_s[6])]

    mix = _mix(branches, z_gate, lw["ln_w"], lw["ln_b"], lw["w_attn_o"], lw["w_pool_o"], lw["w_rwkv_o"], tm=rows_s)
    x = _mm_residual(x, mix, lw["w_out"], tm=384)

    q, xn = _norm_mm3(x, lw["g_ffn"], *lw["peer_wq"], tm=768, tn=512)
    i1, i2, gate = _peer_select(q, *lw["peer_subkeys"])
    act = _peer_act(xn, lw["peer_u"], lw["layer"], i1, i2, tb=PEER_ACT_TB)
    ffn = _peer_out(i1, i2, gate, act, lw["peer_v"], lw["layer"])

    x = _ple(x, ffn, p_rows, lw["g_ple"], lw["ple_wg"], lw["ple_wp"], g_final, tm=256, final=final)

    keep = POOL_HIST - 1
    pool_p_state = jnp.stack([z_pool[(n + 1) * t_p - keep:(n + 1) * t_p] for n in range(n_p)])
    zp_s = z_pool[rows_p:].reshape(n_s, t_s, POOL_W)
    new_p = [pool_p_state, wkv_p, z_rwkv[t_p - 1:rows_p:t_p]]
    new_s = [jnp.concatenate([hist_s[:, 1:], zp_s], axis=1)[:, -keep:], wkv_s, z_rwkv[rows_p + t_s - 1::t_s]]
    return x, kv_p, kv_s, new_p, new_s


def kernel(x_prompt, x_sample, p_prompt, p_sample, cache_attn_w128, cache_attn_w512, cache_attn_w2048, state_pool, state_rwkv_wkv, state_rwkv_shift, g_mix, w_in, w_attn_o, w_pool_o, w_rwkv_o, w_out, pool_w, pool_scale, rwkv_mu, rwkv_w0, rwkv_w2, rwkv_a0, rwkv_a2, rwkv_g2, rwkv_kk, rwkv_ka, rwkv_rk, rwkv_ln_w, rwkv_ln_b, g_ffn, peer_wq, peer_subkeys, peer_u, peer_v, g_ple, ple_wg, ple_wp, g_final):
    n_p, t_p, _ = x_prompt.shape
    n_s, t_s, _ = x_sample.shape
    depth = w_in.shape[0]
    past = PAST_LEN
    rows_p = n_p * t_p
    rows_s = n_s * t_s
    dims = (n_p, t_p, n_s, t_s, past)

    x = jnp.concatenate([x_prompt.reshape(rows_p, D_MODEL), x_sample.reshape(rows_s, D_MODEL)], axis=0)
    cos_p, sin_p = _rope_tables(jnp.arange(t_p, dtype=I32))
    cos_s, sin_s = _rope_tables(past + jnp.arange(t_s, dtype=I32))
    lora_pad = jnp.zeros((128 - 64, RWKV_W), F32)
    peer_v_bf16 = peer_v.astype(BF16)
    to_dev = (0, 1, 3, 4, 5, 2)
    from_dev = (0, 1, 5, 2, 3, 4)
    caches_t = [jnp.transpose(c, to_dev) for c in (cache_attn_w128, cache_attn_w512, cache_attn_w2048)]

    def row(a):
        return a.reshape(1, -1)

    new_p, new_s = [], []
    kv_p = kv_s = None
    for l in range(depth):
        wl = w_in[l]
        lw = {
            "g_mix": g_mix[l],
            "w_qkv": wl[:, :OFF_POOL].astype(BF16), "w_pool": wl[:, OFF_POOL:OFF_RWKV].astype(BF16),
            "w_rwkv": wl[:, OFF_RWKV:OFF_GATE].astype(BF16), "w_gate": wl[:, OFF_GATE:].astype(BF16),
            "cos_p": cos_p, "sin_p": sin_p, "cos_s": cos_s, "sin_s": sin_s, "depth": depth,
            "pool_w": pool_w[l].astype(BF16), "pool_scale": pool_scale[l],
            "mu": row(rwkv_mu[l]), "w0": row(rwkv_w0[l]), "a0": row(rwkv_a0[l]),
            "w2p": jnp.concatenate([rwkv_w2[l], lora_pad], axis=0).astype(BF16),
            "a2p": jnp.concatenate([lora_pad, rwkv_a2[l]], axis=0).astype(BF16),
            "g2": rwkv_g2[l].astype(BF16),
            "kk": row(rwkv_kk[l]), "ka": row(rwkv_ka[l]), "rk": row(rwkv_rk[l]),
            "ln_w": rwkv_ln_w[l], "ln_b": rwkv_ln_b[l],
            "w_attn_o": w_attn_o[l].astype(BF16), "w_pool_o": w_pool_o[l].astype(BF16),
            "w_rwkv_o": w_rwkv_o[l].astype(BF16), "w_out": w_out[l].astype(BF16),
            "g_ffn": g_ffn[l], "peer_wq": _split_bf16(peer_wq[l]), "peer_subkeys": _split_bf16(peer_subkeys[l]),
            "peer_u": peer_u, "peer_v": peer_v_bf16, "layer": l,
            "g_ple": g_ple[l], "ple_wg": ple_wg[l].astype(BF16), "ple_wp": ple_wp[l].astype(BF16),
        }
        state = {"caches_t": caches_t, "kv_p": kv_p, "kv_s": kv_s, "pool": state_pool[l],
                 "wkv": state_rwkv_wkv[l], "shift": state_rwkv_shift[l]}
        p_rows = jnp.concatenate([p_prompt[l].reshape(rows_p, -1), p_sample[l].reshape(rows_s, -1)], axis=0).astype(BF16)
        x, kv_p, kv_s, st_p, st_s = _layer(x, p_rows, lw, state, g_final, dims, l == depth - 1)
        new_p.append(st_p)
        new_s.append(st_s)

    outs = [x[:rows_p].reshape(n_p, t_p, D_MODEL), x[rows_p:].reshape(n_s, t_s, D_MODEL)]
    for gi in range(len(ATT_GROUPS)):
        outs.append(jnp.transpose(kv_p[gi], from_dev))
        outs.append(jnp.transpose(kv_s[gi], from_dev))
    for j in range(3):
        outs.append(jnp.stack([s[j] for s in new_p]))
        outs.append(jnp.stack([s[j] for s in new_s]))
    return tuple(outs)
```

```python
import functools
import math

import jax
import jax.numpy as jnp
from jax import lax
from jax.experimental import pallas as pl
from jax.experimental.pallas import tpu as pltpu

F32 = jnp.float32
BF16 = jnp.bfloat16
I32 = jnp.int32

D_MODEL = 2048
RMS_EPS = 1e-6
HEAD_DIM = 64
ATT_GROUPS = ((128, 1), (512, 4), (2048, 16))
HEADS_PER_GROUP = 4
ATT_W = 768
ATT_OUT = 256
ATT_SPAN = 128
ROPE_THETA = 10000.0
POOL_WINDOWS = (2, 4, 8, 16)
POOL_GROUP = 128
POOL_W = 512
POOL_HIST = 16
RWKV_HEAD = 64
RWKV_HEADS = 12
RWKV_W = 768
RWKV_COLS = 2560
RWKV_LORA_OFF = 2304
GN_EPS = 64e-5
RWKV_CHUNK = 64
RWKV_SEQ_GROUP = 4
OFF_POOL = 2304
OFF_RWKV = 2816
OFF_GATE = 5376
PAST_LEN = 8192
PEER_HEADS = 8
N_KEYS = 128
N_EXPERTS = N_KEYS * N_KEYS
PEER_TOPK = 16
PEER_PAIRS = PEER_HEADS * PEER_TOPK
PEER_TB = 384
PEER_ACT_TB = 1408
PEER_BUILD_UNROLL = 4
PEER_EC = 1024
PEER_OUT_EC = 2048
SUBLANES = 8
VMEM_LIMIT = 56 * 1024 * 1024

NEG_INF = float("-inf")


def _params(*sem):
    return pltpu.CompilerParams(dimension_semantics=sem, vmem_limit_bytes=VMEM_LIMIT)


def _dot(a, b):
    return jnp.dot(a.astype(BF16), b.astype(BF16), preferred_element_type=F32)


def _dot_nt(a, b):
    return lax.dot_general(a.astype(BF16), b.astype(BF16), (((1,), (1,)), ((), ())), preferred_element_type=F32)


def _norm_mm_kernel(x_ref, g_ref, w_ref, o_ref, xn_ref):
    @pl.when(pl.program_id(1) == 0)
    def _():
        x = x_ref[...]
        ms = jnp.mean(x * x, axis=-1, keepdims=True)
        xn_ref[...] = ((x * lax.rsqrt(ms + RMS_EPS)) * g_ref[...]).astype(BF16)

    o_ref[...] = jnp.dot(xn_ref[...], w_ref[...], preferred_element_type=F32)


def _norm_mm(x, g, w, *, tm, tn):
    rows, k = x.shape
    n = w.shape[1]
    return pl.pallas_call(
        _norm_mm_kernel, grid=(rows // tm, n // tn),
        in_specs=[
            pl.BlockSpec((tm, k), lambda i, j: (i, 0)),
            pl.BlockSpec((1, k), lambda i, j: (0, 0)),
            pl.BlockSpec((k, tn), lambda i, j: (0, j)),
        ],
        out_specs=pl.BlockSpec((tm, tn), lambda i, j: (i, j)),
        out_shape=jax.ShapeDtypeStruct((rows, n), F32),
        scratch_shapes=[pltpu.VMEM((tm, k), BF16)],
        compiler_params=_params("arbitrary", "arbitrary"), name="norm_mm",
    )(x, g.reshape(1, k), w)


def _split_bf16(a):
    hi = a.astype(BF16)
    lo = (a - hi.astype(F32)).astype(BF16)
    return hi, lo


def _dot3(ah, al, bh, bl, dims=(((1,), (0,)), ((), ()))):
    def d(p, q):
        return lax.dot_general(p, q, dims, preferred_element_type=F32)

    return d(ah, bh) + (d(ah, bl) + d(al, bh))


def _norm_mm3_kernel(x_ref, g_ref, wh_ref, wl_ref, o_ref, xh_ref, xl_ref):
    @pl.when(pl.program_id(1) == 0)
    def _():
        x = x_ref[...]
        ms = jnp.mean(x * x, axis=-1, keepdims=True)
        xh, xl = _split_bf16((x * lax.rsqrt(ms + RMS_EPS)) * g_ref[...])
        xh_ref[...] = xh
        xl_ref[...] = xl

    o_ref[...] = _dot3(xh_ref[...], xl_ref[...], wh_ref[...], wl_ref[...])


def _norm_mm3(x, g, wh, wl, *, tm, tn):
    rows, k = x.shape
    n = wh.shape[1]
    wspec = pl.BlockSpec((k, tn), lambda i, j: (0, j))
    return pl.pallas_call(
        _norm_mm3_kernel, grid=(rows // tm, n // tn),
        in_specs=[pl.BlockSpec((tm, k), lambda i, j: (i, 0)), pl.BlockSpec((1, k), lambda i, j: (0, 0)), wspec, wspec],
        out_specs=[pl.BlockSpec((tm, tn), lambda i, j: (i, j)), pl.BlockSpec((tm, k), lambda i, j: (i, 0))],
        out_shape=[jax.ShapeDtypeStruct((rows, n), F32), jax.ShapeDtypeStruct((rows, k), BF16)],
        scratch_shapes=[pltpu.VMEM((tm, k), BF16)],
        compiler_params=_params("arbitrary", "arbitrary"), name="norm_mm3",
    )(x, g.reshape(1, k), wh, wl)


def _rotate(x, cos, sin):
    lane = lax.broadcasted_iota(I32, x.shape, 1)
    first_half = (lane & (HEAD_DIM - 1)) < HEAD_DIM // 2
    partner = jnp.where(first_half, pltpu.roll(x, 128 - HEAD_DIM // 2, 1), pltpu.roll(x, HEAD_DIM // 2, 1))
    return x * cos + partner * sin


def _rope_tables(pos):
    half = HEAD_DIM // 2
    inv = ROPE_THETA ** (-jnp.arange(half, dtype=F32) / half)
    ang = pos.astype(F32)[:, None] * inv[None, :]
    cos = jnp.cos(ang)
    sin = jnp.sin(ang)
    cos_t = jnp.concatenate([cos, cos, cos, cos], axis=1)
    sin_t = jnp.concatenate([-sin, sin, -sin, sin], axis=1)
    return cos_t, sin_t


ROT_ROWS = 256


def _prompt_attn_kernel(q_ref, k_ref, v_ref, cos_ref, sin_ref, *rest, dil, seq, keep):
    o_ref, l_ref, kv_ref, qs, ks, vs = rest[-6:]
    chunks = ATT_OUT // 128
    for i in range(seq // ROT_ROWS):
        rs = slice(i * ROT_ROWS, (i + 1) * ROT_ROWS)
        cos = cos_ref[rs, :]
        sin = sin_ref[rs, :]
        for c in range(chunks):
            cs = slice(c * 128, (c + 1) * 128)
            qs[c, rs, :] = _rotate(q_ref[rs, cs], cos, sin)
            ks[c, rs, :] = _rotate(k_ref[rs, cs], cos, sin)
            vs[c, rs, :] = v_ref[rs, cs]
    for h in range(HEADS_PER_GROUP):
        c, half = divmod(h, 2)
        hs = slice(half * HEAD_DIM, (half + 1) * HEAD_DIM)
        for j in range(keep // 128):
            ps = slice(seq - keep + j * 128, seq - keep + (j + 1) * 128)
            kv_ref[0, 0, 0, h, :, j * 128:(j + 1) * 128] = ks[c, ps, hs].T
            kv_ref[0, 0, 1, h, :, j * 128:(j + 1) * 128] = vs[c, ps, hs].T

    def rows_of(start):
        return pl.ds(start, ATT_SPAN, stride=dil) if dil > 1 else pl.ds(start, ATT_SPAN)

    qi = lax.broadcasted_iota(I32, (ATT_SPAN, ATT_SPAN), 0)
    ki = lax.broadcasted_iota(I32, (ATT_SPAN, ATT_SPAN), 1)
    cur_ok = ki <= qi
    prev_ok = ki >= qi
    scale = HEAD_DIM ** -0.5
    n_blocks = seq // dil // ATT_SPAN
    for r in range(dil):
        for b in range(n_blocks):
            cur = rows_of(r + dil * b * ATT_SPAN)
            for c in range(chunks):
                q2, kc2, vc2 = qs[c, cur, :], ks[c, cur, :], vs[c, cur, :]
                if b > 0:
                    prev = rows_of(r + dil * (b - 1) * ATT_SPAN)
                    kp2, vp2 = ks[c, prev, :], vs[c, prev, :]
                outs, lses = [], []
                for half in range(2):
                    hs = slice(half * HEAD_DIM, (half + 1) * HEAD_DIM)
                    q = q2[:, hs]
                    sc = jnp.where(cur_ok, _dot_nt(q, kc2[:, hs]) * scale, NEG_INF)
                    m = jnp.max(sc, axis=-1, keepdims=True)
                    if b > 0:
                        sp = jnp.where(prev_ok, _dot_nt(q, kp2[:, hs]) * scale, NEG_INF)
                        m = jnp.maximum(m, jnp.max(sp, axis=-1, keepdims=True))
                    ec = jnp.exp(sc - m)
                    den = jnp.sum(ec, axis=-1, keepdims=True)
                    acc = _dot(ec, vc2[:, hs])
                    if b > 0:
                        ep = jnp.exp(sp - m)
                        den = den + jnp.sum(ep, axis=-1, keepdims=True)
                        acc = acc + _dot(ep, vp2[:, hs])
                    outs.append(acc / den)
                    lses.append(jnp.broadcast_to(m + jnp.log(den), (ATT_SPAN, HEAD_DIM)))
                o_ref[c, cur, :] = jnp.concatenate(outs, axis=1)
                l_ref[c, cur, :] = jnp.concatenate(lses, axis=1)


def _prompt_attn(z_qkv, cos_t, sin_t, gi, dil, layer, kv_prev, *, depth, n_seq, seq):
    rows_p = n_seq * seq
    win = ATT_GROUPS[gi][0]
    keep = min(win, seq)
    assert seq % (dil * ATT_SPAN) == 0 and keep % 128 == 0 and seq % ROT_ROWS == 0
    tab = pl.BlockSpec((seq, 128), lambda n: (0, 0))
    chunked = pl.BlockSpec((ATT_OUT // 128, seq, 128), lambda n: (0, n, 0))
    kv_shape = (depth, n_seq, 2, HEADS_PER_GROUP, HEAD_DIM, keep)
    kv_spec = pl.BlockSpec((1, 1) + kv_shape[2:], lambda n: (layer, n, 0, 0, 0, 0))
    in_specs = [pl.BlockSpec((seq, ATT_OUT), lambda n: (n, gi)),
                pl.BlockSpec((seq, ATT_OUT), lambda n: (n, ATT_W // ATT_OUT + gi)),
                pl.BlockSpec((seq, ATT_OUT), lambda n: (n, 2 * ATT_W // ATT_OUT + gi)), tab, tab]
    args = [z_qkv, z_qkv, z_qkv, cos_t, sin_t]
    aliases = {}
    if kv_prev is not None:
        in_specs.append(pl.BlockSpec(memory_space=pl.ANY))
        args.append(kv_prev)
        aliases = {len(args) - 1: 2}
    chunk_shape = jax.ShapeDtypeStruct((ATT_OUT // 128, rows_p, 128), F32)
    return pl.pallas_call(
        functools.partial(_prompt_attn_kernel, dil=dil, seq=seq, keep=keep),
        grid=(n_seq,), in_specs=in_specs,
        out_specs=[chunked, chunked, kv_spec],
        out_shape=[chunk_shape, chunk_shape, jax.ShapeDtypeStruct(kv_shape, F32)],
        scratch_shapes=[pltpu.VMEM((ATT_OUT // 128, seq, 128), F32)] * 3,
        input_output_aliases=aliases,
        compiler_params=_params("arbitrary"), name=f"prompt_attn_g{gi}",
    )(*args)


def _merge_groups(os_, ls_):
    m = jnp.maximum(jnp.maximum(ls_[0], ls_[1]), ls_[2])
    es = [jnp.exp(l - m) for l in ls_]
    tot = es[0] + es[1] + es[2]
    return (es[0] / tot) * os_[0] + (es[1] / tot) * os_[1] + (es[2] / tot) * os_[2]


def _merge_kernel(o0, o1, o2, l0, l1, l2, a_ref):
    for c in range(ATT_OUT // 128):
        a_ref[:, c * 128:(c + 1) * 128] = _merge_groups([o0[c], o1[c], o2[c]], [l0[c], l1[c], l2[c]])


def _merge_attn(os_, ls_, *, tm):
    rows = os_[0].shape[1]
    spec = pl.BlockSpec((ATT_OUT // 128, tm, 128), lambda i: (0, i, 0))
    return pl.pallas_call(
        _merge_kernel, grid=(rows // tm,), in_specs=[spec] * 6, out_specs=pl.BlockSpec((tm, ATT_OUT), lambda i: (i, 0)),
        out_shape=jax.ShapeDtypeStruct((rows, ATT_OUT), F32),
        compiler_params=_params("arbitrary"), name="merge_attn",
    )(*os_, *ls_)


def _sample_attn_kernel(z_ref, cos_ref, sin_ref, *rest, t_new):
    caches, (a_ref, n0_ref, n1_ref, n2_ref) = rest[0:3], rest[-4:]
    scale = HEAD_DIM ** -0.5
    cos = cos_ref[...]
    sin = sin_ref[...]
    n_chunks = ATT_W // 128
    q_c = [_rotate(z_ref[:, c * 128:(c + 1) * 128], cos, sin) for c in range(n_chunks)]
    k_c = [_rotate(z_ref[:, ATT_W + c * 128:ATT_W + (c + 1) * 128], cos, sin) for c in range(n_chunks)]
    v_c = [z_ref[:, 2 * ATT_W + c * 128:2 * ATT_W + (c + 1) * 128] for c in range(n_chunks)]
    outs_g, lses_g = [], []
    for gi, (c_ref, n_ref, (win, dil)) in enumerate(zip(caches, (n0_ref, n1_ref, n2_ref), ATT_GROUPS)):
        cache_len = c_ref.shape[-1]
        t_c = lax.broadcasted_iota(I32, (t_new, cache_len), 0)
        c_c = lax.broadcasted_iota(I32, (t_new, cache_len), 1)
        d_c = cache_len + t_c - c_c
        ok_c = jnp.logical_and((d_c & (dil - 1)) == 0, d_c <= ATT_SPAN * dil)
        t_n = lax.broadcasted_iota(I32, (t_new, t_new), 0)
        u_n = lax.broadcasted_iota(I32, (t_new, t_new), 1)
        d_n = t_n - u_n
        ok_n = jnp.logical_and(d_n >= 0, (d_n & (dil - 1)) == 0)
        outs, lses = [], []
        for h in range(HEADS_PER_GROUP):
            chunk, half = divmod(gi * HEADS_PER_GROUP + h, 2)
            hs = slice(half * HEAD_DIM, (half + 1) * HEAD_DIM)
            q, k_new, v_new = q_c[chunk][:, hs], k_c[chunk][:, hs], v_c[chunk][:, hs]
            k_t = c_ref[0, 0, 0, h]
            v_t = c_ref[0, 0, 1, h]
            s_c = jnp.where(ok_c, _dot(q, k_t) * scale, NEG_INF)
            s_n = jnp.where(ok_n, _dot_nt(q, k_new) * scale, NEG_INF)
            m = jnp.maximum(jnp.max(s_c, axis=-1, keepdims=True), jnp.max(s_n, axis=-1, keepdims=True))
            e_c = jnp.exp(s_c - m)
            e_n = jnp.exp(s_n - m)
            den = jnp.sum(e_c, axis=-1, keepdims=True) + jnp.sum(e_n, axis=-1, keepdims=True)
            acc = _dot_nt(e_c, v_t) + _dot(e_n, v_new)
            outs.append(acc / den)
            lses.append(jnp.broadcast_to(m + jnp.log(den), (t_new, HEAD_DIM)))
            n_ref[0, 0, 0, h] = jnp.concatenate([k_t[:, t_new:], k_new.T], axis=1)
            n_ref[0, 0, 1, h] = jnp.concatenate([v_t[:, t_new:], v_new.T], axis=1)
        outs_g.append(jnp.concatenate(outs, axis=1))
        lses_g.append(jnp.concatenate(lses, axis=1))
    a_ref[...] = _merge_groups(outs_g, lses_g)


def _sample_attn(z_qkv, cos_s, sin_s, caches_t, layer, new_prev, *, row0, n_seq, t_new):
    b0 = row0 // t_new

    def cache_spec(c):
        return pl.BlockSpec((1, 1) + c.shape[2:], lambda n: (layer, n, 0, 0, 0, 0))

    tab = pl.BlockSpec((t_new, 128), lambda n: (0, 0))
    in_specs = [pl.BlockSpec((t_new, 3 * ATT_W), lambda n: (b0 + n, 0)), tab, tab] + [cache_spec(c) for c in caches_t]
    args = [z_qkv, cos_s, sin_s, *caches_t]
    aliases = {}
    if new_prev is not None:
        for j, a in enumerate(new_prev):
            in_specs.append(pl.BlockSpec(memory_space=pl.ANY))
            args.append(a)
            aliases[len(args) - 1] = 1 + j
    return pl.pallas_call(
        functools.partial(_sample_attn_kernel, t_new=t_new),
        grid=(n_seq,), in_specs=in_specs,
        out_specs=[pl.BlockSpec((t_new, ATT_OUT), lambda n: (n, 0))] + [cache_spec(c) for c in caches_t],
        out_shape=[jax.ShapeDtypeStruct((n_seq * t_new, ATT_OUT), F32)]
        + [jax.ShapeDtypeStruct(c.shape, F32) for c in caches_t],
        input_output_aliases=aliases,
        compiler_params=_params("arbitrary"), name="sample_attn",
    )(*args)


def _pool_kernel(z_ref, h_ref, w_ref, s_ref, o_ref, buf_ref, *, t_len, pos0):
    buf_ref[0:POOL_HIST, :] = h_ref[0]
    buf_ref[POOL_HIST:POOL_HIST + t_len, :] = z_ref[...]
    pos = pos0 + lax.broadcasted_iota(I32, (t_len, POOL_GROUP), 0)
    for g, win in enumerate(POOL_WINDOWS):
        cs = slice(g * POOL_GROUP, (g + 1) * POOL_GROUP)
        z = buf_ref[POOL_HIST:POOL_HIST + t_len, cs]
        wsum = z
        for i in range(1, win):
            wsum = wsum + buf_ref[POOL_HIST - i:POOL_HIST - i + t_len, cs]
        cnt = jnp.minimum(win, pos + 1).astype(F32)
        y = wsum / cnt - z
        o_ref[:, cs] = _dot(y, w_ref[g]) * s_ref[:, cs]


def _pool_mixer(z_pool, hist, pool_w, pool_scale, *, row0, n_seq, t_len, pos0):
    b0 = row0 // t_len
    return pl.pallas_call(
        functools.partial(_pool_kernel, t_len=t_len, pos0=pos0),
        grid=(n_seq,),
        in_specs=[
            pl.BlockSpec((t_len, POOL_W), lambda n: (b0 + n, 0)),
            pl.BlockSpec((1, POOL_HIST, POOL_W), lambda n: (n, 0, 0)),
            pl.BlockSpec((len(POOL_WINDOWS), POOL_GROUP, POOL_GROUP), lambda n: (0, 0, 0)),
            pl.BlockSpec((1, POOL_W), lambda n: (0, 0)),
        ],
        out_specs=pl.BlockSpec((t_len, POOL_W), lambda n: (n, 0)),
        out_shape=jax.ShapeDtypeStruct((n_seq * t_len, POOL_W), F32),
        scratch_shapes=[pltpu.VMEM((POOL_HIST + t_len, POOL_W), F32)],
        compiler_params=_params("arbitrary"), name="pool_mixer",
    )(z_pool, hist, pool_w, pool_scale.reshape(1, POOL_W))


def _rwkv_prep_kernel(z_ref, zp_ref, first_ref, mu_ref, w0_ref, w2_ref, a0_ref, a2_ref, g2_ref, kk_ref, ka_ref, rk_ref,
                      r_o, lw_o, k_o, v_o, kn_o, b_o, g_o, bonus_o, buf_ref, *, tm):
    i = pl.program_id(1)
    z = z_ref[...]
    prev_row = jnp.where(i == 0, first_ref[0], zp_ref[SUBLANES - 1:SUBLANES, :])
    buf_ref[SUBLANES:SUBLANES + tm, :] = z
    buf_ref[SUBLANES - 1:SUBLANES, :] = prev_row
    shifted = buf_ref[SUBLANES - 1:SUBLANES - 1 + tm, :]
    xm = z + mu_ref[...] * (shifted - z)
    r = xm[:, 0:RWKV_W]
    k = xm[:, RWKV_W:2 * RWKV_W]
    v = xm[:, 2 * RWKV_W:3 * RWKV_W]
    wa = xm[:, RWKV_LORA_OFF:RWKV_LORA_OFF + 128]
    gl = xm[:, RWKV_LORA_OFF + 128:RWKV_COLS]
    xw = w0_ref[...] + _dot(jnp.tanh(wa), w2_ref[...])
    logw = -math.exp(-0.5) * jax.nn.sigmoid(xw)
    a = jax.nn.sigmoid(a0_ref[...] + _dot(wa, a2_ref[...]))
    g_o[...] = _dot(jax.nn.sigmoid(gl), g2_ref[...])
    kkr = k * kk_ref[...]
    kmod = k * (1.0 + (a - 1.0) * ka_ref[...])
    rkk = r * kmod * rk_ref[...]
    bonus = []
    for h in range(RWKV_HEADS):
        sl = slice(h * RWKV_HEAD, (h + 1) * RWKV_HEAD)
        kh = kkr[:, sl]
        nrm = jnp.sqrt(jnp.sum(kh * kh, axis=-1, keepdims=True))
        kn = kh / jnp.maximum(nrm, 1e-12)
        r_o[0, h] = r[:, sl]
        lw_o[0, h] = logw[:, sl]
        k_o[0, h] = kmod[:, sl]
        v_o[0, h] = v[:, sl]
        kn_o[0, h] = kn
        b_o[0, h] = kn * a[:, sl]
        bonus.append(jnp.sum(rkk[:, sl], axis=-1, keepdims=True) * v[:, sl])
    bonus_o[...] = jnp.concatenate(bonus, axis=1)


def _rwkv_prep(z, first_prev, lw, *, row0, n_seq, t_len, tm):
    nblk = t_len // tm
    pb = tm // SUBLANES
    b0 = row0 // tm
    p0 = row0 // SUBLANES

    def vec(n):
        return pl.BlockSpec((1, n), lambda s, i: (0, 0))

    hm = jax.ShapeDtypeStruct((n_seq, RWKV_HEADS, t_len, RWKV_HEAD), F32)
    rm = jax.ShapeDtypeStruct((n_seq * t_len, RWKV_W), F32)
    hm_spec = pl.BlockSpec((1, RWKV_HEADS, tm, RWKV_HEAD), lambda s, i: (s, 0, i, 0))
    rm_spec = pl.BlockSpec((tm, RWKV_W), lambda s, i: (s * nblk + i, 0))
    return pl.pallas_call(
        functools.partial(_rwkv_prep_kernel, tm=tm),
        grid=(n_seq, nblk),
        in_specs=[
            pl.BlockSpec((tm, RWKV_COLS), lambda s, i: (b0 + s * nblk + i, 0)),
            pl.BlockSpec((SUBLANES, RWKV_COLS), lambda s, i: (jnp.maximum(p0 + (s * nblk + i) * pb - 1, 0), 0)),
            pl.BlockSpec((1, 1, RWKV_COLS), lambda s, i: (s, 0, 0)),
            vec(RWKV_COLS), vec(RWKV_W),
            pl.BlockSpec((128, RWKV_W), lambda s, i: (0, 0)),
            vec(RWKV_W),
            pl.BlockSpec((128, RWKV_W), lambda s, i: (0, 0)),
            pl.BlockSpec((128, RWKV_W), lambda s, i: (0, 0)),
            vec(RWKV_W), vec(RWKV_W), vec(RWKV_W),
        ],
        out_specs=[hm_spec] * 6 + [rm_spec] * 2,
        out_shape=[hm] * 6 + [rm] * 2,
        scratch_shapes=[pltpu.VMEM((tm + SUBLANES, RWKV_COLS), F32)],
        compiler_params=_params("arbitrary", "arbitrary"), name="rwkv_prep",
    )(z, z, first_prev, lw["mu"], lw["w0"], lw["w2p"], lw["a0"], lw["a2p"], lw["g2"], lw["kk"], lw["ka"], lw["rk"])


def _bmm(a, b):
    return jnp.einsum("hqk,hkd->hqd", a.astype(BF16), b.astype(BF16), preferred_element_type=F32)


def _bmm_nt(a, b):
    return jnp.einsum("hqd,hkd->hqk", a.astype(BF16), b.astype(BF16), preferred_element_type=F32)


def _bmm_tn(a, b):
    return jnp.einsum("hkq,hkd->hqd", a.astype(BF16), b.astype(BF16), preferred_element_type=F32)


def _rwkv_scan_kernel(r_ref, lw_ref, k_ref, v_ref, kn_ref, b_ref, s0_ref, y_ref, st_ref, s_scr, *, chunk, group):
    c = pl.program_id(1)
    nh = group * RWKV_HEADS

    def heads(ref):
        return ref[...].reshape((nh,) + ref.shape[2:])

    @pl.when(c == 0)
    def _():
        s_scr[...] = heads(s0_ref)

    r, logw, k, v, kn, b = [heads(ref) for ref in (r_ref, lw_ref, k_ref, v_ref, kn_ref, b_ref)]
    row = lax.broadcasted_iota(I32, (chunk, chunk), 0)
    col = lax.broadcasted_iota(I32, (chunk, chunk), 1)
    incl = row >= col
    strict = row > col
    tri = jnp.broadcast_to(incl.astype(BF16)[None], (nh, chunk, chunk))
    lw_hi = logw.astype(BF16)
    lw_lo = (logw - lw_hi.astype(F32)).astype(BF16)
    cum = (jnp.einsum("hqk,hkd->hqd", tri, lw_hi, preferred_element_type=F32)
           + jnp.einsum("hqk,hkd->hqd", tri, lw_lo, preferred_element_type=F32))
    p_inv = jnp.exp(-cum)
    kt = k * p_inv
    bt = b * p_inv
    kap = kn * jnp.exp(cum - logw)
    rho = r * jnp.exp(cum)
    qq = jnp.concatenate([kap, rho], axis=1)
    gram = _bmm_nt(qq, jnp.concatenate([kt, bt], axis=1))
    a_k = jnp.where(strict[None], gram[:, :chunk, :chunk], 0.0)
    a_b = jnp.where(strict[None], gram[:, :chunk, chunk:], 0.0)
    l_k = jnp.where(incl[None], gram[:, chunk:, :chunk], 0.0)
    l_b = jnp.where(incl[None], gram[:, chunk:, chunk:], 0.0)
    x = jnp.broadcast_to((row == col).astype(F32)[None], (nh, chunk, chunk))
    m = 1
    while m < chunk:
        sh = m.bit_length() - 1
        same = (row >> (sh + 1)) == (col >> (sh + 1))
        lower_left = jnp.logical_and(((row >> sh) & 1) == 1, ((col >> sh) & 1) == 0)
        off = jnp.where(jnp.logical_and(same, lower_left)[None], a_b, 0.0)
        x = x - _bmm(_bmm(x, off), x)
        m *= 2
    s = s_scr[...]
    qs = _bmm_nt(qq, s)
    u = _bmm(x, -(qs[:, :chunk] + _bmm(a_k, v)))
    y = qs[:, chunk:] + _bmm(jnp.concatenate([l_k, l_b], axis=2), jnp.concatenate([v, u], axis=1))
    s_new = (s + _bmm_tn(jnp.concatenate([v, u], axis=1), jnp.concatenate([kt, bt], axis=1))) * jnp.exp(cum[:, chunk - 1:chunk, :])
    s_scr[...] = s_new
    st_ref[...] = s_new.reshape(st_ref.shape)
    mean = jnp.mean(y, axis=-1, keepdims=True)
    var = jnp.mean(jnp.square(y - mean), axis=-1, keepdims=True)
    yn = (y - mean) * lax.rsqrt(var + GN_EPS)
    for g in range(group):
        y_ref[g] = jnp.concatenate([yn[g * RWKV_HEADS + h] for h in range(RWKV_HEADS)], axis=1)


def _rwkv_scan(prep, s0, *, chunk, group):
    r, lw, k, v, kn, b = prep
    n_seq, nh, t_len, hd = r.shape
    hm_spec = pl.BlockSpec((group, nh, chunk, hd), lambda s, c: (s, 0, c, 0))
    st_spec = pl.BlockSpec((group, nh, hd, hd), lambda s, c: (s, 0, 0, 0))
    return pl.pallas_call(
        functools.partial(_rwkv_scan_kernel, chunk=chunk, group=group),
        grid=(n_seq // group, t_len // chunk),
        in_specs=[hm_spec] * 6 + [st_spec],
        out_specs=[pl.BlockSpec((group, chunk, RWKV_W), lambda s, c: (s, c, 0)), st_spec],
        out_shape=[jax.ShapeDtypeStruct((n_seq, t_len, RWKV_W), F32), jax.ShapeDtypeStruct((n_seq, nh, hd, hd), F32)],
        scratch_shapes=[pltpu.VMEM((group * nh, hd, hd), F32)],
        compiler_params=_params("arbitrary", "arbitrary"), name="rwkv_scan",
    )(r, lw, k, v, kn, b, s0)


def _mix_kernel(*refs, n_prompt_blocks):
    pairs, (gate_ref, lnw_ref, lnb_ref, wa_ref, wp_ref, wr_ref, o_ref) = refs[:10], refs[10:]
    is_sample = pl.program_id(0) >= n_prompt_blocks
    att, pool, yn, bonus, g = [jnp.where(is_sample, pairs[2 * j + 1][...], pairs[2 * j][...]) for j in range(5)]
    rw = (yn * lnw_ref[...] + lnb_ref[...] + bonus) * g
    mix = jax.nn.sigmoid(gate_ref[:, 0:D_MODEL]) * _dot(att, wa_ref[...])
    mix = mix + jax.nn.sigmoid(gate_ref[:, D_MODEL:2 * D_MODEL]) * _dot(pool, wp_ref[...])
    mix = mix + jax.nn.sigmoid(gate_ref[:, 2 * D_MODEL:3 * D_MODEL]) * _dot(rw, wr_ref[...])
    o_ref[...] = mix.astype(BF16)


def _mix(branches, z_gate, ln_w, ln_b, wa, wp, wr, *, tm):
    rows = z_gate.shape[0]
    n_prompt_blocks = branches[0][0].shape[0] // tm
    assert rows == (n_prompt_blocks + 1) * tm and all(s.shape[0] == tm for _, s in branches)

    def full(a):
        return pl.BlockSpec(a.shape, lambda i: (0, 0))

    specs, args = [], []
    for p, s in branches:
        specs += [pl.BlockSpec((tm, p.shape[1]), lambda i: (jnp.minimum(i, n_prompt_blocks - 1), 0)), full(s)]
        args += [p, s]
    ln_w = ln_w.reshape(1, RWKV_W)
    ln_b = ln_b.reshape(1, RWKV_W)
    return pl.pallas_call(
        functools.partial(_mix_kernel, n_prompt_blocks=n_prompt_blocks), grid=(rows // tm,),
        in_specs=specs + [pl.BlockSpec((tm, 3 * D_MODEL), lambda i: (i, 0)),
                          full(ln_w), full(ln_b), full(wa), full(wp), full(wr)],
        out_specs=pl.BlockSpec((tm, D_MODEL), lambda i: (i, 0)),
        out_shape=jax.ShapeDtypeStruct((rows, D_MODEL), BF16),
        compiler_params=_params("arbitrary"), name="gated_mix",
    )(*args, z_gate, ln_w, ln_b, wa, wp, wr)


def _mm_res_kernel(x_ref, a_ref, w_ref, o_ref):
    o_ref[...] = x_ref[...] + jnp.dot(a_ref[...], w_ref[...], preferred_element_type=F32)


def _mm_residual(x, a, w, *, tm):
    rows, n = x.shape
    k = a.shape[1]
    return pl.pallas_call(
        _mm_res_kernel, grid=(rows // tm,),
        in_specs=[pl.BlockSpec((tm, n), lambda i: (i, 0)), pl.BlockSpec((tm, k), lambda i: (i, 0)),
                  pl.BlockSpec((k, n), lambda i: (0, 0))],
        out_specs=pl.BlockSpec((tm, n), lambda i: (i, 0)),
        out_shape=jax.ShapeDtypeStruct((rows, n), F32),
        compiler_params=_params("arbitrary"), name="out_proj",
    )(x, a, w)


def _topk_cols(scores, ids, ids_ordered, k):
    lanes = scores[0].shape[1]
    iota_k = lax.broadcasted_iota(I32, (k, lanes), 0)
    big = jnp.iinfo(jnp.int32).max

    def arg_max(s, ident, ordered):
        if ordered:
            tiles = [(s[i:i + SUBLANES], ident[i:i + SUBLANES]) for i in range(0, s.shape[0], SUBLANES)]
            while len(tiles) > 1:
                merged = []
                for (va, ia), (vb, ib) in zip(tiles[0::2], tiles[1::2]):
                    later = vb > va
                    merged.append((jnp.where(later, vb, va), jnp.where(later, ib, ia)))
                tiles = merged + tiles[len(tiles) - len(tiles) % 2:]
            s, ident = tiles[0]
        m = jnp.max(s, axis=0, keepdims=True)
        return m, jnp.min(jnp.where(s == m, ident, big), axis=0, keepdims=True)

    def body(j, carry):
        out = []
        for (s, vals, idxs), ident, ordered in zip(carry, ids, ids_ordered):
            m, idx = arg_max(s, ident, ordered)
            vals = jnp.where(iota_k == j, m, vals)
            idxs = jnp.where(iota_k == j, idx, idxs)
            out.append((jnp.where(ident == idx, NEG_INF, s), vals, idxs))
        return tuple(out)

    init = tuple((s, jnp.zeros((k, lanes), F32), jnp.zeros((k, lanes), I32)) for s in scores)
    return [(v, i) for _, v, i in lax.fori_loop(0, k, body, init)]


def _gather_rows(table, sel, k):
    out = jnp.zeros(sel.shape, table.dtype)
    for a in range(k):
        out = jnp.where(sel == a, table[a:a + 1, :], out)
    return out


def _peer_select_kernel(q_ref, skh_ref, skl_ref, i1_o, i2_o, gate_o, i1_s, i2_s, g_s, *, tok):
    kk = PEER_TOPK
    half = kk // 2
    nt = (((1,), (1,)), ((), ()))
    key_id = lax.broadcasted_iota(I32, (N_KEYS, tok), 0)
    row = lax.broadcasted_iota(I32, (half * half + kk, tok), 0)
    cand_id = jnp.where(row < half * half, (row >> (half.bit_length() - 1)) * kk + (row & (half - 1)),
                        jnp.where(row < half * half + half, row - half * half + half, (row - half * half) * kk))

    def finish(slot, top, sel, k1, k2):
        e1 = _gather_rows(k1, sel >> 4, kk)
        e2 = _gather_rows(k2, sel & (kk - 1), kk)
        ex = jnp.exp(top - jnp.max(top, axis=0, keepdims=True))
        gate = ex / jnp.sum(ex, axis=0, keepdims=True)
        r0 = pl.multiple_of(slot * kk, kk)
        i1_s[pl.ds(r0, kk), :] = e1.astype(F32)
        i2_s[pl.ds(r0, kk), :] = e2.astype(F32)
        g_s[pl.ds(r0, kk), :] = gate

    def head(h, carry):
        cand_prev, k1_prev, k2_prev = carry
        c0 = pl.multiple_of(h * 2 * N_KEYS, 2 * N_KEYS)
        q1h, q1l = _split_bf16(q_ref[:, pl.ds(c0, N_KEYS)])
        q2h, q2l = _split_bf16(q_ref[:, pl.ds(c0 + N_KEYS, N_KEYS)])
        s1 = _dot3(skh_ref[h, 0], skl_ref[h, 0], q1h, q1l, nt)
        s2 = _dot3(skh_ref[h, 1], skl_ref[h, 1], q2h, q2l, nt)
        (t1, k1), (t2, k2), (top, sel) = _topk_cols([s1, s2, cand_prev], [key_id, key_id, cand_id],
                                                    [True, True, False], kk)
        finish(jnp.where(h == 0, PEER_HEADS, h - 1), top, sel, k1_prev, k2_prev)
        cand = jnp.concatenate([t1[a:a + 1, :] + t2[0:half, :] for a in range(half)]
                               + [t1[0:1, :] + t2[half:kk, :], t1[half:kk, :] + t2[0:1, :]], axis=0)
        return cand, k1, k2

    zeros_i = jnp.zeros((kk, tok), I32)
    cand, k1, k2 = lax.fori_loop(0, PEER_HEADS, head, (jnp.zeros(cand_id.shape, F32), zeros_i, zeros_i))
    ((top, sel),) = _topk_cols([cand], [cand_id], [False], kk)
    finish(PEER_HEADS - 1, top, sel, k1, k2)
    i1_o[...] = i1_s[0:PEER_PAIRS, :].T.astype(I32)
    i2_o[...] = i2_s[0:PEER_PAIRS, :].T.astype(I32)
    gate_o[...] = g_s[0:PEER_PAIRS, :].T


def _peer_select(q, sk_hi, sk_lo, *, tok=128):
    rows = q.shape[0]
    spec = pl.BlockSpec((tok, PEER_PAIRS), lambda i: (i, 0))
    return pl.pallas_call(
        functools.partial(_peer_select_kernel, tok=tok), grid=(rows // tok,),
        in_specs=[pl.BlockSpec((tok, q.shape[1]), lambda i: (i, 0)),
                  pl.BlockSpec(sk_hi.shape, lambda i: (0, 0, 0, 0)), pl.BlockSpec(sk_lo.shape, lambda i: (0, 0, 0, 0))],
        out_specs=[spec] * 3,
        out_shape=[jax.ShapeDtypeStruct((rows, PEER_PAIRS), I32), jax.ShapeDtypeStruct((rows, PEER_PAIRS), I32),
                   jax.ShapeDtypeStruct((rows, PEER_PAIRS), F32)],
        scratch_shapes=[pltpu.VMEM((PEER_PAIRS + PEER_TOPK, tok), F32)] * 3,
        compiler_params=_params("arbitrary"), name="peer_select",
    )(q, sk_hi, sk_lo)


def _peer_pick(d_ref, i1_ref, i2_ref, act_ref, first_row):
    n_rows = d_ref.shape[1] // N_KEYS
    for g in range(act_ref.shape[0] // SUBLANES):
        rs = slice(g * SUBLANES, (g + 1) * SUBLANES)
        i1 = i1_ref[rs, :]
        i2 = i2_ref[rs, :]
        acc = act_ref[rs, :]
        for j in range(n_rows):
            got = jnp.take_along_axis(d_ref[rs, j * N_KEYS:(j + 1) * N_KEYS], i2, axis=1, mode="promise_in_bounds")
            acc = acc + jnp.where(i1 == first_row + j, got, 0.0)
        act_ref[rs, :] = acc


def _peer_act_kernel(xn_ref, u_ref, i1_ref, i2_ref, act_ref, da_ref, db_ref):
    first = jnp.logical_and(pl.program_id(0) == 0, pl.program_id(1) == 0)
    c = pl.program_id(1)
    slab = PEER_EC // 2
    slab_rows = slab // N_KEYS
    nt = (((1,), (1,)), ((), ()))

    @pl.when(first)
    def _():
        db_ref[...] = jnp.zeros(db_ref.shape, F32)

    @pl.when(c == 0)
    def _():
        act_ref[...] = jnp.zeros(act_ref.shape, F32)

    base = c * 2 * slab_rows
    _peer_pick(db_ref, i1_ref, i2_ref, act_ref, base - slab_rows)
    da_ref[...] = lax.dot_general(xn_ref[...], u_ref[0, 0:slab, :].astype(BF16), nt, preferred_element_type=F32)
    _peer_pick(da_ref, i1_ref, i2_ref, act_ref, base)
    db_ref[...] = lax.dot_general(xn_ref[...], u_ref[0, slab:2 * slab, :].astype(BF16), nt, preferred_element_type=F32)

    @pl.when(c == pl.num_programs(1) - 1)
    def _():
        _peer_pick(db_ref, i1_ref, i2_ref, act_ref, base + slab_rows)


def _peer_act(xn, u_tabs, layer, i1, i2, *, tb):
    rows = xn.shape[0]
    pair_spec = pl.BlockSpec((tb, PEER_PAIRS), lambda i, c: (i, 0))
    return pl.pallas_call(
        _peer_act_kernel, grid=(rows // tb, N_EXPERTS // PEER_EC),
        in_specs=[pl.BlockSpec((tb, D_MODEL), lambda i, c: (i, 0)),
                  pl.BlockSpec((1, PEER_EC, D_MODEL), lambda i, c: (layer, c, 0)), pair_spec, pair_spec],
        out_specs=pair_spec,
        out_shape=jax.ShapeDtypeStruct((rows, PEER_PAIRS), F32),
        scratch_shapes=[pltpu.VMEM((tb, PEER_EC // 2), F32)] * 2,
        compiler_params=_params("arbitrary", "arbitrary"), name="peer_act",
    )(xn, u_tabs, i1, i2)


def _peer_out_kernel(i1_ref, i2_ref, gate_ref, act_ref, v_ref, o_ref, w_ref):
    c = pl.program_id(1)
    n_groups = w_ref.shape[0]
    rows_per_chunk = PEER_OUT_EC // N_KEYS

    @pl.when(c == 0)
    def _():
        key_iota = lax.broadcasted_iota(I32, (N_KEYS, PEER_PAIRS), 0)

        def group(g, carry):
            r0 = pl.multiple_of(g * SUBLANES, SUBLANES)
            i1 = i1_ref[pl.ds(r0, SUBLANES), :]
            i2 = i2_ref[pl.ds(r0, SUBLANES), :]
            a = act_ref[pl.ds(r0, SUBLANES), :]
            wgt = gate_ref[pl.ds(r0, SUBLANES), :] * (0.5 * a * (1.0 + lax.erf(a * (1.0 / math.sqrt(2.0)))))
            for t in range(SUBLANES):
                hit1 = key_iota == jnp.broadcast_to(i1[t:t + 1, :], key_iota.shape)
                hit2 = key_iota == jnp.broadcast_to(i2[t:t + 1, :], key_iota.shape)
                w2 = jnp.where(hit2, jnp.broadcast_to(wgt[t:t + 1, :], key_iota.shape), 0.0)
                w_tok = _dot_nt(jnp.where(hit1, 1.0, 0.0), w2)
                w_ref[g, pl.ds(t, N_KEYS, stride=SUBLANES), :] = w_tok
            return carry

        lax.fori_loop(0, n_groups, group, 0, unroll=PEER_BUILD_UNROLL)

    row0 = c * (rows_per_chunk * SUBLANES)
    lhs = jnp.concatenate(
        [w_ref[:, pl.ds(pl.multiple_of(row0 + j * SUBLANES, SUBLANES), SUBLANES), :].reshape(n_groups * SUBLANES, N_KEYS)
         for j in range(rows_per_chunk)], axis=1)
    part = jnp.dot(lhs.astype(BF16), v_ref[0], preferred_element_type=F32)

    @pl.when(c == 0)
    def _():
        o_ref[...] = part

    @pl.when(c > 0)
    def _():
        o_ref[...] += part


def _peer_out(i1, i2, gate, act, v_tabs, layer):
    rows = i1.shape[0]
    tb = PEER_TB
    pair_spec = pl.BlockSpec((tb, PEER_PAIRS), lambda i, c: (i, 0))
    return pl.pallas_call(
        _peer_out_kernel, grid=(rows // tb, N_EXPERTS // PEER_OUT_EC),
        in_specs=[pair_spec] * 4 + [pl.BlockSpec((1, PEER_OUT_EC, D_MODEL), lambda i, c: (layer, c, 0))],
        out_specs=pl.BlockSpec((tb, D_MODEL), lambda i, c: (i, 0)),
        out_shape=jax.ShapeDtypeStruct((rows, D_MODEL), F32),
        scratch_shapes=[pltpu.VMEM((tb // SUBLANES, N_KEYS * SUBLANES, N_KEYS), F32)],
        compiler_params=_params("arbitrary", "arbitrary"), name="peer_out",
    )(i1, i2, gate, act, v_tabs)


def _ple_kernel(x_ref, f_ref, p_ref, g_ref, wg_ref, wp_ref, gf_ref, o_ref, *, final):
    x = x_ref[...] + f_ref[...]
    ms = jnp.mean(x * x, axis=-1, keepdims=True)
    h = (x * lax.rsqrt(ms + RMS_EPS)) * g_ref[...]
    out = x + jax.nn.sigmoid(_dot(h, wg_ref[...])) * _dot(p_ref[...], wp_ref[...])
    if final:
        ms2 = jnp.mean(out * out, axis=-1, keepdims=True)
        out = (out * lax.rsqrt(ms2 + RMS_EPS)) * gf_ref[...]
    o_ref[...] = out


def _ple(x, ffn, p, g, wg, wp, g_final, *, tm, final):
    rows, n = x.shape
    row = pl.BlockSpec((tm, n), lambda i: (i, 0))
    vec = pl.BlockSpec((1, n), lambda i: (0, 0))
    return pl.pallas_call(
        functools.partial(_ple_kernel, final=final), grid=(rows // tm,),
        in_specs=[row, row, pl.BlockSpec((tm, p.shape[1]), lambda i: (i, 0)), vec,
                  pl.BlockSpec(wg.shape, lambda i: (0, 0)), pl.BlockSpec(wp.shape, lambda i: (0, 0)), vec],
        out_specs=row,
        out_shape=jax.ShapeDtypeStruct((rows, n), F32),
        compiler_params=_params("arbitrary"), name="ple",
    )(x, ffn, p, g.reshape(1, n), wg, wp, g_final.reshape(1, n))


def _layer(x, p_rows, lw, state, g_final, dims, final):
    n_p, t_p, n_s, t_s, past = dims
    rows_p = n_p * t_p
    rows_s = n_s * t_s

    z_qkv = _norm_mm(x, lw["g_mix"], lw["w_qkv"], tm=768, tn=768)
    z_pool = _norm_mm(x, lw["g_mix"], lw["w_pool"], tm=768, tn=POOL_W)
    z_rwkv = _norm_mm(x, lw["g_mix"], lw["w_rwkv"], tm=768, tn=RWKV_COLS // 2)
    z_gate = _norm_mm(x, lw["g_mix"], lw["w_gate"], tm=768, tn=768)

    layer, depth = lw["layer"], lw["depth"]
    os_, ls_, kv_p = [], [], []
    for gi, (_, dil) in enumerate(ATT_GROUPS):
        o, l, kv = _prompt_attn(z_qkv, lw["cos_p"], lw["sin_p"], gi, dil, layer,
                                None if state["kv_p"] is None else state["kv_p"][gi], depth=depth, n_seq=n_p, seq=t_p)
        os_.append(o)
        ls_.append(l)
        kv_p.append(kv)
    att_p = _merge_attn(os_, ls_, tm=512)
    att_s, *kv_s = _sample_attn(z_qkv, lw["cos_s"], lw["sin_s"], state["caches_t"], layer, state["kv_s"],
                                row0=rows_p, n_seq=n_s, t_new=t_s)

    pool_p = _pool_mixer(z_pool, jnp.zeros((n_p, POOL_HIST, POOL_W), F32), lw["pool_w"], lw["pool_scale"],
                         row0=0, n_seq=n_p, t_len=t_p, pos0=0)
    hist_s = jnp.concatenate([jnp.zeros((n_s, 1, POOL_W), F32), state["pool"]], axis=1)
    pool_s = _pool_mixer(z_pool, hist_s, lw["pool_w"], lw["pool_scale"], row0=rows_p, n_seq=n_s, t_len=t_s, pos0=past)

    prep_p = _rwkv_prep(z_rwkv, jnp.zeros((n_p, 1, RWKV_COLS), F32), lw, row0=0, n_seq=n_p, t_len=t_p, tm=256)
    prep_s = _rwkv_prep(z_rwkv, state["shift"][:, None, :], lw, row0=rows_p, n_seq=n_s, t_len=t_s, tm=t_s)
    yn_p, wkv_p = _rwkv_scan(prep_p[:6], jnp.zeros((n_p, RWKV_HEADS, RWKV_HEAD, RWKV_HEAD), F32),
                             chunk=RWKV_CHUNK, group=RWKV_SEQ_GROUP)
    yn_s, wkv_s = _rwkv_scan(prep_s[:6], state["wkv"], chunk=t_s, group=RWKV_SEQ_GROUP)
    branches = [(att_p, att_s), (pool_p, pool_s), (yn_p.reshape(rows_p, RWKV_W), yn_s.reshape(rows_s, RWKV_W)),
                (prep_p[7], prep_s[7]), (prep_p[6], prep_s[6])]

    mix = _mix(branches, z_gate, lw["ln_w"], lw["ln_b"], lw["w_attn_o"], lw["w_pool_o"], lw["w_rwkv_o"], tm=rows_s)
    x = _mm_residual(x, mix, lw["w_out"], tm=384)

    q, xn = _norm_mm3(x, lw["g_ffn"], *lw["peer_wq"], tm=768, tn=512)
    i1, i2, gate = _peer_select(q, *lw["peer_subkeys"])
    act = _peer_act(xn, lw["peer_u"], lw["layer"], i1, i2, tb=PEER_ACT_TB)
    ffn = _peer_out(i1, i2, gate, act, lw["peer_v"], lw["layer"])

    x = _ple(x, ffn, p_rows, lw["g_ple"], lw["ple_wg"], lw["ple_wp"], g_final, tm=256, final=final)

    keep = POOL_HIST - 1
    pool_p_state = jnp.stack([z_pool[(n + 1) * t_p - keep:(n + 1) * t_p] for n in range(n_p)])
    zp_s = z_pool[rows_p:].reshape(n_s, t_s, POOL_W)
    new_p = [pool_p_state, wkv_p, z_rwkv[t_p - 1:rows_p:t_p]]
    new_s = [jnp.concatenate([hist_s[:, 1:], zp_s], axis=1)[:, -keep:], wkv_s, z_rwkv[rows_p + t_s - 1::t_s]]
    return x, kv_p, kv_s, new_p, new_s


def kernel(x_prompt, x_sample, p_prompt, p_sample, cache_attn_w128, cache_attn_w512, cache_attn_w2048, state_pool, state_rwkv_wkv, state_rwkv_shift, g_mix, w_in, w_attn_o, w_pool_o, w_rwkv_o, w_out, pool_w, pool_scale, rwkv_mu, rwkv_w0, rwkv_w2, rwkv_a0, rwkv_a2, rwkv_g2, rwkv_kk, rwkv_ka, rwkv_rk, rwkv_ln_w, rwkv_ln_b, g_ffn, peer_wq, peer_subkeys, peer_u, peer_v, g_ple, ple_wg, ple_wp, g_final):
    n_p, t_p, _ = x_prompt.shape
    n_s, t_s, _ = x_sample.shape
    depth = w_in.shape[0]
    past = PAST_LEN
    rows_p = n_p * t_p
    rows_s = n_s * t_s
    dims = (n_p, t_p, n_s, t_s, past)

    x = jnp.concatenate([x_prompt.reshape(rows_p, D_MODEL), x_sample.reshape(rows_s, D_MODEL)], axis=0)
    cos_p, sin_p = _rope_tables(jnp.arange(t_p, dtype=I32))
    cos_s, sin_s = _rope_tables(past + jnp.arange(t_s, dtype=I32))
    lora_pad = jnp.zeros((128 - 64, RWKV_W), F32)
    peer_v_bf16 = peer_v.astype(BF16)
    to_dev = (0, 1, 3, 4, 5, 2)
    from_dev = (0, 1, 5, 2, 3, 4)
    caches_t = [jnp.transpose(c, to_dev) for c in (cache_attn_w128, cache_attn_w512, cache_attn_w2048)]

    def row(a):
        return a.reshape(1, -1)

    new_p, new_s = [], []
    kv_p = kv_s = None
    for l in range(depth):
        wl = w_in[l]
        lw = {
            "g_mix": g_mix[l],
            "w_qkv": wl[:, :OFF_POOL].astype(BF16), "w_pool": wl[:, OFF_POOL:OFF_RWKV].astype(BF16),
            "w_rwkv": wl[:, OFF_RWKV:OFF_GATE].astype(BF16), "w_gate": wl[:, OFF_GATE:].astype(BF16),
            "cos_p": cos_p, "sin_p": sin_p, "cos_s": cos_s, "sin_s": sin_s, "depth": depth,
            "pool_w": pool_w[l].astype(BF16), "pool_scale": pool_scale[l],
            "mu": row(rwkv_mu[l]), "w0": row(rwkv_w0[l]), "a0": row(rwkv_a0[l]),
            "w2p": jnp.concatenate([rwkv_w2[l], lora_pad], axis=0).astype(BF16),
            "a2p": jnp.concatenate([lora_pad, rwkv_a2[l]], axis=0).astype(BF16),
            "g2": rwkv_g2[l].astype(BF16),
            "kk": row(rwkv_kk[l]), "ka": row(rwkv_ka[l]), "rk": row(rwkv_rk[l]),
            "ln_w": rwkv_ln_w[l], "ln_b": rwkv_ln_b[l],
            "w_attn_o": w_attn_o[l].astype(BF16), "w_pool_o": w_pool_o[l].astype(BF16),
            "w_rwkv_o": w_rwkv_o[l].astype(BF16), "w_out": w_out[l].astype(BF16),
            "g_ffn": g_ffn[l], "peer_wq": _split_bf16(peer_wq[l]), "peer_subkeys": _split_bf16(peer_subkeys[l]),
            "peer_u": peer_u, "peer_v": peer_v_bf16, "layer": l,
            "g_ple": g_ple[l], "ple_wg": ple_wg[l].astype(BF16), "ple_wp": ple_wp[l].astype(BF16),
        }
        state = {"caches_t": caches_t, "kv_p": kv_p, "kv_s": kv_s, "pool": state_pool[l],
                 "wkv": state_rwkv_wkv[l], "shift": state_rwkv_shift[l]}
        p_rows = jnp.concatenate([p_prompt[l].reshape(rows_p, -1), p_sample[l].reshape(rows_s, -1)], axis=0).astype(BF16)
        x, kv_p, kv_s, st_p, st_s = _layer(x, p_rows, lw, state, g_final, dims, l == depth - 1)
        new_p.append(st_p)
        new_s.append(st_s)

    outs = [x[:rows_p].reshape(n_p, t_p, D_MODEL), x[rows_p:].reshape(n_s, t_s, D_MODEL)]
    for gi in range(len(ATT_GROUPS)):
        outs.append(jnp.transpose(kv_p[gi], from_dev))
        outs.append(jnp.transpose(kv_s[gi], from_dev))
    for j in range(3):
        outs.append(jnp.stack([s[j] for s in new_p]))
        outs.append(jnp.stack([s[j] for s in new_s]))
    return tuple(outs)
```

```python
import functools
import math

import jax
import jax.numpy as jnp
from jax import lax
from jax.experimental import pallas as pl
from jax.experimental.pallas import tpu as pltpu

F32 = jnp.float32
BF16 = jnp.bfloat16
I32 = jnp.int32

D_MODEL = 2048
RMS_EPS = 1e-6
HEAD_DIM = 64
ATT_GROUPS = ((128, 1), (512, 4), (2048, 16))
HEADS_PER_GROUP = 4
ATT_W = 768
ATT_OUT = 256
ATT_SPAN = 128
ROPE_THETA = 10000.0
POOL_WINDOWS = (2, 4, 8, 16)
POOL_GROUP = 128
POOL_W = 512
POOL_HIST = 16
RWKV_HEAD = 64
RWKV_HEADS = 12
RWKV_W = 768
RWKV_COLS = 2560
RWKV_LORA_OFF = 2304
GN_EPS = 64e-5
RWKV_CHUNK = 64
RWKV_SEQ_GROUP = 4
OFF_POOL = 2304
OFF_RWKV = 2816
OFF_GATE = 5376
IN_COLS = 11520
Z_GATE = 0
Z_RWKV = 3 * D_MODEL
Z_POOL = Z_RWKV + RWKV_COLS
Z_QKV = Z_POOL + POOL_W
Z_RWKV_BLOCK = RWKV_COLS + POOL_W
PAST_LEN = 8192
PEER_HEADS = 8
N_KEYS = 128
N_EXPERTS = N_KEYS * N_KEYS
PEER_TOPK = 16
PEER_PAIRS = PEER_HEADS * PEER_TOPK
PEER_TB = 384
PEER_ACT_TB = 1408
PEER_BUILD_UNROLL = 4
PEER_EC = 1024
PEER_OUT_EC = 2048
SUBLANES = 8
VMEM_LIMIT = 56 * 1024 * 1024

NEG_INF = float("-inf")


def _params(*sem):
    return pltpu.CompilerParams(dimension_semantics=sem, vmem_limit_bytes=VMEM_LIMIT)


def _dot(a, b):
    return jnp.dot(a.astype(BF16), b.astype(BF16), preferred_element_type=F32)


def _dot_nt(a, b):
    return lax.dot_general(a.astype(BF16), b.astype(BF16), (((1,), (1,)), ((), ())), preferred_element_type=F32)


def _norm_mm_kernel(x_ref, g_ref, w_ref, o_ref, xn_ref):
    @pl.when(pl.program_id(1) == 0)
    def _():
        x = x_ref[...]
        ms = jnp.mean(x * x, axis=-1, keepdims=True)
        xn_ref[...] = ((x * lax.rsqrt(ms + RMS_EPS)) * g_ref[...]).astype(BF16)

    o_ref[...] = jnp.dot(xn_ref[...], w_ref[0], preferred_element_type=F32)


def _norm_mm(x, g, w, layer, *, tm, tn):
    rows, k = x.shape
    n = w.shape[2]
    return pl.pallas_call(
        _norm_mm_kernel, grid=(rows // tm, n // tn),
        in_specs=[
            pl.BlockSpec((tm, k), lambda i, j: (i, 0)),
            pl.BlockSpec((1, k), lambda i, j: (0, 0)),
            pl.BlockSpec((1, k, tn), lambda i, j: (layer, 0, j)),
        ],
        out_specs=pl.BlockSpec((tm, tn), lambda i, j: (i, j)),
        out_shape=jax.ShapeDtypeStruct((rows, n), F32),
        scratch_shapes=[pltpu.VMEM((tm, k), BF16)],
        compiler_params=_params("arbitrary", "arbitrary"), name="norm_mm",
    )(x, g.reshape(1, k), w)


def _split_bf16(a):
    hi = a.astype(BF16)
    lo = (a - hi.astype(F32)).astype(BF16)
    return hi, lo


def _dot3(ah, al, bh, bl, dims=(((1,), (0,)), ((), ()))):
    def d(p, q):
        return lax.dot_general(p, q, dims, preferred_element_type=F32)

    return d(ah, bh) + (d(ah, bl) + d(al, bh))


def _norm_mm3_kernel(x_ref, g_ref, wh_ref, wl_ref, o_ref, xh_ref, xl_ref):
    @pl.when(pl.program_id(1) == 0)
    def _():
        x = x_ref[...]
        ms = jnp.mean(x * x, axis=-1, keepdims=True)
        xh, xl = _split_bf16((x * lax.rsqrt(ms + RMS_EPS)) * g_ref[...])
        xh_ref[...] = xh
        xl_ref[...] = xl

    o_ref[...] = _dot3(xh_ref[...], xl_ref[...], wh_ref[...], wl_ref[...])


def _norm_mm3(x, g, wh, wl, *, tm, tn):
    rows, k = x.shape
    n = wh.shape[1]
    wspec = pl.BlockSpec((k, tn), lambda i, j: (0, j))
    return pl.pallas_call(
        _norm_mm3_kernel, grid=(rows // tm, n // tn),
        in_specs=[pl.BlockSpec((tm, k), lambda i, j: (i, 0)), pl.BlockSpec((1, k), lambda i, j: (0, 0)), wspec, wspec],
        out_specs=[pl.BlockSpec((tm, tn), lambda i, j: (i, j)), pl.BlockSpec((tm, k), lambda i, j: (i, 0))],
        out_shape=[jax.ShapeDtypeStruct((rows, n), F32), jax.ShapeDtypeStruct((rows, k), BF16)],
        scratch_shapes=[pltpu.VMEM((tm, k), BF16)],
        compiler_params=_params("arbitrary", "arbitrary"), name="norm_mm3",
    )(x, g.reshape(1, k), wh, wl)


def _rotate(x, cos, sin):
    lane = lax.broadcasted_iota(I32, x.shape, 1)
    first_half = (lane & (HEAD_DIM - 1)) < HEAD_DIM // 2
    partner = jnp.where(first_half, pltpu.roll(x, 128 - HEAD_DIM // 2, 1), pltpu.roll(x, HEAD_DIM // 2, 1))
    return x * cos + partner * sin


def _rope_tables(pos):
    half = HEAD_DIM // 2
    inv = ROPE_THETA ** (-jnp.arange(half, dtype=F32) / half)
    ang = pos.astype(F32)[:, None] * inv[None, :]
    cos = jnp.cos(ang)
    sin = jnp.sin(ang)
    cos_t = jnp.concatenate([cos, cos, cos, cos], axis=1)
    sin_t = jnp.concatenate([-sin, sin, -sin, sin], axis=1)
    return cos_t, sin_t


ROT_ROWS = 256


def _prompt_attn_kernel(q_ref, k_ref, v_ref, cos_ref, sin_ref, *rest, dil, seq, keep):
    o_ref, l_ref, kv_ref, qs, ks, vs = rest[-6:]
    chunks = ATT_OUT // 128
    for i in range(seq // ROT_ROWS):
        rs = slice(i * ROT_ROWS, (i + 1) * ROT_ROWS)
        cos = cos_ref[rs, :]
        sin = sin_ref[rs, :]
        for c in range(chunks):
            cs = slice(c * 128, (c + 1) * 128)
            qs[c, rs, :] = _rotate(q_ref[rs, cs], cos, sin)
            ks[c, rs, :] = _rotate(k_ref[rs, cs], cos, sin)
            vs[c, rs, :] = v_ref[rs, cs]
    for h in range(HEADS_PER_GROUP):
        c, half = divmod(h, 2)
        hs = slice(half * HEAD_DIM, (half + 1) * HEAD_DIM)
        for j in range(keep // 128):
            ps = slice(seq - keep + j * 128, seq - keep + (j + 1) * 128)
            kv_ref[0, 0, 0, h, :, j * 128:(j + 1) * 128] = ks[c, ps, hs].T
            kv_ref[0, 0, 1, h, :, j * 128:(j + 1) * 128] = vs[c, ps, hs].T

    def rows_of(start):
        return pl.ds(start, ATT_SPAN, stride=dil) if dil > 1 else pl.ds(start, ATT_SPAN)

    qi = lax.broadcasted_iota(I32, (ATT_SPAN, ATT_SPAN), 0)
    ki = lax.broadcasted_iota(I32, (ATT_SPAN, ATT_SPAN), 1)
    cur_ok = ki <= qi
    prev_ok = ki >= qi
    scale = HEAD_DIM ** -0.5
    n_blocks = seq // dil // ATT_SPAN
    for r in range(dil):
        for b in range(n_blocks):
            cur = rows_of(r + dil * b * ATT_SPAN)
            for c in range(chunks):
                q2, kc2, vc2 = qs[c, cur, :], ks[c, cur, :], vs[c, cur, :]
                if b > 0:
                    prev = rows_of(r + dil * (b - 1) * ATT_SPAN)
                    kp2, vp2 = ks[c, prev, :], vs[c, prev, :]
                outs, lses = [], []
                for half in range(2):
                    hs = slice(half * HEAD_DIM, (half + 1) * HEAD_DIM)
                    q = q2[:, hs]
                    sc = jnp.where(cur_ok, _dot_nt(q, kc2[:, hs]) * scale, NEG_INF)
                    m = jnp.max(sc, axis=-1, keepdims=True)
                    if b > 0:
                        sp = jnp.where(prev_ok, _dot_nt(q, kp2[:, hs]) * scale, NEG_INF)
                        m = jnp.maximum(m, jnp.max(sp, axis=-1, keepdims=True))
                    ec = jnp.exp(sc - m)
                    den = jnp.sum(ec, axis=-1, keepdims=True)
                    acc = _dot(ec, vc2[:, hs])
                    if b > 0:
                        ep = jnp.exp(sp - m)
                        den = den + jnp.sum(ep, axis=-1, keepdims=True)
                        acc = acc + _dot(ep, vp2[:, hs])
                    outs.append(acc / den)
                    lses.append(jnp.broadcast_to(m + jnp.log(den), (ATT_SPAN, HEAD_DIM)))
                o_ref[c, cur, :] = jnp.concatenate(outs, axis=1)
                l_ref[c, cur, :] = jnp.concatenate(lses, axis=1)


def _prompt_attn(z_qkv, cos_t, sin_t, gi, dil, layer, kv_prev, *, depth, n_seq, seq):
    rows_p = n_seq * seq
    win = ATT_GROUPS[gi][0]
    keep = min(win, seq)
    assert seq % (dil * ATT_SPAN) == 0 and keep % 128 == 0 and seq % ROT_ROWS == 0
    tab = pl.BlockSpec((seq, 128), lambda n: (0, 0))
    chunked = pl.BlockSpec((ATT_OUT // 128, seq, 128), lambda n: (0, n, 0))
    kv_shape = (depth, n_seq, 2, HEADS_PER_GROUP, HEAD_DIM, keep)
    kv_spec = pl.BlockSpec((1, 1) + kv_shape[2:], lambda n: (layer, n, 0, 0, 0, 0))
    c0 = Z_QKV // ATT_OUT
    in_specs = [pl.BlockSpec((seq, ATT_OUT), lambda n: (n, c0 + gi)),
                pl.BlockSpec((seq, ATT_OUT), lambda n: (n, c0 + ATT_W // ATT_OUT + gi)),
                pl.BlockSpec((seq, ATT_OUT), lambda n: (n, c0 + 2 * ATT_W // ATT_OUT + gi)), tab, tab]
    args = [z_qkv, z_qkv, z_qkv, cos_t, sin_t]
    aliases = {}
    if kv_prev is not None:
        in_specs.append(pl.BlockSpec(memory_space=pl.ANY))
        args.append(kv_prev)
        aliases = {len(args) - 1: 2}
    chunk_shape = jax.ShapeDtypeStruct((ATT_OUT // 128, rows_p, 128), F32)
    return pl.pallas_call(
        functools.partial(_prompt_attn_kernel, dil=dil, seq=seq, keep=keep),
        grid=(n_seq,), in_specs=in_specs,
        out_specs=[chunked, chunked, kv_spec],
        out_shape=[chunk_shape, chunk_shape, jax.ShapeDtypeStruct(kv_shape, F32)],
        scratch_shapes=[pltpu.VMEM((ATT_OUT // 128, seq, 128), F32)] * 3,
        input_output_aliases=aliases,
        compiler_params=_params("arbitrary"), name=f"prompt_attn_g{gi}",
    )(*args)


def _merge_groups(os_, ls_):
    m = jnp.maximum(jnp.maximum(ls_[0], ls_[1]), ls_[2])
    es = [jnp.exp(l - m) for l in ls_]
    tot = es[0] + es[1] + es[2]
    return (es[0] / tot) * os_[0] + (es[1] / tot) * os_[1] + (es[2] / tot) * os_[2]


def _merge_kernel(o0, o1, o2, l0, l1, l2, a_ref):
    for c in range(ATT_OUT // 128):
        a_ref[:, c * 128:(c + 1) * 128] = _merge_groups([o0[c], o1[c], o2[c]], [l0[c], l1[c], l2[c]])


def _merge_attn(os_, ls_, *, tm):
    rows = os_[0].shape[1]
    spec = pl.BlockSpec((ATT_OUT // 128, tm, 128), lambda i: (0, i, 0))
    return pl.pallas_call(
        _merge_kernel, grid=(rows // tm,), in_specs=[spec] * 6, out_specs=pl.BlockSpec((tm, ATT_OUT), lambda i: (i, 0)),
        out_shape=jax.ShapeDtypeStruct((rows, ATT_OUT), F32),
        compiler_params=_params("arbitrary"), name="merge_attn",
    )(*os_, *ls_)


def _sample_attn_kernel(z_ref, cos_ref, sin_ref, *rest, t_new):
    caches, (a_ref, n0_ref, n1_ref, n2_ref) = rest[0:3], rest[-4:]
    scale = HEAD_DIM ** -0.5
    cos = cos_ref[...]
    sin = sin_ref[...]
    n_chunks = ATT_W // 128
    q_c = [_rotate(z_ref[:, c * 128:(c + 1) * 128], cos, sin) for c in range(n_chunks)]
    k_c = [_rotate(z_ref[:, ATT_W + c * 128:ATT_W + (c + 1) * 128], cos, sin) for c in range(n_chunks)]
    v_c = [z_ref[:, 2 * ATT_W + c * 128:2 * ATT_W + (c + 1) * 128] for c in range(n_chunks)]
    outs_g, lses_g = [], []
    for gi, (c_ref, n_ref, (win, dil)) in enumerate(zip(caches, (n0_ref, n1_ref, n2_ref), ATT_GROUPS)):
        cache_len = c_ref.shape[-1]
        t_c = lax.broadcasted_iota(I32, (t_new, cache_len), 0)
        c_c = lax.broadcasted_iota(I32, (t_new, cache_len), 1)
        d_c = cache_len + t_c - c_c
        ok_c = jnp.logical_and((d_c & (dil - 1)) == 0, d_c <= ATT_SPAN * dil)
        t_n = lax.broadcasted_iota(I32, (t_new, t_new), 0)
        u_n = lax.broadcasted_iota(I32, (t_new, t_new), 1)
        d_n = t_n - u_n
        ok_n = jnp.logical_and(d_n >= 0, (d_n & (dil - 1)) == 0)
        outs, lses = [], []
        for h in range(HEADS_PER_GROUP):
            chunk, half = divmod(gi * HEADS_PER_GROUP + h, 2)
            hs = slice(half * HEAD_DIM, (half + 1) * HEAD_DIM)
            q, k_new, v_new = q_c[chunk][:, hs], k_c[chunk][:, hs], v_c[chunk][:, hs]
            k_t = c_ref[0, 0, 0, h]
            v_t = c_ref[0, 0, 1, h]
            s_c = jnp.where(ok_c, _dot(q, k_t) * scale, NEG_INF)
            s_n = jnp.where(ok_n, _dot_nt(q, k_new) * scale, NEG_INF)
            m = jnp.maximum(jnp.max(s_c, axis=-1, keepdims=True), jnp.max(s_n, axis=-1, keepdims=True))
            e_c = jnp.exp(s_c - m)
            e_n = jnp.exp(s_n - m)
            den = jnp.sum(e_c, axis=-1, keepdims=True) + jnp.sum(e_n, axis=-1, keepdims=True)
            acc = _dot_nt(e_c, v_t) + _dot(e_n, v_new)
            outs.append(acc / den)
            lses.append(jnp.broadcast_to(m + jnp.log(den), (t_new, HEAD_DIM)))
            n_ref[0, 0, 0, h] = jnp.concatenate([k_t[:, t_new:], k_new.T], axis=1)
            n_ref[0, 0, 1, h] = jnp.concatenate([v_t[:, t_new:], v_new.T], axis=1)
        outs_g.append(jnp.concatenate(outs, axis=1))
        lses_g.append(jnp.concatenate(lses, axis=1))
    a_ref[...] = _merge_groups(outs_g, lses_g)


def _sample_attn(z_qkv, cos_s, sin_s, caches_t, layer, new_prev, *, row0, n_seq, t_new):
    b0 = row0 // t_new

    def cache_spec(c):
        return pl.BlockSpec((1, 1) + c.shape[2:], lambda n: (layer, n, 0, 0, 0, 0))

    tab = pl.BlockSpec((t_new, 128), lambda n: (0, 0))
    in_specs = ([pl.BlockSpec((t_new, 3 * ATT_W), lambda n: (b0 + n, Z_QKV // (3 * ATT_W))), tab, tab]
                + [cache_spec(c) for c in caches_t])
    args = [z_qkv, cos_s, sin_s, *caches_t]
    aliases = {}
    if new_prev is not None:
        for j, a in enumerate(new_prev):
            in_specs.append(pl.BlockSpec(memory_space=pl.ANY))
            args.append(a)
            aliases[len(args) - 1] = 1 + j
    return pl.pallas_call(
        functools.partial(_sample_attn_kernel, t_new=t_new),
        grid=(n_seq,), in_specs=in_specs,
        out_specs=[pl.BlockSpec((t_new, ATT_OUT), lambda n: (n, 0))] + [cache_spec(c) for c in caches_t],
        out_shape=[jax.ShapeDtypeStruct((n_seq * t_new, ATT_OUT), F32)]
        + [jax.ShapeDtypeStruct(c.shape, F32) for c in caches_t],
        input_output_aliases=aliases,
        compiler_params=_params("arbitrary"), name="sample_attn",
    )(*args)


def _pool_kernel(z_ref, h_ref, w_ref, s_ref, o_ref, buf_ref, *, t_len, pos0):
    buf_ref[0:POOL_HIST, :] = h_ref[0]
    buf_ref[POOL_HIST:POOL_HIST + t_len, :] = z_ref[...]
    pos = pos0 + lax.broadcasted_iota(I32, (t_len, POOL_GROUP), 0)
    for g, win in enumerate(POOL_WINDOWS):
        cs = slice(g * POOL_GROUP, (g + 1) * POOL_GROUP)
        z = buf_ref[POOL_HIST:POOL_HIST + t_len, cs]
        wsum = z
        for i in range(1, win):
            wsum = wsum + buf_ref[POOL_HIST - i:POOL_HIST - i + t_len, cs]
        cnt = jnp.minimum(win, pos + 1).astype(F32)
        y = wsum / cnt - z
        o_ref[:, cs] = _dot(y, w_ref[g]) * s_ref[:, cs]


def _pool_mixer(z_pool, hist, pool_w, pool_scale, *, row0, n_seq, t_len, pos0):
    b0 = row0 // t_len
    return pl.pallas_call(
        functools.partial(_pool_kernel, t_len=t_len, pos0=pos0),
        grid=(n_seq,),
        in_specs=[
            pl.BlockSpec((t_len, POOL_W), lambda n: (b0 + n, Z_POOL // POOL_W)),
            pl.BlockSpec((1, POOL_HIST, POOL_W), lambda n: (n, 0, 0)),
            pl.BlockSpec((len(POOL_WINDOWS), POOL_GROUP, POOL_GROUP), lambda n: (0, 0, 0)),
            pl.BlockSpec((1, POOL_W), lambda n: (0, 0)),
        ],
        out_specs=pl.BlockSpec((t_len, POOL_W), lambda n: (n, 0)),
        out_shape=jax.ShapeDtypeStruct((n_seq * t_len, POOL_W), F32),
        scratch_shapes=[pltpu.VMEM((POOL_HIST + t_len, POOL_W), F32)],
        compiler_params=_params("arbitrary"), name="pool_mixer",
    )(z_pool, hist, pool_w, pool_scale.reshape(1, POOL_W))


def _rwkv_prep_kernel(z_ref, zp_ref, first_ref, mu_ref, w0_ref, w2_ref, a0_ref, a2_ref, g2_ref, kk_ref, ka_ref, rk_ref,
                      r_o, lw_o, k_o, v_o, kn_o, b_o, g_o, bonus_o, buf_ref, *, tm):
    i = pl.program_id(1)
    z = z_ref[:, 0:RWKV_COLS]
    prev_row = jnp.where(i == 0, first_ref[0], zp_ref[SUBLANES - 1:SUBLANES, 0:RWKV_COLS])
    buf_ref[SUBLANES:SUBLANES + tm, :] = z
    buf_ref[SUBLANES - 1:SUBLANES, :] = prev_row
    shifted = buf_ref[SUBLANES - 1:SUBLANES - 1 + tm, :]
    xm = z + mu_ref[...] * (shifted - z)
    r = xm[:, 0:RWKV_W]
    k = xm[:, RWKV_W:2 * RWKV_W]
    v = xm[:, 2 * RWKV_W:3 * RWKV_W]
    wa = xm[:, RWKV_LORA_OFF:RWKV_LORA_OFF + 128]
    gl = xm[:, RWKV_LORA_OFF + 128:RWKV_COLS]
    xw = w0_ref[...] + _dot(jnp.tanh(wa), w2_ref[...])
    logw = -math.exp(-0.5) * jax.nn.sigmoid(xw)
    a = jax.nn.sigmoid(a0_ref[...] + _dot(wa, a2_ref[...]))
    g_o[...] = _dot(jax.nn.sigmoid(gl), g2_ref[...])
    kkr = k * kk_ref[...]
    kmod = k * (1.0 + (a - 1.0) * ka_ref[...])
    rkk = r * kmod * rk_ref[...]
    bonus = []
    for h in range(RWKV_HEADS):
        sl = slice(h * RWKV_HEAD, (h + 1) * RWKV_HEAD)
        kh = kkr[:, sl]
        nrm = jnp.sqrt(jnp.sum(kh * kh, axis=-1, keepdims=True))
        kn = kh / jnp.maximum(nrm, 1e-12)
        r_o[0, h] = r[:, sl]
        lw_o[0, h] = logw[:, sl]
        k_o[0, h] = kmod[:, sl]
        v_o[0, h] = v[:, sl]
        kn_o[0, h] = kn
        b_o[0, h] = kn * a[:, sl]
        bonus.append(jnp.sum(rkk[:, sl], axis=-1, keepdims=True) * v[:, sl])
    bonus_o[...] = jnp.concatenate(bonus, axis=1)


def _rwkv_prep(z, first_prev, lw, *, row0, n_seq, t_len, tm):
    nblk = t_len // tm
    pb = tm // SUBLANES
    b0 = row0 // tm
    p0 = row0 // SUBLANES

    def vec(n):
        return pl.BlockSpec((1, n), lambda s, i: (0, 0))

    hm = jax.ShapeDtypeStruct((n_seq, RWKV_HEADS, t_len, RWKV_HEAD), F32)
    rm = jax.ShapeDtypeStruct((n_seq * t_len, RWKV_W), F32)
    hm_spec = pl.BlockSpec((1, RWKV_HEADS, tm, RWKV_HEAD), lambda s, i: (s, 0, i, 0))
    rm_spec = pl.BlockSpec((tm, RWKV_W), lambda s, i: (s * nblk + i, 0))
    return pl.pallas_call(
        functools.partial(_rwkv_prep_kernel, tm=tm),
        grid=(n_seq, nblk),
        in_specs=[
            pl.BlockSpec((tm, Z_RWKV_BLOCK), lambda s, i: (b0 + s * nblk + i, Z_RWKV // Z_RWKV_BLOCK)),
            pl.BlockSpec((SUBLANES, Z_RWKV_BLOCK),
                         lambda s, i: (jnp.maximum(p0 + (s * nblk + i) * pb - 1, 0), Z_RWKV // Z_RWKV_BLOCK)),
            pl.BlockSpec((1, 1, RWKV_COLS), lambda s, i: (s, 0, 0)),
            vec(RWKV_COLS), vec(RWKV_W),
            pl.BlockSpec((128, RWKV_W), lambda s, i: (0, 0)),
            vec(RWKV_W),
            pl.BlockSpec((128, RWKV_W), lambda s, i: (0, 0)),
            pl.BlockSpec((128, RWKV_W), lambda s, i: (0, 0)),
            vec(RWKV_W), vec(RWKV_W), vec(RWKV_W),
        ],
        out_specs=[hm_spec] * 6 + [rm_spec] * 2,
        out_shape=[hm] * 6 + [rm] * 2,
        scratch_shapes=[pltpu.VMEM((tm + SUBLANES, RWKV_COLS), F32)],
        compiler_params=_params("arbitrary", "arbitrary"), name="rwkv_prep",
    )(z, z, first_prev, lw["mu"], lw["w0"], lw["w2p"], lw["a0"], lw["a2p"], lw["g2"], lw["kk"], lw["ka"], lw["rk"])


def _bmm(a, b):
    return jnp.einsum("hqk,hkd->hqd", a.astype(BF16), b.astype(BF16), preferred_element_type=F32)


def _bmm_nt(a, b):
    return jnp.einsum("hqd,hkd->hqk", a.astype(BF16), b.astype(BF16), preferred_element_type=F32)


def _bmm_tn(a, b):
    return jnp.einsum("hkq,hkd->hqd", a.astype(BF16), b.astype(BF16), preferred_element_type=F32)


def _rwkv_scan_kernel(r_ref, lw_ref, k_ref, v_ref, kn_ref, b_ref, s0_ref, y_ref, st_ref, s_scr, *, chunk, group):
    c = pl.program_id(1)
    nh = group * RWKV_HEADS

    def heads(ref):
        return ref[...].reshape((nh,) + ref.shape[2:])

    @pl.when(c == 0)
    def _():
        s_scr[...] = heads(s0_ref)

    r, logw, k, v, kn, b = [heads(ref) for ref in (r_ref, lw_ref, k_ref, v_ref, kn_ref, b_ref)]
    row = lax.broadcasted_iota(I32, (chunk, chunk), 0)
    col = lax.broadcasted_iota(I32, (chunk, chunk), 1)
    incl = row >= col
    strict = row > col
    tri = jnp.broadcast_to(incl.astype(BF16)[None], (nh, chunk, chunk))
    lw_hi = logw.astype(BF16)
    lw_lo = (logw - lw_hi.astype(F32)).astype(BF16)
    cum = (jnp.einsum("hqk,hkd->hqd", tri, lw_hi, preferred_element_type=F32)
           + jnp.einsum("hqk,hkd->hqd", tri, lw_lo, preferred_element_type=F32))
    p_inv = jnp.exp(-cum)
    kt = k * p_inv
    bt = b * p_inv
    kap = kn * jnp.exp(cum - logw)
    rho = r * jnp.exp(cum)
    qq = jnp.concatenate([kap, rho], axis=1)
    gram = _bmm_nt(qq, jnp.concatenate([kt, bt], axis=1))
    a_k = jnp.where(strict[None], gram[:, :chunk, :chunk], 0.0)
    a_b = jnp.where(strict[None], gram[:, :chunk, chunk:], 0.0)
    l_k = jnp.where(incl[None], gram[:, chunk:, :chunk], 0.0)
    l_b = jnp.where(incl[None], gram[:, chunk:, chunk:], 0.0)
    x = jnp.broadcast_to((row == col).astype(F32)[None], (nh, chunk, chunk))
    m = 1
    while m < chunk:
        sh = m.bit_length() - 1
        same = (row >> (sh + 1)) == (col >> (sh + 1))
        lower_left = jnp.logical_and(((row >> sh) & 1) == 1, ((col >> sh) & 1) == 0)
        off = jnp.where(jnp.logical_and(same, lower_left)[None], a_b, 0.0)
        x = x - _bmm(_bmm(x, off), x)
        m *= 2
    s = s_scr[...]
    qs = _bmm_nt(qq, s)
    u = _bmm(x, -(qs[:, :chunk] + _bmm(a_k, v)))
    y = qs[:, chunk:] + _bmm(jnp.concatenate([l_k, l_b], axis=2), jnp.concatenate([v, u], axis=1))
    s_new = (s + _bmm_tn(jnp.concatenate([v, u], axis=1), jnp.concatenate([kt, bt], axis=1))) * jnp.exp(cum[:, chunk - 1:chunk, :])
    s_scr[...] = s_new
    st_ref[...] = s_new.reshape(st_ref.shape)
    mean = jnp.mean(y, axis=-1, keepdims=True)
    var = jnp.mean(jnp.square(y - mean), axis=-1, keepdims=True)
    yn = (y - mean) * lax.rsqrt(var + GN_EPS)
    for g in range(group):
        y_ref[g] = jnp.concatenate([yn[g * RWKV_HEADS + h] for h in range(RWKV_HEADS)], axis=1)


def _rwkv_scan(prep, s0, *, chunk, group):
    r, lw, k, v, kn, b = prep
    n_seq, nh, t_len, hd = r.shape
    hm_spec = pl.BlockSpec((group, nh, chunk, hd), lambda s, c: (s, 0, c, 0))
    st_spec = pl.BlockSpec((group, nh, hd, hd), lambda s, c: (s, 0, 0, 0))
    return pl.pallas_call(
        functools.partial(_rwkv_scan_kernel, chunk=chunk, group=group),
        grid=(n_seq // group, t_len // chunk),
        in_specs=[hm_spec] * 6 + [st_spec],
        out_specs=[pl.BlockSpec((group, chunk, RWKV_W), lambda s, c: (s, c, 0)), st_spec],
        out_shape=[jax.ShapeDtypeStruct((n_seq, t_len, RWKV_W), F32), jax.ShapeDtypeStruct((n_seq, nh, hd, hd), F32)],
        scratch_shapes=[pltpu.VMEM((group * nh, hd, hd), F32)],
        compiler_params=_params("arbitrary", "arbitrary"), name="rwkv_scan",
    )(r, lw, k, v, kn, b, s0)


def _mix_kernel(*refs, n_prompt_blocks):
    pairs, (gate_ref, lnw_ref, lnb_ref, wa_ref, wp_ref, wr_ref, o_ref) = refs[:10], refs[10:]
    is_sample = pl.program_id(0) >= n_prompt_blocks
    att, pool, yn, bonus, g = [jnp.where(is_sample, pairs[2 * j + 1][...], pairs[2 * j][...]) for j in range(5)]
    rw = (yn * lnw_ref[...] + lnb_ref[...] + bonus) * g
    mix = jax.nn.sigmoid(gate_ref[:, 0:D_MODEL]) * _dot(att, wa_ref[...])
    mix = mix + jax.nn.sigmoid(gate_ref[:, D_MODEL:2 * D_MODEL]) * _dot(pool, wp_ref[...])
    mix = mix + jax.nn.sigmoid(gate_ref[:, 2 * D_MODEL:3 * D_MODEL]) * _dot(rw, wr_ref[...])
    o_ref[...] = mix.astype(BF16)


def _mix(branches, z_gate, ln_w, ln_b, wa, wp, wr, *, tm):
    rows = z_gate.shape[0]
    n_prompt_blocks = branches[0][0].shape[0] // tm
    assert rows == (n_prompt_blocks + 1) * tm and all(s.shape[0] == tm for _, s in branches)

    def full(a):
        return pl.BlockSpec(a.shape, lambda i: (0, 0))

    specs, args = [], []
    for p, s in branches:
        specs += [pl.BlockSpec((tm, p.shape[1]), lambda i: (jnp.minimum(i, n_prompt_blocks - 1), 0)), full(s)]
        args += [p, s]
    ln_w = ln_w.reshape(1, RWKV_W)
    ln_b = ln_b.reshape(1, RWKV_W)
    return pl.pallas_call(
        functools.partial(_mix_kernel, n_prompt_blocks=n_prompt_blocks), grid=(rows // tm,),
        in_specs=specs + [pl.BlockSpec((tm, 3 * D_MODEL), lambda i: (i, 0)),
                          full(ln_w), full(ln_b), full(wa), full(wp), full(wr)],
        out_specs=pl.BlockSpec((tm, D_MODEL), lambda i: (i, 0)),
        out_shape=jax.ShapeDtypeStruct((rows, D_MODEL), BF16),
        compiler_params=_params("arbitrary"), name="gated_mix",
    )(*args, z_gate, ln_w, ln_b, wa, wp, wr)


def _mm_res_kernel(x_ref, a_ref, w_ref, o_ref):
    o_ref[...] = x_ref[...] + jnp.dot(a_ref[...], w_ref[...], preferred_element_type=F32)


def _mm_residual(x, a, w, *, tm):
    rows, n = x.shape
    k = a.shape[1]
    return pl.pallas_call(
        _mm_res_kernel, grid=(rows // tm,),
        in_specs=[pl.BlockSpec((tm, n), lambda i: (i, 0)), pl.BlockSpec((tm, k), lambda i: (i, 0)),
                  pl.BlockSpec((k, n), lambda i: (0, 0))],
        out_specs=pl.BlockSpec((tm, n), lambda i: (i, 0)),
        out_shape=jax.ShapeDtypeStruct((rows, n), F32),
        compiler_params=_params("arbitrary"), name="out_proj",
    )(x, a, w)


def _topk_cols(scores, ids, ids_ordered, k):
    lanes = scores[0].shape[1]
    iota_k = lax.broadcasted_iota(I32, (k, lanes), 0)
    big = jnp.iinfo(jnp.int32).max

    def arg_max(s, ident, ordered):
        if ordered:
            tiles = [(s[i:i + SUBLANES], ident[i:i + SUBLANES]) for i in range(0, s.shape[0], SUBLANES)]
            while len(tiles) > 1:
                merged = []
                for (va, ia), (vb, ib) in zip(tiles[0::2], tiles[1::2]):
                    later = vb > va
                    merged.append((jnp.where(later, vb, va), jnp.where(later, ib, ia)))
                tiles = merged + tiles[len(tiles) - len(tiles) % 2:]
            s, ident = tiles[0]
        m = jnp.max(s, axis=0, keepdims=True)
        return m, jnp.min(jnp.where(s == m, ident, big), axis=0, keepdims=True)

    def body(j, carry):
        out = []
        for (s, vals, idxs), ident, ordered in zip(carry, ids, ids_ordered):
            m, idx = arg_max(s, ident, ordered)
            vals = jnp.where(iota_k == j, m, vals)
            idxs = jnp.where(iota_k == j, idx, idxs)
            out.append((jnp.where(ident == idx, NEG_INF, s), vals, idxs))
        return tuple(out)

    init = tuple((s, jnp.zeros((k, lanes), F32), jnp.zeros((k, lanes), I32)) for s in scores)
    return [(v, i) for _, v, i in lax.fori_loop(0, k, body, init)]


def _gather_rows(table, sel, k):
    out = jnp.zeros(sel.shape, table.dtype)
    for a in range(k):
        out = jnp.where(sel == a, table[a:a + 1, :], out)
    return out


def _peer_select_kernel(q_ref, skh_ref, skl_ref, i1_o, i2_o, gate_o, i1_s, i2_s, g_s, *, tok):
    kk = PEER_TOPK
    half = kk // 2
    nt = (((1,), (1,)), ((), ()))
    key_id = lax.broadcasted_iota(I32, (N_KEYS, tok), 0)
    row = lax.broadcasted_iota(I32, (half * half + kk, tok), 0)
    cand_id = jnp.where(row < half * half, (row >> (half.bit_length() - 1)) * kk + (row & (half - 1)),
                        jnp.where(row < half * half + half, row - half * half + half, (row - half * half) * kk))

    def finish(slot, top, sel, k1, k2):
        e1 = _gather_rows(k1, sel >> 4, kk)
        e2 = _gather_rows(k2, sel & (kk - 1), kk)
        ex = jnp.exp(top - jnp.max(top, axis=0, keepdims=True))
        gate = ex / jnp.sum(ex, axis=0, keepdims=True)
        r0 = pl.multiple_of(slot * kk, kk)
        i1_s[pl.ds(r0, kk), :] = e1.astype(F32)
        i2_s[pl.ds(r0, kk), :] = e2.astype(F32)
        g_s[pl.ds(r0, kk), :] = gate

    def head(h, carry):
        cand_prev, k1_prev, k2_prev = carry
        c0 = pl.multiple_of(h * 2 * N_KEYS, 2 * N_KEYS)
        q1h, q1l = _split_bf16(q_ref[:, pl.ds(c0, N_KEYS)])
        q2h, q2l = _split_bf16(q_ref[:, pl.ds(c0 + N_KEYS, N_KEYS)])
        s1 = _dot3(skh_ref[h, 0], skl_ref[h, 0], q1h, q1l, nt)
        s2 = _dot3(skh_ref[h, 1], skl_ref[h, 1], q2h, q2l, nt)
        (t1, k1), (t2, k2), (top, sel) = _topk_cols([s1, s2, cand_prev], [key_id, key_id, cand_id],
                                                    [True, True, False], kk)
        finish(jnp.where(h == 0, PEER_HEADS, h - 1), top, sel, k1_prev, k2_prev)
        cand = jnp.concatenate([t1[a:a + 1, :] + t2[0:half, :] for a in range(half)]
                               + [t1[0:1, :] + t2[half:kk, :], t1[half:kk, :] + t2[0:1, :]], axis=0)
        return cand, k1, k2

    zeros_i = jnp.zeros((kk, tok), I32)
    cand, k1, k2 = lax.fori_loop(0, PEER_HEADS, head, (jnp.zeros(cand_id.shape, F32), zeros_i, zeros_i))
    ((top, sel),) = _topk_cols([cand], [cand_id], [False], kk)
    finish(PEER_HEADS - 1, top, sel, k1, k2)
    i1_o[...] = i1_s[0:PEER_PAIRS, :].T.astype(I32)
    i2_o[...] = i2_s[0:PEER_PAIRS, :].T.astype(I32)
    gate_o[...] = g_s[0:PEER_PAIRS, :].T


def _peer_select(q, sk_hi, sk_lo, *, tok=128):
    rows = q.shape[0]
    spec = pl.BlockSpec((tok, PEER_PAIRS), lambda i: (i, 0))
    return pl.pallas_call(
        functools.partial(_peer_select_kernel, tok=tok), grid=(rows // tok,),
        in_specs=[pl.BlockSpec((tok, q.shape[1]), lambda i: (i, 0)),
                  pl.BlockSpec(sk_hi.shape, lambda i: (0, 0, 0, 0)), pl.BlockSpec(sk_lo.shape, lambda i: (0, 0, 0, 0))],
        out_specs=[spec] * 3,
        out_shape=[jax.ShapeDtypeStruct((rows, PEER_PAIRS), I32), jax.ShapeDtypeStruct((rows, PEER_PAIRS), I32),
                   jax.ShapeDtypeStruct((rows, PEER_PAIRS), F32)],
        scratch_shapes=[pltpu.VMEM((PEER_PAIRS + PEER_TOPK, tok), F32)] * 3,
        compiler_params=_params("arbitrary"), name="peer_select",
    )(q, sk_hi, sk_lo)


def _peer_pick(d_ref, i1_ref, i2_ref, act_ref, first_row):
    n_rows = d_ref.shape[1] // N_KEYS
    for g in range(act_ref.shape[0] // SUBLANES):
        rs = slice(g * SUBLANES, (g + 1) * SUBLANES)
        i1 = i1_ref[rs, :]
        i2 = i2_ref[rs, :]
        acc = act_ref[rs, :]
        for j in range(n_rows):
            got = jnp.take_along_axis(d_ref[rs, j * N_KEYS:(j + 1) * N_KEYS], i2, axis=1, mode="promise_in_bounds")
            acc = acc + jnp.where(i1 == first_row + j, got, 0.0)
        act_ref[rs, :] = acc


def _peer_act_kernel(xn_ref, u_ref, i1_ref, i2_ref, act_ref, da_ref, db_ref):
    first = jnp.logical_and(pl.program_id(0) == 0, pl.program_id(1) == 0)
    c = pl.program_id(1)
    slab = PEER_EC // 2
    slab_rows = slab // N_KEYS
    nt = (((1,), (1,)), ((), ()))

    @pl.when(first)
    def _():
        db_ref[...] = jnp.zeros(db_ref.shape, F32)

    @pl.when(c == 0)
    def _():
        act_ref[...] = jnp.zeros(act_ref.shape, F32)

    base = c * 2 * slab_rows
    _peer_pick(db_ref, i1_ref, i2_ref, act_ref, base - slab_rows)
    da_ref[...] = lax.dot_general(xn_ref[...], u_ref[0, 0:slab, :].astype(BF16), nt, preferred_element_type=F32)
    _peer_pick(da_ref, i1_ref, i2_ref, act_ref, base)
    db_ref[...] = lax.dot_general(xn_ref[...], u_ref[0, slab:2 * slab, :].astype(BF16), nt, preferred_element_type=F32)

    @pl.when(c == pl.num_programs(1) - 1)
    def _():
        _peer_pick(db_ref, i1_ref, i2_ref, act_ref, base + slab_rows)


def _peer_act(xn, u_tabs, layer, i1, i2, *, tb):
    rows = xn.shape[0]
    pair_spec = pl.BlockSpec((tb, PEER_PAIRS), lambda i, c: (i, 0))
    return pl.pallas_call(
        _peer_act_kernel, grid=(rows // tb, N_EXPERTS // PEER_EC),
        in_specs=[pl.BlockSpec((tb, D_MODEL), lambda i, c: (i, 0)),
                  pl.BlockSpec((1, PEER_EC, D_MODEL), lambda i, c: (layer, c, 0)), pair_spec, pair_spec],
        out_specs=pair_spec,
        out_shape=jax.ShapeDtypeStruct((rows, PEER_PAIRS), F32),
        scratch_shapes=[pltpu.VMEM((tb, PEER_EC // 2), F32)] * 2,
        compiler_params=_params("arbitrary", "arbitrary"), name="peer_act",
    )(xn, u_tabs, i1, i2)


def _peer_out_kernel(i1_ref, i2_ref, gate_ref, act_ref, v_ref, o_ref, w_ref):
    c = pl.program_id(1)
    n_groups = w_ref.shape[0]
    rows_per_chunk = PEER_OUT_EC // N_KEYS

    @pl.when(c == 0)
    def _():
        key_iota = lax.broadcasted_iota(I32, (N_KEYS, PEER_PAIRS), 0)

        def group(g, carry):
            r0 = pl.multiple_of(g * SUBLANES, SUBLANES)
            i1 = i1_ref[pl.ds(r0, SUBLANES), :]
            i2 = i2_ref[pl.ds(r0, SUBLANES), :]
            a = act_ref[pl.ds(r0, SUBLANES), :]
            wgt = gate_ref[pl.ds(r0, SUBLANES), :] * (0.5 * a * (1.0 + lax.erf(a * (1.0 / math.sqrt(2.0)))))
            for t in range(SUBLANES):
                hit1 = key_iota == jnp.broadcast_to(i1[t:t + 1, :], key_iota.shape)
                hit2 = key_iota == jnp.broadcast_to(i2[t:t + 1, :], key_iota.shape)
                w2 = jnp.where(hit2, jnp.broadcast_to(wgt[t:t + 1, :], key_iota.shape), 0.0)
                w_tok = _dot_nt(jnp.where(hit1, 1.0, 0.0), w2)
                w_ref[g, pl.ds(t, N_KEYS, stride=SUBLANES), :] = w_tok
            return carry

        lax.fori_loop(0, n_groups, group, 0, unroll=PEER_BUILD_UNROLL)

    row0 = c * (rows_per_chunk * SUBLANES)
    lhs = jnp.concatenate(
        [w_ref[:, pl.ds(pl.multiple_of(row0 + j * SUBLANES, SUBLANES), SUBLANES), :].reshape(n_groups * SUBLANES, N_KEYS)
         for j in range(rows_per_chunk)], axis=1)
    part = jnp.dot(lhs.astype(BF16), v_ref[0], preferred_element_type=F32)

    @pl.when(c == 0)
    def _():
        o_ref[...] = part

    @pl.when(c > 0)
    def _():
        o_ref[...] += part


def _peer_out(i1, i2, gate, act, v_tabs, layer):
    rows = i1.shape[0]
    tb = PEER_TB
    pair_spec = pl.BlockSpec((tb, PEER_PAIRS), lambda i, c: (i, 0))
    return pl.pallas_call(
        _peer_out_kernel, grid=(rows // tb, N_EXPERTS // PEER_OUT_EC),
        in_specs=[pair_spec] * 4 + [pl.BlockSpec((1, PEER_OUT_EC, D_MODEL), lambda i, c: (layer, c, 0))],
        out_specs=pl.BlockSpec((tb, D_MODEL), lambda i, c: (i, 0)),
        out_shape=jax.ShapeDtypeStruct((rows, D_MODEL), F32),
        scratch_shapes=[pltpu.VMEM((tb // SUBLANES, N_KEYS * SUBLANES, N_KEYS), F32)],
        compiler_params=_params("arbitrary", "arbitrary"), name="peer_out",
    )(i1, i2, gate, act, v_tabs)


def _ple_kernel(x_ref, f_ref, p_ref, g_ref, wg_ref, wp_ref, gf_ref, o_ref, *, final):
    x = x_ref[...] + f_ref[...]
    ms = jnp.mean(x * x, axis=-1, keepdims=True)
    h = (x * lax.rsqrt(ms + RMS_EPS)) * g_ref[...]
    out = x + jax.nn.sigmoid(_dot(h, wg_ref[...])) * _dot(p_ref[...], wp_ref[...])
    if final:
        ms2 = jnp.mean(out * out, axis=-1, keepdims=True)
        out = (out * lax.rsqrt(ms2 + RMS_EPS)) * gf_ref[...]
    o_ref[...] = out


def _ple(x, ffn, p, g, wg, wp, g_final, *, tm, final):
    rows, n = x.shape
    row = pl.BlockSpec((tm, n), lambda i: (i, 0))
    vec = pl.BlockSpec((1, n), lambda i: (0, 0))
    return pl.pallas_call(
        functools.partial(_ple_kernel, final=final), grid=(rows // tm,),
        in_specs=[row, row, pl.BlockSpec((tm, p.shape[1]), lambda i: (i, 0)), vec,
                  pl.BlockSpec(wg.shape, lambda i: (0, 0)), pl.BlockSpec(wp.shape, lambda i: (0, 0)), vec],
        out_specs=row,
        out_shape=jax.ShapeDtypeStruct((rows, n), F32),
        compiler_params=_params("arbitrary"), name="ple",
    )(x, ffn, p, g.reshape(1, n), wg, wp, g_final.reshape(1, n))


def _layer(x, p_rows, lw, state, g_final, dims, final):
    n_p, t_p, n_s, t_s, past = dims
    rows_p = n_p * t_p
    rows_s = n_s * t_s

    z = _norm_mm(x, lw["g_mix"], lw["w_in"], lw["layer"], tm=768, tn=1920)
    z_qkv = z_pool = z_rwkv = z_gate = z

    layer, depth = lw["layer"], lw["depth"]
    os_, ls_, kv_p = [], [], []
    for gi, (_, dil) in enumerate(ATT_GROUPS):
        o, l, kv = _prompt_attn(z_qkv, lw["cos_p"], lw["sin_p"], gi, dil, layer,
                                None if state["kv_p"] is None else state["kv_p"][gi], depth=depth, n_seq=n_p, seq=t_p)
        os_.append(o)
        ls_.append(l)
        kv_p.append(kv)
    att_p = _merge_attn(os_, ls_, tm=512)
    att_s, *kv_s = _sample_attn(z_qkv, lw["cos_s"], lw["sin_s"], state["caches_t"], layer, state["kv_s"],
                                row0=rows_p, n_seq=n_s, t_new=t_s)

    pool_p = _pool_mixer(z_pool, jnp.zeros((n_p, POOL_HIST, POOL_W), F32), lw["pool_w"], lw["pool_scale"],
                         row0=0, n_seq=n_p, t_len=t_p, pos0=0)
    hist_s = jnp.concatenate([jnp.zeros((n_s, 1, POOL_W), F32), state["pool"]], axis=1)
    pool_s = _pool_mixer(z_pool, hist_s, lw["pool_w"], lw["pool_scale"], row0=rows_p, n_seq=n_s, t_len=t_s, pos0=past)

    prep_p = _rwkv_prep(z_rwkv, jnp.zeros((n_p, 1, RWKV_COLS), F32), lw, row0=0, n_seq=n_p, t_len=t_p, tm=256)
    prep_s = _rwkv_prep(z_rwkv, state["shift"][:, None, :], lw, row0=rows_p, n_seq=n_s, t_len=t_s, tm=t_s)
    yn_p, wkv_p = _rwkv_scan(prep_p[:6], jnp.zeros((n_p, RWKV_HEADS, RWKV_HEAD, RWKV_HEAD), F32),
                             chunk=RWKV_CHUNK, group=RWKV_SEQ_GROUP)
    yn_s, wkv_s = _rwkv_scan(prep_s[:6], state["wkv"], chunk=t_s, group=RWKV_SEQ_GROUP)
    branches = [(att_p, att_s), (pool_p, pool_s), (yn_p.reshape(rows_p, RWKV_W), yn_s.reshape(rows_s, RWKV_W)),
                (prep_p[7], prep_s[7]), (prep_p[6], prep_s[6])]

    mix = _mix(branches, z_gate, lw["ln_w"], lw["ln_b"], lw["w_attn_o"], lw["w_pool_o"], lw["w_rwkv_o"], tm=rows_s)
    x = _mm_residual(x, mix, lw["w_out"], tm=384)

    q, xn = _norm_mm3(x, lw["g_ffn"], *lw["peer_wq"], tm=768, tn=512)
    i1, i2, gate = _peer_select(q, *lw["peer_subkeys"])
    act = _peer_act(xn, lw["peer_u"], lw["layer"], i1, i2, tb=PEER_ACT_TB)
    ffn = _peer_out(i1, i2, gate, act, lw["peer_v"], lw["layer"])

    x = _ple(x, ffn, p_rows, lw["g_ple"], lw["ple_wg"], lw["ple_wp"], g_final, tm=256, final=final)

    keep = POOL_HIST - 1
    pool_cols = slice(Z_POOL, Z_POOL + POOL_W)
    rwkv_cols = slice(Z_RWKV, Z_RWKV + RWKV_COLS)
    pool_p_state = jnp.stack([z[(n + 1) * t_p - keep:(n + 1) * t_p, pool_cols] for n in range(n_p)])
    zp_s = z[rows_p:, pool_cols].reshape(n_s, t_s, POOL_W)
    new_p = [pool_p_state, wkv_p, z[t_p - 1:rows_p:t_p, rwkv_cols]]
    new_s = [jnp.concatenate([hist_s[:, 1:], zp_s], axis=1)[:, -keep:], wkv_s, z[rows_p + t_s - 1::t_s, rwkv_cols]]
    return x, kv_p, kv_s, new_p, new_s


def kernel(x_prompt, x_sample, p_prompt, p_sample, cache_attn_w128, cache_attn_w512, cache_attn_w2048, state_pool, state_rwkv_wkv, state_rwkv_shift, g_mix, w_in, w_attn_o, w_pool_o, w_rwkv_o, w_out, pool_w, pool_scale, rwkv_mu, rwkv_w0, rwkv_w2, rwkv_a0, rwkv_a2, rwkv_g2, rwkv_kk, rwkv_ka, rwkv_rk, rwkv_ln_w, rwkv_ln_b, g_ffn, peer_wq, peer_subkeys, peer_u, peer_v, g_ple, ple_wg, ple_wp, g_final):
    n_p, t_p, _ = x_prompt.shape
    n_s, t_s, _ = x_sample.shape
    depth = w_in.shape[0]
    past = PAST_LEN
    rows_p = n_p * t_p
    rows_s = n_s * t_s
    dims = (n_p, t_p, n_s, t_s, past)

    x = jnp.concatenate([x_prompt.reshape(rows_p, D_MODEL), x_sample.reshape(rows_s, D_MODEL)], axis=0)
    cos_p, sin_p = _rope_tables(jnp.arange(t_p, dtype=I32))
    cos_s, sin_s = _rope_tables(past + jnp.arange(t_s, dtype=I32))
    lora_pad = jnp.zeros((128 - 64, RWKV_W), F32)
    peer_v_bf16 = peer_v.astype(BF16)
    to_dev = (0, 1, 3, 4, 5, 2)
    from_dev = (0, 1, 5, 2, 3, 4)
    caches_t = [jnp.transpose(c, to_dev) for c in (cache_attn_w128, cache_attn_w512, cache_attn_w2048)]

    def row(a):
        return a.reshape(1, -1)

    w_in_z = jnp.concatenate([w_in[:, :, OFF_GATE:], w_in[:, :, OFF_RWKV:OFF_GATE], w_in[:, :, OFF_POOL:OFF_RWKV],
                              w_in[:, :, :OFF_POOL]], axis=2).astype(BF16)

    new_p, new_s = [], []
    kv_p = kv_s = None
    for l in range(depth):
        lw = {
            "g_mix": g_mix[l], "w_in": w_in_z,
            "cos_p": cos_p, "sin_p": sin_p, "cos_s": cos_s, "sin_s": sin_s, "depth": depth,
            "pool_w": pool_w[l].astype(BF16), "pool_scale": pool_scale[l],
            "mu": row(rwkv_mu[l]), "w0": row(rwkv_w0[l]), "a0": row(rwkv_a0[l]),
            "w2p": jnp.concatenate([rwkv_w2[l], lora_pad], axis=0).astype(BF16),
            "a2p": jnp.concatenate([lora_pad, rwkv_a2[l]], axis=0).astype(BF16),
            "g2": rwkv_g2[l].astype(BF16),
            "kk": row(rwkv_kk[l]), "ka": row(rwkv_ka[l]), "rk": row(rwkv_rk[l]),
            "ln_w": rwkv_ln_w[l], "ln_b": rwkv_ln_b[l],
            "w_attn_o": w_attn_o[l].astype(BF16), "w_pool_o": w_pool_o[l].astype(BF16),
            "w_rwkv_o": w_rwkv_o[l].astype(BF16), "w_out": w_out[l].astype(BF16),
            "g_ffn": g_ffn[l], "peer_wq": _split_bf16(peer_wq[l]), "peer_subkeys": _split_bf16(peer_subkeys[l]),
            "peer_u": peer_u, "peer_v": peer_v_bf16, "layer": l,
            "g_ple": g_ple[l], "ple_wg": ple_wg[l].astype(BF16), "ple_wp": ple_wp[l].astype(BF16),
        }
        state = {"caches_t": caches_t, "kv_p": kv_p, "kv_s": kv_s, "pool": state_pool[l],
                 "wkv": state_rwkv_wkv[l], "shift": state_rwkv_shift[l]}
        p_rows = jnp.concatenate([p_prompt[l].reshape(rows_p, -1), p_sample[l].reshape(rows_s, -1)], axis=0).astype(BF16)
        x, kv_p, kv_s, st_p, st_s = _layer(x, p_rows, lw, state, g_final, dims, l == depth - 1)
        new_p.append(st_p)
        new_s.append(st_s)

    outs = [x[:rows_p].reshape(n_p, t_p, D_MODEL), x[rows_p:].reshape(n_s, t_s, D_MODEL)]
    for gi in range(len(ATT_GROUPS)):
        outs.append(jnp.transpose(kv_p[gi], from_dev))
        outs.append(jnp.transpose(kv_s[gi], from_dev))
    for j in range(3):
        outs.append(jnp.stack([s[j] for s in new_p]))
        outs.append(jnp.stack([s[j] for s in new_s]))
    return tuple(outs)
```

```python
import functools
import math

import jax
import jax.numpy as jnp
from jax import lax
from jax.experimental import pallas as pl
from jax.experimental.pallas import tpu as pltpu

F32 = jnp.float32
BF16 = jnp.bfloat16
I32 = jnp.int32

D_MODEL = 2048
RMS_EPS = 1e-6
HEAD_DIM = 64
ATT_GROUPS = ((128, 1), (512, 4), (2048, 16))
HEADS_PER_GROUP = 4
ATT_W = 768
ATT_OUT = 256
ATT_SPAN = 128
ROPE_THETA = 10000.0
POOL_WINDOWS = (2, 4, 8, 16)
POOL_GROUP = 128
POOL_W = 512
POOL_HIST = 16
RWKV_HEAD = 64
RWKV_HEADS = 12
RWKV_W = 768
RWKV_COLS = 2560
RWKV_LORA_OFF = 2304
GN_EPS = 64e-5
RWKV_CHUNK = 64
RWKV_SEQ_GROUP = 4
OFF_POOL = 2304
OFF_RWKV = 2816
OFF_GATE = 5376
IN_COLS = 11520
Z_GATE = 0
Z_RWKV = 3 * D_MODEL
Z_POOL = Z_RWKV + RWKV_COLS
Z_QKV = Z_POOL + POOL_W
Z_RWKV_BLOCK = RWKV_COLS + POOL_W
PAST_LEN = 8192
PEER_HEADS = 8
N_KEYS = 128
N_EXPERTS = N_KEYS * N_KEYS
PEER_TOPK = 16
PEER_PAIRS = PEER_HEADS * PEER_TOPK
PEER_TB = 384
PEER_ACT_TB = 1408
PEER_BUILD_UNROLL = 4
PEER_EC = 1024
PEER_OUT_EC = 2048
SUBLANES = 8
VMEM_LIMIT = 56 * 1024 * 1024

NEG_INF = float("-inf")


def _params(*sem):
    return pltpu.CompilerParams(dimension_semantics=sem, vmem_limit_bytes=VMEM_LIMIT)


def _dot(a, b):
    return jnp.dot(a.astype(BF16), b.astype(BF16), preferred_element_type=F32)


def _dot_nt(a, b):
    return lax.dot_general(a.astype(BF16), b.astype(BF16), (((1,), (1,)), ((), ())), preferred_element_type=F32)


def _norm_mm_kernel(x_ref, g_ref, w_ref, o_ref, xn_ref):
    @pl.when(pl.program_id(1) == 0)
    def _():
        x = x_ref[...]
        ms = jnp.mean(x * x, axis=-1, keepdims=True)
        xn_ref[...] = ((x * lax.rsqrt(ms + RMS_EPS)) * g_ref[...]).astype(BF16)

    o_ref[...] = jnp.dot(xn_ref[...], w_ref[0], preferred_element_type=F32)


def _norm_mm(x, g, w, layer, *, tm, tn):
    rows, k = x.shape
    n = w.shape[2]
    return pl.pallas_call(
        _norm_mm_kernel, grid=(rows // tm, n // tn),
        in_specs=[
            pl.BlockSpec((tm, k), lambda i, j: (i, 0)),
            pl.BlockSpec((1, k), lambda i, j: (0, 0)),
            pl.BlockSpec((1, k, tn), lambda i, j: (layer, 0, j)),
        ],
        out_specs=pl.BlockSpec((tm, tn), lambda i, j: (i, j)),
        out_shape=jax.ShapeDtypeStruct((rows, n), F32),
        scratch_shapes=[pltpu.VMEM((tm, k), BF16)],
        compiler_params=_params("arbitrary", "arbitrary"), name="norm_mm",
    )(x, g.reshape(1, k), w)


def _split_bf16(a):
    hi = a.astype(BF16)
    lo = (a - hi.astype(F32)).astype(BF16)
    return hi, lo


def _dot3(ah, al, bh, bl, dims=(((1,), (0,)), ((), ()))):
    def d(p, q):
        return lax.dot_general(p, q, dims, preferred_element_type=F32)

    return d(ah, bh) + (d(ah, bl) + d(al, bh))


def _norm_mm3_kernel(x_ref, g_ref, wh_ref, wl_ref, o_ref, xh_ref, xl_ref):
    @pl.when(pl.program_id(1) == 0)
    def _():
        x = x_ref[...]
        ms = jnp.mean(x * x, axis=-1, keepdims=True)
        xh, xl = _split_bf16((x * lax.rsqrt(ms + RMS_EPS)) * g_ref[...])
        xh_ref[...] = xh
        xl_ref[...] = xl

    o_ref[...] = _dot3(xh_ref[...], xl_ref[...], wh_ref[...], wl_ref[...])


def _norm_mm3(x, g, wh, wl, *, tm, tn):
    rows, k = x.shape
    n = wh.shape[1]
    wspec = pl.BlockSpec((k, tn), lambda i, j: (0, j))
    return pl.pallas_call(
        _norm_mm3_kernel, grid=(rows // tm, n // tn),
        in_specs=[pl.BlockSpec((tm, k), lambda i, j: (i, 0)), pl.BlockSpec((1, k), lambda i, j: (0, 0)), wspec, wspec],
        out_specs=[pl.BlockSpec((tm, tn), lambda i, j: (i, j)), pl.BlockSpec((tm, k), lambda i, j: (i, 0))],
        out_shape=[jax.ShapeDtypeStruct((rows, n), F32), jax.ShapeDtypeStruct((rows, k), BF16)],
        scratch_shapes=[pltpu.VMEM((tm, k), BF16)],
        compiler_params=_params("arbitrary", "arbitrary"), name="norm_mm3",
    )(x, g.reshape(1, k), wh, wl)


def _rotate(x, cos, sin):
    lane = lax.broadcasted_iota(I32, x.shape, 1)
    first_half = (lane & (HEAD_DIM - 1)) < HEAD_DIM // 2
    partner = jnp.where(first_half, pltpu.roll(x, 128 - HEAD_DIM // 2, 1), pltpu.roll(x, HEAD_DIM // 2, 1))
    return x * cos + partner * sin


def _rope_tables(pos):
    half = HEAD_DIM // 2
    inv = ROPE_THETA ** (-jnp.arange(half, dtype=F32) / half)
    ang = pos.astype(F32)[:, None] * inv[None, :]
    cos = jnp.cos(ang)
    sin = jnp.sin(ang)
    cos_t = jnp.concatenate([cos, cos, cos, cos], axis=1)
    sin_t = jnp.concatenate([-sin, sin, -sin, sin], axis=1)
    return cos_t, sin_t


ROT_ROWS = 256


def _prompt_attn_kernel(q_ref, k_ref, v_ref, cos_ref, sin_ref, *rest, dil, seq, keep):
    o_ref, l_ref, kv_ref, qs, ks, vs = rest[-6:]
    chunks = ATT_OUT // 128
    for i in range(seq // ROT_ROWS):
        rs = slice(i * ROT_ROWS, (i + 1) * ROT_ROWS)
        cos = cos_ref[rs, :]
        sin = sin_ref[rs, :]
        for c in range(chunks):
            cs = slice(c * 128, (c + 1) * 128)
            qs[c, rs, :] = _rotate(q_ref[rs, cs], cos, sin)
            ks[c, rs, :] = _rotate(k_ref[rs, cs], cos, sin)
            vs[c, rs, :] = v_ref[rs, cs]
    for c in range(chunks):
        for j in range(keep // 128):
            ps = slice(seq - keep + j * 128, seq - keep + (j + 1) * 128)
            k_t = ks[c, ps, :].T
            v_t = vs[c, ps, :].T
            for half in range(2):
                hs = slice(half * HEAD_DIM, (half + 1) * HEAD_DIM)
                kv_ref[0, 0, 0, 2 * c + half, :, j * 128:(j + 1) * 128] = k_t[hs, :]
                kv_ref[0, 0, 1, 2 * c + half, :, j * 128:(j + 1) * 128] = v_t[hs, :]

    def rows_of(start):
        return pl.ds(start, ATT_SPAN, stride=dil) if dil > 1 else pl.ds(start, ATT_SPAN)

    qi = lax.broadcasted_iota(I32, (ATT_SPAN, ATT_SPAN), 0)
    ki = lax.broadcasted_iota(I32, (ATT_SPAN, ATT_SPAN), 1)
    cur_ok = ki <= qi
    prev_ok = ki >= qi
    first_head = ki < HEAD_DIM
    ones = jnp.ones((ATT_SPAN, 128), BF16)
    scale = HEAD_DIM ** -0.5
    n_blocks = seq // dil // ATT_SPAN
    for r in range(dil):
        for b in range(n_blocks):
            cur = rows_of(r + dil * b * ATT_SPAN)
            for c in range(chunks):
                q2, kc2, vc2 = qs[c, cur, :], ks[c, cur, :], vs[c, cur, :]
                if b > 0:
                    prev = rows_of(r + dil * (b - 1) * ATT_SPAN)
                    kp2, vp2 = ks[c, prev, :], vs[c, prev, :]
                outs, lses = [], []
                for half in range(2):
                    q = jnp.where(first_head == (half == 0), q2, 0.0)
                    sc = jnp.where(cur_ok, _dot_nt(q, kc2) * scale, NEG_INF)
                    m = jnp.max(sc, axis=-1, keepdims=True)
                    if b > 0:
                        sp = jnp.where(prev_ok, _dot_nt(q, kp2) * scale, NEG_INF)
                        m = jnp.maximum(m, jnp.max(sp, axis=-1, keepdims=True))
                    ec = jnp.exp(sc - m).astype(BF16)
                    den = jnp.dot(ec, ones, preferred_element_type=F32)
                    acc = _dot(ec, vc2)
                    if b > 0:
                        ep = jnp.exp(sp - m).astype(BF16)
                        den = den + jnp.dot(ep, ones, preferred_element_type=F32)
                        acc = acc + _dot(ep, vp2)
                    outs.append(acc / den)
                    lses.append(m + jnp.log(den))
                o_ref[c, cur, :] = jnp.where(first_head, outs[0], outs[1])
                l_ref[c, cur, :] = jnp.where(first_head, lses[0], lses[1])


def _prompt_attn(z_qkv, cos_t, sin_t, gi, dil, layer, kv_prev, *, depth, n_seq, seq):
    rows_p = n_seq * seq
    win = ATT_GROUPS[gi][0]
    keep = min(win, seq)
    assert seq % (dil * ATT_SPAN) == 0 and keep % 128 == 0 and seq % ROT_ROWS == 0
    tab = pl.BlockSpec((seq, 128), lambda n: (0, 0))
    chunked = pl.BlockSpec((ATT_OUT // 128, seq, 128), lambda n: (0, n, 0))
    kv_shape = (depth, n_seq, 2, HEADS_PER_GROUP, HEAD_DIM, keep)
    kv_spec = pl.BlockSpec((1, 1) + kv_shape[2:], lambda n: (layer, n, 0, 0, 0, 0))
    c0 = Z_QKV // ATT_OUT
    in_specs = [pl.BlockSpec((seq, ATT_OUT), lambda n: (n, c0 + gi)),
                pl.BlockSpec((seq, ATT_OUT), lambda n: (n, c0 + ATT_W // ATT_OUT + gi)),
                pl.BlockSpec((seq, ATT_OUT), lambda n: (n, c0 + 2 * ATT_W // ATT_OUT + gi)), tab, tab]
    args = [z_qkv, z_qkv, z_qkv, cos_t, sin_t]
    aliases = {}
    if kv_prev is not None:
        in_specs.append(pl.BlockSpec(memory_space=pl.ANY))
        args.append(kv_prev)
        aliases = {len(args) - 1: 2}
    chunk_shape = jax.ShapeDtypeStruct((ATT_OUT // 128, rows_p, 128), F32)
    return pl.pallas_call(
        functools.partial(_prompt_attn_kernel, dil=dil, seq=seq, keep=keep),
        grid=(n_seq,), in_specs=in_specs,
        out_specs=[chunked, chunked, kv_spec],
        out_shape=[chunk_shape, chunk_shape, jax.ShapeDtypeStruct(kv_shape, F32)],
        scratch_shapes=[pltpu.VMEM((ATT_OUT // 128, seq, 128), F32)] * 3,
        input_output_aliases=aliases,
        compiler_params=_params("arbitrary"), name=f"prompt_attn_g{gi}",
    )(*args)


def _merge_groups(os_, ls_):
    m = jnp.maximum(jnp.maximum(ls_[0], ls_[1]), ls_[2])
    es = [jnp.exp(l - m) for l in ls_]
    tot = es[0] + es[1] + es[2]
    return (es[0] / tot) * os_[0] + (es[1] / tot) * os_[1] + (es[2] / tot) * os_[2]


def _merge_kernel(o0, o1, o2, l0, l1, l2, a_ref):
    for c in range(ATT_OUT // 128):
        a_ref[:, c * 128:(c + 1) * 128] = _merge_groups([o0[c], o1[c], o2[c]], [l0[c], l1[c], l2[c]])


def _merge_attn(os_, ls_, *, tm):
    rows = os_[0].shape[1]
    spec = pl.BlockSpec((ATT_OUT // 128, tm, 128), lambda i: (0, i, 0))
    return pl.pallas_call(
        _merge_kernel, grid=(rows // tm,), in_specs=[spec] * 6, out_specs=pl.BlockSpec((tm, ATT_OUT), lambda i: (i, 0)),
        out_shape=jax.ShapeDtypeStruct((rows, ATT_OUT), F32),
        compiler_params=_params("arbitrary"), name="merge_attn",
    )(*os_, *ls_)


def _sample_attn_kernel(z_ref, cos_ref, sin_ref, *rest, t_new):
    caches, (a_ref, n0_ref, n1_ref, n2_ref) = rest[0:3], rest[-4:]
    scale = HEAD_DIM ** -0.5
    cos = cos_ref[...]
    sin = sin_ref[...]
    n_chunks = ATT_W // 128
    q_c = [_rotate(z_ref[:, c * 128:(c + 1) * 128], cos, sin) for c in range(n_chunks)]
    k_c = [_rotate(z_ref[:, ATT_W + c * 128:ATT_W + (c + 1) * 128], cos, sin) for c in range(n_chunks)]
    v_c = [z_ref[:, 2 * ATT_W + c * 128:2 * ATT_W + (c + 1) * 128] for c in range(n_chunks)]
    outs_g, lses_g = [], []
    for gi, (c_ref, n_ref, (win, dil)) in enumerate(zip(caches, (n0_ref, n1_ref, n2_ref), ATT_GROUPS)):
        cache_len = c_ref.shape[-1]
        t_c = lax.broadcasted_iota(I32, (t_new, cache_len), 0)
        c_c = lax.broadcasted_iota(I32, (t_new, cache_len), 1)
        d_c = cache_len + t_c - c_c
        ok_c = jnp.logical_and((d_c & (dil - 1)) == 0, d_c <= ATT_SPAN * dil)
        t_n = lax.broadcasted_iota(I32, (t_new, t_new), 0)
        u_n = lax.broadcasted_iota(I32, (t_new, t_new), 1)
        d_n = t_n - u_n
        ok_n = jnp.logical_and(d_n >= 0, (d_n & (dil - 1)) == 0)
        outs, lses = [], []
        for h in range(HEADS_PER_GROUP):
            chunk, half = divmod(gi * HEADS_PER_GROUP + h, 2)
            hs = slice(half * HEAD_DIM, (half + 1) * HEAD_DIM)
            q, k_new, v_new = q_c[chunk][:, hs], k_c[chunk][:, hs], v_c[chunk][:, hs]
            k_t = c_ref[0, 0, 0, h]
            v_t = c_ref[0, 0, 1, h]
            s_c = jnp.where(ok_c, _dot(q, k_t) * scale, NEG_INF)
            s_n = jnp.where(ok_n, _dot_nt(q, k_new) * scale, NEG_INF)
            m = jnp.maximum(jnp.max(s_c, axis=-1, keepdims=True), jnp.max(s_n, axis=-1, keepdims=True))
            e_c = jnp.exp(s_c - m)
            e_n = jnp.exp(s_n - m)
            den = jnp.sum(e_c, axis=-1, keepdims=True) + jnp.sum(e_n, axis=-1, keepdims=True)
            acc = _dot_nt(e_c, v_t) + _dot(e_n, v_new)
            outs.append(acc / den)
            lses.append(jnp.broadcast_to(m + jnp.log(den), (t_new, HEAD_DIM)))
            n_ref[0, 0, 0, h] = jnp.concatenate([k_t[:, t_new:], k_new.T], axis=1)
            n_ref[0, 0, 1, h] = jnp.concatenate([v_t[:, t_new:], v_new.T], axis=1)
        outs_g.append(jnp.concatenate(outs, axis=1))
        lses_g.append(jnp.concatenate(lses, axis=1))
    a_ref[...] = _merge_groups(outs_g, lses_g)


def _sample_attn(z_qkv, cos_s, sin_s, caches_t, layer, new_prev, *, row0, n_seq, t_new):
    b0 = row0 // t_new

    def cache_spec(c):
        return pl.BlockSpec((1, 1) + c.shape[2:], lambda n: (layer, n, 0, 0, 0, 0))

    tab = pl.BlockSpec((t_new, 128), lambda n: (0, 0))
    in_specs = ([pl.BlockSpec((t_new, 3 * ATT_W), lambda n: (b0 + n, Z_QKV // (3 * ATT_W))), tab, tab]
                + [cache_spec(c) for c in caches_t])
    args = [z_qkv, cos_s, sin_s, *caches_t]
    aliases = {}
    if new_prev is not None:
        for j, a in enumerate(new_prev):
            in_specs.append(pl.BlockSpec(memory_space=pl.ANY))
            args.append(a)
            aliases[len(args) - 1] = 1 + j
    return pl.pallas_call(
        functools.partial(_sample_attn_kernel, t_new=t_new),
        grid=(n_seq,), in_specs=in_specs,
        out_specs=[pl.BlockSpec((t_new, ATT_OUT), lambda n: (n, 0))] + [cache_spec(c) for c in caches_t],
        out_shape=[jax.ShapeDtypeStruct((n_seq * t_new, ATT_OUT), F32)]
        + [jax.ShapeDtypeStruct(c.shape, F32) for c in caches_t],
        input_output_aliases=aliases,
        compiler_params=_params("arbitrary"), name="sample_attn",
    )(*args)


def _pool_kernel(z_ref, h_ref, w_ref, s_ref, o_ref, buf_ref, *, t_len, pos0):
    buf_ref[0:POOL_HIST, :] = h_ref[0]
    buf_ref[POOL_HIST:POOL_HIST + t_len, :] = z_ref[...]
    pos = pos0 + lax.broadcasted_iota(I32, (t_len, POOL_GROUP), 0)
    for g, win in enumerate(POOL_WINDOWS):
        cs = slice(g * POOL_GROUP, (g + 1) * POOL_GROUP)
        z = buf_ref[POOL_HIST:POOL_HIST + t_len, cs]
        wsum = z
        for i in range(1, win):
            wsum = wsum + buf_ref[POOL_HIST - i:POOL_HIST - i + t_len, cs]
        cnt = jnp.minimum(win, pos + 1).astype(F32)
        y = wsum / cnt - z
        o_ref[:, cs] = _dot(y, w_ref[g]) * s_ref[:, cs]


def _pool_mixer(z_pool, hist, pool_w, pool_scale, *, row0, n_seq, t_len, pos0):
    b0 = row0 // t_len
    return pl.pallas_call(
        functools.partial(_pool_kernel, t_len=t_len, pos0=pos0),
        grid=(n_seq,),
        in_specs=[
            pl.BlockSpec((t_len, POOL_W), lambda n: (b0 + n, Z_POOL // POOL_W)),
            pl.BlockSpec((1, POOL_HIST, POOL_W), lambda n: (n, 0, 0)),
            pl.BlockSpec((len(POOL_WINDOWS), POOL_GROUP, POOL_GROUP), lambda n: (0, 0, 0)),
            pl.BlockSpec((1, POOL_W), lambda n: (0, 0)),
        ],
        out_specs=pl.BlockSpec((t_len, POOL_W), lambda n: (n, 0)),
        out_shape=jax.ShapeDtypeStruct((n_seq * t_len, POOL_W), F32),
        scratch_shapes=[pltpu.VMEM((POOL_HIST + t_len, POOL_W), F32)],
        compiler_params=_params("arbitrary"), name="pool_mixer",
    )(z_pool, hist, pool_w, pool_scale.reshape(1, POOL_W))


def _rwkv_prep_kernel(z_ref, zp_ref, first_ref, mu_ref, w0_ref, w2_ref, a0_ref, a2_ref, g2_ref, kk_ref, ka_ref, rk_ref,
                      r_o, lw_o, k_o, v_o, kn_o, b_o, g_o, bonus_o, buf_ref, *, tm):
    i = pl.program_id(1)
    z = z_ref[:, 0:RWKV_COLS]
    prev_row = jnp.where(i == 0, first_ref[0], zp_ref[SUBLANES - 1:SUBLANES, 0:RWKV_COLS])
    buf_ref[SUBLANES:SUBLANES + tm, :] = z
    buf_ref[SUBLANES - 1:SUBLANES, :] = prev_row
    shifted = buf_ref[SUBLANES - 1:SUBLANES - 1 + tm, :]
    xm = z + mu_ref[...] * (shifted - z)
    r = xm[:, 0:RWKV_W]
    k = xm[:, RWKV_W:2 * RWKV_W]
    v = xm[:, 2 * RWKV_W:3 * RWKV_W]
    wa = xm[:, RWKV_LORA_OFF:RWKV_LORA_OFF + 128]
    gl = xm[:, RWKV_LORA_OFF + 128:RWKV_COLS]
    xw = w0_ref[...] + _dot(jnp.tanh(wa), w2_ref[...])
    logw = -math.exp(-0.5) * jax.nn.sigmoid(xw)
    a = jax.nn.sigmoid(a0_ref[...] + _dot(wa, a2_ref[...]))
    g_o[...] = _dot(jax.nn.sigmoid(gl), g2_ref[...])
    kkr = k * kk_ref[...]
    kmod = k * (1.0 + (a - 1.0) * ka_ref[...])
    rkk = r * kmod * rk_ref[...]
    bonus = []
    for h in range(RWKV_HEADS):
        sl = slice(h * RWKV_HEAD, (h + 1) * RWKV_HEAD)
        kh = kkr[:, sl]
        nrm = jnp.sqrt(jnp.sum(kh * kh, axis=-1, keepdims=True))
        kn = kh / jnp.maximum(nrm, 1e-12)
        r_o[0, h] = r[:, sl]
        lw_o[0, h] = logw[:, sl]
        k_o[0, h] = kmod[:, sl]
        v_o[0, h] = v[:, sl]
        kn_o[0, h] = kn
        b_o[0, h] = kn * a[:, sl]
        bonus.append(jnp.sum(rkk[:, sl], axis=-1, keepdims=True) * v[:, sl])
    bonus_o[...] = jnp.concatenate(bonus, axis=1)


def _rwkv_prep(z, first_prev, lw, *, row0, n_seq, t_len, tm):
    nblk = t_len // tm
    pb = tm // SUBLANES
    b0 = row0 // tm
    p0 = row0 // SUBLANES

    def vec(n):
        return pl.BlockSpec((1, n), lambda s, i: (0, 0))

    hm = jax.ShapeDtypeStruct((n_seq, RWKV_HEADS, t_len, RWKV_HEAD), F32)
    rm = jax.ShapeDtypeStruct((n_seq * t_len, RWKV_W), F32)
    hm_spec = pl.BlockSpec((1, RWKV_HEADS, tm, RWKV_HEAD), lambda s, i: (s, 0, i, 0))
    rm_spec = pl.BlockSpec((tm, RWKV_W), lambda s, i: (s * nblk + i, 0))
    return pl.pallas_call(
        functools.partial(_rwkv_prep_kernel, tm=tm),
        grid=(n_seq, nblk),
        in_specs=[
            pl.BlockSpec((tm, Z_RWKV_BLOCK), lambda s, i: (b0 + s * nblk + i, Z_RWKV // Z_RWKV_BLOCK)),
            pl.BlockSpec((SUBLANES, Z_RWKV_BLOCK),
                         lambda s, i: (jnp.maximum(p0 + (s * nblk + i) * pb - 1, 0), Z_RWKV // Z_RWKV_BLOCK)),
            pl.BlockSpec((1, 1, RWKV_COLS), lambda s, i: (s, 0, 0)),
            vec(RWKV_COLS), vec(RWKV_W),
            pl.BlockSpec((128, RWKV_W), lambda s, i: (0, 0)),
            vec(RWKV_W),
            pl.BlockSpec((128, RWKV_W), lambda s, i: (0, 0)),
            pl.BlockSpec((128, RWKV_W), lambda s, i: (0, 0)),
            vec(RWKV_W), vec(RWKV_W), vec(RWKV_W),
        ],
        out_specs=[hm_spec] * 6 + [rm_spec] * 2,
        out_shape=[hm] * 6 + [rm] * 2,
        scratch_shapes=[pltpu.VMEM((tm + SUBLANES, RWKV_COLS), F32)],
        compiler_params=_params("arbitrary", "arbitrary"), name="rwkv_prep",
    )(z, z, first_prev, lw["mu"], lw["w0"], lw["w2p"], lw["a0"], lw["a2p"], lw["g2"], lw["kk"], lw["ka"], lw["rk"])


def _bmm(a, b):
    return jnp.einsum("hqk,hkd->hqd", a.astype(BF16), b.astype(BF16), preferred_element_type=F32)


def _bmm_nt(a, b):
    return jnp.einsum("hqd,hkd->hqk", a.astype(BF16), b.astype(BF16), preferred_element_type=F32)


def _bmm_tn(a, b):
    return jnp.einsum("hkq,hkd->hqd", a.astype(BF16), b.astype(BF16), preferred_element_type=F32)


def _rwkv_scan_kernel(r_ref, lw_ref, k_ref, v_ref, kn_ref, b_ref, s0_ref, y_ref, st_ref, s_scr, *, chunk, group):
    c = pl.program_id(1)
    nh = group * RWKV_HEADS

    def heads(ref):
        return ref[...].reshape((nh,) + ref.shape[2:])

    @pl.when(c == 0)
    def _():
        s_scr[...] = heads(s0_ref)

    r, logw, k, v, kn, b = [heads(ref) for ref in (r_ref, lw_ref, k_ref, v_ref, kn_ref, b_ref)]
    row = lax.broadcasted_iota(I32, (chunk, chunk), 0)
    col = lax.broadcasted_iota(I32, (chunk, chunk), 1)
    incl = row >= col
    strict = row > col
    tri = jnp.broadcast_to(incl.astype(BF16)[None], (nh, chunk, chunk))
    lw_hi = logw.astype(BF16)
    lw_lo = (logw - lw_hi.astype(F32)).astype(BF16)
    cum = (jnp.einsum("hqk,hkd->hqd", tri, lw_hi, preferred_element_type=F32)
           + jnp.einsum("hqk,hkd->hqd", tri, lw_lo, preferred_element_type=F32))
    p_inv = jnp.exp(-cum)
    kt = k * p_inv
    bt = b * p_inv
    kap = kn * jnp.exp(cum - logw)
    rho = r * jnp.exp(cum)
    qq = jnp.concatenate([kap, rho], axis=1)
    gram = _bmm_nt(qq, jnp.concatenate([kt, bt], axis=1))
    a_k = jnp.where(strict[None], gram[:, :chunk, :chunk], 0.0)
    a_b = jnp.where(strict[None], gram[:, :chunk, chunk:], 0.0)
    l_k = jnp.where(incl[None], gram[:, chunk:, :chunk], 0.0)
    l_b = jnp.where(incl[None], gram[:, chunk:, chunk:], 0.0)
    x = jnp.broadcast_to((row == col).astype(F32)[None], (nh, chunk, chunk))
    m = 1
    while m < chunk:
        sh = m.bit_length() - 1
        same = (row >> (sh + 1)) == (col >> (sh + 1))
        lower_left = jnp.logical_and(((row >> sh) & 1) == 1, ((col >> sh) & 1) == 0)
        off = jnp.where(jnp.logical_and(same, lower_left)[None], a_b, 0.0)
        x = x - _bmm(_bmm(x, off), x)
        m *= 2
    s = s_scr[...]
    qs = _bmm_nt(qq, s)
    u = _bmm(x, -(qs[:, :chunk] + _bmm(a_k, v)))
    y = qs[:, chunk:] + _bmm(jnp.concatenate([l_k, l_b], axis=2), jnp.concatenate([v, u], axis=1))
    s_new = (s + _bmm_tn(jnp.concatenate([v, u], axis=1), jnp.concatenate([kt, bt], axis=1))) * jnp.exp(cum[:, chunk - 1:chunk, :])
    s_scr[...] = s_new
    st_ref[...] = s_new.reshape(st_ref.shape)
    mean = jnp.mean(y, axis=-1, keepdims=True)
    var = jnp.mean(jnp.square(y - mean), axis=-1, keepdims=True)
    yn = (y - mean) * lax.rsqrt(var + GN_EPS)
    for g in range(group):
        y_ref[g] = jnp.concatenate([yn[g * RWKV_HEADS + h] for h in range(RWKV_HEADS)], axis=1)


def _rwkv_scan(prep, s0, *, chunk, group):
    r, lw, k, v, kn, b = prep
    n_seq, nh, t_len, hd = r.shape
    hm_spec = pl.BlockSpec((group, nh, chunk, hd), lambda s, c: (s, 0, c, 0))
    st_spec = pl.BlockSpec((group, nh, hd, hd), lambda s, c: (s, 0, 0, 0))
    return pl.pallas_call(
        functools.partial(_rwkv_scan_kernel, chunk=chunk, group=group),
        grid=(n_seq // group, t_len // chunk),
        in_specs=[hm_spec] * 6 + [st_spec],
        out_specs=[pl.BlockSpec((group, chunk, RWKV_W), lambda s, c: (s, c, 0)), st_spec],
        out_shape=[jax.ShapeDtypeStruct((n_seq, t_len, RWKV_W), F32), jax.ShapeDtypeStruct((n_seq, nh, hd, hd), F32)],
        scratch_shapes=[pltpu.VMEM((group * nh, hd, hd), F32)],
        compiler_params=_params("arbitrary", "arbitrary"), name="rwkv_scan",
    )(r, lw, k, v, kn, b, s0)


def _mix_kernel(*refs, n_prompt_blocks):
    pairs, (gate_ref, lnw_ref, lnb_ref, wa_ref, wp_ref, wr_ref, o_ref) = refs[:10], refs[10:]
    is_sample = pl.program_id(0) >= n_prompt_blocks
    att, pool, yn, bonus, g = [jnp.where(is_sample, pairs[2 * j + 1][...], pairs[2 * j][...]) for j in range(5)]
    rw = (yn * lnw_ref[...] + lnb_ref[...] + bonus) * g
    mix = jax.nn.sigmoid(gate_ref[:, 0:D_MODEL]) * _dot(att, wa_ref[...])
    mix = mix + jax.nn.sigmoid(gate_ref[:, D_MODEL:2 * D_MODEL]) * _dot(pool, wp_ref[...])
    mix = mix + jax.nn.sigmoid(gate_ref[:, 2 * D_MODEL:3 * D_MODEL]) * _dot(rw, wr_ref[...])
    o_ref[...] = mix.astype(BF16)


def _mix(branches, z_gate, ln_w, ln_b, wa, wp, wr, *, tm):
    rows = z_gate.shape[0]
    n_prompt_blocks = branches[0][0].shape[0] // tm
    assert rows == (n_prompt_blocks + 1) * tm and all(s.shape[0] == tm for _, s in branches)

    def full(a):
        return pl.BlockSpec(a.shape, lambda i: (0, 0))

    specs, args = [], []
    for p, s in branches:
        specs += [pl.BlockSpec((tm, p.shape[1]), lambda i: (jnp.minimum(i, n_prompt_blocks - 1), 0)), full(s)]
        args += [p, s]
    ln_w = ln_w.reshape(1, RWKV_W)
    ln_b = ln_b.reshape(1, RWKV_W)
    return pl.pallas_call(
        functools.partial(_mix_kernel, n_prompt_blocks=n_prompt_blocks), grid=(rows // tm,),
        in_specs=specs + [pl.BlockSpec((tm, 3 * D_MODEL), lambda i: (i, 0)),
                          full(ln_w), full(ln_b), full(wa), full(wp), full(wr)],
        out_specs=pl.BlockSpec((tm, D_MODEL), lambda i: (i, 0)),
        out_shape=jax.ShapeDtypeStruct((rows, D_MODEL), BF16),
        compiler_params=_params("arbitrary"), name="gated_mix",
    )(*args, z_gate, ln_w, ln_b, wa, wp, wr)


def _mm_res_kernel(x_ref, a_ref, w_ref, o_ref):
    o_ref[...] = x_ref[...] + jnp.dot(a_ref[...], w_ref[...], preferred_element_type=F32)


def _mm_residual(x, a, w, *, tm):
    rows, n = x.shape
    k = a.shape[1]
    return pl.pallas_call(
        _mm_res_kernel, grid=(rows // tm,),
        in_specs=[pl.BlockSpec((tm, n), lambda i: (i, 0)), pl.BlockSpec((tm, k), lambda i: (i, 0)),
                  pl.BlockSpec((k, n), lambda i: (0, 0))],
        out_specs=pl.BlockSpec((tm, n), lambda i: (i, 0)),
        out_shape=jax.ShapeDtypeStruct((rows, n), F32),
        compiler_params=_params("arbitrary"), name="out_proj",
    )(x, a, w)


def _topk_cols(scores, ids, ids_ordered, k):
    lanes = scores[0].shape[1]
    iota_k = lax.broadcasted_iota(I32, (k, lanes), 0)
    big = jnp.iinfo(jnp.int32).max

    def arg_max(s, ident, ordered):
        if ordered:
            tiles = [(s[i:i + SUBLANES], ident[i:i + SUBLANES]) for i in range(0, s.shape[0], SUBLANES)]
            while len(tiles) > 1:
                merged = []
                for (va, ia), (vb, ib) in zip(tiles[0::2], tiles[1::2]):
                    later = vb > va
                    merged.append((jnp.where(later, vb, va), jnp.where(later, ib, ia)))
                tiles = merged + tiles[len(tiles) - len(tiles) % 2:]
            s, ident = tiles[0]
        m = jnp.max(s, axis=0, keepdims=True)
        return m, jnp.min(jnp.where(s == m, ident, big), axis=0, keepdims=True)

    def body(j, carry):
        out = []
        for (s, vals, idxs), ident, ordered in zip(carry, ids, ids_ordered):
            m, idx = arg_max(s, ident, ordered)
            vals = jnp.where(iota_k == j, m, vals)
            idxs = jnp.where(iota_k == j, idx, idxs)
            out.append((jnp.where(ident == idx, NEG_INF, s), vals, idxs))
        return tuple(out)

    init = tuple((s, jnp.zeros((k, lanes), F32), jnp.zeros((k, lanes), I32)) for s in scores)
    return [(v, i) for _, v, i in lax.fori_loop(0, k, body, init)]


def _gather_rows(table, sel, k):
    out = jnp.zeros(sel.shape, table.dtype)
    for a in range(k):
        out = jnp.where(sel == a, table[a:a + 1, :], out)
    return out


def _peer_select_kernel(q_ref, skh_ref, skl_ref, i1_o, i2_o, gate_o, i1_s, i2_s, g_s, *, tok):
    kk = PEER_TOPK
    half = kk // 2
    nt = (((1,), (1,)), ((), ()))
    key_id = lax.broadcasted_iota(I32, (N_KEYS, tok), 0)
    row = lax.broadcasted_iota(I32, (half * half + kk, tok), 0)
    cand_id = jnp.where(row < half * half, (row >> (half.bit_length() - 1)) * kk + (row & (half - 1)),
                        jnp.where(row < half * half + half, row - half * half + half, (row - half * half) * kk))

    def finish(slot, top, sel, k1, k2):
        e1 = _gather_rows(k1, sel >> 4, kk)
        e2 = _gather_rows(k2, sel & (kk - 1), kk)
        ex = jnp.exp(top - jnp.max(top, axis=0, keepdims=True))
        gate = ex / jnp.sum(ex, axis=0, keepdims=True)
        r0 = pl.multiple_of(slot * kk, kk)
        i1_s[pl.ds(r0, kk), :] = e1.astype(F32)
        i2_s[pl.ds(r0, kk), :] = e2.astype(F32)
        g_s[pl.ds(r0, kk), :] = gate

    def head(h, carry):
        cand_prev, k1_prev, k2_prev = carry
        c0 = pl.multiple_of(h * 2 * N_KEYS, 2 * N_KEYS)
        q1h, q1l = _split_bf16(q_ref[:, pl.ds(c0, N_KEYS)])
        q2h, q2l = _split_bf16(q_ref[:, pl.ds(c0 + N_KEYS, N_KEYS)])
        s1 = _dot3(skh_ref[h, 0], skl_ref[h, 0], q1h, q1l, nt)
        s2 = _dot3(skh_ref[h, 1], skl_ref[h, 1], q2h, q2l, nt)
        (t1, k1), (t2, k2), (top, sel) = _topk_cols([s1, s2, cand_prev], [key_id, key_id, cand_id],
                                                    [True, True, False], kk)
        finish(jnp.where(h == 0, PEER_HEADS, h - 1), top, sel, k1_prev, k2_prev)
        cand = jnp.concatenate([t1[a:a + 1, :] + t2[0:half, :] for a in range(half)]
                               + [t1[0:1, :] + t2[half:kk, :], t1[half:kk, :] + t2[0:1, :]], axis=0)
        return cand, k1, k2

    zeros_i = jnp.zeros((kk, tok), I32)
    cand, k1, k2 = lax.fori_loop(0, PEER_HEADS, head, (jnp.zeros(cand_id.shape, F32), zeros_i, zeros_i))
    ((top, sel),) = _topk_cols([cand], [cand_id], [False], kk)
    finish(PEER_HEADS - 1, top, sel, k1, k2)
    i1_o[...] = i1_s[0:PEER_PAIRS, :].T.astype(I32)
    i2_o[...] = i2_s[0:PEER_PAIRS, :].T.astype(I32)
    gate_o[...] = g_s[0:PEER_PAIRS, :].T


def _peer_select(q, sk_hi, sk_lo, *, tok=128):
    rows = q.shape[0]
    spec = pl.BlockSpec((tok, PEER_PAIRS), lambda i: (i, 0))
    return pl.pallas_call(
        functools.partial(_peer_select_kernel, tok=tok), grid=(rows // tok,),
        in_specs=[pl.BlockSpec((tok, q.shape[1]), lambda i: (i, 0)),
                  pl.BlockSpec(sk_hi.shape, lambda i: (0, 0, 0, 0)), pl.BlockSpec(sk_lo.shape, lambda i: (0, 0, 0, 0))],
        out_specs=[spec] * 3,
        out_shape=[jax.ShapeDtypeStruct((rows, PEER_PAIRS), I32), jax.ShapeDtypeStruct((rows, PEER_PAIRS), I32),
                   jax.ShapeDtypeStruct((rows, PEER_PAIRS), F32)],
        scratch_shapes=[pltpu.VMEM((PEER_PAIRS + PEER_TOPK, tok), F32)] * 3,
        compiler_params=_params("arbitrary"), name="peer_select",
    )(q, sk_hi, sk_lo)


def _peer_pick(d_ref, i1_ref, i2_ref, act_ref, first_row):
    n_rows = d_ref.shape[1] // N_KEYS
    for g in range(act_ref.shape[0] // SUBLANES):
        rs = slice(g * SUBLANES, (g + 1) * SUBLANES)
        i1 = i1_ref[rs, :]
        i2 = i2_ref[rs, :]
        acc = act_ref[rs, :]
        for j in range(n_rows):
            got = jnp.take_along_axis(d_ref[rs, j * N_KEYS:(j + 1) * N_KEYS], i2, axis=1, mode="promise_in_bounds")
            acc = acc + jnp.where(i1 == first_row + j, got, 0.0)
        act_ref[rs, :] = acc


def _peer_act_kernel(xn_ref, u_ref, i1_ref, i2_ref, act_ref, da_ref, db_ref):
    first = jnp.logical_and(pl.program_id(0) == 0, pl.program_id(1) == 0)
    c = pl.program_id(1)
    slab = PEER_EC // 2
    slab_rows = slab // N_KEYS
    nt = (((1,), (1,)), ((), ()))

    @pl.when(first)
    def _():
        db_ref[...] = jnp.zeros(db_ref.shape, F32)

    @pl.when(c == 0)
    def _():
        act_ref[...] = jnp.zeros(act_ref.shape, F32)

    base = c * 2 * slab_rows
    _peer_pick(db_ref, i1_ref, i2_ref, act_ref, base - slab_rows)
    da_ref[...] = lax.dot_general(xn_ref[...], u_ref[0, 0:slab, :].astype(BF16), nt, preferred_element_type=F32)
    _peer_pick(da_ref, i1_ref, i2_ref, act_ref, base)
    db_ref[...] = lax.dot_general(xn_ref[...], u_ref[0, slab:2 * slab, :].astype(BF16), nt, preferred_element_type=F32)

    @pl.when(c == pl.num_programs(1) - 1)
    def _():
        _peer_pick(db_ref, i1_ref, i2_ref, act_ref, base + slab_rows)


def _peer_act(xn, u_tabs, layer, i1, i2, *, tb):
    rows = xn.shape[0]
    pair_spec = pl.BlockSpec((tb, PEER_PAIRS), lambda i, c: (i, 0))
    return pl.pallas_call(
        _peer_act_kernel, grid=(rows // tb, N_EXPERTS // PEER_EC),
        in_specs=[pl.BlockSpec((tb, D_MODEL), lambda i, c: (i, 0)),
                  pl.BlockSpec((1, PEER_EC, D_MODEL), lambda i, c: (layer, c, 0)), pair_spec, pair_spec],
        out_specs=pair_spec,
        out_shape=jax.ShapeDtypeStruct((rows, PEER_PAIRS), F32),
        scratch_shapes=[pltpu.VMEM((tb, PEER_EC // 2), F32)] * 2,
        compiler_params=_params("arbitrary", "arbitrary"), name="peer_act",
    )(xn, u_tabs, i1, i2)


def _peer_out_kernel(i1_ref, i2_ref, gate_ref, act_ref, v_ref, o_ref, w_ref):
    c = pl.program_id(1)
    n_groups = w_ref.shape[0]
    rows_per_chunk = PEER_OUT_EC // N_KEYS

    @pl.when(c == 0)
    def _():
        key_iota = lax.broadcasted_iota(I32, (N_KEYS, PEER_PAIRS), 0)

        def group(g, carry):
            r0 = pl.multiple_of(g * SUBLANES, SUBLANES)
            i1 = i1_ref[pl.ds(r0, SUBLANES), :]
            i2 = i2_ref[pl.ds(r0, SUBLANES), :]
            a = act_ref[pl.ds(r0, SUBLANES), :]
            wgt = gate_ref[pl.ds(r0, SUBLANES), :] * (0.5 * a * (1.0 + lax.erf(a * (1.0 / math.sqrt(2.0)))))
            for t in range(SUBLANES):
                hit1 = key_iota == jnp.broadcast_to(i1[t:t + 1, :], key_iota.shape)
                hit2 = key_iota == jnp.broadcast_to(i2[t:t + 1, :], key_iota.shape)
                w2 = jnp.where(hit2, jnp.broadcast_to(wgt[t:t + 1, :], key_iota.shape), 0.0)
                w_tok = _dot_nt(jnp.where(hit1, 1.0, 0.0), w2)
                w_ref[g, pl.ds(t, N_KEYS, stride=SUBLANES), :] = w_tok
            return carry

        lax.fori_loop(0, n_groups, group, 0, unroll=PEER_BUILD_UNROLL)

    row0 = c * (rows_per_chunk * SUBLANES)
    lhs = jnp.concatenate(
        [w_ref[:, pl.ds(pl.multiple_of(row0 + j * SUBLANES, SUBLANES), SUBLANES), :].reshape(n_groups * SUBLANES, N_KEYS)
         for j in range(rows_per_chunk)], axis=1)
    part = jnp.dot(lhs.astype(BF16), v_ref[0], preferred_element_type=F32)

    @pl.when(c == 0)
    def _():
        o_ref[...] = part

    @pl.when(c > 0)
    def _():
        o_ref[...] += part


def _peer_out(i1, i2, gate, act, v_tabs, layer):
    rows = i1.shape[0]
    tb = PEER_TB
    pair_spec = pl.BlockSpec((tb, PEER_PAIRS), lambda i, c: (i, 0))
    return pl.pallas_call(
        _peer_out_kernel, grid=(rows // tb, N_EXPERTS // PEER_OUT_EC),
        in_specs=[pair_spec] * 4 + [pl.BlockSpec((1, PEER_OUT_EC, D_MODEL), lambda i, c: (layer, c, 0))],
        out_specs=pl.BlockSpec((tb, D_MODEL), lambda i, c: (i, 0)),
        out_shape=jax.ShapeDtypeStruct((rows, D_MODEL), F32),
        scratch_shapes=[pltpu.VMEM((tb // SUBLANES, N_KEYS * SUBLANES, N_KEYS), F32)],
        compiler_params=_params("arbitrary", "arbitrary"), name="peer_out",
    )(i1, i2, gate, act, v_tabs)


def _ple_kernel(x_ref, f_ref, p_ref, g_ref, wg_ref, wp_ref, gf_ref, o_ref, *, final):
    x = x_ref[...] + f_ref[...]
    ms = jnp.mean(x * x, axis=-1, keepdims=True)
    h = (x * lax.rsqrt(ms + RMS_EPS)) * g_ref[...]
    out = x + jax.nn.sigmoid(_dot(h, wg_ref[...])) * _dot(p_ref[...], wp_ref[...])
    if final:
        ms2 = jnp.mean(out * out, axis=-1, keepdims=True)
        out = (out * lax.rsqrt(ms2 + RMS_EPS)) * gf_ref[...]
    o_ref[...] = out


def _ple(x, ffn, p, g, wg, wp, g_final, *, tm, final):
    rows, n = x.shape
    row = pl.BlockSpec((tm, n), lambda i: (i, 0))
    vec = pl.BlockSpec((1, n), lambda i: (0, 0))
    return pl.pallas_call(
        functools.partial(_ple_kernel, final=final), grid=(rows // tm,),
        in_specs=[row, row, pl.BlockSpec((tm, p.shape[1]), lambda i: (i, 0)), vec,
                  pl.BlockSpec(wg.shape, lambda i: (0, 0)), pl.BlockSpec(wp.shape, lambda i: (0, 0)), vec],
        out_specs=row,
        out_shape=jax.ShapeDtypeStruct((rows, n), F32),
        compiler_params=_params("arbitrary"), name="ple",
    )(x, ffn, p, g.reshape(1, n), wg, wp, g_final.reshape(1, n))


def _layer(x, p_rows, lw, state, g_final, dims, final):
    n_p, t_p, n_s, t_s, past = dims
    rows_p = n_p * t_p
    rows_s = n_s * t_s

    z = _norm_mm(x, lw["g_mix"], lw["w_in"], lw["layer"], tm=768, tn=1920)
    z_qkv = z_pool = z_rwkv = z_gate = z

    layer, depth = lw["layer"], lw["depth"]
    os_, ls_, kv_p = [], [], []
    for gi, (_, dil) in enumerate(ATT_GROUPS):
        o, l, kv = _prompt_attn(z_qkv, lw["cos_p"], lw["sin_p"], gi, dil, layer,
                                None if state["kv_p"] is None else state["kv_p"][gi], depth=depth, n_seq=n_p, seq=t_p)
        os_.append(o)
        ls_.append(l)
        kv_p.append(kv)
    att_p = _merge_attn(os_, ls_, tm=512)
    att_s, *kv_s = _sample_attn(z_qkv, lw["cos_s"], lw["sin_s"], state["caches_t"], layer, state["kv_s"],
                                row0=rows_p, n_seq=n_s, t_new=t_s)

    pool_p = _pool_mixer(z_pool, jnp.zeros((n_p, POOL_HIST, POOL_W), F32), lw["pool_w"], lw["pool_scale"],
                         row0=0, n_seq=n_p, t_len=t_p, pos0=0)
    hist_s = jnp.concatenate([jnp.zeros((n_s, 1, POOL_W), F32), state["pool"]], axis=1)
    pool_s = _pool_mixer(z_pool, hist_s, lw["pool_w"], lw["pool_scale"], row0=rows_p, n_seq=n_s, t_len=t_s, pos0=past)

    prep_p = _rwkv_prep(z_rwkv, jnp.zeros((n_p, 1, RWKV_COLS), F32), lw, row0=0, n_seq=n_p, t_len=t_p, tm=256)
    prep_s = _rwkv_prep(z_rwkv, state["shift"][:, None, :], lw, row0=rows_p, n_seq=n_s, t_len=t_s, tm=t_s)
    yn_p, wkv_p = _rwkv_scan(prep_p[:6], jnp.zeros((n_p, RWKV_HEADS, RWKV_HEAD, RWKV_HEAD), F32),
                             chunk=RWKV_CHUNK, group=RWKV_SEQ_GROUP)
    yn_s, wkv_s = _rwkv_scan(prep_s[:6], state["wkv"], chunk=t_s, group=RWKV_SEQ_GROUP)
    branches = [(att_p, att_s), (pool_p, pool_s), (yn_p.reshape(rows_p, RWKV_W), yn_s.reshape(rows_s, RWKV_W)),
                (prep_p[7], prep_s[7]), (prep_p[6], prep_s[6])]

    mix = _mix(branches, z_gate, lw["ln_w"], lw["ln_b"], lw["w_attn_o"], lw["w_pool_o"], lw["w_rwkv_o"], tm=rows_s)
    x = _mm_residual(x, mix, lw["w_out"], tm=384)

    q, xn = _norm_mm3(x, lw["g_ffn"], *lw["peer_wq"], tm=768, tn=512)
    i1, i2, gate = _peer_select(q, *lw["peer_subkeys"])
    act = _peer_act(xn, lw["peer_u"], lw["layer"], i1, i2, tb=PEER_ACT_TB)
    ffn = _peer_out(i1, i2, gate, act, lw["peer_v"], lw["layer"])

    x = _ple(x, ffn, p_rows, lw["g_ple"], lw["ple_wg"], lw["ple_wp"], g_final, tm=256, final=final)

    keep = POOL_HIST - 1
    pool_cols = slice(Z_POOL, Z_POOL + POOL_W)
    rwkv_cols = slice(Z_RWKV, Z_RWKV + RWKV_COLS)
    pool_p_state = jnp.stack([z[(n + 1) * t_p - keep:(n + 1) * t_p, pool_cols] for n in range(n_p)])
    zp_s = z[rows_p:, pool_cols].reshape(n_s, t_s, POOL_W)
    new_p = [pool_p_state, wkv_p, z[t_p - 1:rows_p:t_p, rwkv_cols]]
    new_s = [jnp.concatenate([hist_s[:, 1:], zp_s], axis=1)[:, -keep:], wkv_s, z[rows_p + t_s - 1::t_s, rwkv_cols]]
    return x, kv_p, kv_s, new_p, new_s


def kernel(x_prompt, x_sample, p_prompt, p_sample, cache_attn_w128, cache_attn_w512, cache_attn_w2048, state_pool, state_rwkv_wkv, state_rwkv_shift, g_mix, w_in, w_attn_o, w_pool_o, w_rwkv_o, w_out, pool_w, pool_scale, rwkv_mu, rwkv_w0, rwkv_w2, rwkv_a0, rwkv_a2, rwkv_g2, rwkv_kk, rwkv_ka, rwkv_rk, rwkv_ln_w, rwkv_ln_b, g_ffn, peer_wq, peer_subkeys, peer_u, peer_v, g_ple, ple_wg, ple_wp, g_final):
    n_p, t_p, _ = x_prompt.shape
    n_s, t_s, _ = x_sample.shape
    depth = w_in.shape[0]
    past = PAST_LEN
    rows_p = n_p * t_p
    rows_s = n_s * t_s
    dims = (n_p, t_p, n_s, t_s, past)

    x = jnp.concatenate([x_prompt.reshape(rows_p, D_MODEL), x_sample.reshape(rows_s, D_MODEL)], axis=0)
    cos_p, sin_p = _rope_tables(jnp.arange(t_p, dtype=I32))
    cos_s, sin_s = _rope_tables(past + jnp.arange(t_s, dtype=I32))
    lora_pad = jnp.zeros((128 - 64, RWKV_W), F32)
    peer_v_bf16 = peer_v.astype(BF16)
    to_dev = (0, 1, 3, 4, 5, 2)
    from_dev = (0, 1, 5, 2, 3, 4)
    caches_t = [jnp.transpose(c, to_dev) for c in (cache_attn_w128, cache_attn_w512, cache_attn_w2048)]

    def row(a):
        return a.reshape(1, -1)

    w_in_z = jnp.concatenate([w_in[:, :, OFF_GATE:], w_in[:, :, OFF_RWKV:OFF_GATE], w_in[:, :, OFF_POOL:OFF_RWKV],
                              w_in[:, :, :OFF_POOL]], axis=2).astype(BF16)

    new_p, new_s = [], []
    kv_p = kv_s = None
    for l in range(depth):
        lw = {
            "g_mix": g_mix[l], "w_in": w_in_z,
            "cos_p": cos_p, "sin_p": sin_p, "cos_s": cos_s, "sin_s": sin_s, "depth": depth,
            "pool_w": pool_w[l].astype(BF16), "pool_scale": pool_scale[l],
            "mu": row(rwkv_mu[l]), "w0": row(rwkv_w0[l]), "a0": row(rwkv_a0[l]),
            "w2p": jnp.concatenate([rwkv_w2[l], lora_pad], axis=0).astype(BF16),
            "a2p": jnp.concatenate([lora_pad, rwkv_a2[l]], axis=0).astype(BF16),
            "g2": rwkv_g2[l].astype(BF16),
            "kk": row(rwkv_kk[l]), "ka": row(rwkv_ka[l]), "rk": row(rwkv_rk[l]),
            "ln_w": rwkv_ln_w[l], "ln_b": rwkv_ln_b[l],
            "w_attn_o": w_attn_o[l].astype(BF16), "w_pool_o": w_pool_o[l].astype(BF16),
            "w_rwkv_o": w_rwkv_o[l].astype(BF16), "w_out": w_out[l].astype(BF16),
            "g_ffn": g_ffn[l], "peer_wq": _split_bf16(peer_wq[l]), "peer_subkeys": _split_bf16(peer_subkeys[l]),
            "peer_u": peer_u, "peer_v": peer_v_bf16, "layer": l,
            "g_ple": g_ple[l], "ple_wg": ple_wg[l].astype(BF16), "ple_wp": ple_wp[l].astype(BF16),
        }
        state = {"caches_t": caches_t, "kv_p": kv_p, "kv_s": kv_s, "pool": state_pool[l],
                 "wkv": state_rwkv_wkv[l], "shift": state_rwkv_shift[l]}
        p_rows = jnp.concatenate([p_prompt[l].reshape(rows_p, -1), p_sample[l].reshape(rows_s, -1)], axis=0).astype(BF16)
        x, kv_p, kv_s, st_p, st_s = _layer(x, p_rows, lw, state, g_final, dims, l == depth - 1)
        new_p.append(st_p)
        new_s.append(st_s)

    outs = [x[:rows_p].reshape(n_p, t_p, D_MODEL), x[rows_p:].reshape(n_s, t_s, D_MODEL)]
    for gi in range(len(ATT_GROUPS)):
        outs.append(jnp.transpose(kv_p[gi], from_dev))
        outs.append(jnp.transpose(kv_s[gi], from_dev))
    for j in range(3):
        outs.append(jnp.stack([s[j] for s in new_p]))
        outs.append(jnp.stack([s[j] for s in new_s]))
    return tuple(outs)
```

```python
import functools
import math

import jax
import jax.numpy as jnp
from jax import lax
from jax.experimental import pallas as pl
from jax.experimental.pallas import tpu as pltpu

F32 = jnp.float32
BF16 = jnp.bfloat16
I32 = jnp.int32

D_MODEL = 2048
RMS_EPS = 1e-6
HEAD_DIM = 64
ATT_GROUPS = ((128, 1), (512, 4), (2048, 16))
HEADS_PER_GROUP = 4
ATT_W = 768
ATT_OUT = 256
ATT_SPAN = 128
ROPE_THETA = 10000.0
POOL_WINDOWS = (2, 4, 8, 16)
POOL_GROUP = 128
POOL_W = 512
POOL_HIST = 16
RWKV_HEAD = 64
RWKV_HEADS = 12
RWKV_W = 768
RWKV_COLS = 2560
RWKV_LORA_OFF = 2304
GN_EPS = 64e-5
RWKV_CHUNK = 64
RWKV_SEQ_GROUP = 4
OFF_POOL = 2304
OFF_RWKV = 2816
OFF_GATE = 5376
IN_COLS = 11520
Z_GATE = 0
Z_RWKV = 3 * D_MODEL
Z_POOL = Z_RWKV + RWKV_COLS
Z_QKV = Z_POOL + POOL_W
Z_RWKV_BLOCK = RWKV_COLS + POOL_W
PAST_LEN = 8192
PEER_HEADS = 8
N_KEYS = 128
N_EXPERTS = N_KEYS * N_KEYS
PEER_TOPK = 16
PEER_PAIRS = PEER_HEADS * PEER_TOPK
PEER_TB = 384
PEER_ACT_TB = 1408
PEER_BUILD_UNROLL = 4
PEER_EC = 1024
PEER_OUT_EC = 2048
SUBLANES = 8
VMEM_LIMIT = 56 * 1024 * 1024

NEG_INF = float("-inf")


def _params(*sem):
    return pltpu.CompilerParams(dimension_semantics=sem, vmem_limit_bytes=VMEM_LIMIT)


def _dot(a, b):
    return jnp.dot(a.astype(BF16), b.astype(BF16), preferred_element_type=F32)


def _dot_nt(a, b):
    return lax.dot_general(a.astype(BF16), b.astype(BF16), (((1,), (1,)), ((), ())), preferred_element_type=F32)


def _norm_mm_kernel(x_ref, g_ref, w_ref, o_ref, xn_ref):
    @pl.when(pl.program_id(1) == 0)
    def _():
        x = x_ref[...]
        ms = jnp.mean(x * x, axis=-1, keepdims=True)
        xn_ref[...] = ((x * lax.rsqrt(ms + RMS_EPS)) * g_ref[...]).astype(BF16)

    o_ref[...] = jnp.dot(xn_ref[...], w_ref[0], preferred_element_type=F32)


def _norm_mm(x, g, w, layer, *, tm, tn):
    rows, k = x.shape
    n = w.shape[2]
    return pl.pallas_call(
        _norm_mm_kernel, grid=(rows // tm, n // tn),
        in_specs=[
            pl.BlockSpec((tm, k), lambda i, j: (i, 0)),
            pl.BlockSpec((1, k), lambda i, j: (0, 0)),
            pl.BlockSpec((1, k, tn), lambda i, j: (layer, 0, j)),
        ],
        out_specs=pl.BlockSpec((tm, tn), lambda i, j: (i, j)),
        out_shape=jax.ShapeDtypeStruct((rows, n), F32),
        scratch_shapes=[pltpu.VMEM((tm, k), BF16)],
        compiler_params=_params("arbitrary", "arbitrary"), name="norm_mm",
    )(x, g.reshape(1, k), w)


def _split_bf16(a):
    hi = a.astype(BF16)
    lo = (a - hi.astype(F32)).astype(BF16)
    return hi, lo


def _dot3(ah, al, bh, bl, dims=(((1,), (0,)), ((), ()))):
    def d(p, q):
        return lax.dot_general(p, q, dims, preferred_element_type=F32)

    return d(ah, bh) + (d(ah, bl) + d(al, bh))


def _norm_mm3_kernel(x_ref, g_ref, wh_ref, wl_ref, o_ref, xh_ref, xl_ref):
    @pl.when(pl.program_id(1) == 0)
    def _():
        x = x_ref[...]
        ms = jnp.mean(x * x, axis=-1, keepdims=True)
        xh, xl = _split_bf16((x * lax.rsqrt(ms + RMS_EPS)) * g_ref[...])
        xh_ref[...] = xh
        xl_ref[...] = xl

    o_ref[...] = _dot3(xh_ref[...], xl_ref[...], wh_ref[...], wl_ref[...])


def _norm_mm3(x, g, wh, wl, *, tm, tn):
    rows, k = x.shape
    n = wh.shape[1]
    wspec = pl.BlockSpec((k, tn), lambda i, j: (0, j))
    return pl.pallas_call(
        _norm_mm3_kernel, grid=(rows // tm, n // tn),
        in_specs=[pl.BlockSpec((tm, k), lambda i, j: (i, 0)), pl.BlockSpec((1, k), lambda i, j: (0, 0)), wspec, wspec],
        out_specs=[pl.BlockSpec((tm, tn), lambda i, j: (i, j)), pl.BlockSpec((tm, k), lambda i, j: (i, 0))],
        out_shape=[jax.ShapeDtypeStruct((rows, n), F32), jax.ShapeDtypeStruct((rows, k), BF16)],
        scratch_shapes=[pltpu.VMEM((tm, k), BF16)],
        compiler_params=_params("arbitrary", "arbitrary"), name="norm_mm3",
    )(x, g.reshape(1, k), wh, wl)


def _rotate(x, cos, sin):
    lane = lax.broadcasted_iota(I32, x.shape, 1)
    first_half = (lane & (HEAD_DIM - 1)) < HEAD_DIM // 2
    partner = jnp.where(first_half, pltpu.roll(x, 128 - HEAD_DIM // 2, 1), pltpu.roll(x, HEAD_DIM // 2, 1))
    return x * cos + partner * sin


def _rope_tables(pos):
    half = HEAD_DIM // 2
    inv = ROPE_THETA ** (-jnp.arange(half, dtype=F32) / half)
    ang = pos.astype(F32)[:, None] * inv[None, :]
    cos = jnp.cos(ang)
    sin = jnp.sin(ang)
    cos_t = jnp.concatenate([cos, cos, cos, cos], axis=1)
    sin_t = jnp.concatenate([-sin, sin, -sin, sin], axis=1)
    return cos_t, sin_t


ROT_ROWS = 256


def _prompt_attn_kernel(q_ref, k_ref, v_ref, cos_ref, sin_ref, *rest, dil, seq, keep):
    o_ref, l_ref, kv_ref, qs, ks, vs = rest[-6:]
    chunks = ATT_OUT // 128
    for i in range(seq // ROT_ROWS):
        rs = slice(i * ROT_ROWS, (i + 1) * ROT_ROWS)
        cos = cos_ref[rs, :]
        sin = sin_ref[rs, :]
        for c in range(chunks):
            cs = slice(c * 128, (c + 1) * 128)
            qs[c, rs, :] = _rotate(q_ref[rs, cs], cos, sin)
            ks[c, rs, :] = _rotate(k_ref[rs, cs], cos, sin)
            vs[c, rs, :] = v_ref[rs, cs]
    for c in range(chunks):
        for j in range(keep // 128):
            ps = slice(seq - keep + j * 128, seq - keep + (j + 1) * 128)
            k_t = ks[c, ps, :].T
            v_t = vs[c, ps, :].T
            for half in range(2):
                hs = slice(half * HEAD_DIM, (half + 1) * HEAD_DIM)
                kv_ref[0, 0, 0, 2 * c + half, :, j * 128:(j + 1) * 128] = k_t[hs, :]
                kv_ref[0, 0, 1, 2 * c + half, :, j * 128:(j + 1) * 128] = v_t[hs, :]

    def rows_of(start):
        return pl.ds(start, ATT_SPAN, stride=dil) if dil > 1 else pl.ds(start, ATT_SPAN)

    qi = lax.broadcasted_iota(I32, (ATT_SPAN, ATT_SPAN), 0)
    ki = lax.broadcasted_iota(I32, (ATT_SPAN, ATT_SPAN), 1)
    cur_ok = ki <= qi
    prev_ok = ki >= qi
    first_head = ki < HEAD_DIM
    ones = jnp.ones((ATT_SPAN, 128), BF16)
    scale = HEAD_DIM ** -0.5
    n_blocks = seq // dil // ATT_SPAN
    for r in range(dil):
        for b in range(n_blocks):
            cur = rows_of(r + dil * b * ATT_SPAN)
            for c in range(chunks):
                q2, kc2, vc2 = qs[c, cur, :], ks[c, cur, :], vs[c, cur, :]
                if b > 0:
                    prev = rows_of(r + dil * (b - 1) * ATT_SPAN)
                    kp2, vp2 = ks[c, prev, :], vs[c, prev, :]
                outs, lses = [], []
                for half in range(2):
                    q = jnp.where(first_head == (half == 0), q2, 0.0)
                    sc = jnp.where(cur_ok, _dot_nt(q, kc2) * scale, NEG_INF)
                    m = jnp.max(sc, axis=-1, keepdims=True)
                    if b > 0:
                        sp = jnp.where(prev_ok, _dot_nt(q, kp2) * scale, NEG_INF)
                        m = jnp.maximum(m, jnp.max(sp, axis=-1, keepdims=True))
                    ec = jnp.exp(sc - m).astype(BF16)
                    den = jnp.dot(ec, ones, preferred_element_type=F32)
                    acc = _dot(ec, vc2)
                    if b > 0:
                        ep = jnp.exp(sp - m).astype(BF16)
                        den = den + jnp.dot(ep, ones, preferred_element_type=F32)
                        acc = acc + _dot(ep, vp2)
                    outs.append(acc / den)
                    lses.append(m + jnp.log(den))
                o_ref[c, cur, :] = jnp.where(first_head, outs[0], outs[1])
                l_ref[c, cur, :] = jnp.where(first_head, lses[0], lses[1])


def _prompt_attn(z_qkv, cos_t, sin_t, gi, dil, layer, kv_prev, *, depth, n_seq, seq):
    rows_p = n_seq * seq
    win = ATT_GROUPS[gi][0]
    keep = min(win, seq)
    assert seq % (dil * ATT_SPAN) == 0 and keep % 128 == 0 and seq % ROT_ROWS == 0
    tab = pl.BlockSpec((seq, 128), lambda n: (0, 0))
    chunked = pl.BlockSpec((ATT_OUT // 128, seq, 128), lambda n: (0, n, 0))
    kv_shape = (depth, n_seq, 2, HEADS_PER_GROUP, HEAD_DIM, keep)
    kv_spec = pl.BlockSpec((1, 1) + kv_shape[2:], lambda n: (layer, n, 0, 0, 0, 0))
    c0 = Z_QKV // ATT_OUT
    in_specs = [pl.BlockSpec((seq, ATT_OUT), lambda n: (n, c0 + gi)),
                pl.BlockSpec((seq, ATT_OUT), lambda n: (n, c0 + ATT_W // ATT_OUT + gi)),
                pl.BlockSpec((seq, ATT_OUT), lambda n: (n, c0 + 2 * ATT_W // ATT_OUT + gi)), tab, tab]
    args = [z_qkv, z_qkv, z_qkv, cos_t, sin_t]
    aliases = {}
    if kv_prev is not None:
        in_specs.append(pl.BlockSpec(memory_space=pl.ANY))
        args.append(kv_prev)
        aliases = {len(args) - 1: 2}
    chunk_shape = jax.ShapeDtypeStruct((ATT_OUT // 128, rows_p, 128), F32)
    return pl.pallas_call(
        functools.partial(_prompt_attn_kernel, dil=dil, seq=seq, keep=keep),
        grid=(n_seq,), in_specs=in_specs,
        out_specs=[chunked, chunked, kv_spec],
        out_shape=[chunk_shape, chunk_shape, jax.ShapeDtypeStruct(kv_shape, F32)],
        scratch_shapes=[pltpu.VMEM((ATT_OUT // 128, seq, 128), F32)] * 3,
        input_output_aliases=aliases,
        compiler_params=_params("arbitrary"), name=f"prompt_attn_g{gi}",
    )(*args)


def _merge_groups(os_, ls_):
    m = jnp.maximum(jnp.maximum(ls_[0], ls_[1]), ls_[2])
    es = [jnp.exp(l - m) for l in ls_]
    tot = es[0] + es[1] + es[2]
    return (es[0] / tot) * os_[0] + (es[1] / tot) * os_[1] + (es[2] / tot) * os_[2]


def _merge_kernel(o0, o1, o2, l0, l1, l2, a_ref):
    for c in range(ATT_OUT // 128):
        a_ref[:, c * 128:(c + 1) * 128] = _merge_groups([o0[c], o1[c], o2[c]], [l0[c], l1[c], l2[c]])


def _merge_attn(os_, ls_, *, tm):
    rows = os_[0].shape[1]
    spec = pl.BlockSpec((ATT_OUT // 128, tm, 128), lambda i: (0, i, 0))
    return pl.pallas_call(
        _merge_kernel, grid=(rows // tm,), in_specs=[spec] * 6, out_specs=pl.BlockSpec((tm, ATT_OUT), lambda i: (i, 0)),
        out_shape=jax.ShapeDtypeStruct((rows, ATT_OUT), F32),
        compiler_params=_params("arbitrary"), name="merge_attn",
    )(*os_, *ls_)


def _sample_attn_kernel(z_ref, cos_ref, sin_ref, *rest, t_new):
    caches, (a_ref, n0_ref, n1_ref, n2_ref) = rest[0:3], rest[-4:]
    scale = HEAD_DIM ** -0.5
    cos = cos_ref[...]
    sin = sin_ref[...]
    n_chunks = ATT_W // 128
    q_c = [_rotate(z_ref[:, c * 128:(c + 1) * 128], cos, sin) for c in range(n_chunks)]
    k_c = [_rotate(z_ref[:, ATT_W + c * 128:ATT_W + (c + 1) * 128], cos, sin) for c in range(n_chunks)]
    v_c = [z_ref[:, 2 * ATT_W + c * 128:2 * ATT_W + (c + 1) * 128] for c in range(n_chunks)]
    outs_g, lses_g = [], []
    for gi, (c_ref, n_ref, (win, dil)) in enumerate(zip(caches, (n0_ref, n1_ref, n2_ref), ATT_GROUPS)):
        cache_len = c_ref.shape[-1]
        t_c = lax.broadcasted_iota(I32, (t_new, cache_len), 0)
        c_c = lax.broadcasted_iota(I32, (t_new, cache_len), 1)
        d_c = cache_len + t_c - c_c
        ok_c = jnp.logical_and((d_c & (dil - 1)) == 0, d_c <= ATT_SPAN * dil)
        t_n = lax.broadcasted_iota(I32, (t_new, t_new), 0)
        u_n = lax.broadcasted_iota(I32, (t_new, t_new), 1)
        d_n = t_n - u_n
        ok_n = jnp.logical_and(d_n >= 0, (d_n & (dil - 1)) == 0)
        outs, lses = [], []
        for h in range(HEADS_PER_GROUP):
            chunk, half = divmod(gi * HEADS_PER_GROUP + h, 2)
            hs = slice(half * HEAD_DIM, (half + 1) * HEAD_DIM)
            q, k_new, v_new = q_c[chunk][:, hs], k_c[chunk][:, hs], v_c[chunk][:, hs]
            k_t = c_ref[0, 0, 0, h]
            v_t = c_ref[0, 0, 1, h]
            s_c = jnp.where(ok_c, _dot(q, k_t) * scale, NEG_INF)
            s_n = jnp.where(ok_n, _dot_nt(q, k_new) * scale, NEG_INF)
            m = jnp.maximum(jnp.max(s_c, axis=-1, keepdims=True), jnp.max(s_n, axis=-1, keepdims=True))
            e_c = jnp.exp(s_c - m)
            e_n = jnp.exp(s_n - m)
            den = jnp.sum(e_c, axis=-1, keepdims=True) + jnp.sum(e_n, axis=-1, keepdims=True)
            acc = _dot_nt(e_c, v_t) + _dot(e_n, v_new)
            outs.append(acc / den)
            lses.append(jnp.broadcast_to(m + jnp.log(den), (t_new, HEAD_DIM)))
            n_ref[0, 0, 0, h] = jnp.concatenate([k_t[:, t_new:], k_new.T], axis=1)
            n_ref[0, 0, 1, h] = jnp.concatenate([v_t[:, t_new:], v_new.T], axis=1)
        outs_g.append(jnp.concatenate(outs, axis=1))
        lses_g.append(jnp.concatenate(lses, axis=1))
    a_ref[...] = _merge_groups(outs_g, lses_g)


def _sample_attn(z_qkv, cos_s, sin_s, caches_t, layer, new_prev, *, row0, n_seq, t_new):
    b0 = row0 // t_new

    def cache_spec(c):
        return pl.BlockSpec((1, 1) + c.shape[2:], lambda n: (layer, n, 0, 0, 0, 0))

    tab = pl.BlockSpec((t_new, 128), lambda n: (0, 0))
    in_specs = ([pl.BlockSpec((t_new, 3 * ATT_W), lambda n: (b0 + n, Z_QKV // (3 * ATT_W))), tab, tab]
                + [cache_spec(c) for c in caches_t])
    args = [z_qkv, cos_s, sin_s, *caches_t]
    aliases = {}
    if new_prev is not None:
        for j, a in enumerate(new_prev):
            in_specs.append(pl.BlockSpec(memory_space=pl.ANY))
            args.append(a)
            aliases[len(args) - 1] = 1 + j
    return pl.pallas_call(
        functools.partial(_sample_attn_kernel, t_new=t_new),
        grid=(n_seq,), in_specs=in_specs,
        out_specs=[pl.BlockSpec((t_new, ATT_OUT), lambda n: (n, 0))] + [cache_spec(c) for c in caches_t],
        out_shape=[jax.ShapeDtypeStruct((n_seq * t_new, ATT_OUT), F32)]
        + [jax.ShapeDtypeStruct(c.shape, F32) for c in caches_t],
        input_output_aliases=aliases,
        compiler_params=_params("arbitrary"), name="sample_attn",
    )(*args)


def _pool_kernel(z_ref, h_ref, w_ref, s_ref, o_ref, buf_ref, *, t_len, pos0):
    buf_ref[0:POOL_HIST, :] = h_ref[0]
    buf_ref[POOL_HIST:POOL_HIST + t_len, :] = z_ref[...]
    pos = pos0 + lax.broadcasted_iota(I32, (t_len, POOL_GROUP), 0)
    for g, win in enumerate(POOL_WINDOWS):
        cs = slice(g * POOL_GROUP, (g + 1) * POOL_GROUP)
        z = buf_ref[POOL_HIST:POOL_HIST + t_len, cs]
        wsum = z
        for i in range(1, win):
            wsum = wsum + buf_ref[POOL_HIST - i:POOL_HIST - i + t_len, cs]
        cnt = jnp.minimum(win, pos + 1).astype(F32)
        y = wsum / cnt - z
        o_ref[:, cs] = _dot(y, w_ref[g]) * s_ref[:, cs]


def _pool_mixer(z_pool, hist, pool_w, pool_scale, *, row0, n_seq, t_len, pos0):
    b0 = row0 // t_len
    return pl.pallas_call(
        functools.partial(_pool_kernel, t_len=t_len, pos0=pos0),
        grid=(n_seq,),
        in_specs=[
            pl.BlockSpec((t_len, POOL_W), lambda n: (b0 + n, Z_POOL // POOL_W)),
            pl.BlockSpec((1, POOL_HIST, POOL_W), lambda n: (n, 0, 0)),
            pl.BlockSpec((len(POOL_WINDOWS), POOL_GROUP, POOL_GROUP), lambda n: (0, 0, 0)),
            pl.BlockSpec((1, POOL_W), lambda n: (0, 0)),
        ],
        out_specs=pl.BlockSpec((t_len, POOL_W), lambda n: (n, 0)),
        out_shape=jax.ShapeDtypeStruct((n_seq * t_len, POOL_W), F32),
        scratch_shapes=[pltpu.VMEM((POOL_HIST + t_len, POOL_W), F32)],
        compiler_params=_params("arbitrary"), name="pool_mixer",
    )(z_pool, hist, pool_w, pool_scale.reshape(1, POOL_W))


def _rwkv_prep_kernel(z_ref, zp_ref, first_ref, mu_ref, w0_ref, w2_ref, a0_ref, a2_ref, g2_ref, kk_ref, ka_ref, rk_ref,
                      r_o, lw_o, k_o, v_o, kn_o, b_o, g_o, bonus_o, buf_ref, *, tm):
    i = pl.program_id(1)
    z = z_ref[:, 0:RWKV_COLS]
    prev_row = jnp.where(i == 0, first_ref[0], zp_ref[SUBLANES - 1:SUBLANES, 0:RWKV_COLS])
    buf_ref[SUBLANES:SUBLANES + tm, :] = z
    buf_ref[SUBLANES - 1:SUBLANES, :] = prev_row
    shifted = buf_ref[SUBLANES - 1:SUBLANES - 1 + tm, :]
    xm = z + mu_ref[...] * (shifted - z)
    r = xm[:, 0:RWKV_W]
    k = xm[:, RWKV_W:2 * RWKV_W]
    v = xm[:, 2 * RWKV_W:3 * RWKV_W]
    wa = xm[:, RWKV_LORA_OFF:RWKV_LORA_OFF + 128]
    gl = xm[:, RWKV_LORA_OFF + 128:RWKV_COLS]
    xw = w0_ref[...] + _dot(jnp.tanh(wa), w2_ref[...])
    logw = -math.exp(-0.5) * jax.nn.sigmoid(xw)
    a = jax.nn.sigmoid(a0_ref[...] + _dot(wa, a2_ref[...]))
    g_o[...] = _dot(jax.nn.sigmoid(gl), g2_ref[...])
    kkr = k * kk_ref[...]
    kmod = k * (1.0 + (a - 1.0) * ka_ref[...])
    rkk = r * kmod * rk_ref[...]
    bonus = []
    for h in range(RWKV_HEADS):
        sl = slice(h * RWKV_HEAD, (h + 1) * RWKV_HEAD)
        kh = kkr[:, sl]
        nrm = jnp.sqrt(jnp.sum(kh * kh, axis=-1, keepdims=True))
        kn = kh / jnp.maximum(nrm, 1e-12)
        r_o[0, h] = r[:, sl]
        lw_o[0, h] = logw[:, sl]
        k_o[0, h] = kmod[:, sl]
        v_o[0, h] = v[:, sl]
        kn_o[0, h] = kn
        b_o[0, h] = kn * a[:, sl]
        bonus.append(jnp.sum(rkk[:, sl], axis=-1, keepdims=True) * v[:, sl])
    bonus_o[...] = jnp.concatenate(bonus, axis=1)


def _rwkv_prep(z, first_prev, lw, *, row0, n_seq, t_len, tm):
    nblk = t_len // tm
    pb = tm // SUBLANES
    b0 = row0 // tm
    p0 = row0 // SUBLANES

    def vec(n):
        return pl.BlockSpec((1, n), lambda s, i: (0, 0))

    hm = jax.ShapeDtypeStruct((n_seq, RWKV_HEADS, t_len, RWKV_HEAD), F32)
    rm = jax.ShapeDtypeStruct((n_seq * t_len, RWKV_W), F32)
    hm_spec = pl.BlockSpec((1, RWKV_HEADS, tm, RWKV_HEAD), lambda s, i: (s, 0, i, 0))
    rm_spec = pl.BlockSpec((tm, RWKV_W), lambda s, i: (s * nblk + i, 0))
    return pl.pallas_call(
        functools.partial(_rwkv_prep_kernel, tm=tm),
        grid=(n_seq, nblk),
        in_specs=[
            pl.BlockSpec((tm, Z_RWKV_BLOCK), lambda s, i: (b0 + s * nblk + i, Z_RWKV // Z_RWKV_BLOCK)),
            pl.BlockSpec((SUBLANES, Z_RWKV_BLOCK),
                         lambda s, i: (jnp.maximum(p0 + (s * nblk + i) * pb - 1, 0), Z_RWKV // Z_RWKV_BLOCK)),
            pl.BlockSpec((1, 1, RWKV_COLS), lambda s, i: (s, 0, 0)),
            vec(RWKV_COLS), vec(RWKV_W),
            pl.BlockSpec((128, RWKV_W), lambda s, i: (0, 0)),
            vec(RWKV_W),
            pl.BlockSpec((128, RWKV_W), lambda s, i: (0, 0)),
            pl.BlockSpec((128, RWKV_W), lambda s, i: (0, 0)),
            vec(RWKV_W), vec(RWKV_W), vec(RWKV_W),
        ],
        out_specs=[hm_spec] * 6 + [rm_spec] * 2,
        out_shape=[hm] * 6 + [rm] * 2,
        scratch_shapes=[pltpu.VMEM((tm + SUBLANES, RWKV_COLS), F32)],
        compiler_params=_params("arbitrary", "arbitrary"), name="rwkv_prep",
    )(z, z, first_prev, lw["mu"], lw["w0"], lw["w2p"], lw["a0"], lw["a2p"], lw["g2"], lw["kk"], lw["ka"], lw["rk"])


def _bmm(a, b):
    return jnp.einsum("hqk,hkd->hqd", a.astype(BF16), b.astype(BF16), preferred_element_type=F32)


def _bmm_nt(a, b):
    return jnp.einsum("hqd,hkd->hqk", a.astype(BF16), b.astype(BF16), preferred_element_type=F32)


def _bmm_tn(a, b):
    return jnp.einsum("hkq,hkd->hqd", a.astype(BF16), b.astype(BF16), preferred_element_type=F32)


def _rwkv_scan_kernel(r_ref, lw_ref, k_ref, v_ref, kn_ref, b_ref, s0_ref, y_ref, st_ref, s_scr, *, chunk, group):
    c = pl.program_id(1)
    nh = group * RWKV_HEADS

    def heads(ref):
        return ref[...].reshape((nh,) + ref.shape[2:])

    @pl.when(c == 0)
    def _():
        s_scr[...] = heads(s0_ref)

    r, logw, k, v, kn, b = [heads(ref) for ref in (r_ref, lw_ref, k_ref, v_ref, kn_ref, b_ref)]
    row = lax.broadcasted_iota(I32, (chunk, chunk), 0)
    col = lax.broadcasted_iota(I32, (chunk, chunk), 1)
    incl = row >= col
    strict = row > col
    tri = jnp.broadcast_to(incl.astype(BF16)[None], (nh, chunk, chunk))
    lw_hi = logw.astype(BF16)
    lw_lo = (logw - lw_hi.astype(F32)).astype(BF16)
    cum = (jnp.einsum("hqk,hkd->hqd", tri, lw_hi, preferred_element_type=F32)
           + jnp.einsum("hqk,hkd->hqd", tri, lw_lo, preferred_element_type=F32))
    p_inv = jnp.exp(-cum)
    kt = k * p_inv
    bt = b * p_inv
    kap = kn * jnp.exp(cum - logw)
    rho = r * jnp.exp(cum)
    qq = jnp.concatenate([kap, rho], axis=1)
    gram = _bmm_nt(qq, jnp.concatenate([kt, bt], axis=1))
    a_k = jnp.where(strict[None], gram[:, :chunk, :chunk], 0.0)
    a_b = jnp.where(strict[None], gram[:, :chunk, chunk:], 0.0)
    l_k = jnp.where(incl[None], gram[:, chunk:, :chunk], 0.0)
    l_b = jnp.where(incl[None], gram[:, chunk:, chunk:], 0.0)
    x = jnp.broadcast_to((row == col).astype(F32)[None], (nh, chunk, chunk))
    m = 1
    while m < chunk:
        sh = m.bit_length() - 1
        same = (row >> (sh + 1)) == (col >> (sh + 1))
        lower_left = jnp.logical_and(((row >> sh) & 1) == 1, ((col >> sh) & 1) == 0)
        off = jnp.where(jnp.logical_and(same, lower_left)[None], a_b, 0.0)
        x = x - _bmm(_bmm(x, off), x)
        m *= 2
    s = s_scr[...]
    qs = _bmm_nt(qq, s)
    u = _bmm(x, -(qs[:, :chunk] + _bmm(a_k, v)))
    y = qs[:, chunk:] + _bmm(jnp.concatenate([l_k, l_b], axis=2), jnp.concatenate([v, u], axis=1))
    s_new = (s + _bmm_tn(jnp.concatenate([v, u], axis=1), jnp.concatenate([kt, bt], axis=1))) * jnp.exp(cum[:, chunk - 1:chunk, :])
    s_scr[...] = s_new
    st_ref[...] = s_new.reshape(st_ref.shape)
    mean = jnp.mean(y, axis=-1, keepdims=True)
    var = jnp.mean(jnp.square(y - mean), axis=-1, keepdims=True)
    yn = (y - mean) * lax.rsqrt(var + GN_EPS)
    for g in range(group):
        y_ref[g] = jnp.concatenate([yn[g * RWKV_HEADS + h] for h in range(RWKV_HEADS)], axis=1)


def _rwkv_scan(prep, s0, *, chunk, group):
    r, lw, k, v, kn, b = prep
    n_seq, nh, t_len, hd = r.shape
    hm_spec = pl.BlockSpec((group, nh, chunk, hd), lambda s, c: (s, 0, c, 0))
    st_spec = pl.BlockSpec((group, nh, hd, hd), lambda s, c: (s, 0, 0, 0))
    return pl.pallas_call(
        functools.partial(_rwkv_scan_kernel, chunk=chunk, group=group),
        grid=(n_seq // group, t_len // chunk),
        in_specs=[hm_spec] * 6 + [st_spec],
        out_specs=[pl.BlockSpec((group, chunk, RWKV_W), lambda s, c: (s, c, 0)), st_spec],
        out_shape=[jax.ShapeDtypeStruct((n_seq, t_len, RWKV_W), F32), jax.ShapeDtypeStruct((n_seq, nh, hd, hd), F32)],
        scratch_shapes=[pltpu.VMEM((group * nh, hd, hd), F32)],
        compiler_params=_params("arbitrary", "arbitrary"), name="rwkv_scan",
    )(r, lw, k, v, kn, b, s0)


def _mix_kernel(*refs, n_prompt_blocks):
    pairs, (gate_ref, lnw_ref, lnb_ref, wa_ref, wp_ref, wr_ref, o_ref) = refs[:10], refs[10:]
    is_sample = pl.program_id(0) >= n_prompt_blocks
    att, pool, yn, bonus, g = [jnp.where(is_sample, pairs[2 * j + 1][...], pairs[2 * j][...]) for j in range(5)]
    rw = (yn * lnw_ref[...] + lnb_ref[...] + bonus) * g
    mix = jax.nn.sigmoid(gate_ref[:, 0:D_MODEL]) * _dot(att, wa_ref[...])
    mix = mix + jax.nn.sigmoid(gate_ref[:, D_MODEL:2 * D_MODEL]) * _dot(pool, wp_ref[...])
    mix = mix + jax.nn.sigmoid(gate_ref[:, 2 * D_MODEL:3 * D_MODEL]) * _dot(rw, wr_ref[...])
    o_ref[...] = mix.astype(BF16)


def _mix(branches, z_gate, ln_w, ln_b, wa, wp, wr, *, tm):
    rows = z_gate.shape[0]
    n_prompt_blocks = branches[0][0].shape[0] // tm
    assert rows == (n_prompt_blocks + 1) * tm and all(s.shape[0] == tm for _, s in branches)

    def full(a):
        return pl.BlockSpec(a.shape, lambda i: (0, 0))

    specs, args = [], []
    for p, s in branches:
        specs += [pl.BlockSpec((tm, p.shape[1]), lambda i: (jnp.minimum(i, n_prompt_blocks - 1), 0)), full(s)]
        args += [p, s]
    ln_w = ln_w.reshape(1, RWKV_W)
    ln_b = ln_b.reshape(1, RWKV_W)
    return pl.pallas_call(
        functools.partial(_mix_kernel, n_prompt_blocks=n_prompt_blocks), grid=(rows // tm,),
        in_specs=specs + [pl.BlockSpec((tm, 3 * D_MODEL), lambda i: (i, 0)),
                          full(ln_w), full(ln_b), full(wa), full(wp), full(wr)],
        out_specs=pl.BlockSpec((tm, D_MODEL), lambda i: (i, 0)),
        out_shape=jax.ShapeDtypeStruct((rows, D_MODEL), BF16),
        compiler_params=_params("arbitrary"), name="gated_mix",
    )(*args, z_gate, ln_w, ln_b, wa, wp, wr)


def _mm_res_kernel(x_ref, a_ref, w_ref, o_ref):
    o_ref[...] = x_ref[...] + jnp.dot(a_ref[...], w_ref[...], preferred_element_type=F32)


def _mm_residual(x, a, w, *, tm):
    rows, n = x.shape
    k = a.shape[1]
    return pl.pallas_call(
        _mm_res_kernel, grid=(rows // tm,),
        in_specs=[pl.BlockSpec((tm, n), lambda i: (i, 0)), pl.BlockSpec((tm, k), lambda i: (i, 0)),
                  pl.BlockSpec((k, n), lambda i: (0, 0))],
        out_specs=pl.BlockSpec((tm, n), lambda i: (i, 0)),
        out_shape=jax.ShapeDtypeStruct((rows, n), F32),
        compiler_params=_params("arbitrary"), name="out_proj",
    )(x, a, w)


def _topk_cols(scores, ids, ids_ordered, k):
    lanes = scores[0].shape[1]
    iota_k = lax.broadcasted_iota(I32, (k, lanes), 0)
    big = jnp.iinfo(jnp.int32).max

    def arg_max(s, ident, ordered):
        if ordered:
            tiles = [(s[i:i + SUBLANES], ident[i:i + SUBLANES]) for i in range(0, s.shape[0], SUBLANES)]
            while len(tiles) > 1:
                merged = []
                for (va, ia), (vb, ib) in zip(tiles[0::2], tiles[1::2]):
                    later = vb > va
                    merged.append((jnp.where(later, vb, va), jnp.where(later, ib, ia)))
                tiles = merged + tiles[len(tiles) - len(tiles) % 2:]
            s, ident = tiles[0]
        m = jnp.max(s, axis=0, keepdims=True)
        return m, jnp.min(jnp.where(s == m, ident, big), axis=0, keepdims=True)

    def body(j, carry):
        out = []
        for (s, vals, idxs), ident, ordered in zip(carry, ids, ids_ordered):
            m, idx = arg_max(s, ident, ordered)
            vals = jnp.where(iota_k == j, m, vals)
            idxs = jnp.where(iota_k == j, idx, idxs)
            out.append((jnp.where(ident == idx, NEG_INF, s), vals, idxs))
        return tuple(out)

    init = tuple((s, jnp.zeros((k, lanes), F32), jnp.zeros((k, lanes), I32)) for s in scores)
    return [(v, i) for _, v, i in lax.fori_loop(0, k, body, init)]


def _gather_rows(table, sel, k):
    out = jnp.zeros(sel.shape, table.dtype)
    for a in range(k):
        out = jnp.where(sel == a, table[a:a + 1, :], out)
    return out


def _peer_select_kernel(q_ref, skh_ref, skl_ref, i1_o, i2_o, gate_o, i1_s, i2_s, g_s, *, tok):
    kk = PEER_TOPK
    half = kk // 2
    nt = (((1,), (1,)), ((), ()))
    key_id = lax.broadcasted_iota(I32, (N_KEYS, tok), 0)
    row = lax.broadcasted_iota(I32, (half * half + kk, tok), 0)
    cand_id = jnp.where(row < half * half, (row >> (half.bit_length() - 1)) * kk + (row & (half - 1)),
                        jnp.where(row < half * half + half, row - half * half + half, (row - half * half) * kk))

    def finish(slot, top, sel, k1, k2):
        e1 = _gather_rows(k1, sel >> 4, kk)
        e2 = _gather_rows(k2, sel & (kk - 1), kk)
        ex = jnp.exp(top - jnp.max(top, axis=0, keepdims=True))
        gate = ex / jnp.sum(ex, axis=0, keepdims=True)
        r0 = pl.multiple_of(slot * kk, kk)
        i1_s[pl.ds(r0, kk), :] = e1.astype(F32)
        i2_s[pl.ds(r0, kk), :] = e2.astype(F32)
        g_s[pl.ds(r0, kk), :] = gate

    def head(h, carry):
        cand_prev, k1_prev, k2_prev = carry
        c0 = pl.multiple_of(h * 2 * N_KEYS, 2 * N_KEYS)
        q1h, q1l = _split_bf16(q_ref[:, pl.ds(c0, N_KEYS)])
        q2h, q2l = _split_bf16(q_ref[:, pl.ds(c0 + N_KEYS, N_KEYS)])
        s1 = _dot3(skh_ref[h, 0], skl_ref[h, 0], q1h, q1l, nt)
        s2 = _dot3(skh_ref[h, 1], skl_ref[h, 1], q2h, q2l, nt)
        (t1, k1), (t2, k2), (top, sel) = _topk_cols([s1, s2, cand_prev], [key_id, key_id, cand_id],
                                                    [True, True, False], kk)
        finish(jnp.where(h == 0, PEER_HEADS, h - 1), top, sel, k1_prev, k2_prev)
        cand = jnp.concatenate([t1[a:a + 1, :] + t2[0:half, :] for a in range(half)]
                               + [t1[0:1, :] + t2[half:kk, :], t1[half:kk, :] + t2[0:1, :]], axis=0)
        return cand, k1, k2

    zeros_i = jnp.zeros((kk, tok), I32)
    cand, k1, k2 = lax.fori_loop(0, PEER_HEADS, head, (jnp.zeros(cand_id.shape, F32), zeros_i, zeros_i))
    ((top, sel),) = _topk_cols([cand], [cand_id], [False], kk)
    finish(PEER_HEADS - 1, top, sel, k1, k2)
    i1_o[...] = i1_s[0:PEER_PAIRS, :].T.astype(I32)
    i2_o[...] = i2_s[0:PEER_PAIRS, :].T.astype(I32)
    gate_o[...] = g_s[0:PEER_PAIRS, :].T


def _peer_select(q, sk_hi, sk_lo, *, tok=128):
    rows = q.shape[0]
    spec = pl.BlockSpec((tok, PEER_PAIRS), lambda i: (i, 0))
    return pl.pallas_call(
        functools.partial(_peer_select_kernel, tok=tok), grid=(rows // tok,),
        in_specs=[pl.BlockSpec((tok, q.shape[1]), lambda i: (i, 0)),
                  pl.BlockSpec(sk_hi.shape, lambda i: (0, 0, 0, 0)), pl.BlockSpec(sk_lo.shape, lambda i: (0, 0, 0, 0))],
        out_specs=[spec] * 3,
        out_shape=[jax.ShapeDtypeStruct((rows, PEER_PAIRS), I32), jax.ShapeDtypeStruct((rows, PEER_PAIRS), I32),
                   jax.ShapeDtypeStruct((rows, PEER_PAIRS), F32)],
        scratch_shapes=[pltpu.VMEM((PEER_PAIRS + PEER_TOPK, tok), F32)] * 3,
        compiler_params=_params("arbitrary"), name="peer_select",
    )(q, sk_hi, sk_lo)


def _peer_pick(d_ref, i1_ref, i2_ref, act_ref, first_row):
    n_rows = d_ref.shape[1] // N_KEYS
    for g in range(act_ref.shape[0] // SUBLANES):
        rs = slice(g * SUBLANES, (g + 1) * SUBLANES)
        i1 = i1_ref[rs, :]
        i2 = i2_ref[rs, :]
        acc = act_ref[rs, :]
        for j in range(n_rows):
            got = jnp.take_along_axis(d_ref[rs, j * N_KEYS:(j + 1) * N_KEYS], i2, axis=1, mode="promise_in_bounds")
            acc = acc + jnp.where(i1 == first_row + j, got, 0.0)
        act_ref[rs, :] = acc


def _peer_act_kernel(xn_ref, u_ref, i1_ref, i2_ref, act_ref, da_ref, db_ref):
    first = jnp.logical_and(pl.program_id(0) == 0, pl.program_id(1) == 0)
    c = pl.program_id(1)
    slab = PEER_EC // 2
    slab_rows = slab // N_KEYS
    nt = (((1,), (1,)), ((), ()))

    @pl.when(first)
    def _():
        db_ref[...] = jnp.zeros(db_ref.shape, F32)

    @pl.when(c == 0)
    def _():
        act_ref[...] = jnp.zeros(act_ref.shape, F32)

    base = c * 2 * slab_rows
    _peer_pick(db_ref, i1_ref, i2_ref, act_ref, base - slab_rows)
    da_ref[...] = lax.dot_general(xn_ref[...], u_ref[0, 0:slab, :].astype(BF16), nt, preferred_element_type=F32)
    _peer_pick(da_ref, i1_ref, i2_ref, act_ref, base)
    db_ref[...] = lax.dot_general(xn_ref[...], u_ref[0, slab:2 * slab, :].astype(BF16), nt, preferred_element_type=F32)

    @pl.when(c == pl.num_programs(1) - 1)
    def _():
        _peer_pick(db_ref, i1_ref, i2_ref, act_ref, base + slab_rows)


def _peer_act(xn, u_tabs, layer, i1, i2, *, tb):
    rows = xn.shape[0]
    pair_spec = pl.BlockSpec((tb, PEER_PAIRS), lambda i, c: (i, 0))
    return pl.pallas_call(
        _peer_act_kernel, grid=(rows // tb, N_EXPERTS // PEER_EC),
        in_specs=[pl.BlockSpec((tb, D_MODEL), lambda i, c: (i, 0)),
                  pl.BlockSpec((1, PEER_EC, D_MODEL), lambda i, c: (layer, c, 0)), pair_spec, pair_spec],
        out_specs=pair_spec,
        out_shape=jax.ShapeDtypeStruct((rows, PEER_PAIRS), F32),
        scratch_shapes=[pltpu.VMEM((tb, PEER_EC // 2), F32)] * 2,
        compiler_params=_params("arbitrary", "arbitrary"), name="peer_act",
    )(xn, u_tabs, i1, i2)


def _peer_out_kernel(i1_ref, i2_ref, gate_ref, act_ref, v_ref, o_ref, w_ref):
    c = pl.program_id(1)
    n_groups = w_ref.shape[0]
    rows_per_chunk = PEER_OUT_EC // N_KEYS

    @pl.when(c == 0)
    def _():
        key_iota = lax.broadcasted_iota(I32, (N_KEYS, PEER_PAIRS), 0)

        def group(g, carry):
            r0 = pl.multiple_of(g * SUBLANES, SUBLANES)
            i1 = i1_ref[pl.ds(r0, SUBLANES), :]
            i2 = i2_ref[pl.ds(r0, SUBLANES), :]
            a = act_ref[pl.ds(r0, SUBLANES), :]
            wgt = gate_ref[pl.ds(r0, SUBLANES), :] * (0.5 * a * (1.0 + lax.erf(a * (1.0 / math.sqrt(2.0)))))
            for t in range(SUBLANES):
                hit1 = key_iota == jnp.broadcast_to(i1[t:t + 1, :], key_iota.shape)
                hit2 = key_iota == jnp.broadcast_to(i2[t:t + 1, :], key_iota.shape)
                w2 = jnp.where(hit2, jnp.broadcast_to(wgt[t:t + 1, :], key_iota.shape), 0.0)
                w_tok = _dot_nt(jnp.where(hit1, 1.0, 0.0), w2)
                w_ref[g, pl.ds(t, N_KEYS, stride=SUBLANES), :] = w_tok
            return carry

        lax.fori_loop(0, n_groups, group, 0, unroll=PEER_BUILD_UNROLL)

    row0 = c * (rows_per_chunk * SUBLANES)
    lhs = jnp.concatenate(
        [w_ref[:, pl.ds(pl.multiple_of(row0 + j * SUBLANES, SUBLANES), SUBLANES), :].reshape(n_groups * SUBLANES, N_KEYS)
         for j in range(rows_per_chunk)], axis=1)
    part = jnp.dot(lhs.astype(BF16), v_ref[0], preferred_element_type=F32)

    @pl.when(c == 0)
    def _():
        o_ref[...] = part

    @pl.when(c > 0)
    def _():
        o_ref[...] += part


def _peer_out(i1, i2, gate, act, v_tabs, layer):
    rows = i1.shape[0]
    tb = PEER_TB
    pair_spec = pl.BlockSpec((tb, PEER_PAIRS), lambda i, c: (i, 0))
    return pl.pallas_call(
        _peer_out_kernel, grid=(rows // tb, N_EXPERTS // PEER_OUT_EC),
        in_specs=[pair_spec] * 4 + [pl.BlockSpec((1, PEER_OUT_EC, D_MODEL), lambda i, c: (layer, c, 0))],
        out_specs=pl.BlockSpec((tb, D_MODEL), lambda i, c: (i, 0)),
        out_shape=jax.ShapeDtypeStruct((rows, D_MODEL), F32),
        scratch_shapes=[pltpu.VMEM((tb // SUBLANES, N_KEYS * SUBLANES, N_KEYS), F32)],
        compiler_params=_params("arbitrary", "arbitrary"), name="peer_out",
    )(i1, i2, gate, act, v_tabs)


def _ple_kernel(x_ref, f_ref, p_ref, g_ref, wg_ref, wp_ref, gf_ref, *o_refs, n_prompt_blocks):
    x = x_ref[...] + f_ref[...]
    ms = jnp.mean(x * x, axis=-1, keepdims=True)
    h = (x * lax.rsqrt(ms + RMS_EPS)) * g_ref[...]
    out = x + jax.nn.sigmoid(_dot(h, wg_ref[...])) * _dot(p_ref[...], wp_ref[...])
    if n_prompt_blocks is None:
        o_refs[0][...] = out
    else:
        ms2 = jnp.mean(out * out, axis=-1, keepdims=True)
        out = (out * lax.rsqrt(ms2 + RMS_EPS)) * gf_ref[...]
        is_sample = pl.program_id(0) >= n_prompt_blocks

        @pl.when(jnp.logical_not(is_sample))
        def _():
            o_refs[0][...] = out

        @pl.when(is_sample)
        def _():
            o_refs[1][...] = out


def _ple(x, ffn, p, g, wg, wp, g_final, *, tm, final, rows_prompt):
    rows, n = x.shape
    row = pl.BlockSpec((tm, n), lambda i: (i, 0))
    vec = pl.BlockSpec((1, n), lambda i: (0, 0))
    if final:
        n_prompt_blocks = rows_prompt // tm
        assert rows == (n_prompt_blocks + 1) * tm
        out_specs = [pl.BlockSpec((tm, n), lambda i: (jnp.minimum(i, n_prompt_blocks - 1), 0)),
                     pl.BlockSpec((tm, n), lambda i: (0, 0))]
        out_shape = [jax.ShapeDtypeStruct((rows_prompt, n), F32), jax.ShapeDtypeStruct((tm, n), F32)]
    else:
        n_prompt_blocks = None
        out_specs = row
        out_shape = jax.ShapeDtypeStruct((rows, n), F32)
    return pl.pallas_call(
        functools.partial(_ple_kernel, n_prompt_blocks=n_prompt_blocks), grid=(rows // tm,),
        in_specs=[row, row, pl.BlockSpec((tm, p.shape[1]), lambda i: (i, 0)), vec,
                  pl.BlockSpec(wg.shape, lambda i: (0, 0)), pl.BlockSpec(wp.shape, lambda i: (0, 0)), vec],
        out_specs=out_specs, out_shape=out_shape,
        compiler_params=_params("arbitrary"), name="ple",
    )(x, ffn, p, g.reshape(1, n), wg, wp, g_final.reshape(1, n))


def _layer(x, p_rows, lw, state, g_final, dims, final):
    n_p, t_p, n_s, t_s, past = dims
    rows_p = n_p * t_p
    rows_s = n_s * t_s

    z = _norm_mm(x, lw["g_mix"], lw["w_in"], lw["layer"], tm=768, tn=1920)
    z_qkv = z_pool = z_rwkv = z_gate = z

    layer, depth = lw["layer"], lw["depth"]
    os_, ls_, kv_p = [], [], []
    for gi, (_, dil) in enumerate(ATT_GROUPS):
        o, l, kv = _prompt_attn(z_qkv, lw["cos_p"], lw["sin_p"], gi, dil, layer,
                                None if state["kv_p"] is None else state["kv_p"][gi], depth=depth, n_seq=n_p, seq=t_p)
        os_.append(o)
        ls_.append(l)
        kv_p.append(kv)
    att_p = _merge_attn(os_, ls_, tm=512)
    att_s, *kv_s = _sample_attn(z_qkv, lw["cos_s"], lw["sin_s"], state["caches_t"], layer, state["kv_s"],
                                row0=rows_p, n_seq=n_s, t_new=t_s)

    pool_p = _pool_mixer(z_pool, jnp.zeros((n_p, POOL_HIST, POOL_W), F32), lw["pool_w"], lw["pool_scale"],
                         row0=0, n_seq=n_p, t_len=t_p, pos0=0)
    hist_s = jnp.concatenate([jnp.zeros((n_s, 1, POOL_W), F32), state["pool"]], axis=1)
    pool_s = _pool_mixer(z_pool, hist_s, lw["pool_w"], lw["pool_scale"], row0=rows_p, n_seq=n_s, t_len=t_s, pos0=past)

    prep_p = _rwkv_prep(z_rwkv, jnp.zeros((n_p, 1, RWKV_COLS), F32), lw, row0=0, n_seq=n_p, t_len=t_p, tm=256)
    prep_s = _rwkv_prep(z_rwkv, state["shift"][:, None, :], lw, row0=rows_p, n_seq=n_s, t_len=t_s, tm=t_s)
    yn_p, wkv_p = _rwkv_scan(prep_p[:6], jnp.zeros((n_p, RWKV_HEADS, RWKV_HEAD, RWKV_HEAD), F32),
                             chunk=RWKV_CHUNK, group=RWKV_SEQ_GROUP)
    yn_s, wkv_s = _rwkv_scan(prep_s[:6], state["wkv"], chunk=t_s, group=RWKV_SEQ_GROUP)
    branches = [(att_p, att_s), (pool_p, pool_s), (yn_p.reshape(rows_p, RWKV_W), yn_s.reshape(rows_s, RWKV_W)),
                (prep_p[7], prep_s[7]), (prep_p[6], prep_s[6])]

    mix = _mix(branches, z_gate, lw["ln_w"], lw["ln_b"], lw["w_attn_o"], lw["w_pool_o"], lw["w_rwkv_o"], tm=rows_s)
    x = _mm_residual(x, mix, lw["w_out"], tm=384)

    q, xn = _norm_mm3(x, lw["g_ffn"], *lw["peer_wq"], tm=768, tn=512)
    i1, i2, gate = _peer_select(q, *lw["peer_subkeys"])
    act = _peer_act(xn, lw["peer_u"], lw["layer"], i1, i2, tb=PEER_ACT_TB)
    ffn = _peer_out(i1, i2, gate, act, lw["peer_v"], lw["layer"])

    x = _ple(x, ffn, p_rows, lw["g_ple"], lw["ple_wg"], lw["ple_wp"], g_final, tm=rows_s, final=final, rows_prompt=rows_p)

    keep = POOL_HIST - 1
    pool_cols = slice(Z_POOL, Z_POOL + POOL_W)
    rwkv_cols = slice(Z_RWKV, Z_RWKV + RWKV_COLS)
    pool_p_state = jnp.stack([z[(n + 1) * t_p - keep:(n + 1) * t_p, pool_cols] for n in range(n_p)])
    zp_s = z[rows_p:, pool_cols].reshape(n_s, t_s, POOL_W)
    new_p = [pool_p_state, wkv_p, z[t_p - 1:rows_p:t_p, rwkv_cols]]
    new_s = [jnp.concatenate([hist_s[:, 1:], zp_s], axis=1)[:, -keep:], wkv_s, z[rows_p + t_s - 1::t_s, rwkv_cols]]
    return x, kv_p, kv_s, new_p, new_s


def kernel(x_prompt, x_sample, p_prompt, p_sample, cache_attn_w128, cache_attn_w512, cache_attn_w2048, state_pool, state_rwkv_wkv, state_rwkv_shift, g_mix, w_in, w_attn_o, w_pool_o, w_rwkv_o, w_out, pool_w, pool_scale, rwkv_mu, rwkv_w0, rwkv_w2, rwkv_a0, rwkv_a2, rwkv_g2, rwkv_kk, rwkv_ka, rwkv_rk, rwkv_ln_w, rwkv_ln_b, g_ffn, peer_wq, peer_subkeys, peer_u, peer_v, g_ple, ple_wg, ple_wp, g_final):
    n_p, t_p, _ = x_prompt.shape
    n_s, t_s, _ = x_sample.shape
    depth = w_in.shape[0]
    past = PAST_LEN
    rows_p = n_p * t_p
    rows_s = n_s * t_s
    dims = (n_p, t_p, n_s, t_s, past)

    x = jnp.concatenate([x_prompt.reshape(rows_p, D_MODEL), x_sample.reshape(rows_s, D_MODEL)], axis=0)
    cos_p, sin_p = _rope_tables(jnp.arange(t_p, dtype=I32))
    cos_s, sin_s = _rope_tables(past + jnp.arange(t_s, dtype=I32))
    lora_pad = jnp.zeros((128 - 64, RWKV_W), F32)
    peer_v_bf16 = peer_v.astype(BF16)
    to_dev = (0, 1, 3, 4, 5, 2)
    from_dev = (0, 1, 5, 2, 3, 4)
    caches_t = [jnp.transpose(c, to_dev) for c in (cache_attn_w128, cache_attn_w512, cache_attn_w2048)]

    def row(a):
        return a.reshape(1, -1)

    w_in_z = jnp.concatenate([w_in[:, :, a:b].astype(BF16) for a, b in
                              ((OFF_GATE, IN_COLS), (OFF_RWKV, OFF_GATE), (OFF_POOL, OFF_RWKV), (0, OFF_POOL))], axis=2)

    new_p, new_s = [], []
    kv_p = kv_s = None
    for l in range(depth):
        lw = {
            "g_mix": g_mix[l], "w_in": w_in_z,
            "cos_p": cos_p, "sin_p": sin_p, "cos_s": cos_s, "sin_s": sin_s, "depth": depth,
            "pool_w": pool_w[l].astype(BF16), "pool_scale": pool_scale[l],
            "mu": row(rwkv_mu[l]), "w0": row(rwkv_w0[l]), "a0": row(rwkv_a0[l]),
            "w2p": jnp.concatenate([rwkv_w2[l], lora_pad], axis=0).astype(BF16),
            "a2p": jnp.concatenate([lora_pad, rwkv_a2[l]], axis=0).astype(BF16),
            "g2": rwkv_g2[l].astype(BF16),
            "kk": row(rwkv_kk[l]), "ka": row(rwkv_ka[l]), "rk": row(rwkv_rk[l]),
            "ln_w": rwkv_ln_w[l], "ln_b": rwkv_ln_b[l],
            "w_attn_o": w_attn_o[l].astype(BF16), "w_pool_o": w_pool_o[l].astype(BF16),
            "w_rwkv_o": w_rwkv_o[l].astype(BF16), "w_out": w_out[l].astype(BF16),
            "g_ffn": g_ffn[l], "peer_wq": _split_bf16(peer_wq[l]), "peer_subkeys": _split_bf16(peer_subkeys[l]),
            "peer_u": peer_u, "peer_v": peer_v_bf16, "layer": l,
            "g_ple": g_ple[l], "ple_wg": ple_wg[l].astype(BF16), "ple_wp": ple_wp[l].astype(BF16),
        }
        state = {"caches_t": caches_t, "kv_p": kv_p, "kv_s": kv_s, "pool": state_pool[l],
                 "wkv": state_rwkv_wkv[l], "shift": state_rwkv_shift[l]}
        p_rows = jnp.concatenate([p_prompt[l].reshape(rows_p, -1), p_sample[l].reshape(rows_s, -1)], axis=0).astype(BF16)
        x, kv_p, kv_s, st_p, st_s = _layer(x, p_rows, lw, state, g_final, dims, l == depth - 1)
        new_p.append(st_p)
        new_s.append(st_s)

    y_p, y_s = x
    outs = [y_p.reshape(n_p, t_p, D_MODEL), y_s.reshape(n_s, t_s, D_MODEL)]
    for gi in range(len(ATT_GROUPS)):
        outs.append(jnp.transpose(kv_p[gi], from_dev))
        outs.append(jnp.transpose(kv_s[gi], from_dev))
    for j in range(3):
        outs.append(jnp.stack([s[j] for s in new_p]))
        outs.append(jnp.stack([s[j] for s in new_s]))
    return tuple(outs)
```

```python
import functools
import math

import jax
import jax.numpy as jnp
from jax import lax
from jax.experimental import pallas as pl
from jax.experimental.pallas import tpu as pltpu

F32 = jnp.float32
BF16 = jnp.bfloat16
I32 = jnp.int32

D_MODEL = 2048
RMS_EPS = 1e-6
HEAD_DIM = 64
ATT_GROUPS = ((128, 1), (512, 4), (2048, 16))
HEADS_PER_GROUP = 4
ATT_W = 768
ATT_OUT = 256
ATT_SPAN = 128
ROPE_THETA = 10000.0
POOL_WINDOWS = (2, 4, 8, 16)
POOL_GROUP = 128
POOL_W = 512
POOL_HIST = 16
RWKV_HEAD = 64
RWKV_HEADS = 12
RWKV_W = 768
RWKV_COLS = 2560
RWKV_LORA_OFF = 2304
GN_EPS = 64e-5
RWKV_CHUNK = 64
RWKV_SEQ_GROUP = 4
OFF_POOL = 2304
OFF_RWKV = 2816
OFF_GATE = 5376
IN_COLS = 11520
Z_GATE = 0
Z_RWKV = 3 * D_MODEL
Z_POOL = Z_RWKV + RWKV_COLS
Z_QKV = Z_POOL + POOL_W
Z_RWKV_BLOCK = RWKV_COLS + POOL_W
PAST_LEN = 8192
PEER_HEADS = 8
N_KEYS = 128
N_EXPERTS = N_KEYS * N_KEYS
PEER_TOPK = 16
PEER_PAIRS = PEER_HEADS * PEER_TOPK
PEER_TB = 384
PEER_ACT_TB = 1408
PEER_BUILD_UNROLL = 4
PEER_EC = 1024
PEER_OUT_EC = 2048
SUBLANES = 8
VMEM_LIMIT = 56 * 1024 * 1024

NEG_INF = float("-inf")


def _params(*sem):
    return pltpu.CompilerParams(dimension_semantics=sem, vmem_limit_bytes=VMEM_LIMIT)


def _dot(a, b):
    return jnp.dot(a.astype(BF16), b.astype(BF16), preferred_element_type=F32)


def _dot_nt(a, b):
    return lax.dot_general(a.astype(BF16), b.astype(BF16), (((1,), (1,)), ((), ())), preferred_element_type=F32)


def _norm_mm_kernel(x_ref, g_ref, w_ref, o_ref, xn_ref):
    @pl.when(pl.program_id(1) == 0)
    def _():
        x = x_ref[...]
        ms = jnp.mean(x * x, axis=-1, keepdims=True)
        xn_ref[...] = ((x * lax.rsqrt(ms + RMS_EPS)) * g_ref[...]).astype(BF16)

    o_ref[...] = jnp.dot(xn_ref[...], w_ref[0], preferred_element_type=F32)


def _norm_mm(x, g, w, layer, *, tm, tn):
    rows, k = x.shape
    n = w.shape[2]
    return pl.pallas_call(
        _norm_mm_kernel, grid=(rows // tm, n // tn),
        in_specs=[
            pl.BlockSpec((tm, k), lambda i, j: (i, 0)),
            pl.BlockSpec((1, k), lambda i, j: (0, 0)),
            pl.BlockSpec((1, k, tn), lambda i, j: (layer, 0, j)),
        ],
        out_specs=pl.BlockSpec((tm, tn), lambda i, j: (i, j)),
        out_shape=jax.ShapeDtypeStruct((rows, n), F32),
        scratch_shapes=[pltpu.VMEM((tm, k), BF16)],
        compiler_params=_params("arbitrary", "arbitrary"), name="norm_mm",
    )(x, g.reshape(1, k), w)


Z_SEGMENTS = ((OFF_GATE, IN_COLS), (OFF_RWKV, OFF_GATE), (OFF_POOL, OFF_RWKV), (0, OFF_POOL))


def _repack_kernel(w_ref, o_ref):
    dst = 0
    for a, b in Z_SEGMENTS:
        o_ref[0, :, dst:dst + (b - a)] = w_ref[0, :, a:b].astype(BF16)
        dst += b - a


def _repack_w_in(w_in, *, tk):
    depth, k, n = w_in.shape
    spec = pl.BlockSpec((1, tk, n), lambda l, i: (l, i, 0))
    return pl.pallas_call(
        _repack_kernel, grid=(depth, k // tk), in_specs=[spec], out_specs=spec,
        out_shape=jax.ShapeDtypeStruct(w_in.shape, BF16),
        compiler_params=_params("arbitrary", "arbitrary"), name="repack_w_in",
    )(w_in)


def _split_bf16(a):
    hi = a.astype(BF16)
    lo = (a - hi.astype(F32)).astype(BF16)
    return hi, lo


def _dot3(ah, al, bh, bl, dims=(((1,), (0,)), ((), ()))):
    def d(p, q):
        return lax.dot_general(p, q, dims, preferred_element_type=F32)

    return d(ah, bh) + (d(ah, bl) + d(al, bh))


def _norm_mm3_kernel(x_ref, g_ref, wh_ref, wl_ref, o_ref, xh_ref, xl_ref):
    @pl.when(pl.program_id(1) == 0)
    def _():
        x = x_ref[...]
        ms = jnp.mean(x * x, axis=-1, keepdims=True)
        xh, xl = _split_bf16((x * lax.rsqrt(ms + RMS_EPS)) * g_ref[...])
        xh_ref[...] = xh
        xl_ref[...] = xl

    o_ref[...] = _dot3(xh_ref[...], xl_ref[...], wh_ref[...], wl_ref[...])


def _norm_mm3(x, g, wh, wl, *, tm, tn):
    rows, k = x.shape
    n = wh.shape[1]
    wspec = pl.BlockSpec((k, tn), lambda i, j: (0, j))
    return pl.pallas_call(
        _norm_mm3_kernel, grid=(rows // tm, n // tn),
        in_specs=[pl.BlockSpec((tm, k), lambda i, j: (i, 0)), pl.BlockSpec((1, k), lambda i, j: (0, 0)), wspec, wspec],
        out_specs=[pl.BlockSpec((tm, tn), lambda i, j: (i, j)), pl.BlockSpec((tm, k), lambda i, j: (i, 0))],
        out_shape=[jax.ShapeDtypeStruct((rows, n), F32), jax.ShapeDtypeStruct((rows, k), BF16)],
        scratch_shapes=[pltpu.VMEM((tm, k), BF16)],
        compiler_params=_params("arbitrary", "arbitrary"), name="norm_mm3",
    )(x, g.reshape(1, k), wh, wl)


def _rotate(x, cos, sin):
    lane = lax.broadcasted_iota(I32, x.shape, 1)
    first_half = (lane & (HEAD_DIM - 1)) < HEAD_DIM // 2
    partner = jnp.where(first_half, pltpu.roll(x, 128 - HEAD_DIM // 2, 1), pltpu.roll(x, HEAD_DIM // 2, 1))
    return x * cos + partner * sin


def _rope_tables(pos):
    half = HEAD_DIM // 2
    inv = ROPE_THETA ** (-jnp.arange(half, dtype=F32) / half)
    ang = pos.astype(F32)[:, None] * inv[None, :]
    cos = jnp.cos(ang)
    sin = jnp.sin(ang)
    cos_t = jnp.concatenate([cos, cos, cos, cos], axis=1)
    sin_t = jnp.concatenate([-sin, sin, -sin, sin], axis=1)
    return cos_t, sin_t


ROT_ROWS = 256


def _prompt_attn_kernel(q_ref, k_ref, v_ref, cos_ref, sin_ref, *rest, dil, seq, keep):
    o_ref, l_ref, kv_ref, qs, ks, vs = rest[-6:]
    chunks = ATT_OUT // 128
    for i in range(seq // ROT_ROWS):
        rs = slice(i * ROT_ROWS, (i + 1) * ROT_ROWS)
        cos = cos_ref[rs, :]
        sin = sin_ref[rs, :]
        for c in range(chunks):
            cs = slice(c * 128, (c + 1) * 128)
            qs[c, rs, :] = _rotate(q_ref[rs, cs], cos, sin)
            ks[c, rs, :] = _rotate(k_ref[rs, cs], cos, sin)
            vs[c, rs, :] = v_ref[rs, cs]
    for c in range(chunks):
        for j in range(keep // 128):
            ps = slice(seq - keep + j * 128, seq - keep + (j + 1) * 128)
            k_t = ks[c, ps, :].T
            v_t = vs[c, ps, :].T
            for half in range(2):
                hs = slice(half * HEAD_DIM, (half + 1) * HEAD_DIM)
                kv_ref[0, 0, 0, 2 * c + half, :, j * 128:(j + 1) * 128] = k_t[hs, :]
                kv_ref[0, 0, 1, 2 * c + half, :, j * 128:(j + 1) * 128] = v_t[hs, :]

    def rows_of(start):
        return pl.ds(start, ATT_SPAN, stride=dil) if dil > 1 else pl.ds(start, ATT_SPAN)

    qi = lax.broadcasted_iota(I32, (ATT_SPAN, ATT_SPAN), 0)
    ki = lax.broadcasted_iota(I32, (ATT_SPAN, ATT_SPAN), 1)
    cur_ok = ki <= qi
    prev_ok = ki >= qi
    first_head = ki < HEAD_DIM
    ones = jnp.ones((ATT_SPAN, 128), BF16)
    scale = HEAD_DIM ** -0.5
    n_blocks = seq // dil // ATT_SPAN
    for r in range(dil):
        for b in range(n_blocks):
            cur = rows_of(r + dil * b * ATT_SPAN)
            for c in range(chunks):
                q2, kc2, vc2 = qs[c, cur, :], ks[c, cur, :], vs[c, cur, :]
                if b > 0:
                    prev = rows_of(r + dil * (b - 1) * ATT_SPAN)
                    kp2, vp2 = ks[c, prev, :], vs[c, prev, :]
                outs, lses = [], []
                for half in range(2):
                    q = jnp.where(first_head == (half == 0), q2, 0.0)
                    sc = jnp.where(cur_ok, _dot_nt(q, kc2) * scale, NEG_INF)
                    m = jnp.max(sc, axis=-1, keepdims=True)
                    if b > 0:
                        sp = jnp.where(prev_ok, _dot_nt(q, kp2) * scale, NEG_INF)
                        m = jnp.maximum(m, jnp.max(sp, axis=-1, keepdims=True))
                    ec = jnp.exp(sc - m).astype(BF16)
                    den = jnp.dot(ec, ones, preferred_element_type=F32)
                    acc = _dot(ec, vc2)
                    if b > 0:
                        ep = jnp.exp(sp - m).astype(BF16)
                        den = den + jnp.dot(ep, ones, preferred_element_type=F32)
                        acc = acc + _dot(ep, vp2)
                    outs.append(acc / den)
                    lses.append(m + jnp.log(den))
                o_ref[c, cur, :] = jnp.where(first_head, outs[0], outs[1])
                l_ref[c, cur, :] = jnp.where(first_head, lses[0], lses[1])


def _prompt_attn(z_qkv, cos_t, sin_t, gi, dil, layer, kv_prev, *, depth, n_seq, seq):
    rows_p = n_seq * seq
    win = ATT_GROUPS[gi][0]
    keep = min(win, seq)
    assert seq % (dil * ATT_SPAN) == 0 and keep % 128 == 0 and seq % ROT_ROWS == 0
    tab = pl.BlockSpec((seq, 128), lambda n: (0, 0))
    chunked = pl.BlockSpec((ATT_OUT // 128, seq, 128), lambda n: (0, n, 0))
    kv_shape = (depth, n_seq, 2, HEADS_PER_GROUP, HEAD_DIM, keep)
    kv_spec = pl.BlockSpec((1, 1) + kv_shape[2:], lambda n: (layer, n, 0, 0, 0, 0))
    c0 = Z_QKV // ATT_OUT
    in_specs = [pl.BlockSpec((seq, ATT_OUT), lambda n: (n, c0 + gi)),
                pl.BlockSpec((seq, ATT_OUT), lambda n: (n, c0 + ATT_W // ATT_OUT + gi)),
                pl.BlockSpec((seq, ATT_OUT), lambda n: (n, c0 + 2 * ATT_W // ATT_OUT + gi)), tab, tab]
    args = [z_qkv, z_qkv, z_qkv, cos_t, sin_t]
    aliases = {}
    if kv_prev is not None:
        in_specs.append(pl.BlockSpec(memory_space=pl.ANY))
        args.append(kv_prev)
        aliases = {len(args) - 1: 2}
    chunk_shape = jax.ShapeDtypeStruct((ATT_OUT // 128, rows_p, 128), F32)
    return pl.pallas_call(
        functools.partial(_prompt_attn_kernel, dil=dil, seq=seq, keep=keep),
        grid=(n_seq,), in_specs=in_specs,
        out_specs=[chunked, chunked, kv_spec],
        out_shape=[chunk_shape, chunk_shape, jax.ShapeDtypeStruct(kv_shape, F32)],
        scratch_shapes=[pltpu.VMEM((ATT_OUT // 128, seq, 128), F32)] * 3,
        input_output_aliases=aliases,
        compiler_params=_params("arbitrary"), name=f"prompt_attn_g{gi}",
    )(*args)


def _merge_groups(os_, ls_):
    m = jnp.maximum(jnp.maximum(ls_[0], ls_[1]), ls_[2])
    es = [jnp.exp(l - m) for l in ls_]
    tot = es[0] + es[1] + es[2]
    return (es[0] / tot) * os_[0] + (es[1] / tot) * os_[1] + (es[2] / tot) * os_[2]


def _merge_kernel(o0, o1, o2, l0, l1, l2, a_ref):
    for c in range(ATT_OUT // 128):
        a_ref[:, c * 128:(c + 1) * 128] = _merge_groups([o0[c], o1[c], o2[c]], [l0[c], l1[c], l2[c]])


def _merge_attn(os_, ls_, *, tm):
    rows = os_[0].shape[1]
    spec = pl.BlockSpec((ATT_OUT // 128, tm, 128), lambda i: (0, i, 0))
    return pl.pallas_call(
        _merge_kernel, grid=(rows // tm,), in_specs=[spec] * 6, out_specs=pl.BlockSpec((tm, ATT_OUT), lambda i: (i, 0)),
        out_shape=jax.ShapeDtypeStruct((rows, ATT_OUT), F32),
        compiler_params=_params("arbitrary"), name="merge_attn",
    )(*os_, *ls_)


def _sample_attn_kernel(z_ref, cos_ref, sin_ref, *rest, t_new):
    caches, (a_ref, n0_ref, n1_ref, n2_ref) = rest[0:3], rest[-4:]
    scale = HEAD_DIM ** -0.5
    cos = cos_ref[...]
    sin = sin_ref[...]
    n_chunks = ATT_W // 128
    q_c = [_rotate(z_ref[:, c * 128:(c + 1) * 128], cos, sin) for c in range(n_chunks)]
    k_c = [_rotate(z_ref[:, ATT_W + c * 128:ATT_W + (c + 1) * 128], cos, sin) for c in range(n_chunks)]
    v_c = [z_ref[:, 2 * ATT_W + c * 128:2 * ATT_W + (c + 1) * 128] for c in range(n_chunks)]
    outs_g, lses_g = [], []
    for gi, (c_ref, n_ref, (win, dil)) in enumerate(zip(caches, (n0_ref, n1_ref, n2_ref), ATT_GROUPS)):
        cache_len = c_ref.shape[-1]
        t_c = lax.broadcasted_iota(I32, (t_new, cache_len), 0)
        c_c = lax.broadcasted_iota(I32, (t_new, cache_len), 1)
        d_c = cache_len + t_c - c_c
        ok_c = jnp.logical_and((d_c & (dil - 1)) == 0, d_c <= ATT_SPAN * dil)
        t_n = lax.broadcasted_iota(I32, (t_new, t_new), 0)
        u_n = lax.broadcasted_iota(I32, (t_new, t_new), 1)
        d_n = t_n - u_n
        ok_n = jnp.logical_and(d_n >= 0, (d_n & (dil - 1)) == 0)
        outs, lses = [], []
        for h in range(HEADS_PER_GROUP):
            chunk, half = divmod(gi * HEADS_PER_GROUP + h, 2)
            hs = slice(half * HEAD_DIM, (half + 1) * HEAD_DIM)
            q, k_new, v_new = q_c[chunk][:, hs], k_c[chunk][:, hs], v_c[chunk][:, hs]
            k_t = c_ref[0, 0, 0, h]
            v_t = c_ref[0, 0, 1, h]
            s_c = jnp.where(ok_c, _dot(q, k_t) * scale, NEG_INF)
            s_n = jnp.where(ok_n, _dot_nt(q, k_new) * scale, NEG_INF)
            m = jnp.maximum(jnp.max(s_c, axis=-1, keepdims=True), jnp.max(s_n, axis=-1, keepdims=True))
            e_c = jnp.exp(s_c - m)
            e_n = jnp.exp(s_n - m)
            den = jnp.sum(e_c, axis=-1, keepdims=True) + jnp.sum(e_n, axis=-1, keepdims=True)
            acc = _dot_nt(e_c, v_t) + _dot(e_n, v_new)
            outs.append(acc / den)
            lses.append(jnp.broadcast_to(m + jnp.log(den), (t_new, HEAD_DIM)))
            n_ref[0, 0, 0, h] = jnp.concatenate([k_t[:, t_new:], k_new.T], axis=1)
            n_ref[0, 0, 1, h] = jnp.concatenate([v_t[:, t_new:], v_new.T], axis=1)
        outs_g.append(jnp.concatenate(outs, axis=1))
        lses_g.append(jnp.concatenate(lses, axis=1))
    a_ref[...] = _merge_groups(outs_g, lses_g)


def _sample_attn(z_qkv, cos_s, sin_s, caches_t, layer, new_prev, *, row0, n_seq, t_new):
    b0 = row0 // t_new

    def cache_spec(c):
        return pl.BlockSpec((1, 1) + c.shape[2:], lambda n: (layer, n, 0, 0, 0, 0))

    tab = pl.BlockSpec((t_new, 128), lambda n: (0, 0))
    in_specs = ([pl.BlockSpec((t_new, 3 * ATT_W), lambda n: (b0 + n, Z_QKV // (3 * ATT_W))), tab, tab]
                + [cache_spec(c) for c in caches_t])
    args = [z_qkv, cos_s, sin_s, *caches_t]
    aliases = {}
    if new_prev is not None:
        for j, a in enumerate(new_prev):
            in_specs.append(pl.BlockSpec(memory_space=pl.ANY))
            args.append(a)
            aliases[len(args) - 1] = 1 + j
    return pl.pallas_call(
        functools.partial(_sample_attn_kernel, t_new=t_new),
        grid=(n_seq,), in_specs=in_specs,
        out_specs=[pl.BlockSpec((t_new, ATT_OUT), lambda n: (n, 0))] + [cache_spec(c) for c in caches_t],
        out_shape=[jax.ShapeDtypeStruct((n_seq * t_new, ATT_OUT), F32)]
        + [jax.ShapeDtypeStruct(c.shape, F32) for c in caches_t],
        input_output_aliases=aliases,
        compiler_params=_params("arbitrary"), name="sample_attn",
    )(*args)


def _pool_kernel(z_ref, h_ref, w_ref, s_ref, o_ref, buf_ref, *, t_len, pos0):
    buf_ref[0:POOL_HIST, :] = h_ref[0]
    buf_ref[POOL_HIST:POOL_HIST + t_len, :] = z_ref[...]
    pos = pos0 + lax.broadcasted_iota(I32, (t_len, POOL_GROUP), 0)
    for g, win in enumerate(POOL_WINDOWS):
        cs = slice(g * POOL_GROUP, (g + 1) * POOL_GROUP)
        z = buf_ref[POOL_HIST:POOL_HIST + t_len, cs]
        wsum = z
        for i in range(1, win):
            wsum = wsum + buf_ref[POOL_HIST - i:POOL_HIST - i + t_len, cs]
        cnt = jnp.minimum(win, pos + 1).astype(F32)
        y = wsum / cnt - z
        o_ref[:, cs] = _dot(y, w_ref[g]) * s_ref[:, cs]


def _pool_mixer(z_pool, hist, pool_w, pool_scale, *, row0, n_seq, t_len, pos0):
    b0 = row0 // t_len
    return pl.pallas_call(
        functools.partial(_pool_kernel, t_len=t_len, pos0=pos0),
        grid=(n_seq,),
        in_specs=[
            pl.BlockSpec((t_len, POOL_W), lambda n: (b0 + n, Z_POOL // POOL_W)),
            pl.BlockSpec((1, POOL_HIST, POOL_W), lambda n: (n, 0, 0)),
            pl.BlockSpec((len(POOL_WINDOWS), POOL_GROUP, POOL_GROUP), lambda n: (0, 0, 0)),
            pl.BlockSpec((1, POOL_W), lambda n: (0, 0)),
        ],
        out_specs=pl.BlockSpec((t_len, POOL_W), lambda n: (n, 0)),
        out_shape=jax.ShapeDtypeStruct((n_seq * t_len, POOL_W), F32),
        scratch_shapes=[pltpu.VMEM((POOL_HIST + t_len, POOL_W), F32)],
        compiler_params=_params("arbitrary"), name="pool_mixer",
    )(z_pool, hist, pool_w, pool_scale.reshape(1, POOL_W))


def _rwkv_prep_kernel(z_ref, zp_ref, first_ref, mu_ref, w0_ref, w2_ref, a0_ref, a2_ref, g2_ref, kk_ref, ka_ref, rk_ref,
                      r_o, lw_o, k_o, v_o, kn_o, b_o, g_o, bonus_o, buf_ref, *, tm):
    i = pl.program_id(1)
    z = z_ref[:, 0:RWKV_COLS]
    prev_row = jnp.where(i == 0, first_ref[0], zp_ref[SUBLANES - 1:SUBLANES, 0:RWKV_COLS])
    buf_ref[SUBLANES:SUBLANES + tm, :] = z
    buf_ref[SUBLANES - 1:SUBLANES, :] = prev_row
    shifted = buf_ref[SUBLANES - 1:SUBLANES - 1 + tm, :]
    xm = z + mu_ref[...] * (shifted - z)
    r = xm[:, 0:RWKV_W]
    k = xm[:, RWKV_W:2 * RWKV_W]
    v = xm[:, 2 * RWKV_W:3 * RWKV_W]
    wa = xm[:, RWKV_LORA_OFF:RWKV_LORA_OFF + 128]
    gl = xm[:, RWKV_LORA_OFF + 128:RWKV_COLS]
    xw = w0_ref[...] + _dot(jnp.tanh(wa), w2_ref[...])
    logw = -math.exp(-0.5) * jax.nn.sigmoid(xw)
    a = jax.nn.sigmoid(a0_ref[...] + _dot(wa, a2_ref[...]))
    g_o[...] = _dot(jax.nn.sigmoid(gl), g2_ref[...])
    kkr = k * kk_ref[...]
    kmod = k * (1.0 + (a - 1.0) * ka_ref[...])
    rkk = r * kmod * rk_ref[...]
    bonus = []
    for h in range(RWKV_HEADS):
        sl = slice(h * RWKV_HEAD, (h + 1) * RWKV_HEAD)
        kh = kkr[:, sl]
        nrm = jnp.sqrt(jnp.sum(kh * kh, axis=-1, keepdims=True))
        kn = kh / jnp.maximum(nrm, 1e-12)
        r_o[0, h] = r[:, sl]
        lw_o[0, h] = logw[:, sl]
        k_o[0, h] = kmod[:, sl]
        v_o[0, h] = v[:, sl]
        kn_o[0, h] = kn
        b_o[0, h] = kn * a[:, sl]
        bonus.append(jnp.sum(rkk[:, sl], axis=-1, keepdims=True) * v[:, sl])
    bonus_o[...] = jnp.concatenate(bonus, axis=1)


def _rwkv_prep(z, first_prev, lw, *, row0, n_seq, t_len, tm):
    nblk = t_len // tm
    pb = tm // SUBLANES
    b0 = row0 // tm
    p0 = row0 // SUBLANES

    def vec(n):
        return pl.BlockSpec((1, n), lambda s, i: (0, 0))

    hm = jax.ShapeDtypeStruct((n_seq, RWKV_HEADS, t_len, RWKV_HEAD), F32)
    rm = jax.ShapeDtypeStruct((n_seq * t_len, RWKV_W), F32)
    hm_spec = pl.BlockSpec((1, RWKV_HEADS, tm, RWKV_HEAD), lambda s, i: (s, 0, i, 0))
    rm_spec = pl.BlockSpec((tm, RWKV_W), lambda s, i: (s * nblk + i, 0))
    return pl.pallas_call(
        functools.partial(_rwkv_prep_kernel, tm=tm),
        grid=(n_seq, nblk),
        in_specs=[
            pl.BlockSpec((tm, Z_RWKV_BLOCK), lambda s, i: (b0 + s * nblk + i, Z_RWKV // Z_RWKV_BLOCK)),
            pl.BlockSpec((SUBLANES, Z_RWKV_BLOCK),
                         lambda s, i: (jnp.maximum(p0 + (s * nblk + i) * pb - 1, 0), Z_RWKV // Z_RWKV_BLOCK)),
            pl.BlockSpec((1, 1, RWKV_COLS), lambda s, i: (s, 0, 0)),
            vec(RWKV_COLS), vec(RWKV_W),
            pl.BlockSpec((128, RWKV_W), lambda s, i: (0, 0)),
            vec(RWKV_W),
            pl.BlockSpec((128, RWKV_W), lambda s, i: (0, 0)),
            pl.BlockSpec((128, RWKV_W), lambda s, i: (0, 0)),
            vec(RWKV_W), vec(RWKV_W), vec(RWKV_W),
        ],
        out_specs=[hm_spec] * 6 + [rm_spec] * 2,
        out_shape=[hm] * 6 + [rm] * 2,
        scratch_shapes=[pltpu.VMEM((tm + SUBLANES, RWKV_COLS), F32)],
        compiler_params=_params("arbitrary", "arbitrary"), name="rwkv_prep",
    )(z, z, first_prev, lw["mu"], lw["w0"], lw["w2p"], lw["a0"], lw["a2p"], lw["g2"], lw["kk"], lw["ka"], lw["rk"])


def _bmm(a, b):
    return jnp.einsum("hqk,hkd->hqd", a.astype(BF16), b.astype(BF16), preferred_element_type=F32)


def _bmm_nt(a, b):
    return jnp.einsum("hqd,hkd->hqk", a.astype(BF16), b.astype(BF16), preferred_element_type=F32)


def _bmm_tn(a, b):
    return jnp.einsum("hkq,hkd->hqd", a.astype(BF16), b.astype(BF16), preferred_element_type=F32)


def _rwkv_scan_kernel(r_ref, lw_ref, k_ref, v_ref, kn_ref, b_ref, s0_ref, y_ref, st_ref, s_scr, *, chunk, group):
    c = pl.program_id(1)
    nh = group * RWKV_HEADS

    def heads(ref):
        return ref[...].reshape((nh,) + ref.shape[2:])

    @pl.when(c == 0)
    def _():
        s_scr[...] = heads(s0_ref)

    r, logw, k, v, kn, b = [heads(ref) for ref in (r_ref, lw_ref, k_ref, v_ref, kn_ref, b_ref)]
    row = lax.broadcasted_iota(I32, (chunk, chunk), 0)
    col = lax.broadcasted_iota(I32, (chunk, chunk), 1)
    incl = row >= col
    strict = row > col
    tri = jnp.broadcast_to(incl.astype(BF16)[None], (nh, chunk, chunk))
    lw_hi = logw.astype(BF16)
    lw_lo = (logw - lw_hi.astype(F32)).astype(BF16)
    cum = (jnp.einsum("hqk,hkd->hqd", tri, lw_hi, preferred_element_type=F32)
           + jnp.einsum("hqk,hkd->hqd", tri, lw_lo, preferred_element_type=F32))
    p_inv = jnp.exp(-cum)
    kt = k * p_inv
    bt = b * p_inv
    kap = kn * jnp.exp(cum - logw)
    rho = r * jnp.exp(cum)
    qq = jnp.concatenate([kap, rho], axis=1)
    gram = _bmm_nt(qq, jnp.concatenate([kt, bt], axis=1))
    a_k = jnp.where(strict[None], gram[:, :chunk, :chunk], 0.0)
    a_b = jnp.where(strict[None], gram[:, :chunk, chunk:], 0.0)
    l_k = jnp.where(incl[None], gram[:, chunk:, :chunk], 0.0)
    l_b = jnp.where(incl[None], gram[:, chunk:, chunk:], 0.0)
    x = jnp.broadcast_to((row == col).astype(F32)[None], (nh, chunk, chunk))
    m = 1
    while m < chunk:
        sh = m.bit_length() - 1
        same = (row >> (sh + 1)) == (col >> (sh + 1))
        lower_left = jnp.logical_and(((row >> sh) & 1) == 1, ((col >> sh) & 1) == 0)
        off = jnp.where(jnp.logical_and(same, lower_left)[None], a_b, 0.0)
        x = x - _bmm(_bmm(x, off), x)
        m *= 2
    s = s_scr[...]
    qs = _bmm_nt(qq, s)
    u = _bmm(x, -(qs[:, :chunk] + _bmm(a_k, v)))
    y = qs[:, chunk:] + _bmm(jnp.concatenate([l_k, l_b], axis=2), jnp.concatenate([v, u], axis=1))
    s_new = (s + _bmm_tn(jnp.concatenate([v, u], axis=1), jnp.concatenate([kt, bt], axis=1))) * jnp.exp(cum[:, chunk - 1:chunk, :])
    s_scr[...] = s_new
    st_ref[...] = s_new.reshape(st_ref.shape)
    mean = jnp.mean(y, axis=-1, keepdims=True)
    var = jnp.mean(jnp.square(y - mean), axis=-1, keepdims=True)
    yn = (y - mean) * lax.rsqrt(var + GN_EPS)
    for g in range(group):
        y_ref[g] = jnp.concatenate([yn[g * RWKV_HEADS + h] for h in range(RWKV_HEADS)], axis=1)


def _rwkv_scan(prep, s0, *, chunk, group):
    r, lw, k, v, kn, b = prep
    n_seq, nh, t_len, hd = r.shape
    hm_spec = pl.BlockSpec((group, nh, chunk, hd), lambda s, c: (s, 0, c, 0))
    st_spec = pl.BlockSpec((group, nh, hd, hd), lambda s, c: (s, 0, 0, 0))
    return pl.pallas_call(
        functools.partial(_rwkv_scan_kernel, chunk=chunk, group=group),
        grid=(n_seq // group, t_len // chunk),
        in_specs=[hm_spec] * 6 + [st_spec],
        out_specs=[pl.BlockSpec((group, chunk, RWKV_W), lambda s, c: (s, c, 0)), st_spec],
        out_shape=[jax.ShapeDtypeStruct((n_seq, t_len, RWKV_W), F32), jax.ShapeDtypeStruct((n_seq, nh, hd, hd), F32)],
        scratch_shapes=[pltpu.VMEM((group * nh, hd, hd), F32)],
        compiler_params=_params("arbitrary", "arbitrary"), name="rwkv_scan",
    )(r, lw, k, v, kn, b, s0)


def _mix_kernel(*refs, n_prompt_blocks):
    pairs, (gate_ref, lnw_ref, lnb_ref, wa_ref, wp_ref, wr_ref, o_ref) = refs[:10], refs[10:]
    is_sample = pl.program_id(0) >= n_prompt_blocks
    att, pool, yn, bonus, g = [jnp.where(is_sample, pairs[2 * j + 1][...], pairs[2 * j][...]) for j in range(5)]
    rw = (yn * lnw_ref[...] + lnb_ref[...] + bonus) * g
    mix = jax.nn.sigmoid(gate_ref[:, 0:D_MODEL]) * _dot(att, wa_ref[...])
    mix = mix + jax.nn.sigmoid(gate_ref[:, D_MODEL:2 * D_MODEL]) * _dot(pool, wp_ref[...])
    mix = mix + jax.nn.sigmoid(gate_ref[:, 2 * D_MODEL:3 * D_MODEL]) * _dot(rw, wr_ref[...])
    o_ref[...] = mix.astype(BF16)


def _mix(branches, z_gate, ln_w, ln_b, wa, wp, wr, *, tm):
    rows = z_gate.shape[0]
    n_prompt_blocks = branches[0][0].shape[0] // tm
    assert rows == (n_prompt_blocks + 1) * tm and all(s.shape[0] == tm for _, s in branches)

    def full(a):
        return pl.BlockSpec(a.shape, lambda i: (0, 0))

    specs, args = [], []
    for p, s in branches:
        specs += [pl.BlockSpec((tm, p.shape[1]), lambda i: (jnp.minimum(i, n_prompt_blocks - 1), 0)), full(s)]
        args += [p, s]
    ln_w = ln_w.reshape(1, RWKV_W)
    ln_b = ln_b.reshape(1, RWKV_W)
    return pl.pallas_call(
        functools.partial(_mix_kernel, n_prompt_blocks=n_prompt_blocks), grid=(rows // tm,),
        in_specs=specs + [pl.BlockSpec((tm, 3 * D_MODEL), lambda i: (i, 0)),
                          full(ln_w), full(ln_b), full(wa), full(wp), full(wr)],
        out_specs=pl.BlockSpec((tm, D_MODEL), lambda i: (i, 0)),
        out_shape=jax.ShapeDtypeStruct((rows, D_MODEL), BF16),
        compiler_params=_params("arbitrary"), name="gated_mix",
    )(*args, z_gate, ln_w, ln_b, wa, wp, wr)


def _mm_res_kernel(x_ref, a_ref, w_ref, o_ref):
    o_ref[...] = x_ref[...] + jnp.dot(a_ref[...], w_ref[...], preferred_element_type=F32)


def _mm_residual(x, a, w, *, tm):
    rows, n = x.shape
    k = a.shape[1]
    return pl.pallas_call(
        _mm_res_kernel, grid=(rows // tm,),
        in_specs=[pl.BlockSpec((tm, n), lambda i: (i, 0)), pl.BlockSpec((tm, k), lambda i: (i, 0)),
                  pl.BlockSpec((k, n), lambda i: (0, 0))],
        out_specs=pl.BlockSpec((tm, n), lambda i: (i, 0)),
        out_shape=jax.ShapeDtypeStruct((rows, n), F32),
        compiler_params=_params("arbitrary"), name="out_proj",
    )(x, a, w)


def _topk_cols(scores, ids, ids_ordered, k):
    lanes = scores[0].shape[1]
    iota_k = lax.broadcasted_iota(I32, (k, lanes), 0)
    big = jnp.iinfo(jnp.int32).max

    def arg_max(s, ident, ordered):
        if ordered:
            tiles = [(s[i:i + SUBLANES], ident[i:i + SUBLANES]) for i in range(0, s.shape[0], SUBLANES)]
            while len(tiles) > 1:
                merged = []
                for (va, ia), (vb, ib) in zip(tiles[0::2], tiles[1::2]):
                    later = vb > va
                    merged.append((jnp.where(later, vb, va), jnp.where(later, ib, ia)))
                tiles = merged + tiles[len(tiles) - len(tiles) % 2:]
            s, ident = tiles[0]
        m = jnp.max(s, axis=0, keepdims=True)
        return m, jnp.min(jnp.where(s == m, ident, big), axis=0, keepdims=True)

    def body(j, carry):
        out = []
        for (s, vals, idxs), ident, ordered in zip(carry, ids, ids_ordered):
            m, idx = arg_max(s, ident, ordered)
            vals = jnp.where(iota_k == j, m, vals)
            idxs = jnp.where(iota_k == j, idx, idxs)
            out.append((jnp.where(ident == idx, NEG_INF, s), vals, idxs))
        return tuple(out)

    init = tuple((s, jnp.zeros((k, lanes), F32), jnp.zeros((k, lanes), I32)) for s in scores)
    return [(v, i) for _, v, i in lax.fori_loop(0, k, body, init)]


def _gather_rows(table, sel, k):
    out = jnp.zeros(sel.shape, table.dtype)
    for a in range(k):
        out = jnp.where(sel == a, table[a:a + 1, :], out)
    return out


def _peer_select_kernel(q_ref, skh_ref, skl_ref, i1_o, i2_o, gate_o, i1_s, i2_s, g_s, *, tok):
    kk = PEER_TOPK
    half = kk // 2
    nt = (((1,), (1,)), ((), ()))
    key_id = lax.broadcasted_iota(I32, (N_KEYS, tok), 0)
    row = lax.broadcasted_iota(I32, (half * half + kk, tok), 0)
    cand_id = jnp.where(row < half * half, (row >> (half.bit_length() - 1)) * kk + (row & (half - 1)),
                        jnp.where(row < half * half + half, row - half * half + half, (row - half * half) * kk))

    def finish(slot, top, sel, k1, k2):
        e1 = _gather_rows(k1, sel >> 4, kk)
        e2 = _gather_rows(k2, sel & (kk - 1), kk)
        ex = jnp.exp(top - jnp.max(top, axis=0, keepdims=True))
        gate = ex / jnp.sum(ex, axis=0, keepdims=True)
        r0 = pl.multiple_of(slot * kk, kk)
        i1_s[pl.ds(r0, kk), :] = e1.astype(F32)
        i2_s[pl.ds(r0, kk), :] = e2.astype(F32)
        g_s[pl.ds(r0, kk), :] = gate

    def head(h, carry):
        cand_prev, k1_prev, k2_prev = carry
        c0 = pl.multiple_of(h * 2 * N_KEYS, 2 * N_KEYS)
        q1h, q1l = _split_bf16(q_ref[:, pl.ds(c0, N_KEYS)])
        q2h, q2l = _split_bf16(q_ref[:, pl.ds(c0 + N_KEYS, N_KEYS)])
        s1 = _dot3(skh_ref[h, 0], skl_ref[h, 0], q1h, q1l, nt)
        s2 = _dot3(skh_ref[h, 1], skl_ref[h, 1], q2h, q2l, nt)
        (t1, k1), (t2, k2), (top, sel) = _topk_cols([s1, s2, cand_prev], [key_id, key_id, cand_id],
                                                    [True, True, False], kk)
        finish(jnp.where(h == 0, PEER_HEADS, h - 1), top, sel, k1_prev, k2_prev)
        cand = jnp.concatenate([t1[a:a + 1, :] + t2[0:half, :] for a in range(half)]
                               + [t1[0:1, :] + t2[half:kk, :], t1[half:kk, :] + t2[0:1, :]], axis=0)
        return cand, k1, k2

    zeros_i = jnp.zeros((kk, tok), I32)
    cand, k1, k2 = lax.fori_loop(0, PEER_HEADS, head, (jnp.zeros(cand_id.shape, F32), zeros_i, zeros_i))
    ((top, sel),) = _topk_cols([cand], [cand_id], [False], kk)
    finish(PEER_HEADS - 1, top, sel, k1, k2)
    i1_o[...] = i1_s[0:PEER_PAIRS, :].T.astype(I32)
    i2_o[...] = i2_s[0:PEER_PAIRS, :].T.astype(I32)
    gate_o[...] = g_s[0:PEER_PAIRS, :].T


def _peer_select(q, sk_hi, sk_lo, *, tok=128):
    rows = q.shape[0]
    spec = pl.BlockSpec((tok, PEER_PAIRS), lambda i: (i, 0))
    return pl.pallas_call(
        functools.partial(_peer_select_kernel, tok=tok), grid=(rows // tok,),
        in_specs=[pl.BlockSpec((tok, q.shape[1]), lambda i: (i, 0)),
                  pl.BlockSpec(sk_hi.shape, lambda i: (0, 0, 0, 0)), pl.BlockSpec(sk_lo.shape, lambda i: (0, 0, 0, 0))],
        out_specs=[spec] * 3,
        out_shape=[jax.ShapeDtypeStruct((rows, PEER_PAIRS), I32), jax.ShapeDtypeStruct((rows, PEER_PAIRS), I32),
                   jax.ShapeDtypeStruct((rows, PEER_PAIRS), F32)],
        scratch_shapes=[pltpu.VMEM((PEER_PAIRS + PEER_TOPK, tok), F32)] * 3,
        compiler_params=_params("arbitrary"), name="peer_select",
    )(q, sk_hi, sk_lo)


def _peer_pick(d_ref, i1_ref, i2_ref, act_ref, first_row):
    n_rows = d_ref.shape[1] // N_KEYS
    for g in range(act_ref.shape[0] // SUBLANES):
        rs = slice(g * SUBLANES, (g + 1) * SUBLANES)
        i1 = i1_ref[rs, :]
        i2 = i2_ref[rs, :]
        acc = act_ref[rs, :]
        for j in range(n_rows):
            got = jnp.take_along_axis(d_ref[rs, j * N_KEYS:(j + 1) * N_KEYS], i2, axis=1, mode="promise_in_bounds")
            acc = acc + jnp.where(i1 == first_row + j, got, 0.0)
        act_ref[rs, :] = acc


def _peer_act_kernel(xn_ref, u_ref, i1_ref, i2_ref, act_ref, da_ref, db_ref):
    first = jnp.logical_and(pl.program_id(0) == 0, pl.program_id(1) == 0)
    c = pl.program_id(1)
    slab = PEER_EC // 2
    slab_rows = slab // N_KEYS
    nt = (((1,), (1,)), ((), ()))

    @pl.when(first)
    def _():
        db_ref[...] = jnp.zeros(db_ref.shape, F32)

    @pl.when(c == 0)
    def _():
        act_ref[...] = jnp.zeros(act_ref.shape, F32)

    base = c * 2 * slab_rows
    _peer_pick(db_ref, i1_ref, i2_ref, act_ref, base - slab_rows)
    da_ref[...] = lax.dot_general(xn_ref[...], u_ref[0, 0:slab, :].astype(BF16), nt, preferred_element_type=F32)
    _peer_pick(da_ref, i1_ref, i2_ref, act_ref, base)
    db_ref[...] = lax.dot_general(xn_ref[...], u_ref[0, slab:2 * slab, :].astype(BF16), nt, preferred_element_type=F32)

    @pl.when(c == pl.num_programs(1) - 1)
    def _():
        _peer_pick(db_ref, i1_ref, i2_ref, act_ref, base + slab_rows)


def _peer_act(xn, u_tabs, layer, i1, i2, *, tb):
    rows = xn.shape[0]
    pair_spec = pl.BlockSpec((tb, PEER_PAIRS), lambda i, c: (i, 0))
    return pl.pallas_call(
        _peer_act_kernel, grid=(rows // tb, N_EXPERTS // PEER_EC),
        in_specs=[pl.BlockSpec((tb, D_MODEL), lambda i, c: (i, 0)),
                  pl.BlockSpec((1, PEER_EC, D_MODEL), lambda i, c: (layer, c, 0)), pair_spec, pair_spec],
        out_specs=pair_spec,
        out_shape=jax.ShapeDtypeStruct((rows, PEER_PAIRS), F32),
        scratch_shapes=[pltpu.VMEM((tb, PEER_EC // 2), F32)] * 2,
        compiler_params=_params("arbitrary", "arbitrary"), name="peer_act",
    )(xn, u_tabs, i1, i2)


def _peer_out_kernel(i1_ref, i2_ref, gate_ref, act_ref, v_ref, o_ref, w_ref):
    c = pl.program_id(1)
    n_groups = w_ref.shape[0]
    rows_per_chunk = PEER_OUT_EC // N_KEYS

    @pl.when(c == 0)
    def _():
        key_iota = lax.broadcasted_iota(I32, (N_KEYS, PEER_PAIRS), 0)

        def group(g, carry):
            r0 = pl.multiple_of(g * SUBLANES, SUBLANES)
            i1 = i1_ref[pl.ds(r0, SUBLANES), :]
            i2 = i2_ref[pl.ds(r0, SUBLANES), :]
            a = act_ref[pl.ds(r0, SUBLANES), :]
            wgt = gate_ref[pl.ds(r0, SUBLANES), :] * (0.5 * a * (1.0 + lax.erf(a * (1.0 / math.sqrt(2.0)))))
            for t in range(SUBLANES):
                hit1 = key_iota == jnp.broadcast_to(i1[t:t + 1, :], key_iota.shape)
                hit2 = key_iota == jnp.broadcast_to(i2[t:t + 1, :], key_iota.shape)
                w2 = jnp.where(hit2, jnp.broadcast_to(wgt[t:t + 1, :], key_iota.shape), 0.0)
                w_tok = _dot_nt(jnp.where(hit1, 1.0, 0.0), w2)
                w_ref[g, pl.ds(t, N_KEYS, stride=SUBLANES), :] = w_tok
            return carry

        lax.fori_loop(0, n_groups, group, 0, unroll=PEER_BUILD_UNROLL)

    row0 = c * (rows_per_chunk * SUBLANES)
    lhs = jnp.concatenate(
        [w_ref[:, pl.ds(pl.multiple_of(row0 + j * SUBLANES, SUBLANES), SUBLANES), :].reshape(n_groups * SUBLANES, N_KEYS)
         for j in range(rows_per_chunk)], axis=1)
    part = jnp.dot(lhs.astype(BF16), v_ref[0], preferred_element_type=F32)

    @pl.when(c == 0)
    def _():
        o_ref[...] = part

    @pl.when(c > 0)
    def _():
        o_ref[...] += part


def _peer_out(i1, i2, gate, act, v_tabs, layer):
    rows = i1.shape[0]
    tb = PEER_TB
    pair_spec = pl.BlockSpec((tb, PEER_PAIRS), lambda i, c: (i, 0))
    return pl.pallas_call(
        _peer_out_kernel, grid=(rows // tb, N_EXPERTS // PEER_OUT_EC),
        in_specs=[pair_spec] * 4 + [pl.BlockSpec((1, PEER_OUT_EC, D_MODEL), lambda i, c: (layer, c, 0))],
        out_specs=pl.BlockSpec((tb, D_MODEL), lambda i, c: (i, 0)),
        out_shape=jax.ShapeDtypeStruct((rows, D_MODEL), F32),
        scratch_shapes=[pltpu.VMEM((tb // SUBLANES, N_KEYS * SUBLANES, N_KEYS), F32)],
        compiler_params=_params("arbitrary", "arbitrary"), name="peer_out",
    )(i1, i2, gate, act, v_tabs)


def _ple_kernel(x_ref, f_ref, p_ref, g_ref, wg_ref, wp_ref, gf_ref, *o_refs, n_prompt_blocks):
    x = x_ref[...] + f_ref[...]
    ms = jnp.mean(x * x, axis=-1, keepdims=True)
    h = (x * lax.rsqrt(ms + RMS_EPS)) * g_ref[...]
    out = x + jax.nn.sigmoid(_dot(h, wg_ref[...])) * _dot(p_ref[...], wp_ref[...])
    if n_prompt_blocks is None:
        o_refs[0][...] = out
    else:
        ms2 = jnp.mean(out * out, axis=-1, keepdims=True)
        out = (out * lax.rsqrt(ms2 + RMS_EPS)) * gf_ref[...]
        is_sample = pl.program_id(0) >= n_prompt_blocks

        @pl.when(jnp.logical_not(is_sample))
        def _():
            o_refs[0][...] = out

        @pl.when(is_sample)
        def _():
            o_refs[1][...] = out


def _ple(x, ffn, p, g, wg, wp, g_final, *, tm, final, rows_prompt):
    rows, n = x.shape
    row = pl.BlockSpec((tm, n), lambda i: (i, 0))
    vec = pl.BlockSpec((1, n), lambda i: (0, 0))
    if final:
        n_prompt_blocks = rows_prompt // tm
        assert rows == (n_prompt_blocks + 1) * tm
        out_specs = [pl.BlockSpec((tm, n), lambda i: (jnp.minimum(i, n_prompt_blocks - 1), 0)),
                     pl.BlockSpec((tm, n), lambda i: (0, 0))]
        out_shape = [jax.ShapeDtypeStruct((rows_prompt, n), F32), jax.ShapeDtypeStruct((tm, n), F32)]
    else:
        n_prompt_blocks = None
        out_specs = row
        out_shape = jax.ShapeDtypeStruct((rows, n), F32)
    return pl.pallas_call(
        functools.partial(_ple_kernel, n_prompt_blocks=n_prompt_blocks), grid=(rows // tm,),
        in_specs=[row, row, pl.BlockSpec((tm, p.shape[1]), lambda i: (i, 0)), vec,
                  pl.BlockSpec(wg.shape, lambda i: (0, 0)), pl.BlockSpec(wp.shape, lambda i: (0, 0)), vec],
        out_specs=out_specs, out_shape=out_shape,
        compiler_params=_params("arbitrary"), name="ple",
    )(x, ffn, p, g.reshape(1, n), wg, wp, g_final.reshape(1, n))


def _layer(x, p_rows, lw, state, g_final, dims, final):
    n_p, t_p, n_s, t_s, past = dims
    rows_p = n_p * t_p
    rows_s = n_s * t_s

    z = _norm_mm(x, lw["g_mix"], lw["w_in"], lw["layer"], tm=768, tn=1920)
    z_qkv = z_pool = z_rwkv = z_gate = z

    layer, depth = lw["layer"], lw["depth"]
    os_, ls_, kv_p = [], [], []
    for gi, (_, dil) in enumerate(ATT_GROUPS):
        o, l, kv = _prompt_attn(z_qkv, lw["cos_p"], lw["sin_p"], gi, dil, layer,
                                None if state["kv_p"] is None else state["kv_p"][gi], depth=depth, n_seq=n_p, seq=t_p)
        os_.append(o)
        ls_.append(l)
        kv_p.append(kv)
    att_p = _merge_attn(os_, ls_, tm=512)
    att_s, *kv_s = _sample_attn(z_qkv, lw["cos_s"], lw["sin_s"], state["caches_t"], layer, state["kv_s"],
                                row0=rows_p, n_seq=n_s, t_new=t_s)

    pool_p = _pool_mixer(z_pool, jnp.zeros((n_p, POOL_HIST, POOL_W), F32), lw["pool_w"], lw["pool_scale"],
                         row0=0, n_seq=n_p, t_len=t_p, pos0=0)
    hist_s = jnp.concatenate([jnp.zeros((n_s, 1, POOL_W), F32), state["pool"]], axis=1)
    pool_s = _pool_mixer(z_pool, hist_s, lw["pool_w"], lw["pool_scale"], row0=rows_p, n_seq=n_s, t_len=t_s, pos0=past)

    prep_p = _rwkv_prep(z_rwkv, jnp.zeros((n_p, 1, RWKV_COLS), F32), lw, row0=0, n_seq=n_p, t_len=t_p, tm=256)
    prep_s = _rwkv_prep(z_rwkv, state["shift"][:, None, :], lw, row0=rows_p, n_seq=n_s, t_len=t_s, tm=t_s)
    yn_p, wkv_p = _rwkv_scan(prep_p[:6], jnp.zeros((n_p, RWKV_HEADS, RWKV_HEAD, RWKV_HEAD), F32),
                             chunk=RWKV_CHUNK, group=RWKV_SEQ_GROUP)
    yn_s, wkv_s = _rwkv_scan(prep_s[:6], state["wkv"], chunk=t_s, group=RWKV_SEQ_GROUP)
    branches = [(att_p, att_s), (pool_p, pool_s), (yn_p.reshape(rows_p, RWKV_W), yn_s.reshape(rows_s, RWKV_W)),
                (prep_p[7], prep_s[7]), (prep_p[6], prep_s[6])]

    mix = _mix(branches, z_gate, lw["ln_w"], lw["ln_b"], lw["w_attn_o"], lw["w_pool_o"], lw["w_rwkv_o"], tm=rows_s)
    x = _mm_residual(x, mix, lw["w_out"], tm=384)

    q, xn = _norm_mm3(x, lw["g_ffn"], *lw["peer_wq"], tm=768, tn=512)
    i1, i2, gate = _peer_select(q, *lw["peer_subkeys"])
    act = _peer_act(xn, lw["peer_u"], lw["layer"], i1, i2, tb=PEER_ACT_TB)
    ffn = _peer_out(i1, i2, gate, act, lw["peer_v"], lw["layer"])

    x = _ple(x, ffn, p_rows, lw["g_ple"], lw["ple_wg"], lw["ple_wp"], g_final, tm=rows_s, final=final, rows_prompt=rows_p)

    keep = POOL_HIST - 1
    pool_cols = slice(Z_POOL, Z_POOL + POOL_W)
    rwkv_cols = slice(Z_RWKV, Z_RWKV + RWKV_COLS)
    pool_p_state = jnp.stack([z[(n + 1) * t_p - keep:(n + 1) * t_p, pool_cols] for n in range(n_p)])
    zp_s = z[rows_p:, pool_cols].reshape(n_s, t_s, POOL_W)
    new_p = [pool_p_state, wkv_p, z[t_p - 1:rows_p:t_p, rwkv_cols]]
    new_s = [jnp.concatenate([hist_s[:, 1:], zp_s], axis=1)[:, -keep:], wkv_s, z[rows_p + t_s - 1::t_s, rwkv_cols]]
    return x, kv_p, kv_s, new_p, new_s


def kernel(x_prompt, x_sample, p_prompt, p_sample, cache_attn_w128, cache_attn_w512, cache_attn_w2048, state_pool, state_rwkv_wkv, state_rwkv_shift, g_mix, w_in, w_attn_o, w_pool_o, w_rwkv_o, w_out, pool_w, pool_scale, rwkv_mu, rwkv_w0, rwkv_w2, rwkv_a0, rwkv_a2, rwkv_g2, rwkv_kk, rwkv_ka, rwkv_rk, rwkv_ln_w, rwkv_ln_b, g_ffn, peer_wq, peer_subkeys, peer_u, peer_v, g_ple, ple_wg, ple_wp, g_final):
    n_p, t_p, _ = x_prompt.shape
    n_s, t_s, _ = x_sample.shape
    depth = w_in.shape[0]
    past = PAST_LEN
    rows_p = n_p * t_p
    rows_s = n_s * t_s
    dims = (n_p, t_p, n_s, t_s, past)

    x = jnp.concatenate([x_prompt.reshape(rows_p, D_MODEL), x_sample.reshape(rows_s, D_MODEL)], axis=0)
    cos_p, sin_p = _rope_tables(jnp.arange(t_p, dtype=I32))
    cos_s, sin_s = _rope_tables(past + jnp.arange(t_s, dtype=I32))
    lora_pad = jnp.zeros((128 - 64, RWKV_W), F32)
    peer_v_bf16 = peer_v.astype(BF16)
    to_dev = (0, 1, 3, 4, 5, 2)
    from_dev = (0, 1, 5, 2, 3, 4)
    caches_t = [jnp.transpose(c, to_dev) for c in (cache_attn_w128, cache_attn_w512, cache_attn_w2048)]

    def row(a):
        return a.reshape(1, -1)

    w_in_z = _repack_w_in(w_in, tk=256)

    new_p, new_s = [], []
    kv_p = kv_s = None
    for l in range(depth):
        lw = {
            "g_mix": g_mix[l], "w_in": w_in_z,
            "cos_p": cos_p, "sin_p": sin_p, "cos_s": cos_s, "sin_s": sin_s, "depth": depth,
            "pool_w": pool_w[l].astype(BF16), "pool_scale": pool_scale[l],
            "mu": row(rwkv_mu[l]), "w0": row(rwkv_w0[l]), "a0": row(rwkv_a0[l]),
            "w2p": jnp.concatenate([rwkv_w2[l], lora_pad], axis=0).astype(BF16),
            "a2p": jnp.concatenate([lora_pad, rwkv_a2[l]], axis=0).astype(BF16),
            "g2": rwkv_g2[l].astype(BF16),
            "kk": row(rwkv_kk[l]), "ka": row(rwkv_ka[l]), "rk": row(rwkv_rk[l]),
            "ln_w": rwkv_ln_w[l], "ln_b": rwkv_ln_b[l],
            "w_attn_o": w_attn_o[l].astype(BF16), "w_pool_o": w_pool_o[l].astype(BF16),
            "w_rwkv_o": w_rwkv_o[l].astype(BF16), "w_out": w_out[l].astype(BF16),
            "g_ffn": g_ffn[l], "peer_wq": _split_bf16(peer_wq[l]), "peer_subkeys": _split_bf16(peer_subkeys[l]),
            "peer_u": peer_u, "peer_v": peer_v_bf16, "layer": l,
            "g_ple": g_ple[l], "ple_wg": ple_wg[l].astype(BF16), "ple_wp": ple_wp[l].astype(BF16),
        }
        state = {"caches_t": caches_t, "kv_p": kv_p, "kv_s": kv_s, "pool": state_pool[l],
                 "wkv": state_rwkv_wkv[l], "shift": state_rwkv_shift[l]}
        p_rows = jnp.concatenate([p_prompt[l].reshape(rows_p, -1), p_sample[l].reshape(rows_s, -1)], axis=0).astype(BF16)
        x, kv_p, kv_s, st_p, st_s = _layer(x, p_rows, lw, state, g_final, dims, l == depth - 1)
        new_p.append(st_p)
        new_s.append(st_s)

    y_p, y_s = x
    outs = [y_p.reshape(n_p, t_p, D_MODEL), y_s.reshape(n_s, t_s, D_MODEL)]
    for gi in range(len(ATT_GROUPS)):
        outs.append(jnp.transpose(kv_p[gi], from_dev))
        outs.append(jnp.transpose(kv_s[gi], from_dev))
    for j in range(3):
        outs.append(jnp.stack([s[j] for s in new_p]))
        outs.append(jnp.stack([s[j] for s in new_s]))
    return tuple(outs)
```

```python
import functools
import math

import jax
import jax.numpy as jnp
from jax import lax
from jax.experimental import pallas as pl
from jax.experimental.pallas import tpu as pltpu

F32 = jnp.float32
BF16 = jnp.bfloat16
I32 = jnp.int32

D_MODEL = 2048
RMS_EPS = 1e-6
HEAD_DIM = 64
ATT_GROUPS = ((128, 1), (512, 4), (2048, 16))
HEADS_PER_GROUP = 4
ATT_W = 768
ATT_OUT = 256
ATT_SPAN = 128
ROPE_THETA = 10000.0
POOL_WINDOWS = (2, 4, 8, 16)
POOL_GROUP = 128
POOL_W = 512
POOL_HIST = 16
RWKV_HEAD = 64
RWKV_HEADS = 12
RWKV_W = 768
RWKV_COLS = 2560
RWKV_LORA_OFF = 2304
GN_EPS = 64e-5
RWKV_CHUNK = 64
RWKV_SEQ_GROUP = 4
RWKV_SAMPLE_GROUP = 8
OFF_POOL = 2304
OFF_RWKV = 2816
OFF_GATE = 5376
IN_COLS = 11520
Z_GATE = 0
Z_RWKV = 3 * D_MODEL
Z_POOL = Z_RWKV + RWKV_COLS
Z_QKV = Z_POOL + POOL_W
Z_RWKV_BLOCK = RWKV_COLS + POOL_W
PAST_LEN = 8192
PEER_HEADS = 8
N_KEYS = 128
N_EXPERTS = N_KEYS * N_KEYS
PEER_TOPK = 16
PEER_PAIRS = PEER_HEADS * PEER_TOPK
PEER_TB = 384
PEER_ACT_TB = 1408
PEER_BUILD_UNROLL = 4
PEER_EC = 1024
PEER_OUT_EC = 2048
SUBLANES = 8
VMEM_LIMIT = 56 * 1024 * 1024

NEG_INF = float("-inf")


def _params(*sem):
    return pltpu.CompilerParams(dimension_semantics=sem, vmem_limit_bytes=VMEM_LIMIT)


def _dot(a, b):
    return jnp.dot(a.astype(BF16), b.astype(BF16), preferred_element_type=F32)


def _dot_nt(a, b):
    return lax.dot_general(a.astype(BF16), b.astype(BF16), (((1,), (1,)), ((), ())), preferred_element_type=F32)


def _norm_mm_kernel(x_ref, g_ref, w_ref, o_ref, xn_ref):
    @pl.when(pl.program_id(1) == 0)
    def _():
        x = x_ref[...]
        ms = jnp.mean(x * x, axis=-1, keepdims=True)
        xn_ref[...] = ((x * lax.rsqrt(ms + RMS_EPS)) * g_ref[...]).astype(BF16)

    o_ref[...] = jnp.dot(xn_ref[...], w_ref[0], preferred_element_type=F32)


def _norm_mm(x, g, w, layer, *, tm, tn):
    rows, k = x.shape
    n = w.shape[2]
    return pl.pallas_call(
        _norm_mm_kernel, grid=(rows // tm, n // tn),
        in_specs=[
            pl.BlockSpec((tm, k), lambda i, j: (i, 0)),
            pl.BlockSpec((1, k), lambda i, j: (0, 0)),
            pl.BlockSpec((1, k, tn), lambda i, j: (layer, 0, j)),
        ],
        out_specs=pl.BlockSpec((tm, tn), lambda i, j: (i, j)),
        out_shape=jax.ShapeDtypeStruct((rows, n), F32),
        scratch_shapes=[pltpu.VMEM((tm, k), BF16)],
        compiler_params=_params("arbitrary", "arbitrary"), name="norm_mm",
    )(x, g.reshape(1, k), w)


Z_SEGMENTS = ((OFF_GATE, IN_COLS), (OFF_RWKV, OFF_GATE), (OFF_POOL, OFF_RWKV), (0, OFF_POOL))


def _repack_kernel(w_ref, o_ref):
    dst = 0
    for a, b in Z_SEGMENTS:
        o_ref[0, :, dst:dst + (b - a)] = w_ref[0, :, a:b].astype(BF16)
        dst += b - a


def _repack_w_in(w_in, *, tk):
    depth, k, n = w_in.shape
    spec = pl.BlockSpec((1, tk, n), lambda l, i: (l, i, 0))
    return pl.pallas_call(
        _repack_kernel, grid=(depth, k // tk), in_specs=[spec], out_specs=spec,
        out_shape=jax.ShapeDtypeStruct(w_in.shape, BF16),
        compiler_params=_params("arbitrary", "arbitrary"), name="repack_w_in",
    )(w_in)


def _split_bf16(a):
    hi = a.astype(BF16)
    lo = (a - hi.astype(F32)).astype(BF16)
    return hi, lo


def _dot3(ah, al, bh, bl, dims=(((1,), (0,)), ((), ()))):
    def d(p, q):
        return lax.dot_general(p, q, dims, preferred_element_type=F32)

    return d(ah, bh) + (d(ah, bl) + d(al, bh))


def _norm_mm3_kernel(x_ref, g_ref, wh_ref, wl_ref, o_ref, xh_ref, xl_ref):
    @pl.when(pl.program_id(1) == 0)
    def _():
        x = x_ref[...]
        ms = jnp.mean(x * x, axis=-1, keepdims=True)
        xh, xl = _split_bf16((x * lax.rsqrt(ms + RMS_EPS)) * g_ref[...])
        xh_ref[...] = xh
        xl_ref[...] = xl

    o_ref[...] = _dot3(xh_ref[...], xl_ref[...], wh_ref[...], wl_ref[...])


def _norm_mm3(x, g, wh, wl, *, tm, tn):
    rows, k = x.shape
    n = wh.shape[1]
    wspec = pl.BlockSpec((k, tn), lambda i, j: (0, j))
    return pl.pallas_call(
        _norm_mm3_kernel, grid=(rows // tm, n // tn),
        in_specs=[pl.BlockSpec((tm, k), lambda i, j: (i, 0)), pl.BlockSpec((1, k), lambda i, j: (0, 0)), wspec, wspec],
        out_specs=[pl.BlockSpec((tm, tn), lambda i, j: (i, j)), pl.BlockSpec((tm, k), lambda i, j: (i, 0))],
        out_shape=[jax.ShapeDtypeStruct((rows, n), F32), jax.ShapeDtypeStruct((rows, k), BF16)],
        scratch_shapes=[pltpu.VMEM((tm, k), BF16)],
        compiler_params=_params("arbitrary", "arbitrary"), name="norm_mm3",
    )(x, g.reshape(1, k), wh, wl)


def _rotate(x, cos, sin):
    lane = lax.broadcasted_iota(I32, x.shape, 1)
    first_half = (lane & (HEAD_DIM - 1)) < HEAD_DIM // 2
    partner = jnp.where(first_half, pltpu.roll(x, 128 - HEAD_DIM // 2, 1), pltpu.roll(x, HEAD_DIM // 2, 1))
    return x * cos + partner * sin


def _rope_tables(pos):
    half = HEAD_DIM // 2
    inv = ROPE_THETA ** (-jnp.arange(half, dtype=F32) / half)
    ang = pos.astype(F32)[:, None] * inv[None, :]
    cos = jnp.cos(ang)
    sin = jnp.sin(ang)
    cos_t = jnp.concatenate([cos, cos, cos, cos], axis=1)
    sin_t = jnp.concatenate([-sin, sin, -sin, sin], axis=1)
    return cos_t, sin_t


ROT_ROWS = 256


def _prompt_attn_kernel(q_ref, k_ref, v_ref, cos_ref, sin_ref, *rest, dil, seq, keep):
    o_ref, l_ref, kv_ref, qs, ks, vs = rest[-6:]
    chunks = ATT_OUT // 128
    for i in range(seq // ROT_ROWS):
        rs = slice(i * ROT_ROWS, (i + 1) * ROT_ROWS)
        cos = cos_ref[rs, :]
        sin = sin_ref[rs, :]
        for c in range(chunks):
            cs = slice(c * 128, (c + 1) * 128)
            qs[c, rs, :] = _rotate(q_ref[rs, cs], cos, sin)
            ks[c, rs, :] = _rotate(k_ref[rs, cs], cos, sin)
            vs[c, rs, :] = v_ref[rs, cs]
    for c in range(chunks):
        for j in range(keep // 128):
            ps = slice(seq - keep + j * 128, seq - keep + (j + 1) * 128)
            k_t = ks[c, ps, :].T
            v_t = vs[c, ps, :].T
            for half in range(2):
                hs = slice(half * HEAD_DIM, (half + 1) * HEAD_DIM)
                kv_ref[0, 0, 0, 2 * c + half, :, j * 128:(j + 1) * 128] = k_t[hs, :]
                kv_ref[0, 0, 1, 2 * c + half, :, j * 128:(j + 1) * 128] = v_t[hs, :]

    def rows_of(start):
        return pl.ds(start, ATT_SPAN, stride=dil) if dil > 1 else pl.ds(start, ATT_SPAN)

    qi = lax.broadcasted_iota(I32, (ATT_SPAN, ATT_SPAN), 0)
    ki = lax.broadcasted_iota(I32, (ATT_SPAN, ATT_SPAN), 1)
    cur_ok = ki <= qi
    prev_ok = ki >= qi
    first_head = ki < HEAD_DIM
    ones = jnp.ones((ATT_SPAN, 128), BF16)
    scale = HEAD_DIM ** -0.5
    n_blocks = seq // dil // ATT_SPAN
    for r in range(dil):
        for b in range(n_blocks):
            cur = rows_of(r + dil * b * ATT_SPAN)
            for c in range(chunks):
                q2, kc2, vc2 = qs[c, cur, :], ks[c, cur, :], vs[c, cur, :]
                if b > 0:
                    prev = rows_of(r + dil * (b - 1) * ATT_SPAN)
                    kp2, vp2 = ks[c, prev, :], vs[c, prev, :]
                outs, lses = [], []
                for half in range(2):
                    q = jnp.where(first_head == (half == 0), q2, 0.0)
                    sc = jnp.where(cur_ok, _dot_nt(q, kc2) * scale, NEG_INF)
                    m = jnp.max(sc, axis=-1, keepdims=True)
                    if b > 0:
                        sp = jnp.where(prev_ok, _dot_nt(q, kp2) * scale, NEG_INF)
                        m = jnp.maximum(m, jnp.max(sp, axis=-1, keepdims=True))
                    ec = jnp.exp(sc - m).astype(BF16)
                    den = jnp.dot(ec, ones, preferred_element_type=F32)
                    acc = _dot(ec, vc2)
                    if b > 0:
                        ep = jnp.exp(sp - m).astype(BF16)
                        den = den + jnp.dot(ep, ones, preferred_element_type=F32)
                        acc = acc + _dot(ep, vp2)
                    outs.append(acc / den)
                    lses.append(m + jnp.log(den))
                o_ref[c, cur, :] = jnp.where(first_head, outs[0], outs[1])
                l_ref[c, cur, :] = jnp.where(first_head, lses[0], lses[1])


def _prompt_attn(z_qkv, cos_t, sin_t, gi, dil, layer, kv_prev, *, depth, n_seq, seq):
    rows_p = n_seq * seq
    win = ATT_GROUPS[gi][0]
    keep = min(win, seq)
    assert seq % (dil * ATT_SPAN) == 0 and keep % 128 == 0 and seq % ROT_ROWS == 0
    tab = pl.BlockSpec((seq, 128), lambda n: (0, 0))
    chunked = pl.BlockSpec((ATT_OUT // 128, seq, 128), lambda n: (0, n, 0))
    kv_shape = (depth, n_seq, 2, HEADS_PER_GROUP, HEAD_DIM, keep)
    kv_spec = pl.BlockSpec((1, 1) + kv_shape[2:], lambda n: (layer, n, 0, 0, 0, 0))
    c0 = Z_QKV // ATT_OUT
    in_specs = [pl.BlockSpec((seq, ATT_OUT), lambda n: (n, c0 + gi)),
                pl.BlockSpec((seq, ATT_OUT), lambda n: (n, c0 + ATT_W // ATT_OUT + gi)),
                pl.BlockSpec((seq, ATT_OUT), lambda n: (n, c0 + 2 * ATT_W // ATT_OUT + gi)), tab, tab]
    args = [z_qkv, z_qkv, z_qkv, cos_t, sin_t]
    aliases = {}
    if kv_prev is not None:
        in_specs.append(pl.BlockSpec(memory_space=pl.ANY))
        args.append(kv_prev)
        aliases = {len(args) - 1: 2}
    chunk_shape = jax.ShapeDtypeStruct((ATT_OUT // 128, rows_p, 128), F32)
    return pl.pallas_call(
        functools.partial(_prompt_attn_kernel, dil=dil, seq=seq, keep=keep),
        grid=(n_seq,), in_specs=in_specs,
        out_specs=[chunked, chunked, kv_spec],
        out_shape=[chunk_shape, chunk_shape, jax.ShapeDtypeStruct(kv_shape, F32)],
        scratch_shapes=[pltpu.VMEM((ATT_OUT // 128, seq, 128), F32)] * 3,
        input_output_aliases=aliases,
        compiler_params=_params("arbitrary"), name=f"prompt_attn_g{gi}",
    )(*args)


def _merge_groups(os_, ls_):
    m = jnp.maximum(jnp.maximum(ls_[0], ls_[1]), ls_[2])
    es = [jnp.exp(l - m) for l in ls_]
    tot = es[0] + es[1] + es[2]
    return (es[0] / tot) * os_[0] + (es[1] / tot) * os_[1] + (es[2] / tot) * os_[2]


def _merge_kernel(o0, o1, o2, l0, l1, l2, a_ref):
    for c in range(ATT_OUT // 128):
        a_ref[:, c * 128:(c + 1) * 128] = _merge_groups([o0[c], o1[c], o2[c]], [l0[c], l1[c], l2[c]])


def _merge_attn(os_, ls_, *, tm):
    rows = os_[0].shape[1]
    spec = pl.BlockSpec((ATT_OUT // 128, tm, 128), lambda i: (0, i, 0))
    return pl.pallas_call(
        _merge_kernel, grid=(rows // tm,), in_specs=[spec] * 6, out_specs=pl.BlockSpec((tm, ATT_OUT), lambda i: (i, 0)),
        out_shape=jax.ShapeDtypeStruct((rows, ATT_OUT), F32),
        compiler_params=_params("arbitrary"), name="merge_attn",
    )(*os_, *ls_)


def _sample_attn_kernel(z_ref, cos_ref, sin_ref, *rest, t_new):
    caches, (a_ref, n0_ref, n1_ref, n2_ref) = rest[0:3], rest[-4:]
    scale = HEAD_DIM ** -0.5
    cos = cos_ref[...]
    sin = sin_ref[...]
    n_chunks = ATT_W // 128
    q_c = [_rotate(z_ref[:, c * 128:(c + 1) * 128], cos, sin) for c in range(n_chunks)]
    k_c = [_rotate(z_ref[:, ATT_W + c * 128:ATT_W + (c + 1) * 128], cos, sin) for c in range(n_chunks)]
    v_c = [z_ref[:, 2 * ATT_W + c * 128:2 * ATT_W + (c + 1) * 128] for c in range(n_chunks)]
    outs_g, lses_g = [], []
    for gi, (c_ref, n_ref, (win, dil)) in enumerate(zip(caches, (n0_ref, n1_ref, n2_ref), ATT_GROUPS)):
        cache_len = c_ref.shape[-1]
        t_c = lax.broadcasted_iota(I32, (t_new, cache_len), 0)
        c_c = lax.broadcasted_iota(I32, (t_new, cache_len), 1)
        d_c = cache_len + t_c - c_c
        ok_c = jnp.logical_and((d_c & (dil - 1)) == 0, d_c <= ATT_SPAN * dil)
        t_n = lax.broadcasted_iota(I32, (t_new, t_new), 0)
        u_n = lax.broadcasted_iota(I32, (t_new, t_new), 1)
        d_n = t_n - u_n
        ok_n = jnp.logical_and(d_n >= 0, (d_n & (dil - 1)) == 0)
        outs, lses = [], []
        for h in range(HEADS_PER_GROUP):
            chunk, half = divmod(gi * HEADS_PER_GROUP + h, 2)
            hs = slice(half * HEAD_DIM, (half + 1) * HEAD_DIM)
            q, k_new, v_new = q_c[chunk][:, hs], k_c[chunk][:, hs], v_c[chunk][:, hs]
            k_t = c_ref[0, 0, 0, h]
            v_t = c_ref[0, 0, 1, h]
            s_c = jnp.where(ok_c, _dot(q, k_t) * scale, NEG_INF)
            s_n = jnp.where(ok_n, _dot_nt(q, k_new) * scale, NEG_INF)
            m = jnp.maximum(jnp.max(s_c, axis=-1, keepdims=True), jnp.max(s_n, axis=-1, keepdims=True))
            e_c = jnp.exp(s_c - m)
            e_n = jnp.exp(s_n - m)
            den = jnp.sum(e_c, axis=-1, keepdims=True) + jnp.sum(e_n, axis=-1, keepdims=True)
            acc = _dot_nt(e_c, v_t) + _dot(e_n, v_new)
            outs.append(acc / den)
            lses.append(jnp.broadcast_to(m + jnp.log(den), (t_new, HEAD_DIM)))
            n_ref[0, 0, 0, h] = jnp.concatenate([k_t[:, t_new:], k_new.T], axis=1)
            n_ref[0, 0, 1, h] = jnp.concatenate([v_t[:, t_new:], v_new.T], axis=1)
        outs_g.append(jnp.concatenate(outs, axis=1))
        lses_g.append(jnp.concatenate(lses, axis=1))
    a_ref[...] = _merge_groups(outs_g, lses_g)


def _sample_attn(z_qkv, cos_s, sin_s, caches_t, layer, new_prev, *, row0, n_seq, t_new):
    b0 = row0 // t_new

    def cache_spec(c):
        return pl.BlockSpec((1, 1) + c.shape[2:], lambda n: (layer, n, 0, 0, 0, 0))

    tab = pl.BlockSpec((t_new, 128), lambda n: (0, 0))
    in_specs = ([pl.BlockSpec((t_new, 3 * ATT_W), lambda n: (b0 + n, Z_QKV // (3 * ATT_W))), tab, tab]
                + [cache_spec(c) for c in caches_t])
    args = [z_qkv, cos_s, sin_s, *caches_t]
    aliases = {}
    if new_prev is not None:
        for j, a in enumerate(new_prev):
            in_specs.append(pl.BlockSpec(memory_space=pl.ANY))
            args.append(a)
            aliases[len(args) - 1] = 1 + j
    return pl.pallas_call(
        functools.partial(_sample_attn_kernel, t_new=t_new),
        grid=(n_seq,), in_specs=in_specs,
        out_specs=[pl.BlockSpec((t_new, ATT_OUT), lambda n: (n, 0))] + [cache_spec(c) for c in caches_t],
        out_shape=[jax.ShapeDtypeStruct((n_seq * t_new, ATT_OUT), F32)]
        + [jax.ShapeDtypeStruct(c.shape, F32) for c in caches_t],
        input_output_aliases=aliases,
        compiler_params=_params("arbitrary"), name="sample_attn",
    )(*args)


def _pool_kernel(z_ref, h_ref, w_ref, s_ref, o_ref, buf_ref, *, t_len, pos0):
    buf_ref[0:POOL_HIST, :] = h_ref[0]
    buf_ref[POOL_HIST:POOL_HIST + t_len, :] = z_ref[...]
    pos = pos0 + lax.broadcasted_iota(I32, (t_len, POOL_GROUP), 0)
    for g, win in enumerate(POOL_WINDOWS):
        cs = slice(g * POOL_GROUP, (g + 1) * POOL_GROUP)
        z = buf_ref[POOL_HIST:POOL_HIST + t_len, cs]
        wsum = z
        for i in range(1, win):
            wsum = wsum + buf_ref[POOL_HIST - i:POOL_HIST - i + t_len, cs]
        cnt = jnp.minimum(win, pos + 1).astype(F32)
        y = wsum / cnt - z
        o_ref[:, cs] = _dot(y, w_ref[g]) * s_ref[:, cs]


def _pool_mixer(z_pool, hist, pool_w, pool_scale, *, row0, n_seq, t_len, pos0):
    b0 = row0 // t_len
    return pl.pallas_call(
        functools.partial(_pool_kernel, t_len=t_len, pos0=pos0),
        grid=(n_seq,),
        in_specs=[
            pl.BlockSpec((t_len, POOL_W), lambda n: (b0 + n, Z_POOL // POOL_W)),
            pl.BlockSpec((1, POOL_HIST, POOL_W), lambda n: (n, 0, 0)),
            pl.BlockSpec((len(POOL_WINDOWS), POOL_GROUP, POOL_GROUP), lambda n: (0, 0, 0)),
            pl.BlockSpec((1, POOL_W), lambda n: (0, 0)),
        ],
        out_specs=pl.BlockSpec((t_len, POOL_W), lambda n: (n, 0)),
        out_shape=jax.ShapeDtypeStruct((n_seq * t_len, POOL_W), F32),
        scratch_shapes=[pltpu.VMEM((POOL_HIST + t_len, POOL_W), F32)],
        compiler_params=_params("arbitrary"), name="pool_mixer",
    )(z_pool, hist, pool_w, pool_scale.reshape(1, POOL_W))


def _rwkv_prep_kernel(z_ref, zp_ref, first_ref, mu_ref, w0_ref, w2_ref, a0_ref, a2_ref, g2_ref, kk_ref, ka_ref, rk_ref,
                      r_o, lw_o, k_o, v_o, kn_o, b_o, g_o, bonus_o, buf_ref, *, tm):
    i = pl.program_id(1)
    z = z_ref[:, 0:RWKV_COLS]
    prev_row = jnp.where(i == 0, first_ref[0], zp_ref[SUBLANES - 1:SUBLANES, 0:RWKV_COLS])
    buf_ref[SUBLANES:SUBLANES + tm, :] = z
    buf_ref[SUBLANES - 1:SUBLANES, :] = prev_row
    shifted = buf_ref[SUBLANES - 1:SUBLANES - 1 + tm, :]
    xm = z + mu_ref[...] * (shifted - z)
    r = xm[:, 0:RWKV_W]
    k = xm[:, RWKV_W:2 * RWKV_W]
    v = xm[:, 2 * RWKV_W:3 * RWKV_W]
    wa = xm[:, RWKV_LORA_OFF:RWKV_LORA_OFF + 128]
    gl = xm[:, RWKV_LORA_OFF + 128:RWKV_COLS]
    xw = w0_ref[...] + _dot(jnp.tanh(wa), w2_ref[...])
    logw = -math.exp(-0.5) * jax.nn.sigmoid(xw)
    a = jax.nn.sigmoid(a0_ref[...] + _dot(wa, a2_ref[...]))
    g_o[...] = _dot(jax.nn.sigmoid(gl), g2_ref[...])
    kkr = k * kk_ref[...]
    kmod = k * (1.0 + (a - 1.0) * ka_ref[...])
    rkk = r * kmod * rk_ref[...]
    bonus = []
    for h in range(RWKV_HEADS):
        sl = slice(h * RWKV_HEAD, (h + 1) * RWKV_HEAD)
        kh = kkr[:, sl]
        nrm = jnp.sqrt(jnp.sum(kh * kh, axis=-1, keepdims=True))
        kn = kh / jnp.maximum(nrm, 1e-12)
        r_o[0, h] = r[:, sl]
        lw_o[0, h] = logw[:, sl]
        k_o[0, h] = kmod[:, sl]
        v_o[0, h] = v[:, sl]
        kn_o[0, h] = kn
        b_o[0, h] = kn * a[:, sl]
        bonus.append(jnp.sum(rkk[:, sl], axis=-1, keepdims=True) * v[:, sl])
    bonus_o[...] = jnp.concatenate(bonus, axis=1)


def _rwkv_prep(z, first_prev, lw, *, row0, n_seq, t_len, tm):
    nblk = t_len // tm
    pb = tm // SUBLANES
    b0 = row0 // tm
    p0 = row0 // SUBLANES

    def vec(n):
        return pl.BlockSpec((1, n), lambda s, i: (0, 0))

    hm = jax.ShapeDtypeStruct((n_seq, RWKV_HEADS, t_len, RWKV_HEAD), F32)
    rm = jax.ShapeDtypeStruct((n_seq * t_len, RWKV_W), F32)
    hm_spec = pl.BlockSpec((1, RWKV_HEADS, tm, RWKV_HEAD), lambda s, i: (s, 0, i, 0))
    rm_spec = pl.BlockSpec((tm, RWKV_W), lambda s, i: (s * nblk + i, 0))
    return pl.pallas_call(
        functools.partial(_rwkv_prep_kernel, tm=tm),
        grid=(n_seq, nblk),
        in_specs=[
            pl.BlockSpec((tm, Z_RWKV_BLOCK), lambda s, i: (b0 + s * nblk + i, Z_RWKV // Z_RWKV_BLOCK)),
            pl.BlockSpec((SUBLANES, Z_RWKV_BLOCK),
                         lambda s, i: (jnp.maximum(p0 + (s * nblk + i) * pb - 1, 0), Z_RWKV // Z_RWKV_BLOCK)),
            pl.BlockSpec((1, 1, RWKV_COLS), lambda s, i: (s, 0, 0)),
            vec(RWKV_COLS), vec(RWKV_W),
            pl.BlockSpec((128, RWKV_W), lambda s, i: (0, 0)),
            vec(RWKV_W),
            pl.BlockSpec((128, RWKV_W), lambda s, i: (0, 0)),
            pl.BlockSpec((128, RWKV_W), lambda s, i: (0, 0)),
            vec(RWKV_W), vec(RWKV_W), vec(RWKV_W),
        ],
        out_specs=[hm_spec] * 6 + [rm_spec] * 2,
        out_shape=[hm] * 6 + [rm] * 2,
        scratch_shapes=[pltpu.VMEM((tm + SUBLANES, RWKV_COLS), F32)],
        compiler_params=_params("arbitrary", "arbitrary"), name="rwkv_prep",
    )(z, z, first_prev, lw["mu"], lw["w0"], lw["w2p"], lw["a0"], lw["a2p"], lw["g2"], lw["kk"], lw["ka"], lw["rk"])


def _bmm(a, b):
    return jnp.einsum("hqk,hkd->hqd", a.astype(BF16), b.astype(BF16), preferred_element_type=F32)


def _bmm_nt(a, b):
    return jnp.einsum("hqd,hkd->hqk", a.astype(BF16), b.astype(BF16), preferred_element_type=F32)


def _bmm_tn(a, b):
    return jnp.einsum("hkq,hkd->hqd", a.astype(BF16), b.astype(BF16), preferred_element_type=F32)


def _rwkv_scan_kernel(r_ref, lw_ref, k_ref, v_ref, kn_ref, b_ref, s0_ref, y_ref, st_ref, s_scr, *, chunk, group):
    c = pl.program_id(1)
    nh = group * RWKV_HEADS

    def heads(ref):
        return ref[...].reshape((nh,) + ref.shape[2:])

    @pl.when(c == 0)
    def _():
        s_scr[...] = heads(s0_ref)

    r, logw, k, v, kn, b = [heads(ref) for ref in (r_ref, lw_ref, k_ref, v_ref, kn_ref, b_ref)]
    row = lax.broadcasted_iota(I32, (chunk, chunk), 0)
    col = lax.broadcasted_iota(I32, (chunk, chunk), 1)
    incl = row >= col
    strict = row > col
    tri = jnp.broadcast_to(incl.astype(BF16)[None], (nh, chunk, chunk))
    lw_hi = logw.astype(BF16)
    lw_lo = (logw - lw_hi.astype(F32)).astype(BF16)
    cum = (jnp.einsum("hqk,hkd->hqd", tri, lw_hi, preferred_element_type=F32)
           + jnp.einsum("hqk,hkd->hqd", tri, lw_lo, preferred_element_type=F32))
    p_inv = jnp.exp(-cum)
    kt = k * p_inv
    bt = b * p_inv
    kap = kn * jnp.exp(cum - logw)
    rho = r * jnp.exp(cum)
    qq = jnp.concatenate([kap, rho], axis=1)
    gram = _bmm_nt(qq, jnp.concatenate([kt, bt], axis=1))
    a_k = jnp.where(strict[None], gram[:, :chunk, :chunk], 0.0)
    a_b = jnp.where(strict[None], gram[:, :chunk, chunk:], 0.0)
    l_k = jnp.where(incl[None], gram[:, chunk:, :chunk], 0.0)
    l_b = jnp.where(incl[None], gram[:, chunk:, chunk:], 0.0)
    x = jnp.broadcast_to((row == col).astype(F32)[None], (nh, chunk, chunk))
    m = 1
    while m < chunk:
        sh = m.bit_length() - 1
        same = (row >> (sh + 1)) == (col >> (sh + 1))
        lower_left = jnp.logical_and(((row >> sh) & 1) == 1, ((col >> sh) & 1) == 0)
        off = jnp.where(jnp.logical_and(same, lower_left)[None], a_b, 0.0)
        x = x - _bmm(_bmm(x, off), x)
        m *= 2
    s = s_scr[...]
    qs = _bmm_nt(qq, s)
    u = _bmm(x, -(qs[:, :chunk] + _bmm(a_k, v)))
    y = qs[:, chunk:] + _bmm(jnp.concatenate([l_k, l_b], axis=2), jnp.concatenate([v, u], axis=1))
    s_new = (s + _bmm_tn(jnp.concatenate([v, u], axis=1), jnp.concatenate([kt, bt], axis=1))) * jnp.exp(cum[:, chunk - 1:chunk, :])
    s_scr[...] = s_new
    st_ref[...] = s_new.reshape(st_ref.shape)
    mean = jnp.mean(y, axis=-1, keepdims=True)
    var = jnp.mean(jnp.square(y - mean), axis=-1, keepdims=True)
    yn = (y - mean) * lax.rsqrt(var + GN_EPS)
    for g in range(group):
        y_ref[g] = jnp.concatenate([yn[g * RWKV_HEADS + h] for h in range(RWKV_HEADS)], axis=1)


def _rwkv_scan(prep, s0, *, chunk, group):
    r, lw, k, v, kn, b = prep
    n_seq, nh, t_len, hd = r.shape
    hm_spec = pl.BlockSpec((group, nh, chunk, hd), lambda s, c: (s, 0, c, 0))
    st_spec = pl.BlockSpec((group, nh, hd, hd), lambda s, c: (s, 0, 0, 0))
    return pl.pallas_call(
        functools.partial(_rwkv_scan_kernel, chunk=chunk, group=group),
        grid=(n_seq // group, t_len // chunk),
        in_specs=[hm_spec] * 6 + [st_spec],
        out_specs=[pl.BlockSpec((group, chunk, RWKV_W), lambda s, c: (s, c, 0)), st_spec],
        out_shape=[jax.ShapeDtypeStruct((n_seq, t_len, RWKV_W), F32), jax.ShapeDtypeStruct((n_seq, nh, hd, hd), F32)],
        scratch_shapes=[pltpu.VMEM((group * nh, hd, hd), F32)],
        compiler_params=_params("arbitrary", "arbitrary"), name="rwkv_scan",
    )(r, lw, k, v, kn, b, s0)


def _mix_kernel(*refs, n_prompt_blocks):
    pairs, (gate_ref, lnw_ref, lnb_ref, wa_ref, wp_ref, wr_ref, o_ref) = refs[:10], refs[10:]
    is_sample = pl.program_id(0) >= n_prompt_blocks
    att, pool, yn, bonus, g = [jnp.where(is_sample, pairs[2 * j + 1][...], pairs[2 * j][...]) for j in range(5)]
    rw = (yn * lnw_ref[...] + lnb_ref[...] + bonus) * g
    mix = jax.nn.sigmoid(gate_ref[:, 0:D_MODEL]) * _dot(att, wa_ref[...])
    mix = mix + jax.nn.sigmoid(gate_ref[:, D_MODEL:2 * D_MODEL]) * _dot(pool, wp_ref[...])
    mix = mix + jax.nn.sigmoid(gate_ref[:, 2 * D_MODEL:3 * D_MODEL]) * _dot(rw, wr_ref[...])
    o_ref[...] = mix.astype(BF16)


def _mix(branches, z_gate, ln_w, ln_b, wa, wp, wr, *, tm):
    rows = z_gate.shape[0]
    n_prompt_blocks = branches[0][0].shape[0] // tm
    assert rows == (n_prompt_blocks + 1) * tm and all(s.shape[0] == tm for _, s in branches)

    def full(a):
        return pl.BlockSpec(a.shape, lambda i: (0, 0))

    specs, args = [], []
    for p, s in branches:
        specs += [pl.BlockSpec((tm, p.shape[1]), lambda i: (jnp.minimum(i, n_prompt_blocks - 1), 0)), full(s)]
        args += [p, s]
    ln_w = ln_w.reshape(1, RWKV_W)
    ln_b = ln_b.reshape(1, RWKV_W)
    return pl.pallas_call(
        functools.partial(_mix_kernel, n_prompt_blocks=n_prompt_blocks), grid=(rows // tm,),
        in_specs=specs + [pl.BlockSpec((tm, 3 * D_MODEL), lambda i: (i, 0)),
                          full(ln_w), full(ln_b), full(wa), full(wp), full(wr)],
        out_specs=pl.BlockSpec((tm, D_MODEL), lambda i: (i, 0)),
        out_shape=jax.ShapeDtypeStruct((rows, D_MODEL), BF16),
        compiler_params=_params("arbitrary"), name="gated_mix",
    )(*args, z_gate, ln_w, ln_b, wa, wp, wr)


def _mm_res_kernel(x_ref, a_ref, w_ref, o_ref):
    o_ref[...] = x_ref[...] + jnp.dot(a_ref[...], w_ref[...], preferred_element_type=F32)


def _mm_residual(x, a, w, *, tm):
    rows, n = x.shape
    k = a.shape[1]
    return pl.pallas_call(
        _mm_res_kernel, grid=(rows // tm,),
        in_specs=[pl.BlockSpec((tm, n), lambda i: (i, 0)), pl.BlockSpec((tm, k), lambda i: (i, 0)),
                  pl.BlockSpec((k, n), lambda i: (0, 0))],
        out_specs=pl.BlockSpec((tm, n), lambda i: (i, 0)),
        out_shape=jax.ShapeDtypeStruct((rows, n), F32),
        compiler_params=_params("arbitrary"), name="out_proj",
    )(x, a, w)


def _topk_cols(scores, ids, ids_ordered, k):
    lanes = scores[0].shape[1]
    iota_k = lax.broadcasted_iota(I32, (k, lanes), 0)
    big = jnp.iinfo(jnp.int32).max

    def arg_max(s, ident, ordered):
        if ordered:
            tiles = [(s[i:i + SUBLANES], ident[i:i + SUBLANES]) for i in range(0, s.shape[0], SUBLANES)]
            while len(tiles) > 1:
                merged = []
                for (va, ia), (vb, ib) in zip(tiles[0::2], tiles[1::2]):
                    later = vb > va
                    merged.append((jnp.where(later, vb, va), jnp.where(later, ib, ia)))
                tiles = merged + tiles[len(tiles) - len(tiles) % 2:]
            s, ident = tiles[0]
        m = jnp.max(s, axis=0, keepdims=True)
        return m, jnp.min(jnp.where(s == m, ident, big), axis=0, keepdims=True)

    def body(j, carry):
        out = []
        for (s, vals, idxs), ident, ordered in zip(carry, ids, ids_ordered):
            m, idx = arg_max(s, ident, ordered)
            vals = jnp.where(iota_k == j, m, vals)
            idxs = jnp.where(iota_k == j, idx, idxs)
            out.append((jnp.where(ident == idx, NEG_INF, s), vals, idxs))
        return tuple(out)

    init = tuple((s, jnp.zeros((k, lanes), F32), jnp.zeros((k, lanes), I32)) for s in scores)
    return [(v, i) for _, v, i in lax.fori_loop(0, k, body, init)]


def _gather_rows(table, sel, k):
    out = jnp.zeros(sel.shape, table.dtype)
    for a in range(k):
        out = jnp.where(sel == a, table[a:a + 1, :], out)
    return out


def _peer_select_kernel(q_ref, skh_ref, skl_ref, i1_o, i2_o, gate_o, i1_s, i2_s, g_s, *, tok):
    kk = PEER_TOPK
    half = kk // 2
    nt = (((1,), (1,)), ((), ()))
    key_id = lax.broadcasted_iota(I32, (N_KEYS, tok), 0)
    row = lax.broadcasted_iota(I32, (half * half + kk, tok), 0)
    cand_id = jnp.where(row < half * half, (row >> (half.bit_length() - 1)) * kk + (row & (half - 1)),
                        jnp.where(row < half * half + half, row - half * half + half, (row - half * half) * kk))

    def finish(slot, top, sel, k1, k2):
        e1 = _gather_rows(k1, sel >> 4, kk)
        e2 = _gather_rows(k2, sel & (kk - 1), kk)
        ex = jnp.exp(top - jnp.max(top, axis=0, keepdims=True))
        gate = ex / jnp.sum(ex, axis=0, keepdims=True)
        r0 = pl.multiple_of(slot * kk, kk)
        i1_s[pl.ds(r0, kk), :] = e1.astype(F32)
        i2_s[pl.ds(r0, kk), :] = e2.astype(F32)
        g_s[pl.ds(r0, kk), :] = gate

    def head(h, carry):
        cand_prev, k1_prev, k2_prev = carry
        c0 = pl.multiple_of(h * 2 * N_KEYS, 2 * N_KEYS)
        q1h, q1l = _split_bf16(q_ref[:, pl.ds(c0, N_KEYS)])
        q2h, q2l = _split_bf16(q_ref[:, pl.ds(c0 + N_KEYS, N_KEYS)])
        s1 = _dot3(skh_ref[h, 0], skl_ref[h, 0], q1h, q1l, nt)
        s2 = _dot3(skh_ref[h, 1], skl_ref[h, 1], q2h, q2l, nt)
        (t1, k1), (t2, k2), (top, sel) = _topk_cols([s1, s2, cand_prev], [key_id, key_id, cand_id],
                                                    [True, True, False], kk)
        finish(jnp.where(h == 0, PEER_HEADS, h - 1), top, sel, k1_prev, k2_prev)
        cand = jnp.concatenate([t1[a:a + 1, :] + t2[0:half, :] for a in range(half)]
                               + [t1[0:1, :] + t2[half:kk, :], t1[half:kk, :] + t2[0:1, :]], axis=0)
        return cand, k1, k2

    zeros_i = jnp.zeros((kk, tok), I32)
    cand, k1, k2 = lax.fori_loop(0, PEER_HEADS, head, (jnp.zeros(cand_id.shape, F32), zeros_i, zeros_i))
    ((top, sel),) = _topk_cols([cand], [cand_id], [False], kk)
    finish(PEER_HEADS - 1, top, sel, k1, k2)
    i1_o[...] = i1_s[0:PEER_PAIRS, :].T.astype(I32)
    i2_o[...] = i2_s[0:PEER_PAIRS, :].T.astype(I32)
    gate_o[...] = g_s[0:PEER_PAIRS, :].T


def _peer_select(q, sk_hi, sk_lo, *, tok=128):
    rows = q.shape[0]
    spec = pl.BlockSpec((tok, PEER_PAIRS), lambda i: (i, 0))
    return pl.pallas_call(
        functools.partial(_peer_select_kernel, tok=tok), grid=(rows // tok,),
        in_specs=[pl.BlockSpec((tok, q.shape[1]), lambda i: (i, 0)),
                  pl.BlockSpec(sk_hi.shape, lambda i: (0, 0, 0, 0)), pl.BlockSpec(sk_lo.shape, lambda i: (0, 0, 0, 0))],
        out_specs=[spec] * 3,
        out_shape=[jax.ShapeDtypeStruct((rows, PEER_PAIRS), I32), jax.ShapeDtypeStruct((rows, PEER_PAIRS), I32),
                   jax.ShapeDtypeStruct((rows, PEER_PAIRS), F32)],
        scratch_shapes=[pltpu.VMEM((PEER_PAIRS + PEER_TOPK, tok), F32)] * 3,
        compiler_params=_params("arbitrary"), name="peer_select",
    )(q, sk_hi, sk_lo)


def _peer_pick(d_ref, i1_ref, i2_ref, act_ref, first_row):
    n_rows = d_ref.shape[1] // N_KEYS
    for g in range(act_ref.shape[0] // SUBLANES):
        rs = slice(g * SUBLANES, (g + 1) * SUBLANES)
        i1 = i1_ref[rs, :]
        i2 = i2_ref[rs, :]
        acc = act_ref[rs, :]
        for j in range(n_rows):
            got = jnp.take_along_axis(d_ref[rs, j * N_KEYS:(j + 1) * N_KEYS], i2, axis=1, mode="promise_in_bounds")
            acc = acc + jnp.where(i1 == first_row + j, got, 0.0)
        act_ref[rs, :] = acc


def _peer_act_kernel(xn_ref, u_ref, i1_ref, i2_ref, act_ref, da_ref, db_ref):
    first = jnp.logical_and(pl.program_id(0) == 0, pl.program_id(1) == 0)
    c = pl.program_id(1)
    slab = PEER_EC // 2
    slab_rows = slab // N_KEYS
    nt = (((1,), (1,)), ((), ()))

    @pl.when(first)
    def _():
        db_ref[...] = jnp.zeros(db_ref.shape, F32)

    @pl.when(c == 0)
    def _():
        act_ref[...] = jnp.zeros(act_ref.shape, F32)

    base = c * 2 * slab_rows
    _peer_pick(db_ref, i1_ref, i2_ref, act_ref, base - slab_rows)
    da_ref[...] = lax.dot_general(xn_ref[...], u_ref[0, 0:slab, :].astype(BF16), nt, preferred_element_type=F32)
    _peer_pick(da_ref, i1_ref, i2_ref, act_ref, base)
    db_ref[...] = lax.dot_general(xn_ref[...], u_ref[0, slab:2 * slab, :].astype(BF16), nt, preferred_element_type=F32)

    @pl.when(c == pl.num_programs(1) - 1)
    def _():
        _peer_pick(db_ref, i1_ref, i2_ref, act_ref, base + slab_rows)


def _peer_act(xn, u_tabs, layer, i1, i2, *, tb):
    rows = xn.shape[0]
    pair_spec = pl.BlockSpec((tb, PEER_PAIRS), lambda i, c: (i, 0))
    return pl.pallas_call(
        _peer_act_kernel, grid=(rows // tb, N_EXPERTS // PEER_EC),
        in_specs=[pl.BlockSpec((tb, D_MODEL), lambda i, c: (i, 0)),
                  pl.BlockSpec((1, PEER_EC, D_MODEL), lambda i, c: (layer, c, 0)), pair_spec, pair_spec],
        out_specs=pair_spec,
        out_shape=jax.ShapeDtypeStruct((rows, PEER_PAIRS), F32),
        scratch_shapes=[pltpu.VMEM((tb, PEER_EC // 2), F32)] * 2,
        compiler_params=_params("arbitrary", "arbitrary"), name="peer_act",
    )(xn, u_tabs, i1, i2)


def _peer_out_kernel(i1_ref, i2_ref, gate_ref, act_ref, v_ref, o_ref, w_ref):
    c = pl.program_id(1)
    n_groups = w_ref.shape[0]
    rows_per_chunk = PEER_OUT_EC // N_KEYS

    @pl.when(c == 0)
    def _():
        key_iota = lax.broadcasted_iota(I32, (N_KEYS, PEER_PAIRS), 0)

        def group(g, carry):
            r0 = pl.multiple_of(g * SUBLANES, SUBLANES)
            i1 = i1_ref[pl.ds(r0, SUBLANES), :]
            i2 = i2_ref[pl.ds(r0, SUBLANES), :]
            a = act_ref[pl.ds(r0, SUBLANES), :]
            wgt = gate_ref[pl.ds(r0, SUBLANES), :] * (0.5 * a * (1.0 + lax.erf(a * (1.0 / math.sqrt(2.0)))))
            for t in range(SUBLANES):
                hit1 = key_iota == jnp.broadcast_to(i1[t:t + 1, :], key_iota.shape)
                hit2 = key_iota == jnp.broadcast_to(i2[t:t + 1, :], key_iota.shape)
                w2 = jnp.where(hit2, jnp.broadcast_to(wgt[t:t + 1, :], key_iota.shape), 0.0)
                w_tok = _dot_nt(jnp.where(hit1, 1.0, 0.0), w2)
                w_ref[g, pl.ds(t, N_KEYS, stride=SUBLANES), :] = w_tok
            return carry

        lax.fori_loop(0, n_groups, group, 0, unroll=PEER_BUILD_UNROLL)

    row0 = c * (rows_per_chunk * SUBLANES)
    lhs = jnp.concatenate(
        [w_ref[:, pl.ds(pl.multiple_of(row0 + j * SUBLANES, SUBLANES), SUBLANES), :].reshape(n_groups * SUBLANES, N_KEYS)
         for j in range(rows_per_chunk)], axis=1)
    part = jnp.dot(lhs.astype(BF16), v_ref[0], preferred_element_type=F32)

    @pl.when(c == 0)
    def _():
        o_ref[...] = part

    @pl.when(c > 0)
    def _():
        o_ref[...] += part


def _peer_out(i1, i2, gate, act, v_tabs, layer):
    rows = i1.shape[0]
    tb = PEER_TB
    pair_spec = pl.BlockSpec((tb, PEER_PAIRS), lambda i, c: (i, 0))
    return pl.pallas_call(
        _peer_out_kernel, grid=(rows // tb, N_EXPERTS // PEER_OUT_EC),
        in_specs=[pair_spec] * 4 + [pl.BlockSpec((1, PEER_OUT_EC, D_MODEL), lambda i, c: (layer, c, 0))],
        out_specs=pl.BlockSpec((tb, D_MODEL), lambda i, c: (i, 0)),
        out_shape=jax.ShapeDtypeStruct((rows, D_MODEL), F32),
        scratch_shapes=[pltpu.VMEM((tb // SUBLANES, N_KEYS * SUBLANES, N_KEYS), F32)],
        compiler_params=_params("arbitrary", "arbitrary"), name="peer_out",
    )(i1, i2, gate, act, v_tabs)


def _ple_kernel(x_ref, f_ref, p_ref, g_ref, wg_ref, wp_ref, gf_ref, *o_refs, n_prompt_blocks):
    x = x_ref[...] + f_ref[...]
    ms = jnp.mean(x * x, axis=-1, keepdims=True)
    h = (x * lax.rsqrt(ms + RMS_EPS)) * g_ref[...]
    out = x + jax.nn.sigmoid(_dot(h, wg_ref[...])) * _dot(p_ref[...], wp_ref[...])
    if n_prompt_blocks is None:
        o_refs[0][...] = out
    else:
        ms2 = jnp.mean(out * out, axis=-1, keepdims=True)
        out = (out * lax.rsqrt(ms2 + RMS_EPS)) * gf_ref[...]
        is_sample = pl.program_id(0) >= n_prompt_blocks

        @pl.when(jnp.logical_not(is_sample))
        def _():
            o_refs[0][...] = out

        @pl.when(is_sample)
        def _():
            o_refs[1][...] = out


def _ple(x, ffn, p, g, wg, wp, g_final, *, tm, final, rows_prompt):
    rows, n = x.shape
    row = pl.BlockSpec((tm, n), lambda i: (i, 0))
    vec = pl.BlockSpec((1, n), lambda i: (0, 0))
    if final:
        n_prompt_blocks = rows_prompt // tm
        assert rows == (n_prompt_blocks + 1) * tm
        out_specs = [pl.BlockSpec((tm, n), lambda i: (jnp.minimum(i, n_prompt_blocks - 1), 0)),
                     pl.BlockSpec((tm, n), lambda i: (0, 0))]
        out_shape = [jax.ShapeDtypeStruct((rows_prompt, n), F32), jax.ShapeDtypeStruct((tm, n), F32)]
    else:
        n_prompt_blocks = None
        out_specs = row
        out_shape = jax.ShapeDtypeStruct((rows, n), F32)
    return pl.pallas_call(
        functools.partial(_ple_kernel, n_prompt_blocks=n_prompt_blocks), grid=(rows // tm,),
        in_specs=[row, row, pl.BlockSpec((tm, p.shape[1]), lambda i: (i, 0)), vec,
                  pl.BlockSpec(wg.shape, lambda i: (0, 0)), pl.BlockSpec(wp.shape, lambda i: (0, 0)), vec],
        out_specs=out_specs, out_shape=out_shape,
        compiler_params=_params("arbitrary"), name="ple",
    )(x, ffn, p, g.reshape(1, n), wg, wp, g_final.reshape(1, n))


def _layer(x, p_rows, lw, state, g_final, dims, final):
    n_p, t_p, n_s, t_s, past = dims
    rows_p = n_p * t_p
    rows_s = n_s * t_s

    z = _norm_mm(x, lw["g_mix"], lw["w_in"], lw["layer"], tm=768, tn=1920)
    z_qkv = z_pool = z_rwkv = z_gate = z

    layer, depth = lw["layer"], lw["depth"]
    os_, ls_, kv_p = [], [], []
    for gi, (_, dil) in enumerate(ATT_GROUPS):
        o, l, kv = _prompt_attn(z_qkv, lw["cos_p"], lw["sin_p"], gi, dil, layer,
                                None if state["kv_p"] is None else state["kv_p"][gi], depth=depth, n_seq=n_p, seq=t_p)
        os_.append(o)
        ls_.append(l)
        kv_p.append(kv)
    att_p = _merge_attn(os_, ls_, tm=512)
    att_s, *kv_s = _sample_attn(z_qkv, lw["cos_s"], lw["sin_s"], state["caches_t"], layer, state["kv_s"],
                                row0=rows_p, n_seq=n_s, t_new=t_s)

    pool_p = _pool_mixer(z_pool, jnp.zeros((n_p, POOL_HIST, POOL_W), F32), lw["pool_w"], lw["pool_scale"],
                         row0=0, n_seq=n_p, t_len=t_p, pos0=0)
    hist_s = jnp.concatenate([jnp.zeros((n_s, 1, POOL_W), F32), state["pool"]], axis=1)
    pool_s = _pool_mixer(z_pool, hist_s, lw["pool_w"], lw["pool_scale"], row0=rows_p, n_seq=n_s, t_len=t_s, pos0=past)

    prep_p = _rwkv_prep(z_rwkv, jnp.zeros((n_p, 1, RWKV_COLS), F32), lw, row0=0, n_seq=n_p, t_len=t_p, tm=256)
    prep_s = _rwkv_prep(z_rwkv, state["shift"][:, None, :], lw, row0=rows_p, n_seq=n_s, t_len=t_s, tm=t_s)
    yn_p, wkv_p = _rwkv_scan(prep_p[:6], jnp.zeros((n_p, RWKV_HEADS, RWKV_HEAD, RWKV_HEAD), F32),
                             chunk=RWKV_CHUNK, group=RWKV_SEQ_GROUP)
    yn_s, wkv_s = _rwkv_scan(prep_s[:6], state["wkv"], chunk=t_s, group=RWKV_SAMPLE_GROUP)
    branches = [(att_p, att_s), (pool_p, pool_s), (yn_p.reshape(rows_p, RWKV_W), yn_s.reshape(rows_s, RWKV_W)),
                (prep_p[7], prep_s[7]), (prep_p[6], prep_s[6])]

    mix = _mix(branches, z_gate, lw["ln_w"], lw["ln_b"], lw["w_attn_o"], lw["w_pool_o"], lw["w_rwkv_o"], tm=rows_s)
    x = _mm_residual(x, mix, lw["w_out"], tm=384)

    q, xn = _norm_mm3(x, lw["g_ffn"], *lw["peer_wq"], tm=768, tn=512)
    i1, i2, gate = _peer_select(q, *lw["peer_subkeys"])
    act = _peer_act(xn, lw["peer_u"], lw["layer"], i1, i2, tb=PEER_ACT_TB)
    ffn = _peer_out(i1, i2, gate, act, lw["peer_v"], lw["layer"])

    x = _ple(x, ffn, p_rows, lw["g_ple"], lw["ple_wg"], lw["ple_wp"], g_final, tm=rows_s, final=final, rows_prompt=rows_p)

    keep = POOL_HIST - 1
    pool_cols = slice(Z_POOL, Z_POOL + POOL_W)
    rwkv_cols = slice(Z_RWKV, Z_RWKV + RWKV_COLS)
    pool_p_state = jnp.stack([z[(n + 1) * t_p - keep:(n + 1) * t_p, pool_cols] for n in range(n_p)])
    zp_s = z[rows_p:, pool_cols].reshape(n_s, t_s, POOL_W)
    new_p = [pool_p_state, wkv_p, z[t_p - 1:rows_p:t_p, rwkv_cols]]
    new_s = [jnp.concatenate([hist_s[:, 1:], zp_s], axis=1)[:, -keep:], wkv_s, z[rows_p + t_s - 1::t_s, rwkv_cols]]
    return x, kv_p, kv_s, new_p, new_s


def kernel(x_prompt, x_sample, p_prompt, p_sample, cache_attn_w128, cache_attn_w512, cache_attn_w2048, state_pool, state_rwkv_wkv, state_rwkv_shift, g_mix, w_in, w_attn_o, w_pool_o, w_rwkv_o, w_out, pool_w, pool_scale, rwkv_mu, rwkv_w0, rwkv_w2, rwkv_a0, rwkv_a2, rwkv_g2, rwkv_kk, rwkv_ka, rwkv_rk, rwkv_ln_w, rwkv_ln_b, g_ffn, peer_wq, peer_subkeys, peer_u, peer_v, g_ple, ple_wg, ple_wp, g_final):
    n_p, t_p, _ = x_prompt.shape
    n_s, t_s, _ = x_sample.shape
    depth = w_in.shape[0]
    past = PAST_LEN
    rows_p = n_p * t_p
    rows_s = n_s * t_s
    dims = (n_p, t_p, n_s, t_s, past)

    x = jnp.concatenate([x_prompt.reshape(rows_p, D_MODEL), x_sample.reshape(rows_s, D_MODEL)], axis=0)
    cos_p, sin_p = _rope_tables(jnp.arange(t_p, dtype=I32))
    cos_s, sin_s = _rope_tables(past + jnp.arange(t_s, dtype=I32))
    lora_pad = jnp.zeros((128 - 64, RWKV_W), F32)
    peer_v_bf16 = peer_v.astype(BF16)
    to_dev = (0, 1, 3, 4, 5, 2)
    from_dev = (0, 1, 5, 2, 3, 4)
    caches_t = [jnp.transpose(c, to_dev) for c in (cache_attn_w128, cache_attn_w512, cache_attn_w2048)]

    def row(a):
        return a.reshape(1, -1)

    w_in_z = _repack_w_in(w_in, tk=256)

    new_p, new_s = [], []
    kv_p = kv_s = None
    for l in range(depth):
        lw = {
            "g_mix": g_mix[l], "w_in": w_in_z,
            "cos_p": cos_p, "sin_p": sin_p, "cos_s": cos_s, "sin_s": sin_s, "depth": depth,
            "pool_w": pool_w[l].astype(BF16), "pool_scale": pool_scale[l],
            "mu": row(rwkv_mu[l]), "w0": row(rwkv_w0[l]), "a0": row(rwkv_a0[l]),
            "w2p": jnp.concatenate([rwkv_w2[l], lora_pad], axis=0).astype(BF16),
            "a2p": jnp.concatenate([lora_pad, rwkv_a2[l]], axis=0).astype(BF16),
            "g2": rwkv_g2[l].astype(BF16),
            "kk": row(rwkv_kk[l]), "ka": row(rwkv_ka[l]), "rk": row(rwkv_rk[l]),
            "ln_w": rwkv_ln_w[l], "ln_b": rwkv_ln_b[l],
            "w_attn_o": w_attn_o[l].astype(BF16), "w_pool_o": w_pool_o[l].astype(BF16),
            "w_rwkv_o": w_rwkv_o[l].astype(BF16), "w_out": w_out[l].astype(BF16),
            "g_ffn": g_ffn[l], "peer_wq": _split_bf16(peer_wq[l]), "peer_subkeys": _split_bf16(peer_subkeys[l]),
            "peer_u": peer_u, "peer_v": peer_v_bf16, "layer": l,
            "g_ple": g_ple[l], "ple_wg": ple_wg[l].astype(BF16), "ple_wp": ple_wp[l].astype(BF16),
        }
        state = {"caches_t": caches_t, "kv_p": kv_p, "kv_s": kv_s, "pool": state_pool[l],
                 "wkv": state_rwkv_wkv[l], "shift": state_rwkv_shift[l]}
        p_rows = jnp.concatenate([p_prompt[l].reshape(rows_p, -1), p_sample[l].reshape(rows_s, -1)], axis=0).astype(BF16)
        x, kv_p, kv_s, st_p, st_s = _layer(x, p_rows, lw, state, g_final, dims, l == depth - 1)
        new_p.append(st_p)
        new_s.append(st_s)

    y_p, y_s = x
    outs = [y_p.reshape(n_p, t_p, D_MODEL), y_s.reshape(n_s, t_s, D_MODEL)]
    for gi in range(len(ATT_GROUPS)):
        outs.append(jnp.transpose(kv_p[gi], from_dev))
        outs.append(jnp.transpose(kv_s[gi], from_dev))
    for j in range(3):
        outs.append(jnp.stack([s[j] for s in new_p]))
        outs.append(jnp.stack([s[j] for s in new_s]))
    return tuple(outs)
```

```python
import functools
import math

import jax
import jax.numpy as jnp
from jax import lax
from jax.experimental import pallas as pl
from jax.experimental.pallas import tpu as pltpu

F32 = jnp.float32
BF16 = jnp.bfloat16
I32 = jnp.int32

D_MODEL = 2048
RMS_EPS = 1e-6
HEAD_DIM = 64
ATT_GROUPS = ((128, 1), (512, 4), (2048, 16))
HEADS_PER_GROUP = 4
ATT_W = 768
ATT_OUT = 256
ATT_SPAN = 128
ROPE_THETA = 10000.0
POOL_WINDOWS = (2, 4, 8, 16)
POOL_GROUP = 128
POOL_W = 512
POOL_HIST = 16
RWKV_HEAD = 64
RWKV_HEADS = 12
RWKV_W = 768
RWKV_COLS = 2560
RWKV_LORA_OFF = 2304
GN_EPS = 64e-5
RWKV_CHUNK = 64
RWKV_SEQ_GROUP = 4
RWKV_SEQ_SUB = 2
RWKV_SAMPLE_GROUP = 8
OFF_POOL = 2304
OFF_RWKV = 2816
OFF_GATE = 5376
IN_COLS = 11520
Z_GATE = 0
Z_RWKV = 3 * D_MODEL
Z_POOL = Z_RWKV + RWKV_COLS
Z_QKV = Z_POOL + POOL_W
Z_RWKV_BLOCK = RWKV_COLS + POOL_W
PAST_LEN = 8192
PEER_HEADS = 8
N_KEYS = 128
N_EXPERTS = N_KEYS * N_KEYS
PEER_TOPK = 16
PEER_PAIRS = PEER_HEADS * PEER_TOPK
PEER_TB = 384
PEER_ACT_TB = 1408
PEER_BUILD_UNROLL = 4
PEER_EC = 1024
PEER_OUT_EC = 2048
SUBLANES = 8
VMEM_LIMIT = 56 * 1024 * 1024

NEG_INF = float("-inf")


def _params(*sem):
    return pltpu.CompilerParams(dimension_semantics=sem, vmem_limit_bytes=VMEM_LIMIT)


def _dot(a, b):
    return jnp.dot(a.astype(BF16), b.astype(BF16), preferred_element_type=F32)


def _dot_nt(a, b):
    return lax.dot_general(a.astype(BF16), b.astype(BF16), (((1,), (1,)), ((), ())), preferred_element_type=F32)


def _norm_mm_kernel(x_ref, g_ref, w_ref, o_ref, xn_ref):
    @pl.when(pl.program_id(1) == 0)
    def _():
        x = x_ref[...]
        ms = jnp.mean(x * x, axis=-1, keepdims=True)
        xn_ref[...] = ((x * lax.rsqrt(ms + RMS_EPS)) * g_ref[...]).astype(BF16)

    o_ref[...] = jnp.dot(xn_ref[...], w_ref[0], preferred_element_type=F32)


def _norm_mm(x, g, w, layer, *, tm, tn):
    rows, k = x.shape
    n = w.shape[2]
    return pl.pallas_call(
        _norm_mm_kernel, grid=(rows // tm, n // tn),
        in_specs=[
            pl.BlockSpec((tm, k), lambda i, j: (i, 0)),
            pl.BlockSpec((1, k), lambda i, j: (0, 0)),
            pl.BlockSpec((1, k, tn), lambda i, j: (layer, 0, j)),
        ],
        out_specs=pl.BlockSpec((tm, tn), lambda i, j: (i, j)),
        out_shape=jax.ShapeDtypeStruct((rows, n), F32),
        scratch_shapes=[pltpu.VMEM((tm, k), BF16)],
        compiler_params=_params("arbitrary", "arbitrary"), name="norm_mm",
    )(x, g.reshape(1, k), w)


Z_SEGMENTS = ((OFF_GATE, IN_COLS), (OFF_RWKV, OFF_GATE), (OFF_POOL, OFF_RWKV), (0, OFF_POOL))


def _repack_kernel(w_ref, o_ref):
    dst = 0
    for a, b in Z_SEGMENTS:
        o_ref[0, :, dst:dst + (b - a)] = w_ref[0, :, a:b].astype(BF16)
        dst += b - a


def _repack_w_in(w_in, *, tk):
    depth, k, n = w_in.shape
    spec = pl.BlockSpec((1, tk, n), lambda l, i: (l, i, 0))
    return pl.pallas_call(
        _repack_kernel, grid=(depth, k // tk), in_specs=[spec], out_specs=spec,
        out_shape=jax.ShapeDtypeStruct(w_in.shape, BF16),
        compiler_params=_params("arbitrary", "arbitrary"), name="repack_w_in",
    )(w_in)


def _split_bf16(a):
    hi = a.astype(BF16)
    lo = (a - hi.astype(F32)).astype(BF16)
    return hi, lo


def _dot3(ah, al, bh, bl, dims=(((1,), (0,)), ((), ()))):
    def d(p, q):
        return lax.dot_general(p, q, dims, preferred_element_type=F32)

    return d(ah, bh) + (d(ah, bl) + d(al, bh))


def _norm_mm3_kernel(x_ref, g_ref, wh_ref, wl_ref, o_ref, xh_ref, xl_ref):
    @pl.when(pl.program_id(1) == 0)
    def _():
        x = x_ref[...]
        ms = jnp.mean(x * x, axis=-1, keepdims=True)
        xh, xl = _split_bf16((x * lax.rsqrt(ms + RMS_EPS)) * g_ref[...])
        xh_ref[...] = xh
        xl_ref[...] = xl

    o_ref[...] = _dot3(xh_ref[...], xl_ref[...], wh_ref[...], wl_ref[...])


def _norm_mm3(x, g, wh, wl, *, tm, tn):
    rows, k = x.shape
    n = wh.shape[1]
    wspec = pl.BlockSpec((k, tn), lambda i, j: (0, j))
    return pl.pallas_call(
        _norm_mm3_kernel, grid=(rows // tm, n // tn),
        in_specs=[pl.BlockSpec((tm, k), lambda i, j: (i, 0)), pl.BlockSpec((1, k), lambda i, j: (0, 0)), wspec, wspec],
        out_specs=[pl.BlockSpec((tm, tn), lambda i, j: (i, j)), pl.BlockSpec((tm, k), lambda i, j: (i, 0))],
        out_shape=[jax.ShapeDtypeStruct((rows, n), F32), jax.ShapeDtypeStruct((rows, k), BF16)],
        scratch_shapes=[pltpu.VMEM((tm, k), BF16)],
        compiler_params=_params("arbitrary", "arbitrary"), name="norm_mm3",
    )(x, g.reshape(1, k), wh, wl)


def _rotate(x, cos, sin):
    lane = lax.broadcasted_iota(I32, x.shape, 1)
    first_half = (lane & (HEAD_DIM - 1)) < HEAD_DIM // 2
    partner = jnp.where(first_half, pltpu.roll(x, 128 - HEAD_DIM // 2, 1), pltpu.roll(x, HEAD_DIM // 2, 1))
    return x * cos + partner * sin


def _rope_tables(pos):
    half = HEAD_DIM // 2
    inv = ROPE_THETA ** (-jnp.arange(half, dtype=F32) / half)
    ang = pos.astype(F32)[:, None] * inv[None, :]
    cos = jnp.cos(ang)
    sin = jnp.sin(ang)
    cos_t = jnp.concatenate([cos, cos, cos, cos], axis=1)
    sin_t = jnp.concatenate([-sin, sin, -sin, sin], axis=1)
    return cos_t, sin_t


ROT_ROWS = 256


def _prompt_attn_kernel(q_ref, k_ref, v_ref, cos_ref, sin_ref, *rest, dil, seq, keep):
    o_ref, l_ref, kv_ref, qs, ks, vs = rest[-6:]
    chunks = ATT_OUT // 128
    for i in range(seq // ROT_ROWS):
        rs = slice(i * ROT_ROWS, (i + 1) * ROT_ROWS)
        cos = cos_ref[rs, :]
        sin = sin_ref[rs, :]
        for c in range(chunks):
            cs = slice(c * 128, (c + 1) * 128)
            qs[c, rs, :] = _rotate(q_ref[rs, cs], cos, sin)
            ks[c, rs, :] = _rotate(k_ref[rs, cs], cos, sin)
            vs[c, rs, :] = v_ref[rs, cs]
    for c in range(chunks):
        for j in range(keep // 128):
            ps = slice(seq - keep + j * 128, seq - keep + (j + 1) * 128)
            k_t = ks[c, ps, :].T
            v_t = vs[c, ps, :].T
            for half in range(2):
                hs = slice(half * HEAD_DIM, (half + 1) * HEAD_DIM)
                kv_ref[0, 0, 0, 2 * c + half, :, j * 128:(j + 1) * 128] = k_t[hs, :]
                kv_ref[0, 0, 1, 2 * c + half, :, j * 128:(j + 1) * 128] = v_t[hs, :]

    def rows_of(start):
        return pl.ds(start, ATT_SPAN, stride=dil) if dil > 1 else pl.ds(start, ATT_SPAN)

    qi = lax.broadcasted_iota(I32, (ATT_SPAN, ATT_SPAN), 0)
    ki = lax.broadcasted_iota(I32, (ATT_SPAN, ATT_SPAN), 1)
    cur_ok = ki <= qi
    prev_ok = ki >= qi
    first_head = ki < HEAD_DIM
    ones = jnp.ones((ATT_SPAN, 128), BF16)
    scale = HEAD_DIM ** -0.5
    n_blocks = seq // dil // ATT_SPAN
    for r in range(dil):
        for b in range(n_blocks):
            cur = rows_of(r + dil * b * ATT_SPAN)
            for c in range(chunks):
                q2, kc2, vc2 = qs[c, cur, :], ks[c, cur, :], vs[c, cur, :]
                if b > 0:
                    prev = rows_of(r + dil * (b - 1) * ATT_SPAN)
                    kp2, vp2 = ks[c, prev, :], vs[c, prev, :]
                outs, lses = [], []
                for half in range(2):
                    q = jnp.where(first_head == (half == 0), q2, 0.0)
                    sc = jnp.where(cur_ok, _dot_nt(q, kc2) * scale, NEG_INF)
                    m = jnp.max(sc, axis=-1, keepdims=True)
                    if b > 0:
                        sp = jnp.where(prev_ok, _dot_nt(q, kp2) * scale, NEG_INF)
                        m = jnp.maximum(m, jnp.max(sp, axis=-1, keepdims=True))
                    ec = jnp.exp(sc - m).astype(BF16)
                    den = jnp.dot(ec, ones, preferred_element_type=F32)
                    acc = _dot(ec, vc2)
                    if b > 0:
                        ep = jnp.exp(sp - m).astype(BF16)
                        den = den + jnp.dot(ep, ones, preferred_element_type=F32)
                        acc = acc + _dot(ep, vp2)
                    outs.append(acc / den)
                    lses.append(m + jnp.log(den))
                o_ref[c, cur, :] = jnp.where(first_head, outs[0], outs[1])
                l_ref[c, cur, :] = jnp.where(first_head, lses[0], lses[1])


def _prompt_attn(z_qkv, cos_t, sin_t, gi, dil, layer, kv_prev, *, depth, n_seq, seq):
    rows_p = n_seq * seq
    win = ATT_GROUPS[gi][0]
    keep = min(win, seq)
    assert seq % (dil * ATT_SPAN) == 0 and keep % 128 == 0 and seq % ROT_ROWS == 0
    tab = pl.BlockSpec((seq, 128), lambda n: (0, 0))
    chunked = pl.BlockSpec((ATT_OUT // 128, seq, 128), lambda n: (0, n, 0))
    kv_shape = (depth, n_seq, 2, HEADS_PER_GROUP, HEAD_DIM, keep)
    kv_spec = pl.BlockSpec((1, 1) + kv_shape[2:], lambda n: (layer, n, 0, 0, 0, 0))
    c0 = Z_QKV // ATT_OUT
    in_specs = [pl.BlockSpec((seq, ATT_OUT), lambda n: (n, c0 + gi)),
                pl.BlockSpec((seq, ATT_OUT), lambda n: (n, c0 + ATT_W // ATT_OUT + gi)),
                pl.BlockSpec((seq, ATT_OUT), lambda n: (n, c0 + 2 * ATT_W // ATT_OUT + gi)), tab, tab]
    args = [z_qkv, z_qkv, z_qkv, cos_t, sin_t]
    aliases = {}
    if kv_prev is not None:
        in_specs.append(pl.BlockSpec(memory_space=pl.ANY))
        args.append(kv_prev)
        aliases = {len(args) - 1: 2}
    chunk_shape = jax.ShapeDtypeStruct((ATT_OUT // 128, rows_p, 128), F32)
    return pl.pallas_call(
        functools.partial(_prompt_attn_kernel, dil=dil, seq=seq, keep=keep),
        grid=(n_seq,), in_specs=in_specs,
        out_specs=[chunked, chunked, kv_spec],
        out_shape=[chunk_shape, chunk_shape, jax.ShapeDtypeStruct(kv_shape, F32)],
        scratch_shapes=[pltpu.VMEM((ATT_OUT // 128, seq, 128), F32)] * 3,
        input_output_aliases=aliases,
        compiler_params=_params("arbitrary"), name=f"prompt_attn_g{gi}",
    )(*args)


def _merge_groups(os_, ls_):
    m = jnp.maximum(jnp.maximum(ls_[0], ls_[1]), ls_[2])
    es = [jnp.exp(l - m) for l in ls_]
    tot = es[0] + es[1] + es[2]
    return (es[0] / tot) * os_[0] + (es[1] / tot) * os_[1] + (es[2] / tot) * os_[2]


def _merge_kernel(o0, o1, o2, l0, l1, l2, a_ref):
    for c in range(ATT_OUT // 128):
        a_ref[:, c * 128:(c + 1) * 128] = _merge_groups([o0[c], o1[c], o2[c]], [l0[c], l1[c], l2[c]])


def _merge_attn(os_, ls_, *, tm):
    rows = os_[0].shape[1]
    spec = pl.BlockSpec((ATT_OUT // 128, tm, 128), lambda i: (0, i, 0))
    return pl.pallas_call(
        _merge_kernel, grid=(rows // tm,), in_specs=[spec] * 6, out_specs=pl.BlockSpec((tm, ATT_OUT), lambda i: (i, 0)),
        out_shape=jax.ShapeDtypeStruct((rows, ATT_OUT), F32),
        compiler_params=_params("arbitrary"), name="merge_attn",
    )(*os_, *ls_)


def _sample_attn_kernel(z_ref, cos_ref, sin_ref, *rest, t_new):
    caches, (a_ref, n0_ref, n1_ref, n2_ref) = rest[0:3], rest[-4:]
    scale = HEAD_DIM ** -0.5
    cos = cos_ref[...]
    sin = sin_ref[...]
    n_chunks = ATT_W // 128
    q_c = [_rotate(z_ref[:, c * 128:(c + 1) * 128], cos, sin) for c in range(n_chunks)]
    k_c = [_rotate(z_ref[:, ATT_W + c * 128:ATT_W + (c + 1) * 128], cos, sin) for c in range(n_chunks)]
    v_c = [z_ref[:, 2 * ATT_W + c * 128:2 * ATT_W + (c + 1) * 128] for c in range(n_chunks)]
    outs_g, lses_g = [], []
    for gi, (c_ref, n_ref, (win, dil)) in enumerate(zip(caches, (n0_ref, n1_ref, n2_ref), ATT_GROUPS)):
        cache_len = c_ref.shape[-1]
        t_c = lax.broadcasted_iota(I32, (t_new, cache_len), 0)
        c_c = lax.broadcasted_iota(I32, (t_new, cache_len), 1)
        d_c = cache_len + t_c - c_c
        ok_c = jnp.logical_and((d_c & (dil - 1)) == 0, d_c <= ATT_SPAN * dil)
        t_n = lax.broadcasted_iota(I32, (t_new, t_new), 0)
        u_n = lax.broadcasted_iota(I32, (t_new, t_new), 1)
        d_n = t_n - u_n
        ok_n = jnp.logical_and(d_n >= 0, (d_n & (dil - 1)) == 0)
        outs, lses = [], []
        for h in range(HEADS_PER_GROUP):
            chunk, half = divmod(gi * HEADS_PER_GROUP + h, 2)
            hs = slice(half * HEAD_DIM, (half + 1) * HEAD_DIM)
            q, k_new, v_new = q_c[chunk][:, hs], k_c[chunk][:, hs], v_c[chunk][:, hs]
            k_t = c_ref[0, 0, 0, h]
            v_t = c_ref[0, 0, 1, h]
            s_c = jnp.where(ok_c, _dot(q, k_t) * scale, NEG_INF)
            s_n = jnp.where(ok_n, _dot_nt(q, k_new) * scale, NEG_INF)
            m = jnp.maximum(jnp.max(s_c, axis=-1, keepdims=True), jnp.max(s_n, axis=-1, keepdims=True))
            e_c = jnp.exp(s_c - m)
            e_n = jnp.exp(s_n - m)
            den = jnp.sum(e_c, axis=-1, keepdims=True) + jnp.sum(e_n, axis=-1, keepdims=True)
            acc = _dot_nt(e_c, v_t) + _dot(e_n, v_new)
            outs.append(acc / den)
            lses.append(jnp.broadcast_to(m + jnp.log(den), (t_new, HEAD_DIM)))
            n_ref[0, 0, 0, h] = jnp.concatenate([k_t[:, t_new:], k_new.T], axis=1)
            n_ref[0, 0, 1, h] = jnp.concatenate([v_t[:, t_new:], v_new.T], axis=1)
        outs_g.append(jnp.concatenate(outs, axis=1))
        lses_g.append(jnp.concatenate(lses, axis=1))
    a_ref[...] = _merge_groups(outs_g, lses_g)


def _sample_attn(z_qkv, cos_s, sin_s, caches_t, layer, new_prev, *, row0, n_seq, t_new):
    b0 = row0 // t_new

    def cache_spec(c):
        return pl.BlockSpec((1, 1) + c.shape[2:], lambda n: (layer, n, 0, 0, 0, 0))

    tab = pl.BlockSpec((t_new, 128), lambda n: (0, 0))
    in_specs = ([pl.BlockSpec((t_new, 3 * ATT_W), lambda n: (b0 + n, Z_QKV // (3 * ATT_W))), tab, tab]
                + [cache_spec(c) for c in caches_t])
    args = [z_qkv, cos_s, sin_s, *caches_t]
    aliases = {}
    if new_prev is not None:
        for j, a in enumerate(new_prev):
            in_specs.append(pl.BlockSpec(memory_space=pl.ANY))
            args.append(a)
            aliases[len(args) - 1] = 1 + j
    return pl.pallas_call(
        functools.partial(_sample_attn_kernel, t_new=t_new),
        grid=(n_seq,), in_specs=in_specs,
        out_specs=[pl.BlockSpec((t_new, ATT_OUT), lambda n: (n, 0))] + [cache_spec(c) for c in caches_t],
        out_shape=[jax.ShapeDtypeStruct((n_seq * t_new, ATT_OUT), F32)]
        + [jax.ShapeDtypeStruct(c.shape, F32) for c in caches_t],
        input_output_aliases=aliases,
        compiler_params=_params("arbitrary"), name="sample_attn",
    )(*args)


def _pool_kernel(z_ref, h_ref, w_ref, s_ref, o_ref, buf_ref, *, t_len, pos0):
    buf_ref[0:POOL_HIST, :] = h_ref[0]
    buf_ref[POOL_HIST:POOL_HIST + t_len, :] = z_ref[...]
    pos = pos0 + lax.broadcasted_iota(I32, (t_len, POOL_GROUP), 0)
    for g, win in enumerate(POOL_WINDOWS):
        cs = slice(g * POOL_GROUP, (g + 1) * POOL_GROUP)
        z = buf_ref[POOL_HIST:POOL_HIST + t_len, cs]
        wsum = z
        for i in range(1, win):
            wsum = wsum + buf_ref[POOL_HIST - i:POOL_HIST - i + t_len, cs]
        cnt = jnp.minimum(win, pos + 1).astype(F32)
        y = wsum / cnt - z
        o_ref[:, cs] = _dot(y, w_ref[g]) * s_ref[:, cs]


def _pool_mixer(z_pool, hist, pool_w, pool_scale, *, row0, n_seq, t_len, pos0):
    b0 = row0 // t_len
    return pl.pallas_call(
        functools.partial(_pool_kernel, t_len=t_len, pos0=pos0),
        grid=(n_seq,),
        in_specs=[
            pl.BlockSpec((t_len, POOL_W), lambda n: (b0 + n, Z_POOL // POOL_W)),
            pl.BlockSpec((1, POOL_HIST, POOL_W), lambda n: (n, 0, 0)),
            pl.BlockSpec((len(POOL_WINDOWS), POOL_GROUP, POOL_GROUP), lambda n: (0, 0, 0)),
            pl.BlockSpec((1, POOL_W), lambda n: (0, 0)),
        ],
        out_specs=pl.BlockSpec((t_len, POOL_W), lambda n: (n, 0)),
        out_shape=jax.ShapeDtypeStruct((n_seq * t_len, POOL_W), F32),
        scratch_shapes=[pltpu.VMEM((POOL_HIST + t_len, POOL_W), F32)],
        compiler_params=_params("arbitrary"), name="pool_mixer",
    )(z_pool, hist, pool_w, pool_scale.reshape(1, POOL_W))


def _rwkv_prep_kernel(z_ref, zp_ref, first_ref, mu_ref, w0_ref, w2_ref, a0_ref, a2_ref, g2_ref, kk_ref, ka_ref, rk_ref,
                      r_o, lw_o, k_o, v_o, kn_o, b_o, g_o, bonus_o, buf_ref, *, tm):
    i = pl.program_id(1)
    z = z_ref[:, 0:RWKV_COLS]
    prev_row = jnp.where(i == 0, first_ref[0], zp_ref[SUBLANES - 1:SUBLANES, 0:RWKV_COLS])
    buf_ref[SUBLANES:SUBLANES + tm, :] = z
    buf_ref[SUBLANES - 1:SUBLANES, :] = prev_row
    shifted = buf_ref[SUBLANES - 1:SUBLANES - 1 + tm, :]
    xm = z + mu_ref[...] * (shifted - z)
    r = xm[:, 0:RWKV_W]
    k = xm[:, RWKV_W:2 * RWKV_W]
    v = xm[:, 2 * RWKV_W:3 * RWKV_W]
    wa = xm[:, RWKV_LORA_OFF:RWKV_LORA_OFF + 128]
    gl = xm[:, RWKV_LORA_OFF + 128:RWKV_COLS]
    xw = w0_ref[...] + _dot(jnp.tanh(wa), w2_ref[...])
    logw = -math.exp(-0.5) * jax.nn.sigmoid(xw)
    a = jax.nn.sigmoid(a0_ref[...] + _dot(wa, a2_ref[...]))
    g_o[...] = _dot(jax.nn.sigmoid(gl), g2_ref[...])
    kkr = k * kk_ref[...]
    kmod = k * (1.0 + (a - 1.0) * ka_ref[...])
    rkk = r * kmod * rk_ref[...]
    bonus = []
    for h in range(RWKV_HEADS):
        sl = slice(h * RWKV_HEAD, (h + 1) * RWKV_HEAD)
        kh = kkr[:, sl]
        nrm = jnp.sqrt(jnp.sum(kh * kh, axis=-1, keepdims=True))
        kn = kh / jnp.maximum(nrm, 1e-12)
        r_o[0, h] = r[:, sl]
        lw_o[0, h] = logw[:, sl]
        k_o[0, h] = kmod[:, sl]
        v_o[0, h] = v[:, sl]
        kn_o[0, h] = kn
        b_o[0, h] = kn * a[:, sl]
        bonus.append(jnp.sum(rkk[:, sl], axis=-1, keepdims=True) * v[:, sl])
    bonus_o[...] = jnp.concatenate(bonus, axis=1)


def _rwkv_prep(z, first_prev, lw, *, row0, n_seq, t_len, tm):
    nblk = t_len // tm
    pb = tm // SUBLANES
    b0 = row0 // tm
    p0 = row0 // SUBLANES

    def vec(n):
        return pl.BlockSpec((1, n), lambda s, i: (0, 0))

    hm = jax.ShapeDtypeStruct((n_seq, RWKV_HEADS, t_len, RWKV_HEAD), F32)
    rm = jax.ShapeDtypeStruct((n_seq * t_len, RWKV_W), F32)
    hm_spec = pl.BlockSpec((1, RWKV_HEADS, tm, RWKV_HEAD), lambda s, i: (s, 0, i, 0))
    rm_spec = pl.BlockSpec((tm, RWKV_W), lambda s, i: (s * nblk + i, 0))
    return pl.pallas_call(
        functools.partial(_rwkv_prep_kernel, tm=tm),
        grid=(n_seq, nblk),
        in_specs=[
            pl.BlockSpec((tm, Z_RWKV_BLOCK), lambda s, i: (b0 + s * nblk + i, Z_RWKV // Z_RWKV_BLOCK)),
            pl.BlockSpec((SUBLANES, Z_RWKV_BLOCK),
                         lambda s, i: (jnp.maximum(p0 + (s * nblk + i) * pb - 1, 0), Z_RWKV // Z_RWKV_BLOCK)),
            pl.BlockSpec((1, 1, RWKV_COLS), lambda s, i: (s, 0, 0)),
            vec(RWKV_COLS), vec(RWKV_W),
            pl.BlockSpec((128, RWKV_W), lambda s, i: (0, 0)),
            vec(RWKV_W),
            pl.BlockSpec((128, RWKV_W), lambda s, i: (0, 0)),
            pl.BlockSpec((128, RWKV_W), lambda s, i: (0, 0)),
            vec(RWKV_W), vec(RWKV_W), vec(RWKV_W),
        ],
        out_specs=[hm_spec] * 6 + [rm_spec] * 2,
        out_shape=[hm] * 6 + [rm] * 2,
        scratch_shapes=[pltpu.VMEM((tm + SUBLANES, RWKV_COLS), F32)],
        compiler_params=_params("arbitrary", "arbitrary"), name="rwkv_prep",
    )(z, z, first_prev, lw["mu"], lw["w0"], lw["w2p"], lw["a0"], lw["a2p"], lw["g2"], lw["kk"], lw["ka"], lw["rk"])


def _bmm(a, b):
    return jnp.einsum("hqk,hkd->hqd", a.astype(BF16), b.astype(BF16), preferred_element_type=F32)


def _bmm_nt(a, b):
    return jnp.einsum("hqd,hkd->hqk", a.astype(BF16), b.astype(BF16), preferred_element_type=F32)


def _bmm_tn(a, b):
    return jnp.einsum("hkq,hkd->hqd", a.astype(BF16), b.astype(BF16), preferred_element_type=F32)


def _rwkv_scan_kernel(*refs, chunk, group, sub):
    s0_ref, s_scr = refs[6], refs[9]

    @pl.when(pl.program_id(1) == 0)
    def _():
        s_scr[...] = s0_ref[...].reshape(s_scr.shape)

    def part(i, carry):
        _rwkv_chunk_solve(*refs, i * sub, chunk=chunk, sub=sub)
        return carry

    lax.fori_loop(0, group // sub, part, 0)


def _rwkv_chunk_solve(r_ref, lw_ref, k_ref, v_ref, kn_ref, b_ref, s0_ref, y_ref, st_ref, s_scr, g0, *, chunk, sub):
    nh = sub * RWKV_HEADS
    h0 = g0 * RWKV_HEADS

    def heads(ref):
        return ref[pl.ds(g0, sub)].reshape((nh,) + ref.shape[2:])

    r, logw, k, v, kn, b = [heads(ref) for ref in (r_ref, lw_ref, k_ref, v_ref, kn_ref, b_ref)]
    row = lax.broadcasted_iota(I32, (chunk, chunk), 0)
    col = lax.broadcasted_iota(I32, (chunk, chunk), 1)
    incl = row >= col
    strict = row > col
    tri = jnp.broadcast_to(incl.astype(BF16)[None], (nh, chunk, chunk))
    lw_hi = logw.astype(BF16)
    lw_lo = (logw - lw_hi.astype(F32)).astype(BF16)
    cum = (jnp.einsum("hqk,hkd->hqd", tri, lw_hi, preferred_element_type=F32)
           + jnp.einsum("hqk,hkd->hqd", tri, lw_lo, preferred_element_type=F32))
    p_inv = jnp.exp(-cum)
    kt = k * p_inv
    bt = b * p_inv
    kap = kn * jnp.exp(cum - logw)
    rho = r * jnp.exp(cum)
    qq = jnp.concatenate([kap, rho], axis=1)
    gram = _bmm_nt(qq, jnp.concatenate([kt, bt], axis=1))
    a_k = jnp.where(strict[None], gram[:, :chunk, :chunk], 0.0)
    a_b = jnp.where(strict[None], gram[:, :chunk, chunk:], 0.0)
    l_k = jnp.where(incl[None], gram[:, chunk:, :chunk], 0.0)
    l_b = jnp.where(incl[None], gram[:, chunk:, chunk:], 0.0)
    x = jnp.broadcast_to((row == col).astype(F32)[None], (nh, chunk, chunk))
    m = 1
    while m < chunk:
        sh = m.bit_length() - 1
        same = (row >> (sh + 1)) == (col >> (sh + 1))
        lower_left = jnp.logical_and(((row >> sh) & 1) == 1, ((col >> sh) & 1) == 0)
        off = jnp.where(jnp.logical_and(same, lower_left)[None], a_b, 0.0)
        x = x - _bmm(_bmm(x, off), x)
        m *= 2
    s = s_scr[pl.ds(h0, nh)]
    qs = _bmm_nt(qq, s)
    u = _bmm(x, -(qs[:, :chunk] + _bmm(a_k, v)))
    y = qs[:, chunk:] + _bmm(jnp.concatenate([l_k, l_b], axis=2), jnp.concatenate([v, u], axis=1))
    s_new = (s + _bmm_tn(jnp.concatenate([v, u], axis=1), jnp.concatenate([kt, bt], axis=1))) * jnp.exp(cum[:, chunk - 1:chunk, :])
    s_scr[pl.ds(h0, nh)] = s_new
    st_ref[pl.ds(g0, sub)] = s_new.reshape((sub,) + st_ref.shape[1:])
    mean = jnp.mean(y, axis=-1, keepdims=True)
    var = jnp.mean(jnp.square(y - mean), axis=-1, keepdims=True)
    yn = (y - mean) * lax.rsqrt(var + GN_EPS)
    for g in range(sub):
        y_ref[g0 + g] = jnp.concatenate([yn[g * RWKV_HEADS + h] for h in range(RWKV_HEADS)], axis=1)


def _rwkv_scan(prep, s0, *, chunk, group, sub):
    r, lw, k, v, kn, b = prep
    n_seq, nh, t_len, hd = r.shape
    hm_spec = pl.BlockSpec((group, nh, chunk, hd), lambda s, c: (s, 0, c, 0))
    st_spec = pl.BlockSpec((group, nh, hd, hd), lambda s, c: (s, 0, 0, 0))
    return pl.pallas_call(
        functools.partial(_rwkv_scan_kernel, chunk=chunk, group=group, sub=sub),
        grid=(n_seq // group, t_len // chunk),
        in_specs=[hm_spec] * 6 + [st_spec],
        out_specs=[pl.BlockSpec((group, chunk, RWKV_W), lambda s, c: (s, c, 0)), st_spec],
        out_shape=[jax.ShapeDtypeStruct((n_seq, t_len, RWKV_W), F32), jax.ShapeDtypeStruct((n_seq, nh, hd, hd), F32)],
        scratch_shapes=[pltpu.VMEM((group * nh, hd, hd), F32)],
        compiler_params=_params("arbitrary", "arbitrary"), name="rwkv_scan",
    )(r, lw, k, v, kn, b, s0)


def _mix_kernel(*refs, n_prompt_blocks):
    pairs, (gate_ref, lnw_ref, lnb_ref, wa_ref, wp_ref, wr_ref, o_ref) = refs[:10], refs[10:]
    is_sample = pl.program_id(0) >= n_prompt_blocks
    att, pool, yn, bonus, g = [jnp.where(is_sample, pairs[2 * j + 1][...], pairs[2 * j][...]) for j in range(5)]
    rw = (yn * lnw_ref[...] + lnb_ref[...] + bonus) * g
    mix = jax.nn.sigmoid(gate_ref[:, 0:D_MODEL]) * _dot(att, wa_ref[...])
    mix = mix + jax.nn.sigmoid(gate_ref[:, D_MODEL:2 * D_MODEL]) * _dot(pool, wp_ref[...])
    mix = mix + jax.nn.sigmoid(gate_ref[:, 2 * D_MODEL:3 * D_MODEL]) * _dot(rw, wr_ref[...])
    o_ref[...] = mix.astype(BF16)


def _mix(branches, z_gate, ln_w, ln_b, wa, wp, wr, *, tm):
    rows = z_gate.shape[0]
    n_prompt_blocks = branches[0][0].shape[0] // tm
    assert rows == (n_prompt_blocks + 1) * tm and all(s.shape[0] == tm for _, s in branches)

    def full(a):
        return pl.BlockSpec(a.shape, lambda i: (0, 0))

    specs, args = [], []
    for p, s in branches:
        specs += [pl.BlockSpec((tm, p.shape[1]), lambda i: (jnp.minimum(i, n_prompt_blocks - 1), 0)), full(s)]
        args += [p, s]
    ln_w = ln_w.reshape(1, RWKV_W)
    ln_b = ln_b.reshape(1, RWKV_W)
    return pl.pallas_call(
        functools.partial(_mix_kernel, n_prompt_blocks=n_prompt_blocks), grid=(rows // tm,),
        in_specs=specs + [pl.BlockSpec((tm, 3 * D_MODEL), lambda i: (i, 0)),
                          full(ln_w), full(ln_b), full(wa), full(wp), full(wr)],
        out_specs=pl.BlockSpec((tm, D_MODEL), lambda i: (i, 0)),
        out_shape=jax.ShapeDtypeStruct((rows, D_MODEL), BF16),
        compiler_params=_params("arbitrary"), name="gated_mix",
    )(*args, z_gate, ln_w, ln_b, wa, wp, wr)


def _mm_res_kernel(x_ref, a_ref, w_ref, o_ref):
    o_ref[...] = x_ref[...] + jnp.dot(a_ref[...], w_ref[...], preferred_element_type=F32)


def _mm_residual(x, a, w, *, tm):
    rows, n = x.shape
    k = a.shape[1]
    return pl.pallas_call(
        _mm_res_kernel, grid=(rows // tm,),
        in_specs=[pl.BlockSpec((tm, n), lambda i: (i, 0)), pl.BlockSpec((tm, k), lambda i: (i, 0)),
                  pl.BlockSpec((k, n), lambda i: (0, 0))],
        out_specs=pl.BlockSpec((tm, n), lambda i: (i, 0)),
        out_shape=jax.ShapeDtypeStruct((rows, n), F32),
        compiler_params=_params("arbitrary"), name="out_proj",
    )(x, a, w)


def _topk_cols(scores, ids, ids_ordered, k):
    lanes = scores[0].shape[1]
    iota_k = lax.broadcasted_iota(I32, (k, lanes), 0)
    big = jnp.iinfo(jnp.int32).max

    def arg_max(s, ident, ordered):
        if ordered:
            tiles = [(s[i:i + SUBLANES], ident[i:i + SUBLANES]) for i in range(0, s.shape[0], SUBLANES)]
            while len(tiles) > 1:
                merged = []
                for (va, ia), (vb, ib) in zip(tiles[0::2], tiles[1::2]):
                    later = vb > va
                    merged.append((jnp.where(later, vb, va), jnp.where(later, ib, ia)))
                tiles = merged + tiles[len(tiles) - len(tiles) % 2:]
            s, ident = tiles[0]
        m = jnp.max(s, axis=0, keepdims=True)
        return m, jnp.min(jnp.where(s == m, ident, big), axis=0, keepdims=True)

    def body(j, carry):
        out = []
        for (s, vals, idxs), ident, ordered in zip(carry, ids, ids_ordered):
            m, idx = arg_max(s, ident, ordered)
            vals = jnp.where(iota_k == j, m, vals)
            idxs = jnp.where(iota_k == j, idx, idxs)
            out.append((jnp.where(ident == idx, NEG_INF, s), vals, idxs))
        return tuple(out)

    init = tuple((s, jnp.zeros((k, lanes), F32), jnp.zeros((k, lanes), I32)) for s in scores)
    return [(v, i) for _, v, i in lax.fori_loop(0, k, body, init)]


def _gather_rows(table, sel, k):
    out = jnp.zeros(sel.shape, table.dtype)
    for a in range(k):
        out = jnp.where(sel == a, table[a:a + 1, :], out)
    return out


def _peer_select_kernel(q_ref, skh_ref, skl_ref, i1_o, i2_o, gate_o, i1_s, i2_s, g_s, *, tok):
    kk = PEER_TOPK
    half = kk // 2
    nt = (((1,), (1,)), ((), ()))
    key_id = lax.broadcasted_iota(I32, (N_KEYS, tok), 0)
    row = lax.broadcasted_iota(I32, (half * half + kk, tok), 0)
    cand_id = jnp.where(row < half * half, (row >> (half.bit_length() - 1)) * kk + (row & (half - 1)),
                        jnp.where(row < half * half + half, row - half * half + half, (row - half * half) * kk))

    def finish(slot, top, sel, k1, k2):
        e1 = _gather_rows(k1, sel >> 4, kk)
        e2 = _gather_rows(k2, sel & (kk - 1), kk)
        ex = jnp.exp(top - jnp.max(top, axis=0, keepdims=True))
        gate = ex / jnp.sum(ex, axis=0, keepdims=True)
        r0 = pl.multiple_of(slot * kk, kk)
        i1_s[pl.ds(r0, kk), :] = e1.astype(F32)
        i2_s[pl.ds(r0, kk), :] = e2.astype(F32)
        g_s[pl.ds(r0, kk), :] = gate

    def head(h, carry):
        cand_prev, k1_prev, k2_prev = carry
        c0 = pl.multiple_of(h * 2 * N_KEYS, 2 * N_KEYS)
        q1h, q1l = _split_bf16(q_ref[:, pl.ds(c0, N_KEYS)])
        q2h, q2l = _split_bf16(q_ref[:, pl.ds(c0 + N_KEYS, N_KEYS)])
        s1 = _dot3(skh_ref[h, 0], skl_ref[h, 0], q1h, q1l, nt)
        s2 = _dot3(skh_ref[h, 1], skl_ref[h, 1], q2h, q2l, nt)
        (t1, k1), (t2, k2), (top, sel) = _topk_cols([s1, s2, cand_prev], [key_id, key_id, cand_id],
                                                    [True, True, False], kk)
        finish(jnp.where(h == 0, PEER_HEADS, h - 1), top, sel, k1_prev, k2_prev)
        cand = jnp.concatenate([t1[a:a + 1, :] + t2[0:half, :] for a in range(half)]
                               + [t1[0:1, :] + t2[half:kk, :], t1[half:kk, :] + t2[0:1, :]], axis=0)
        return cand, k1, k2

    zeros_i = jnp.zeros((kk, tok), I32)
    cand, k1, k2 = lax.fori_loop(0, PEER_HEADS, head, (jnp.zeros(cand_id.shape, F32), zeros_i, zeros_i))
    ((top, sel),) = _topk_cols([cand], [cand_id], [False], kk)
    finish(PEER_HEADS - 1, top, sel, k1, k2)
    i1_o[...] = i1_s[0:PEER_PAIRS, :].T.astype(I32)
    i2_o[...] = i2_s[0:PEER_PAIRS, :].T.astype(I32)
    gate_o[...] = g_s[0:PEER_PAIRS, :].T


def _peer_select(q, sk_hi, sk_lo, *, tok=128):
    rows = q.shape[0]
    spec = pl.BlockSpec((tok, PEER_PAIRS), lambda i: (i, 0))
    return pl.pallas_call(
        functools.partial(_peer_select_kernel, tok=tok), grid=(rows // tok,),
        in_specs=[pl.BlockSpec((tok, q.shape[1]), lambda i: (i, 0)),
                  pl.BlockSpec(sk_hi.shape, lambda i: (0, 0, 0, 0)), pl.BlockSpec(sk_lo.shape, lambda i: (0, 0, 0, 0))],
        out_specs=[spec] * 3,
        out_shape=[jax.ShapeDtypeStruct((rows, PEER_PAIRS), I32), jax.ShapeDtypeStruct((rows, PEER_PAIRS), I32),
                   jax.ShapeDtypeStruct((rows, PEER_PAIRS), F32)],
        scratch_shapes=[pltpu.VMEM((PEER_PAIRS + PEER_TOPK, tok), F32)] * 3,
        compiler_params=_params("arbitrary"), name="peer_select",
    )(q, sk_hi, sk_lo)


def _peer_pick(d_ref, i1_ref, i2_ref, act_ref, first_row):
    n_rows = d_ref.shape[1] // N_KEYS
    for g in range(act_ref.shape[0] // SUBLANES):
        rs = slice(g * SUBLANES, (g + 1) * SUBLANES)
        i1 = i1_ref[rs, :]
        i2 = i2_ref[rs, :]
        acc = act_ref[rs, :]
        for j in range(n_rows):
            got = jnp.take_along_axis(d_ref[rs, j * N_KEYS:(j + 1) * N_KEYS], i2, axis=1, mode="promise_in_bounds")
            acc = acc + jnp.where(i1 == first_row + j, got, 0.0)
        act_ref[rs, :] = acc


def _peer_act_kernel(xn_ref, u_ref, i1_ref, i2_ref, act_ref, da_ref, db_ref):
    first = jnp.logical_and(pl.program_id(0) == 0, pl.program_id(1) == 0)
    c = pl.program_id(1)
    slab = PEER_EC // 2
    slab_rows = slab // N_KEYS
    nt = (((1,), (1,)), ((), ()))

    @pl.when(first)
    def _():
        db_ref[...] = jnp.zeros(db_ref.shape, F32)

    @pl.when(c == 0)
    def _():
        act_ref[...] = jnp.zeros(act_ref.shape, F32)

    base = c * 2 * slab_rows
    _peer_pick(db_ref, i1_ref, i2_ref, act_ref, base - slab_rows)
    da_ref[...] = lax.dot_general(xn_ref[...], u_ref[0, 0:slab, :].astype(BF16), nt, preferred_element_type=F32)
    _peer_pick(da_ref, i1_ref, i2_ref, act_ref, base)
    db_ref[...] = lax.dot_general(xn_ref[...], u_ref[0, slab:2 * slab, :].astype(BF16), nt, preferred_element_type=F32)

    @pl.when(c == pl.num_programs(1) - 1)
    def _():
        _peer_pick(db_ref, i1_ref, i2_ref, act_ref, base + slab_rows)


def _peer_act(xn, u_tabs, layer, i1, i2, *, tb):
    rows = xn.shape[0]
    pair_spec = pl.BlockSpec((tb, PEER_PAIRS), lambda i, c: (i, 0))
    return pl.pallas_call(
        _peer_act_kernel, grid=(rows // tb, N_EXPERTS // PEER_EC),
        in_specs=[pl.BlockSpec((tb, D_MODEL), lambda i, c: (i, 0)),
                  pl.BlockSpec((1, PEER_EC, D_MODEL), lambda i, c: (layer, c, 0)), pair_spec, pair_spec],
        out_specs=pair_spec,
        out_shape=jax.ShapeDtypeStruct((rows, PEER_PAIRS), F32),
        scratch_shapes=[pltpu.VMEM((tb, PEER_EC // 2), F32)] * 2,
        compiler_params=_params("arbitrary", "arbitrary"), name="peer_act",
    )(xn, u_tabs, i1, i2)


def _peer_out_kernel(i1_ref, i2_ref, gate_ref, act_ref, v_ref, o_ref, w_ref):
    c = pl.program_id(1)
    n_groups = w_ref.shape[0]
    rows_per_chunk = PEER_OUT_EC // N_KEYS

    @pl.when(c == 0)
    def _():
        key_iota = lax.broadcasted_iota(I32, (N_KEYS, PEER_PAIRS), 0)

        def group(g, carry):
            r0 = pl.multiple_of(g * SUBLANES, SUBLANES)
            i1 = i1_ref[pl.ds(r0, SUBLANES), :]
            i2 = i2_ref[pl.ds(r0, SUBLANES), :]
            a = act_ref[pl.ds(r0, SUBLANES), :]
            wgt = gate_ref[pl.ds(r0, SUBLANES), :] * (0.5 * a * (1.0 + lax.erf(a * (1.0 / math.sqrt(2.0)))))
            for t in range(SUBLANES):
                hit1 = key_iota == jnp.broadcast_to(i1[t:t + 1, :], key_iota.shape)
                hit2 = key_iota == jnp.broadcast_to(i2[t:t + 1, :], key_iota.shape)
                w2 = jnp.where(hit2, jnp.broadcast_to(wgt[t:t + 1, :], key_iota.shape), 0.0)
                w_tok = _dot_nt(jnp.where(hit1, 1.0, 0.0), w2)
                w_ref[g, pl.ds(t, N_KEYS, stride=SUBLANES), :] = w_tok
            return carry

        lax.fori_loop(0, n_groups, group, 0, unroll=PEER_BUILD_UNROLL)

    row0 = c * (rows_per_chunk * SUBLANES)
    lhs = jnp.concatenate(
        [w_ref[:, pl.ds(pl.multiple_of(row0 + j * SUBLANES, SUBLANES), SUBLANES), :].reshape(n_groups * SUBLANES, N_KEYS)
         for j in range(rows_per_chunk)], axis=1)
    part = jnp.dot(lhs.astype(BF16), v_ref[0], preferred_element_type=F32)

    @pl.when(c == 0)
    def _():
        o_ref[...] = part

    @pl.when(c > 0)
    def _():
        o_ref[...] += part


def _peer_out(i1, i2, gate, act, v_tabs, layer):
    rows = i1.shape[0]
    tb = PEER_TB
    pair_spec = pl.BlockSpec((tb, PEER_PAIRS), lambda i, c: (i, 0))
    return pl.pallas_call(
        _peer_out_kernel, grid=(rows // tb, N_EXPERTS // PEER_OUT_EC),
        in_specs=[pair_spec] * 4 + [pl.BlockSpec((1, PEER_OUT_EC, D_MODEL), lambda i, c: (layer, c, 0))],
        out_specs=pl.BlockSpec((tb, D_MODEL), lambda i, c: (i, 0)),
        out_shape=jax.ShapeDtypeStruct((rows, D_MODEL), F32),
        scratch_shapes=[pltpu.VMEM((tb // SUBLANES, N_KEYS * SUBLANES, N_KEYS), F32)],
        compiler_params=_params("arbitrary", "arbitrary"), name="peer_out",
    )(i1, i2, gate, act, v_tabs)


def _ple_kernel(x_ref, f_ref, p_ref, g_ref, wg_ref, wp_ref, gf_ref, *o_refs, n_prompt_blocks):
    x = x_ref[...] + f_ref[...]
    ms = jnp.mean(x * x, axis=-1, keepdims=True)
    h = (x * lax.rsqrt(ms + RMS_EPS)) * g_ref[...]
    out = x + jax.nn.sigmoid(_dot(h, wg_ref[...])) * _dot(p_ref[...], wp_ref[...])
    if n_prompt_blocks is None:
        o_refs[0][...] = out
    else:
        ms2 = jnp.mean(out * out, axis=-1, keepdims=True)
        out = (out * lax.rsqrt(ms2 + RMS_EPS)) * gf_ref[...]
        is_sample = pl.program_id(0) >= n_prompt_blocks

        @pl.when(jnp.logical_not(is_sample))
        def _():
            o_refs[0][...] = out

        @pl.when(is_sample)
        def _():
            o_refs[1][...] = out


def _ple(x, ffn, p, g, wg, wp, g_final, *, tm, final, rows_prompt):
    rows, n = x.shape
    row = pl.BlockSpec((tm, n), lambda i: (i, 0))
    vec = pl.BlockSpec((1, n), lambda i: (0, 0))
    if final:
        n_prompt_blocks = rows_prompt // tm
        assert rows == (n_prompt_blocks + 1) * tm
        out_specs = [pl.BlockSpec((tm, n), lambda i: (jnp.minimum(i, n_prompt_blocks - 1), 0)),
                     pl.BlockSpec((tm, n), lambda i: (0, 0))]
        out_shape = [jax.ShapeDtypeStruct((rows_prompt, n), F32), jax.ShapeDtypeStruct((tm, n), F32)]
    else:
        n_prompt_blocks = None
        out_specs = row
        out_shape = jax.ShapeDtypeStruct((rows, n), F32)
    return pl.pallas_call(
        functools.partial(_ple_kernel, n_prompt_blocks=n_prompt_blocks), grid=(rows // tm,),
        in_specs=[row, row, pl.BlockSpec((tm, p.shape[1]), lambda i: (i, 0)), vec,
                  pl.BlockSpec(wg.shape, lambda i: (0, 0)), pl.BlockSpec(wp.shape, lambda i: (0, 0)), vec],
        out_specs=out_specs, out_shape=out_shape,
        compiler_params=_params("arbitrary"), name="ple",
    )(x, ffn, p, g.reshape(1, n), wg, wp, g_final.reshape(1, n))


def _layer(x, p_rows, lw, state, g_final, dims, final):
    n_p, t_p, n_s, t_s, past = dims
    rows_p = n_p * t_p
    rows_s = n_s * t_s

    z = _norm_mm(x, lw["g_mix"], lw["w_in"], lw["layer"], tm=768, tn=1920)
    z_qkv = z_pool = z_rwkv = z_gate = z

    layer, depth = lw["layer"], lw["depth"]
    os_, ls_, kv_p = [], [], []
    for gi, (_, dil) in enumerate(ATT_GROUPS):
        o, l, kv = _prompt_attn(z_qkv, lw["cos_p"], lw["sin_p"], gi, dil, layer,
                                None if state["kv_p"] is None else state["kv_p"][gi], depth=depth, n_seq=n_p, seq=t_p)
        os_.append(o)
        ls_.append(l)
        kv_p.append(kv)
    att_p = _merge_attn(os_, ls_, tm=512)
    att_s, *kv_s = _sample_attn(z_qkv, lw["cos_s"], lw["sin_s"], state["caches_t"], layer, state["kv_s"],
                                row0=rows_p, n_seq=n_s, t_new=t_s)

    pool_p = _pool_mixer(z_pool, jnp.zeros((n_p, POOL_HIST, POOL_W), F32), lw["pool_w"], lw["pool_scale"],
                         row0=0, n_seq=n_p, t_len=t_p, pos0=0)
    hist_s = jnp.concatenate([jnp.zeros((n_s, 1, POOL_W), F32), state["pool"]], axis=1)
    pool_s = _pool_mixer(z_pool, hist_s, lw["pool_w"], lw["pool_scale"], row0=rows_p, n_seq=n_s, t_len=t_s, pos0=past)

    prep_p = _rwkv_prep(z_rwkv, jnp.zeros((n_p, 1, RWKV_COLS), F32), lw, row0=0, n_seq=n_p, t_len=t_p, tm=256)
    prep_s = _rwkv_prep(z_rwkv, state["shift"][:, None, :], lw, row0=rows_p, n_seq=n_s, t_len=t_s, tm=t_s)
    yn_p, wkv_p = _rwkv_scan(prep_p[:6], jnp.zeros((n_p, RWKV_HEADS, RWKV_HEAD, RWKV_HEAD), F32),
                             chunk=RWKV_CHUNK, group=RWKV_SEQ_GROUP, sub=RWKV_SEQ_SUB)
    yn_s, wkv_s = _rwkv_scan(prep_s[:6], state["wkv"], chunk=t_s, group=RWKV_SAMPLE_GROUP, sub=RWKV_SAMPLE_GROUP)
    branches = [(att_p, att_s), (pool_p, pool_s), (yn_p.reshape(rows_p, RWKV_W), yn_s.reshape(rows_s, RWKV_W)),
                (prep_p[7], prep_s[7]), (prep_p[6], prep_s[6])]

    mix = _mix(branches, z_gate, lw["ln_w"], lw["ln_b"], lw["w_attn_o"], lw["w_pool_o"], lw["w_rwkv_o"], tm=rows_s)
    x = _mm_residual(x, mix, lw["w_out"], tm=384)

    q, xn = _norm_mm3(x, lw["g_ffn"], *lw["peer_wq"], tm=768, tn=512)
    i1, i2, gate = _peer_select(q, *lw["peer_subkeys"])
    act = _peer_act(xn, lw["peer_u"], lw["layer"], i1, i2, tb=PEER_ACT_TB)
    ffn = _peer_out(i1, i2, gate, act, lw["peer_v"], lw["layer"])

    x = _ple(x, ffn, p_rows, lw["g_ple"], lw["ple_wg"], lw["ple_wp"], g_final, tm=rows_s, final=final, rows_prompt=rows_p)

    keep = POOL_HIST - 1
    pool_cols = slice(Z_POOL, Z_POOL + POOL_W)
    rwkv_cols = slice(Z_RWKV, Z_RWKV + RWKV_COLS)
    pool_p_state = jnp.stack([z[(n + 1) * t_p - keep:(n + 1) * t_p, pool_cols] for n in range(n_p)])
    zp_s = z[rows_p:, pool_cols].reshape(n_s, t_s, POOL_W)
    new_p = [pool_p_state, wkv_p, z[t_p - 1:rows_p:t_p, rwkv_cols]]
    new_s = [jnp.concatenate([hist_s[:, 1:], zp_s], axis=1)[:, -keep:], wkv_s, z[rows_p + t_s - 1::t_s, rwkv_cols]]
    return x, kv_p, kv_s, new_p, new_s


def kernel(x_prompt, x_sample, p_prompt, p_sample, cache_attn_w128, cache_attn_w512, cache_attn_w2048, state_pool, state_rwkv_wkv, state_rwkv_shift, g_mix, w_in, w_attn_o, w_pool_o, w_rwkv_o, w_out, pool_w, pool_scale, rwkv_mu, rwkv_w0, rwkv_w2, rwkv_a0, rwkv_a2, rwkv_g2, rwkv_kk, rwkv_ka, rwkv_rk, rwkv_ln_w, rwkv_ln_b, g_ffn, peer_wq, peer_subkeys, peer_u, peer_v, g_ple, ple_wg, ple_wp, g_final):
    n_p, t_p, _ = x_prompt.shape
    n_s, t_s, _ = x_sample.shape
    depth = w_in.shape[0]
    past = PAST_LEN
    rows_p = n_p * t_p
    rows_s = n_s * t_s
    dims = (n_p, t_p, n_s, t_s, past)

    x = jnp.concatenate([x_prompt.reshape(rows_p, D_MODEL), x_sample.reshape(rows_s, D_MODEL)], axis=0)
    cos_p, sin_p = _rope_tables(jnp.arange(t_p, dtype=I32))
    cos_s, sin_s = _rope_tables(past + jnp.arange(t_s, dtype=I32))
    lora_pad = jnp.zeros((128 - 64, RWKV_W), F32)
    peer_v_bf16 = peer_v.astype(BF16)
    to_dev = (0, 1, 3, 4, 5, 2)
    from_dev = (0, 1, 5, 2, 3, 4)
    caches_t = [jnp.transpose(c, to_dev) for c in (cache_attn_w128, cache_attn_w512, cache_attn_w2048)]

    def row(a):
        return a.reshape(1, -1)

    w_in_z = _repack_w_in(w_in, tk=256)

    new_p, new_s = [], []
    kv_p = kv_s = None
    for l in range(depth):
        lw = {
            "g_mix": g_mix[l], "w_in": w_in_z,
            "cos_p": cos_p, "sin_p": sin_p, "cos_s": cos_s, "sin_s": sin_s, "depth": depth,
            "pool_w": pool_w[l].astype(BF16), "pool_scale": pool_scale[l],
            "mu": row(rwkv_mu[l]), "w0": row(rwkv_w0[l]), "a0": row(rwkv_a0[l]),
            "w2p": jnp.concatenate([rwkv_w2[l], lora_pad], axis=0).astype(BF16),
            "a2p": jnp.concatenate([lora_pad, rwkv_a2[l]], axis=0).astype(BF16),
            "g2": rwkv_g2[l].astype(BF16),
            "kk": row(rwkv_kk[l]), "ka": row(rwkv_ka[l]), "rk": row(rwkv_rk[l]),
            "ln_w": rwkv_ln_w[l], "ln_b": rwkv_ln_b[l],
            "w_attn_o": w_attn_o[l].astype(BF16), "w_pool_o": w_pool_o[l].astype(BF16),
            "w_rwkv_o": w_rwkv_o[l].astype(BF16), "w_out": w_out[l].astype(BF16),
            "g_ffn": g_ffn[l], "peer_wq": _split_bf16(peer_wq[l]), "peer_subkeys": _split_bf16(peer_subkeys[l]),
            "peer_u": peer_u, "peer_v": peer_v_bf16, "layer": l,
            "g_ple": g_ple[l], "ple_wg": ple_wg[l].astype(BF16), "ple_wp": ple_wp[l].astype(BF16),
        }
        state = {"caches_t": caches_t, "kv_p": kv_p, "kv_s": kv_s, "pool": state_pool[l],
                 "wkv": state_rwkv_wkv[l], "shift": state_rwkv_shift[l]}
        p_rows = jnp.concatenate([p_prompt[l].reshape(rows_p, -1), p_sample[l].reshape(rows_s, -1)], axis=0).astype(BF16)
        x, kv_p, kv_s, st_p, st_s = _layer(x, p_rows, lw, state, g_final, dims, l == depth - 1)
        new_p.append(st_p)
        new_s.append(st_s)

    y_p, y_s = x
    outs = [y_p.reshape(n_p, t_p, D_MODEL), y_s.reshape(n_s, t_s, D_MODEL)]
    for gi in range(len(ATT_GROUPS)):
        outs.append(jnp.transpose(kv_p[gi], from_dev))
        outs.append(jnp.transpose(kv_s[gi], from_dev))
    for j in range(3):
        outs.append(jnp.stack([s[j] for s in new_p]))
        outs.append(jnp.stack([s[j] for s in new_s]))
    return tuple(outs)
```
